```python
import jax, jax.numpy as jnp
from jax import lax
import numpy as np

D_MODEL = 2048
BATCH = 4
SEQ = 4096
DEPTH = 1

N_META = 16
HG_HEADS = 8
HG_DK = 128
HG_DV = D_MODEL // 2 // HG_HEADS
HG_F = HG_HEADS * HG_DK
HG_V = HG_HEADS * HG_DV
CHUNK = 64
SC_WIDTH = D_MODEL // 2
CONV_K = 3
N_BRANCH = 2
IN_COLS = 2 * HG_F + 2 * HG_V + 3 * SC_WIDTH + N_BRANCH * D_MODEL
N_EXPERTS = 32
TOP_K = 4
D_FF = D_MODEL
SWIGLU_LIMIT = 7.0
SWIGLU_ALPHA = 1.702
MOE_BLOCK = 128
EPS = 1e-6

kernel_name = "hybrid_hgrn2_shortconv_moe_block"


def _rmsnorm(x, w):
    xf = x.astype(jnp.float32)
    y = xf * lax.rsqrt(jnp.mean(xf * xf, axis=-1, keepdims=True) + EPS) * w.astype(jnp.float32)
    return y.astype(x.dtype)


def _hgrn2_chunked(q, log_f, k, v):
    bsz, l = q.shape[:2]
    pad = (-N_META) % CHUNK

    def prep(t):
        t = jnp.pad(t.astype(jnp.float32), ((0, 0), (pad, 0), (0, 0), (0, 0)))
        n_chunks = t.shape[1] // CHUNK
        t = t.reshape(bsz, n_chunks, CHUNK, t.shape[2], t.shape[3])
        return jnp.transpose(t, (1, 0, 3, 2, 4))

    xs = (prep(q), prep(log_f), prep(k), prep(v))
    causal = jnp.tril(jnp.ones((CHUNK, CHUNK), dtype=bool))

    def step(state, chunk):
        qc, lfc, kc, vc = chunk
        bc = jnp.cumsum(lfc, axis=2)
        diff = bc[:, :, :, None, :] - bc[:, :, None, :, :]
        decay = jnp.exp(jnp.where(causal[:, :, None], diff, -jnp.inf))
        scores = jnp.einsum('bhtd,bhsd,bhtsd->bhts', qc, kc, decay)
        o = (jnp.einsum('bhts,bhsv->bhtv', scores, vc)
             + jnp.einsum('bhtd,bhdv->bhtv', qc * jnp.exp(bc), state))
        b_last = bc[:, :, -1:, :]
        state = (jnp.exp(b_last[:, :, 0, :])[..., None] * state
                 + jnp.einsum('bhsd,bhsv->bhdv', kc * jnp.exp(b_last - bc), vc))
        return state, o

    s0 = jnp.zeros((bsz, HG_HEADS, HG_DK, HG_DV), jnp.float32)
    _, o = lax.scan(step, s0, xs)
    o = jnp.transpose(o, (1, 0, 3, 2, 4)).reshape(bsz, -1, HG_HEADS, HG_DV)
    return o[:, pad:]


def _short_conv(u, w):
    return lax.conv_general_dilated(
        u, w[:, None, :].astype(u.dtype), window_strides=(1,), padding=[(CONV_K - 1, 0)],
        dimension_numbers=('NWC', 'WIO', 'NWC'), feature_group_count=u.shape[-1])


def _mixer(h, lower_bound, w_in, g_norm_w, w_hgrn_out, conv_w, w_conv_out, w_o):
    bsz, l, _ = h.shape
    sizes = (HG_F, HG_F, HG_V, HG_V, SC_WIDTH, SC_WIDTH, SC_WIDTH, D_MODEL, D_MODEL)
    split_at = np.cumsum(sizes[:-1]).tolist()
    proj = h @ w_in
    q, f_pre, v, g, sc_v, sc_b, sc_c, gate_a, gate_b = jnp.split(proj, split_at, axis=-1)

    f = lower_bound + (1.0 - lower_bound) * jax.nn.sigmoid(f_pre.astype(jnp.float32))
    log_f = jnp.log(f)
    k = 1.0 - f
    heads = lambda t, d: t.reshape(bsz, l, HG_HEADS, d)
    o = _hgrn2_chunked(heads(jax.nn.silu(q), HG_DK), heads(log_f, HG_DK), heads(k, HG_DK), heads(v, HG_DV))
    o = _rmsnorm(o, g_norm_w).astype(h.dtype).reshape(bsz, l, HG_V) * jax.nn.silu(g)
    y_a = o @ w_hgrn_out

    y_b = (sc_b * _short_conv(sc_c * sc_v, conv_w)) @ w_conv_out

    merged = jax.nn.sigmoid(gate_a) * y_a + jax.nn.sigmoid(gate_b) * y_b
    return merged @ w_o


def _moe(h, w_router, b_router, w_up, b_up, w_down, b_down):
    bsz, l, d = h.shape
    xt = h.reshape(-1, d)
    n = xt.shape[0]
    logits = (xt @ w_router + b_router).astype(jnp.float32)
    top_logits, top_idx = lax.top_k(logits, TOP_K)
    gate = jax.nn.softmax(top_logits, axis=-1)
    flat_e = top_idx.reshape(-1)
    flat_tok = jnp.repeat(jnp.arange(n, dtype=jnp.int32), TOP_K)
    flat_w = gate.reshape(-1)
    order = jnp.argsort(flat_e)
    se, stok, sw = flat_e[order], flat_tok[order], flat_w[order]
    counts = jnp.zeros((N_EXPERTS,), jnp.int32).at[flat_e].add(1)
    padded = (counts + MOE_BLOCK - 1) // MOE_BLOCK * MOE_BLOCK
    pad_end = jnp.cumsum(padded)
    pad_start = pad_end - padded
    sort_start = jnp.cumsum(counts) - counts
    rows = pad_start[se] + (jnp.arange(n * TOP_K, dtype=jnp.int32) - sort_start[se])
    n_rows = -(-(n * TOP_K + N_EXPERTS * (MOE_BLOCK - 1)) // MOE_BLOCK) * MOE_BLOCK
    n_blocks = n_rows // MOE_BLOCK
    row_tok = jnp.full((n_rows,), n, jnp.int32).at[rows].set(stok)
    row_w = jnp.zeros((n_rows,), jnp.float32).at[rows].set(sw)
    block_e = jnp.minimum(
        jnp.searchsorted(pad_end, jnp.arange(n_blocks, dtype=jnp.int32) * MOE_BLOCK, side='right'),
        N_EXPERTS - 1)
    x_rows = jnp.concatenate([xt, jnp.zeros((1, d), xt.dtype)], axis=0)[row_tok]
    x_rows = x_rows.reshape(n_blocks, MOE_BLOCK, d)

    def expert_block(args):
        xb, e = args
        hu = xb @ w_up[e] + b_up[e]
        g, u = jnp.split(hu, 2, axis=-1)
        g = jnp.minimum(g, SWIGLU_LIMIT)
        u = jnp.clip(u, -SWIGLU_LIMIT, SWIGLU_LIMIT)
        return ((u + 1.0) * (g * jax.nn.sigmoid(SWIGLU_ALPHA * g))) @ w_down[e] + b_down[e]

    y_rows = lax.map(expert_block, (x_rows, block_e)).reshape(n_rows, d)
    y = jax.ops.segment_sum(y_rows * row_w[:, None].astype(y_rows.dtype), row_tok, num_segments=n + 1)[:n]
    return y.reshape(bsz, l, d)


def setup_inputs(seed: int = 0) -> dict:
    key = jax.random.key(seed)
    ks = jax.random.split(key, 19)
    nrm = lambda k, shape, scale: jax.random.normal(k, shape, jnp.float32) * scale
    return {
        'x': nrm(ks[0], (BATCH, SEQ, D_MODEL), 1.0),
        'meta_tokens': nrm(ks[1], (N_META, D_MODEL), 1.0),
        'norm_mix_w': 1.0 + nrm(ks[2], (DEPTH, D_MODEL), 0.02),
        'w_in': nrm(ks[3], (DEPTH, D_MODEL, IN_COLS), D_MODEL ** -0.5),
        'lb_logits': nrm(ks[4], (DEPTH + 1, HG_F), 0.5),
        'g_norm_w': 1.0 + nrm(ks[5], (DEPTH, HG_DV), 0.02),
        'w_hgrn_out': nrm(ks[6], (DEPTH, HG_V, D_MODEL), HG_V ** -0.5),
        'conv_w': nrm(ks[7], (DEPTH, CONV_K, SC_WIDTH), CONV_K ** -0.5),
        'w_conv_out': nrm(ks[8], (DEPTH, SC_WIDTH, D_MODEL), SC_WIDTH ** -0.5),
        'w_o': nrm(ks[9], (DEPTH, D_MODEL, D_MODEL), D_MODEL ** -0.5),
        'norm_ffn_w': 1.0 + nrm(ks[10], (DEPTH, D_MODEL), 0.02),
        'w_router': nrm(ks[11], (DEPTH, D_MODEL, N_EXPERTS), D_MODEL ** -0.5),
        'b_router': nrm(ks[12], (DEPTH, N_EXPERTS), 0.01),
        'w_up': nrm(ks[13], (DEPTH, N_EXPERTS, D_MODEL, 2 * D_FF), D_MODEL ** -0.5),
        'b_up': nrm(ks[14], (DEPTH, N_EXPERTS, 2 * D_FF), 0.02),
        'w_down': nrm(ks[15], (DEPTH, N_EXPERTS, D_FF, D_MODEL), D_FF ** -0.5),
        'b_down': nrm(ks[16], (DEPTH, N_EXPERTS, D_MODEL), 0.02),
        'final_norm_w': 1.0 + nrm(ks[17], (D_MODEL,), 0.02),
    }


def reference(x, meta_tokens, norm_mix_w, w_in, lb_logits, g_norm_w, w_hgrn_out, conv_w, w_conv_out,
              w_o, norm_ffn_w, w_router, b_router, w_up, b_up, w_down, b_down, final_norm_w):
    bsz = x.shape[0]
    meta = jnp.broadcast_to(meta_tokens[None].astype(x.dtype), (bsz, N_META, D_MODEL))
    h = jnp.concatenate([meta, x], axis=1)
    lower_bounds = jnp.cumsum(jax.nn.softmax(lb_logits.astype(jnp.float32), axis=0), axis=0)
    for layer in range(DEPTH):
        h = h + _mixer(_rmsnorm(h, norm_mix_w[layer]), lower_bounds[layer], w_in[layer], g_norm_w[layer],
                       w_hgrn_out[layer], conv_w[layer], w_conv_out[layer], w_o[layer])
        h = h + _moe(_rmsnorm(h, norm_ffn_w[layer]), w_router[layer], b_router[layer], w_up[layer],
                     b_up[layer], w_down[layer], b_down[layer])
    out = _rmsnorm(h, final_norm_w)
    return out[:, N_META:]
```

```python
import functools

import numpy as np
import jax
import jax.numpy as jnp
from jax import lax
from jax.experimental import pallas as pl
from jax.experimental.pallas import tpu as pltpu

F32 = jnp.float32
BF16 = jnp.bfloat16

N_META = 16
HEADS = 8
HEAD_DIM = 128
N_EXPERTS = 32
TOP_K = 4
SWIGLU_LIMIT = 7.0
SWIGLU_ALPHA = 1.702
EPS = 1e-6

CHUNK = 64
HGRN_ROWS = 512
COL_TILE = 1024
EXPERT_ROWS = 1024
EXPERT_SUB = 256
EXPERT_FF_TILE = 256
COMBINE_ROWS = 128

V7X_VMEM_LIMIT = 56 * 1024 * 1024


def _dot(a, b):
    return jnp.dot(a, b, preferred_element_type=F32)


def _dot_bt(a, b):
    return lax.dot_general(a, b, (((1,), (1,)), ((), ())), preferred_element_type=F32)


def _dot_at(a, b):
    return lax.dot_general(a, b, (((0,), (0,)), ((), ())), preferred_element_type=F32)


def _split3(x):
    hi = x.astype(BF16)
    r1 = x - hi.astype(F32)
    mid = r1.astype(BF16)
    lo = (r1 - mid.astype(F32)).astype(BF16)
    return hi, mid, lo


def _params(*sem):
    return pltpu.CompilerParams(dimension_semantics=sem, vmem_limit_bytes=V7X_VMEM_LIMIT)


def _prenorm_kernel(x_ref, w_ref, o_ref):
    x = x_ref[...]
    ms = jnp.mean(x * x, axis=-1, keepdims=True)
    o_ref[...] = (x * lax.rsqrt(ms + EPS) * w_ref[...]).astype(o_ref.dtype)


def _prenorm(x, w, tm):
    m, d = x.shape
    return pl.pallas_call(
        _prenorm_kernel,
        grid=(m // tm,),
        in_specs=[pl.BlockSpec((tm, d), lambda i: (i, 0)), pl.BlockSpec((1, d), lambda i: (0, 0))],
        out_specs=pl.BlockSpec((tm, d), lambda i: (i, 0)),
        out_shape=jax.ShapeDtypeStruct((m, d), BF16),
        compiler_params=_params("arbitrary"),
        name="prenorm",
    )(x, w.reshape(1, d))


def _inproj_kernel(col_ref, code_ref, x_ref, w_ref, o_ref, wb_ref):
    n = pl.program_id(0)

    @pl.when(pl.program_id(1) == 0)
    def _():
        wb_ref[...] = w_ref[...].astype(BF16)

    z = _dot(x_ref[...], wb_ref[...])
    code = code_ref[n]

    @pl.when(code == 0)
    def _():
        o_ref[...] = z.astype(o_ref.dtype)

    @pl.when(code == 1)
    def _():
        o_ref[...] = (z * jax.nn.sigmoid(z)).astype(o_ref.dtype)

    @pl.when(code == 2)
    def _():
        o_ref[...] = jax.nn.sigmoid(z).astype(o_ref.dtype)


def _inproj(xn, w_in, cols, codes, tm):
    m, d = xn.shape
    nt = len(cols)
    grid_spec = pltpu.PrefetchScalarGridSpec(
        num_scalar_prefetch=2,
        grid=(nt, m // tm),
        in_specs=[
            pl.BlockSpec((tm, d), lambda n, i, col, code: (i, 0)),
            pl.BlockSpec((d, COL_TILE), lambda n, i, col, code: (0, col[n])),
        ],
        out_specs=pl.BlockSpec((tm, COL_TILE), lambda n, i, col, code: (i, n)),
        scratch_shapes=[pltpu.VMEM((d, COL_TILE), BF16)],
    )
    return pl.pallas_call(
        _inproj_kernel,
        grid_spec=grid_spec,
        out_shape=jax.ShapeDtypeStruct((m, nt * COL_TILE), BF16),
        compiler_params=_params("arbitrary", "arbitrary"),
        name="inproj",
    )(jnp.asarray(cols, jnp.int32), jnp.asarray(codes, jnp.int32), xn, w_in)


def _fgate_kernel(x_ref, w_ref, lbl_ref, lf_ref, kk_ref, wb_ref):
    @pl.when(pl.program_id(0) == 0)
    def _():
        wb_ref[...] = w_ref[...].astype(BF16)

    z = _dot(x_ref[...], wb_ref[...])
    lbl = lbl_ref[...]
    e = jnp.exp(lbl - jnp.max(lbl, axis=0, keepdims=True))
    lb = e[0:1] / jnp.sum(e, axis=0, keepdims=True)
    f = lb + (1.0 - lb) * jax.nn.sigmoid(z)
    lf_ref[...] = jnp.log(f)
    kk_ref[...] = (1.0 - lb) * jax.nn.sigmoid(-z)


def _fgate(xn, w_in, lb_logits, col, tm):
    m, d = xn.shape
    r = lb_logits.shape[0]
    out = jax.ShapeDtypeStruct((m, COL_TILE), F32)
    return pl.pallas_call(
        _fgate_kernel,
        grid=(m // tm,),
        in_specs=[
            pl.BlockSpec((tm, d), lambda i: (i, 0)),
            pl.BlockSpec((d, COL_TILE), lambda i: (0, col)),
            pl.BlockSpec((r, COL_TILE), lambda i: (0, 0)),
        ],
        out_specs=[pl.BlockSpec((tm, COL_TILE), lambda i: (i, 0))] * 2,
        out_shape=[out, out],
        scratch_shapes=[pltpu.VMEM((d, COL_TILE), BF16)],
        compiler_params=_params("arbitrary"),
        name="fgate",
    )(xn, w_in, lb_logits)


def _hgrn_constants(c):
    nl = int(np.log2(c))
    assert (1 << nl) == c
    rr = np.arange(c)[:, None]
    uu = np.arange(c)[None, :]
    mats = [uu <= rr, uu > rr]
    masks, sels = [], []
    for lvl in range(nl):
        b = 1 << lvl
        start = (rr // (2 * b)) * (2 * b)
        mid = start + b - 1
        second = (rr - start) >= b
        mats.append(np.where(second, (uu > mid) & (uu <= rr), (uu > rr) & (uu <= mid)))
        masks.append(((rr // (2 * b)) == (uu // (2 * b))) & ((rr % (2 * b)) >= b) & ((uu % (2 * b)) < b))
        sels.append(np.broadcast_to(second, (c, HEAD_DIM)))
    masks.append(np.eye(c, dtype=bool))
    mall = jnp.asarray(np.concatenate(mats, 0).astype(np.float32), BF16)
    return mall, jnp.asarray(np.stack(masks).astype(np.float32)), jnp.asarray(np.stack(sels).astype(np.float32))


def _hgrn_kernel(q_ref, v_ref, g_ref, lf_ref, kk_ref, gw_ref, s0_ref, mall_ref, mask_ref, sel_ref,
                 o_ref, sfin_ref, st_ref, *, chunk, n_chunks):
    nl = mask_ref.shape[0] - 1
    step = pl.program_id(1)

    @pl.when(step == 0)
    def _():
        st_ref[...] = s0_ref[...]

    def chunk_body(ci, carry):
        rows = pl.ds(pl.multiple_of(ci * chunk, chunk), chunk)
        mall = mall_ref[...]
        for h in range(HEADS):
            cs = slice(h * HEAD_DIM, (h + 1) * HEAD_DIM)
            hi, mid, lo = _split3(lf_ref[rows, cs])
            e_all = jnp.exp(_dot(mall, hi) + _dot(mall, mid) + _dot(mall, lo))
            qb = q_ref[rows, cs]
            q = qb.astype(F32)
            k = kk_ref[rows, cs]
            v = v_ref[rows, cs]
            scores = mask_ref[nl] * _dot_bt(qb, k.astype(BF16))
            for lvl in range(nl):
                e_l = e_all[(2 + lvl) * chunk:(3 + lvl) * chunk]
                x = (jnp.where(sel_ref[lvl] > 0.5, q, k) * e_l).astype(BF16)
                scores = scores + mask_ref[lvl] * _dot_bt(x, x)
            st = st_ref[h]
            q_in = (q * e_all[0:chunk]).astype(BF16)
            o = _dot(scores.astype(BF16), v) + _dot_bt(q_in, st.astype(BF16))
            k_out = (k * e_all[chunk:2 * chunk]).astype(BF16)
            st_ref[h] = st * e_all[chunk - 1:chunk] + _dot_at(v, k_out)
            ms = jnp.mean(o * o, axis=-1, keepdims=True)
            on = o * lax.rsqrt(ms + EPS) * gw_ref[...]
            o_ref[rows, cs] = (on * g_ref[rows, cs].astype(F32)).astype(o_ref.dtype)
        return carry

    lax.fori_loop(0, n_chunks, chunk_body, 0)

    @pl.when(step == pl.num_programs(1) - 1)
    def _():
        sfin_ref[0] = st_ref[...]


def _hgrn(proj, lf, kk, g_norm_w, s0, consts, bsz, seq, rows):
    mall, masks, sels = consts
    steps = seq // rows
    w = HEADS * HEAD_DIM
    assert w == COL_TILE
    row_map = lambda col: (lambda b, s: (b * steps + s, col))
    const2 = lambda b, s: (0, 0)
    const3 = lambda b, s: (0, 0, 0)
    kern = functools.partial(_hgrn_kernel, chunk=CHUNK, n_chunks=rows // CHUNK)
    return pl.pallas_call(
        kern,
        grid=(bsz, steps),
        in_specs=[
            pl.BlockSpec((rows, w), row_map(0)),
            pl.BlockSpec((rows, w), row_map(1)),
            pl.BlockSpec((rows, w), row_map(2)),
            pl.BlockSpec((rows, w), row_map(0)),
            pl.BlockSpec((rows, w), row_map(0)),
            pl.BlockSpec((1, HEAD_DIM), const2),
            pl.BlockSpec((HEADS, HEAD_DIM, HEAD_DIM), const3),
            pl.BlockSpec(mall.shape, const2),
            pl.BlockSpec(masks.shape, const3),
            pl.BlockSpec(sels.shape, const3),
        ],
        out_specs=[
            pl.BlockSpec((rows, w), row_map(0)),
            pl.BlockSpec((1, HEADS, HEAD_DIM, HEAD_DIM), lambda b, s: (b, 0, 0, 0)),
        ],
        out_shape=[
            jax.ShapeDtypeStruct((bsz * seq, w), BF16),
            jax.ShapeDtypeStruct((bsz, HEADS, HEAD_DIM, HEAD_DIM), F32),
        ],
        scratch_shapes=[pltpu.VMEM((HEADS, HEAD_DIM, HEAD_DIM), F32)],
        compiler_params=_params("arbitrary", "arbitrary"),
        name="hgrn2",
    )(proj, proj, proj, lf, kk, g_norm_w.reshape(1, HEAD_DIM), s0, mall, masks, sels)


def _pack_bf16_pairs(x):
    n = x.shape[1] // 2
    lo = lax.bitcast_convert_type(x[:, :n].astype(BF16).astype(F32), jnp.uint32)
    hi = lax.bitcast_convert_type(x[:, n:].astype(BF16).astype(F32), jnp.uint32)
    return (lo >> 16) | (hi & jnp.uint32(0xFFFF0000))


def _unpack_bf16_pairs(w):
    lo = lax.bitcast_convert_type(w << 16, F32)
    hi = lax.bitcast_convert_type(w & jnp.uint32(0xFFFF0000), F32)
    return jnp.concatenate([lo, hi], axis=1).astype(BF16)


def _mixer_out_kernel(og_ref, scv_ref, scb_ref, scc_ref, ga_ref, gb_ref, x_ref,
                      pv_ref, pc_ref, mv_ref, mc_ref,
                      wa_ref, wb_ref, wo_ref, cw_ref, nw_ref, wr_ref, br_ref,
                      h1_ref, xp_ref, idx_ref, gate_ref, *, tiles_per_seq):
    i = pl.program_id(0)
    tm = x_ref.shape[0]
    first = (i % tiles_per_seq) == 0

    u = scc_ref[...].astype(F32) * scv_ref[...].astype(F32)
    halo_prev = pc_ref[...].astype(F32) * pv_ref[...].astype(F32)
    halo_meta = mc_ref[...].astype(F32) * mv_ref[...].astype(F32)
    halo = jnp.where(first, halo_meta, halo_prev)
    hr = halo.shape[0]
    r = lax.broadcasted_iota(jnp.int32, (tm, 1), 0)
    u1 = jnp.where(r == 0, halo[hr - 1:hr], pltpu.roll(u, 1, 0))
    u2 = jnp.where(r == 0, halo[hr - 2:hr - 1], jnp.where(r == 1, halo[hr - 1:hr], pltpu.roll(u, 2, 0)))
    conv = cw_ref[2:3] * u + cw_ref[1:2] * u1 + cw_ref[0:1] * u2
    yb_in = (scb_ref[...].astype(F32) * conv).astype(BF16)

    y_a = _dot(og_ref[...], wa_ref[...])
    y_b = _dot(yb_in, wb_ref[...])
    merged = (ga_ref[...].astype(F32) * y_a + gb_ref[...].astype(F32) * y_b).astype(BF16)
    h1 = x_ref[...] + _dot(merged, wo_ref[...])
    h1_ref[...] = h1

    ms = jnp.mean(h1 * h1, axis=-1, keepdims=True)
    xn = h1 * lax.rsqrt(ms + EPS) * nw_ref[...]
    xp_ref[...] = _pack_bf16_pairs(xn)

    xh = xn.astype(BF16)
    xl = (xn - xh.astype(F32)).astype(BF16)
    wr = wr_ref[...]
    wh = wr.astype(BF16)
    wl = (wr - wh.astype(F32)).astype(BF16)
    logits = _dot_bt(wh, xh) + _dot_bt(wh, xl) + _dot_bt(wl, xh) + br_ref[...]
    ne = logits.shape[0]
    ie = lax.broadcasted_iota(jnp.int32, logits.shape, 0)
    tops, idxs = [], []
    for _ in range(TOP_K):
        mx = jnp.max(logits, axis=0, keepdims=True)
        ix = jnp.min(jnp.where(logits == mx, ie, ne), axis=0, keepdims=True)
        tops.append(mx)
        idxs.append(ix)
        logits = jnp.where(ie == ix, -jnp.inf, logits)
    es = [jnp.exp(t - tops[0]) for t in tops]
    den = es[0]
    for e in es[1:]:
        den = den + e
    gate_ref[...] = jnp.concatenate([e / den for e in es], axis=0)
    idx_ref[...] = jnp.concatenate(idxs, axis=0)


def _mixer_out(og, proj, proj_meta, x2d, wa, wb, wo, conv_w, norm_w, w_router, b_router, seq, tm):
    m, d = x2d.shape
    w = COL_TILE
    halo = proj_meta.shape[0]
    assert tm % halo == 0 and seq % tm == 0 and d == 2 * w
    ne = w_router.shape[1]
    per_halo = tm // halo
    row = lambda col: (lambda i: (i, col))
    prev = lambda col: (lambda i: (jnp.maximum(i * per_halo - 1, 0), col))
    const = lambda i: (0, 0)
    whole = lambda a: pl.BlockSpec(a.shape, const)
    wr_t = w_router.T
    kern = functools.partial(_mixer_out_kernel, tiles_per_seq=seq // tm)
    return pl.pallas_call(
        kern,
        grid=(m // tm,),
        in_specs=[
            pl.BlockSpec((tm, w), row(0)),
            pl.BlockSpec((tm, w), row(3)),
            pl.BlockSpec((tm, w), row(4)),
            pl.BlockSpec((tm, w), row(5)),
            pl.BlockSpec((tm, d), row(3)),
            pl.BlockSpec((tm, d), row(4)),
            pl.BlockSpec((tm, d), row(0)),
            pl.BlockSpec((halo, w), prev(3)),
            pl.BlockSpec((halo, w), prev(5)),
            pl.BlockSpec((halo, w), lambda i: (0, 3)),
            pl.BlockSpec((halo, w), lambda i: (0, 5)),
            whole(wa), whole(wb), whole(wo),
            pl.BlockSpec(conv_w.shape, const),
            pl.BlockSpec((1, d), const),
            pl.BlockSpec((ne, d), const),
            pl.BlockSpec((ne, 1), const),
        ],
        out_specs=[
            pl.BlockSpec((tm, d), row(0)),
            pl.BlockSpec((tm, d // 2), row(0)),
            pl.BlockSpec((TOP_K, tm), lambda i: (0, i)),
            pl.BlockSpec((TOP_K, tm), lambda i: (0, i)),
        ],
        out_shape=[
            jax.ShapeDtypeStruct((m, d), F32),
            jax.ShapeDtypeStruct((m, d // 2), jnp.uint32),
            jax.ShapeDtypeStruct((TOP_K, m), jnp.int32),
            jax.ShapeDtypeStruct((TOP_K, m), F32),
        ],
        compiler_params=_params("arbitrary"),
        name="mixer_out",
    )(og, proj, proj, proj, proj, proj, x2d, proj, proj, proj_meta, proj_meta,
      wa, wb, wo, conv_w, norm_w.reshape(1, d), wr_t, b_router.reshape(ne, 1))


def _gather_kernel(act_ref, tok_ref, x_hbm, o_ref, sem):
    i = pl.program_id(0)
    n = o_ref.shape[0]

    @pl.when(act_ref[i] > 0)
    def _():
        def issue(r, carry):
            t = tok_ref[0, 0, r]
            pltpu.make_async_copy(x_hbm.at[pl.ds(t, 1)], o_ref.at[pl.ds(r, 1)], sem).start()
            return carry

        lax.fori_loop(0, n, issue, 0, unroll=8)
        pltpu.make_async_copy(x_hbm.at[pl.ds(0, n)], o_ref, sem).wait()

    @pl.when(act_ref[i] == 0)
    def _():
        o_ref[...] = jnp.zeros_like(o_ref)


def _gather_rows(xp, row_tok, sub_active):
    n_rows = row_tok.shape[0]
    width = xp.shape[1]
    n_sub = n_rows // EXPERT_SUB
    grid_spec = pltpu.PrefetchScalarGridSpec(
        num_scalar_prefetch=1,
        grid=(n_sub,),
        in_specs=[
            pl.BlockSpec((1, 1, EXPERT_SUB), lambda i, act: (i, 0, 0), memory_space=pltpu.SMEM),
            pl.BlockSpec(memory_space=pl.ANY),
        ],
        out_specs=pl.BlockSpec((EXPERT_SUB, width), lambda i, act: (i, 0)),
        scratch_shapes=[pltpu.SemaphoreType.DMA(())],
    )
    return pl.pallas_call(
        _gather_kernel,
        grid_spec=grid_spec,
        out_shape=jax.ShapeDtypeStruct((n_rows, width), xp.dtype),
        compiler_params=_params("arbitrary"),
        name="gather_rows",
    )(sub_active, row_tok.reshape(n_sub, 1, EXPERT_SUB), xp)


def _expert_kernel(sbi_ref, sbe_ref, nsub_ref, jmap_ref, x_ref, wg_ref, wu_ref, wd_ref, bg_ref, bu_ref, bd_ref,
                   o_ref, xb_ref, wgb_ref, wub_ref, wdb_ref):
    s = pl.program_id(0)
    j = pl.program_id(1)
    nsub = nsub_ref[s]
    n_sub_blocks = o_ref.shape[0] // EXPERT_SUB

    @pl.when(nsub > 0)
    def _():
        @pl.when(j == 0)
        def _():
            xb_ref[...] = _unpack_bf16_pairs(x_ref[...])

        wgb_ref[...] = wg_ref[0].astype(BF16)
        wub_ref[...] = wu_ref[0].astype(BF16)
        wdb_ref[...] = wd_ref[0].astype(BF16)

        for b in range(n_sub_blocks):
            rows = slice(b * EXPERT_SUB, (b + 1) * EXPERT_SUB)

            @pl.when(b < nsub)
            def _():
                xs = xb_ref[rows, :]
                g = _dot(xs, wgb_ref[...]) + bg_ref[0]
                u = _dot(xs, wub_ref[...]) + bu_ref[0]
                g = jnp.minimum(g, SWIGLU_LIMIT)
                u = jnp.clip(u, -SWIGLU_LIMIT, SWIGLU_LIMIT)
                act = ((u + 1.0) * (g * jax.nn.sigmoid(SWIGLU_ALPHA * g))).astype(BF16)
                y = _dot(act, wdb_ref[...])

                @pl.when(j == 0)
                def _():
                    o_ref[rows, :] = y + bd_ref[0]

                @pl.when(j > 0)
                def _():
                    o_ref[rows, :] += y

    for b in range(n_sub_blocks):
        @pl.when(jnp.logical_and(b >= nsub, j == 0))
        def _():
            o_ref[b * EXPERT_SUB:(b + 1) * EXPERT_SUB, :] = jnp.zeros((EXPERT_SUB, o_ref.shape[1]), o_ref.dtype)


def _experts(x_rows, w_up, b_up, w_down, b_down, sb_idx, sb_expert, sb_nsub):
    n_rows, half = x_rows.shape
    d = 2 * half
    ne, _, ff2 = w_up.shape
    ff = ff2 // 2
    tf = EXPERT_FF_TILE
    nj = ff // tf
    n_sb = n_rows // EXPERT_ROWS
    jmap = jnp.where(sb_nsub[:, None] > 0, jnp.arange(nj, dtype=jnp.int32)[None, :], nj - 1).astype(jnp.int32)
    grid_spec = pltpu.PrefetchScalarGridSpec(
        num_scalar_prefetch=4,
        grid=(n_sb, nj),
        in_specs=[
            pl.BlockSpec((EXPERT_ROWS, half), lambda s, j, sbi, sbe, ns, jm: (sbi[s], 0)),
            pl.BlockSpec((1, d, tf), lambda s, j, sbi, sbe, ns, jm: (sbe[s], 0, jm[s, j])),
            pl.BlockSpec((1, d, tf), lambda s, j, sbi, sbe, ns, jm: (sbe[s], 0, nj + jm[s, j])),
            pl.BlockSpec((1, tf, d), lambda s, j, sbi, sbe, ns, jm: (sbe[s], jm[s, j], 0)),
            pl.BlockSpec((1, 1, tf), lambda s, j, sbi, sbe, ns, jm: (sbe[s], 0, jm[s, j])),
            pl.BlockSpec((1, 1, tf), lambda s, j, sbi, sbe, ns, jm: (sbe[s], 0, nj + jm[s, j])),
            pl.BlockSpec((1, 1, d), lambda s, j, sbi, sbe, ns, jm: (sbe[s], 0, 0)),
        ],
        out_specs=pl.BlockSpec((EXPERT_ROWS, d), lambda s, j, sbi, sbe, ns, jm: (s, 0)),
        scratch_shapes=[
            pltpu.VMEM((EXPERT_ROWS, d), BF16),
            pltpu.VMEM((d, tf), BF16),
            pltpu.VMEM((d, tf), BF16),
            pltpu.VMEM((tf, d), BF16),
        ],
    )
    return pl.pallas_call(
        _expert_kernel,
        grid_spec=grid_spec,
        out_shape=jax.ShapeDtypeStruct((n_rows, d), F32),
        compiler_params=_params("arbitrary", "arbitrary"),
        name="experts",
    )(sb_idx, sb_expert, sb_nsub, jmap, x_rows, w_up, w_up, w_down,
      b_up.reshape(ne, 1, ff2), b_up.reshape(ne, 1, ff2), b_down.reshape(ne, 1, d))


def _combine_kernel(pos_ref, gate_ref, h1_ref, fw_ref, y_hbm, o_ref, buf_ref, sem):
    tt = h1_ref.shape[0]

    def issue(r, carry):
        for k in range(TOP_K):
            p = pos_ref[0, 0, k * tt + r]
            pltpu.make_async_copy(y_hbm.at[pl.ds(p, 1)], buf_ref.at[k, pl.ds(r, 1)], sem).start()
        return carry

    lax.fori_loop(0, tt, issue, 0, unroll=4)
    for k in range(TOP_K):
        pltpu.make_async_copy(y_hbm.at[pl.ds(0, tt)], buf_ref.at[k], sem).wait()

    gate = gate_ref[...]
    gpad = jnp.concatenate([gate, jnp.zeros((tt - TOP_K, tt), F32)], axis=0)
    gcol = gpad.T
    acc = h1_ref[...]
    for k in range(TOP_K):
        acc = acc + gcol[:, k:k + 1] * buf_ref[k]
    ms = jnp.mean(acc * acc, axis=-1, keepdims=True)
    o_ref[...] = acc * lax.rsqrt(ms + EPS) * fw_ref[...]


def _combine(y_rows, pos, gate, h1, final_w):
    m, d = h1.shape
    tt = COMBINE_ROWS
    nt = m // tt
    pos_t = pos.reshape(TOP_K, nt, tt).transpose(1, 0, 2).reshape(nt, 1, TOP_K * tt)
    return pl.pallas_call(
        _combine_kernel,
        grid=(nt,),
        in_specs=[
            pl.BlockSpec((1, 1, TOP_K * tt), lambda i: (i, 0, 0), memory_space=pltpu.SMEM),
            pl.BlockSpec((TOP_K, tt), lambda i: (0, i)),
            pl.BlockSpec((tt, d), lambda i: (i, 0)),
            pl.BlockSpec((1, d), lambda i: (0, 0)),
            pl.BlockSpec(memory_space=pl.ANY),
        ],
        out_specs=pl.BlockSpec((tt, d), lambda i: (i, 0)),
        out_shape=jax.ShapeDtypeStruct((m, d), F32),
        scratch_shapes=[pltpu.VMEM((TOP_K, tt, d), F32), pltpu.SemaphoreType.DMA(())],
        compiler_params=_params("arbitrary"),
        name="combine",
    )(pos_t, gate, h1, final_w.reshape(1, d), y_rows)


def _routing_tables(idx, m):
    n_assign = TOP_K * m
    flat_e = idx.reshape(n_assign)
    onehot = flat_e[:, None] == jnp.arange(N_EXPERTS, dtype=jnp.int32)[None, :]
    csum = jnp.cumsum(onehot.astype(jnp.int32), axis=0)
    rank = jnp.sum(jnp.where(onehot, csum - 1, 0), axis=1)
    counts = csum[-1]
    padded = (counts + EXPERT_ROWS - 1) // EXPERT_ROWS * EXPERT_ROWS
    pad_end = jnp.cumsum(padded)
    pad_start = pad_end - padded
    pos = (jnp.sum(jnp.where(onehot, pad_start[None, :], 0), axis=1) + rank).astype(jnp.int32)
    n_sb = -(-(n_assign + N_EXPERTS * (EXPERT_ROWS - 1)) // EXPERT_ROWS)
    n_rows = n_sb * EXPERT_ROWS
    tok = (jnp.arange(n_assign, dtype=jnp.int32) % m).astype(jnp.int32)
    row_tok = jnp.zeros((n_rows,), jnp.int32).at[pos].set(tok)
    sb_start = jnp.arange(n_sb, dtype=jnp.int32) * EXPERT_ROWS
    sb_e = jnp.minimum(jnp.searchsorted(pad_end, sb_start, side='right'), N_EXPERTS - 1).astype(jnp.int32)
    valid = jnp.clip(pad_start[sb_e] + counts[sb_e] - sb_start, 0, EXPERT_ROWS)
    sb_nsub = ((valid + EXPERT_SUB - 1) // EXPERT_SUB).astype(jnp.int32)
    n_act = (pad_end[-1] // EXPERT_ROWS).astype(jnp.int32)
    sb_idx = jnp.minimum(jnp.arange(n_sb, dtype=jnp.int32), n_act - 1)
    sb_expert = sb_e[sb_idx]
    per = EXPERT_ROWS // EXPERT_SUB
    sub_active = (jnp.arange(n_sb * per, dtype=jnp.int32) % per < jnp.repeat(sb_nsub, per)).astype(jnp.int32)
    return pos, row_tok, sb_idx, sb_expert, sb_nsub, sub_active


def _main_tiles(d):
    w = HEADS * HEAD_DIM
    sc = d // 2
    sizes = (w, w, w, w, sc, sc, sc, d, d)
    acts = (1, None, 0, 1, 0, 0, 0, 2, 2)
    order = (0, 2, 3, 4, 5, 6, 7, 8)
    starts = np.concatenate([[0], np.cumsum(sizes)])
    cols, codes = [], []
    for seg in order:
        assert sizes[seg] % COL_TILE == 0 and starts[seg] % COL_TILE == 0
        for t in range(sizes[seg] // COL_TILE):
            cols.append(int(starts[seg]) // COL_TILE + t)
            codes.append(acts[seg])
    return cols, codes, int(starts[1]) // COL_TILE


def _layer(x2d, meta, bsz, seq, norm_mix_w, w_in, lb_logits, g_norm_w, w_hgrn_out, conv_w, w_conv_out, w_o,
           norm_ffn_w, w_router, b_router, w_up, b_up, w_down, b_down, final_norm_w):
    m, d = x2d.shape
    cols, codes, fcol = _main_tiles(d)
    consts = _hgrn_constants(CHUNK)

    xn_meta = _prenorm(meta, norm_mix_w, N_META)
    proj_meta = _inproj(xn_meta, w_in, cols, codes, N_META)
    lf_meta, kk_meta = _fgate(xn_meta, w_in, lb_logits, fcol, N_META)
    pad = CHUNK - N_META
    front = lambda a: jnp.pad(a, ((pad, 0), (0, 0)))
    s_zero = jnp.zeros((HEADS, HEAD_DIM, HEAD_DIM), F32)
    _, s_meta = _hgrn(front(proj_meta), front(lf_meta), front(kk_meta), g_norm_w, s_zero, consts, 1, CHUNK, CHUNK)

    xn = _prenorm(x2d, norm_mix_w, 512)
    proj = _inproj(xn, w_in, cols, codes, 1024)
    lf, kk = _fgate(xn, w_in, lb_logits, fcol, 1024)
    og, _ = _hgrn(proj, lf, kk, g_norm_w, s_meta[0], consts, bsz, seq, HGRN_ROWS)
    h1, xp, idx, gate = _mixer_out(og, proj, proj_meta, x2d, w_hgrn_out.astype(BF16), w_conv_out.astype(BF16),
                                   w_o.astype(BF16), conv_w, norm_ffn_w, w_router, b_router, seq, 256)

    pos, row_tok, sb_idx, sb_expert, sb_nsub, sub_active = _routing_tables(idx, m)
    x_rows = _gather_rows(xp, row_tok, sub_active)
    y_rows = _experts(x_rows, w_up, b_up, w_down, b_down, sb_idx, sb_expert, sb_nsub)
    return _combine(y_rows, pos, gate, h1, final_norm_w)


def kernel(x, meta_tokens, norm_mix_w, w_in, lb_logits, g_norm_w, w_hgrn_out, conv_w, w_conv_out, w_o, norm_ffn_w,
           w_router, b_router, w_up, b_up, w_down, b_down, final_norm_w):
    bsz, seq, d = x.shape
    assert norm_mix_w.shape[0] == 1, "single-layer block"
    out = _layer(x.reshape(bsz * seq, d), meta_tokens.astype(x.dtype), bsz, seq, norm_mix_w[0], w_in[0], lb_logits,
                 g_norm_w[0], w_hgrn_out[0], conv_w[0], w_conv_out[0], w_o[0], norm_ffn_w[0], w_router[0],
                 b_router[0], w_up[0], b_up[0], w_down[0], b_down[0], final_norm_w)
    return out.reshape(bsz, seq, d)
```

```python
import functools

import numpy as np
import jax
import jax.numpy as jnp
from jax import lax
from jax.experimental import pallas as pl
from jax.experimental.pallas import tpu as pltpu

F32 = jnp.float32
BF16 = jnp.bfloat16

N_META = 16
HEADS = 8
HEAD_DIM = 128
N_EXPERTS = 32
TOP_K = 4
SWIGLU_LIMIT = 7.0
SWIGLU_ALPHA = 1.702
EPS = 1e-6

CHUNK = 64
HGRN_ROWS = 512
COL_TILE = 1024
EXPERT_ROWS = 1024
EXPERT_SUB = 256
EXPERT_FF_TILE = 256
COMBINE_ROWS = 128

V7X_VMEM_LIMIT = 56 * 1024 * 1024


def _dot(a, b):
    return jnp.dot(a, b, preferred_element_type=F32)


def _dot_bt(a, b):
    return lax.dot_general(a, b, (((1,), (1,)), ((), ())), preferred_element_type=F32)


def _dot_at(a, b):
    return lax.dot_general(a, b, (((0,), (0,)), ((), ())), preferred_element_type=F32)


def _split3(x):
    hi = x.astype(BF16)
    r1 = x - hi.astype(F32)
    mid = r1.astype(BF16)
    lo = (r1 - mid.astype(F32)).astype(BF16)
    return hi, mid, lo


def _params(*sem):
    return pltpu.CompilerParams(dimension_semantics=sem, vmem_limit_bytes=V7X_VMEM_LIMIT)


def _prenorm_kernel(x_ref, w_ref, o_ref):
    x = x_ref[...]
    ms = jnp.mean(x * x, axis=-1, keepdims=True)
    o_ref[...] = (x * lax.rsqrt(ms + EPS) * w_ref[...]).astype(o_ref.dtype)


def _prenorm(x, w, tm):
    m, d = x.shape
    return pl.pallas_call(
        _prenorm_kernel,
        grid=(m // tm,),
        in_specs=[pl.BlockSpec((tm, d), lambda i: (i, 0)), pl.BlockSpec((1, d), lambda i: (0, 0))],
        out_specs=pl.BlockSpec((tm, d), lambda i: (i, 0)),
        out_shape=jax.ShapeDtypeStruct((m, d), BF16),
        compiler_params=_params("arbitrary"),
        name="prenorm",
    )(x, w.reshape(1, d))


def _inproj_kernel(col_ref, code_ref, x_ref, w_ref, o_ref, wb_ref):
    n = pl.program_id(0)

    @pl.when(pl.program_id(1) == 0)
    def _():
        wb_ref[...] = w_ref[...].astype(BF16)

    z = _dot(x_ref[...], wb_ref[...])
    code = code_ref[n]

    @pl.when(code == 0)
    def _():
        o_ref[...] = z.astype(o_ref.dtype)

    @pl.when(code == 1)
    def _():
        o_ref[...] = (z * jax.nn.sigmoid(z)).astype(o_ref.dtype)

    @pl.when(code == 2)
    def _():
        o_ref[...] = jax.nn.sigmoid(z).astype(o_ref.dtype)


def _inproj(xn, w_in, cols, codes, tm):
    m, d = xn.shape
    nt = len(cols)
    grid_spec = pltpu.PrefetchScalarGridSpec(
        num_scalar_prefetch=2,
        grid=(nt, m // tm),
        in_specs=[
            pl.BlockSpec((tm, d), lambda n, i, col, code: (i, 0)),
            pl.BlockSpec((d, COL_TILE), lambda n, i, col, code: (0, col[n])),
        ],
        out_specs=pl.BlockSpec((tm, COL_TILE), lambda n, i, col, code: (i, n)),
        scratch_shapes=[pltpu.VMEM((d, COL_TILE), BF16)],
    )
    return pl.pallas_call(
        _inproj_kernel,
        grid_spec=grid_spec,
        out_shape=jax.ShapeDtypeStruct((m, nt * COL_TILE), BF16),
        compiler_params=_params("arbitrary", "arbitrary"),
        name="inproj",
    )(jnp.asarray(cols, jnp.int32), jnp.asarray(codes, jnp.int32), xn, w_in)


def _fgate_kernel(x_ref, w_ref, lbl_ref, lf_ref, kk_ref, wb_ref):
    @pl.when(pl.program_id(0) == 0)
    def _():
        wb_ref[...] = w_ref[...].astype(BF16)

    z = _dot(x_ref[...], wb_ref[...])
    lbl = lbl_ref[...]
    e = jnp.exp(lbl - jnp.max(lbl, axis=0, keepdims=True))
    lb = e[0:1] / jnp.sum(e, axis=0, keepdims=True)
    f = lb + (1.0 - lb) * jax.nn.sigmoid(z)
    lf_ref[...] = jnp.log(f)
    kk_ref[...] = (1.0 - lb) * jax.nn.sigmoid(-z)


def _fgate(xn, w_in, lb_logits, col, tm):
    m, d = xn.shape
    r = lb_logits.shape[0]
    out = jax.ShapeDtypeStruct((m, COL_TILE), F32)
    return pl.pallas_call(
        _fgate_kernel,
        grid=(m // tm,),
        in_specs=[
            pl.BlockSpec((tm, d), lambda i: (i, 0)),
            pl.BlockSpec((d, COL_TILE), lambda i: (0, col)),
            pl.BlockSpec((r, COL_TILE), lambda i: (0, 0)),
        ],
        out_specs=[pl.BlockSpec((tm, COL_TILE), lambda i: (i, 0))] * 2,
        out_shape=[out, out],
        scratch_shapes=[pltpu.VMEM((d, COL_TILE), BF16)],
        compiler_params=_params("arbitrary"),
        name="fgate",
    )(xn, w_in, lb_logits)


def _hgrn_constants(c):
    nl = int(np.log2(c))
    assert (1 << nl) == c
    rr = np.arange(c)[:, None]
    uu = np.arange(c)[None, :]
    mats = [uu <= rr, uu > rr]
    masks, sels = [], []
    for lvl in range(nl):
        b = 1 << lvl
        start = (rr // (2 * b)) * (2 * b)
        mid = start + b - 1
        second = (rr - start) >= b
        mats.append(np.where(second, (uu > mid) & (uu <= rr), (uu > rr) & (uu <= mid)))
        masks.append(((rr // (2 * b)) == (uu // (2 * b))) & ((rr % (2 * b)) >= b) & ((uu % (2 * b)) < b))
        sels.append(np.broadcast_to(second, (c, HEAD_DIM)))
    masks.append(np.eye(c, dtype=bool))
    mall = jnp.asarray(np.concatenate(mats, 0).astype(np.float32), BF16)
    return mall, jnp.asarray(np.stack(masks).astype(np.float32)), jnp.asarray(np.stack(sels).astype(np.float32))


def _hgrn_kernel(q_ref, v_ref, g_ref, lf_ref, kk_ref, gw_ref, s0_ref, mall_ref, mask_ref, sel_ref,
                 o_ref, sfin_ref, st_ref, *, chunk, n_chunks):
    nl = mask_ref.shape[0] - 1
    step = pl.program_id(1)

    @pl.when(step == 0)
    def _():
        st_ref[...] = s0_ref[...]

    def chunk_body(ci, carry):
        rows = pl.ds(pl.multiple_of(ci * chunk, chunk), chunk)
        mall = mall_ref[...]
        for h in range(HEADS):
            cs = slice(h * HEAD_DIM, (h + 1) * HEAD_DIM)
            hi, mid, lo = _split3(lf_ref[rows, cs])
            e_all = jnp.exp(_dot(mall, hi) + _dot(mall, mid) + _dot(mall, lo))
            qb = q_ref[rows, cs]
            q = qb.astype(F32)
            k = kk_ref[rows, cs]
            v = v_ref[rows, cs]
            scores = mask_ref[nl] * _dot_bt(qb, k.astype(BF16))
            for lvl in range(nl):
                e_l = e_all[(2 + lvl) * chunk:(3 + lvl) * chunk]
                x = (jnp.where(sel_ref[lvl] > 0.5, q, k) * e_l).astype(BF16)
                scores = scores + mask_ref[lvl] * _dot_bt(x, x)
            st = st_ref[h]
            q_in = (q * e_all[0:chunk]).astype(BF16)
            o = _dot(scores.astype(BF16), v) + _dot_bt(q_in, st.astype(BF16))
            k_out = (k * e_all[chunk:2 * chunk]).astype(BF16)
            st_ref[h] = st * e_all[chunk - 1:chunk] + _dot_at(v, k_out)
            ms = jnp.mean(o * o, axis=-1, keepdims=True)
            on = o * lax.rsqrt(ms + EPS) * gw_ref[...]
            o_ref[rows, cs] = (on * g_ref[rows, cs].astype(F32)).astype(o_ref.dtype)
        return carry

    lax.fori_loop(0, n_chunks, chunk_body, 0)

    @pl.when(step == pl.num_programs(1) - 1)
    def _():
        sfin_ref[0] = st_ref[...]


def _hgrn(proj, lf, kk, g_norm_w, s0, consts, bsz, seq, rows):
    mall, masks, sels = consts
    steps = seq // rows
    w = HEADS * HEAD_DIM
    assert w == COL_TILE
    row_map = lambda col: (lambda b, s: (b * steps + s, col))
    const2 = lambda b, s: (0, 0)
    const3 = lambda b, s: (0, 0, 0)
    kern = functools.partial(_hgrn_kernel, chunk=CHUNK, n_chunks=rows // CHUNK)
    return pl.pallas_call(
        kern,
        grid=(bsz, steps),
        in_specs=[
            pl.BlockSpec((rows, w), row_map(0)),
            pl.BlockSpec((rows, w), row_map(1)),
            pl.BlockSpec((rows, w), row_map(2)),
            pl.BlockSpec((rows, w), row_map(0)),
            pl.BlockSpec((rows, w), row_map(0)),
            pl.BlockSpec((1, HEAD_DIM), const2),
            pl.BlockSpec((HEADS, HEAD_DIM, HEAD_DIM), const3),
            pl.BlockSpec(mall.shape, const2),
            pl.BlockSpec(masks.shape, const3),
            pl.BlockSpec(sels.shape, const3),
        ],
        out_specs=[
            pl.BlockSpec((rows, w), row_map(0)),
            pl.BlockSpec((1, HEADS, HEAD_DIM, HEAD_DIM), lambda b, s: (b, 0, 0, 0)),
        ],
        out_shape=[
            jax.ShapeDtypeStruct((bsz * seq, w), BF16),
            jax.ShapeDtypeStruct((bsz, HEADS, HEAD_DIM, HEAD_DIM), F32),
        ],
        scratch_shapes=[pltpu.VMEM((HEADS, HEAD_DIM, HEAD_DIM), F32)],
        compiler_params=_params("arbitrary", "arbitrary"),
        name="hgrn2",
    )(proj, proj, proj, lf, kk, g_norm_w.reshape(1, HEAD_DIM), s0, mall, masks, sels)


def _mixer_out_kernel(og_ref, scv_ref, scb_ref, scc_ref, ga_ref, gb_ref, x_ref,
                      pv_ref, pc_ref, mv_ref, mc_ref,
                      wa_ref, wb_ref, wo_ref, cw_ref, nw_ref, wr_ref, br_ref,
                      h1_ref, xn_ref, idx_ref, gate_ref, *, tiles_per_seq):
    i = pl.program_id(0)
    tm = x_ref.shape[0]
    first = (i % tiles_per_seq) == 0

    u = scc_ref[...].astype(F32) * scv_ref[...].astype(F32)
    halo_prev = pc_ref[...].astype(F32) * pv_ref[...].astype(F32)
    halo_meta = mc_ref[...].astype(F32) * mv_ref[...].astype(F32)
    halo = jnp.where(first, halo_meta, halo_prev)
    hr = halo.shape[0]
    r = lax.broadcasted_iota(jnp.int32, (tm, 1), 0)
    u1 = jnp.where(r == 0, halo[hr - 1:hr], pltpu.roll(u, 1, 0))
    u2 = jnp.where(r == 0, halo[hr - 2:hr - 1], jnp.where(r == 1, halo[hr - 1:hr], pltpu.roll(u, 2, 0)))
    conv = cw_ref[2:3] * u + cw_ref[1:2] * u1 + cw_ref[0:1] * u2
    yb_in = (scb_ref[...].astype(F32) * conv).astype(BF16)

    y_a = _dot(og_ref[...], wa_ref[...])
    y_b = _dot(yb_in, wb_ref[...])
    merged = (ga_ref[...].astype(F32) * y_a + gb_ref[...].astype(F32) * y_b).astype(BF16)
    h1 = x_ref[...] + _dot(merged, wo_ref[...])
    h1_ref[...] = h1

    ms = jnp.mean(h1 * h1, axis=-1, keepdims=True)
    xn = h1 * lax.rsqrt(ms + EPS) * nw_ref[...]
    xn_ref[...] = xn

    xh = xn.astype(BF16)
    xl = (xn - xh.astype(F32)).astype(BF16)
    wr = wr_ref[...]
    wh = wr.astype(BF16)
    wl = (wr - wh.astype(F32)).astype(BF16)
    logits = _dot_bt(wh, xh) + _dot_bt(wh, xl) + _dot_bt(wl, xh) + br_ref[...]
    ne = logits.shape[0]
    ie = lax.broadcasted_iota(jnp.int32, logits.shape, 0)
    tops, idxs = [], []
    for _ in range(TOP_K):
        mx = jnp.max(logits, axis=0, keepdims=True)
        ix = jnp.min(jnp.where(logits == mx, ie, ne), axis=0, keepdims=True)
        tops.append(mx)
        idxs.append(ix)
        logits = jnp.where(ie == ix, -jnp.inf, logits)
    es = [jnp.exp(t - tops[0]) for t in tops]
    den = es[0]
    for e in es[1:]:
        den = den + e
    gate_ref[...] = jnp.concatenate([e / den for e in es], axis=0)
    idx_ref[...] = jnp.concatenate(idxs, axis=0)


def _mixer_out(og, proj, proj_meta, x2d, wa, wb, wo, conv_w, norm_w, w_router, b_router, seq, tm):
    m, d = x2d.shape
    w = COL_TILE
    halo = proj_meta.shape[0]
    assert tm % halo == 0 and seq % tm == 0 and d == 2 * w
    ne = w_router.shape[1]
    per_halo = tm // halo
    row = lambda col: (lambda i: (i, col))
    prev = lambda col: (lambda i: (jnp.maximum(i * per_halo - 1, 0), col))
    const = lambda i: (0, 0)
    whole = lambda a: pl.BlockSpec(a.shape, const)
    wr_t = w_router.T
    kern = functools.partial(_mixer_out_kernel, tiles_per_seq=seq // tm)
    return pl.pallas_call(
        kern,
        grid=(m // tm,),
        in_specs=[
            pl.BlockSpec((tm, w), row(0)),
            pl.BlockSpec((tm, w), row(3)),
            pl.BlockSpec((tm, w), row(4)),
            pl.BlockSpec((tm, w), row(5)),
            pl.BlockSpec((tm, d), row(3)),
            pl.BlockSpec((tm, d), row(4)),
            pl.BlockSpec((tm, d), row(0)),
            pl.BlockSpec((halo, w), prev(3)),
            pl.BlockSpec((halo, w), prev(5)),
            pl.BlockSpec((halo, w), lambda i: (0, 3)),
            pl.BlockSpec((halo, w), lambda i: (0, 5)),
            whole(wa), whole(wb), whole(wo),
            pl.BlockSpec(conv_w.shape, const),
            pl.BlockSpec((1, d), const),
            pl.BlockSpec((ne, d), const),
            pl.BlockSpec((ne, 1), const),
        ],
        out_specs=[
            pl.BlockSpec((tm, d), row(0)),
            pl.BlockSpec((tm, d), row(0)),
            pl.BlockSpec((TOP_K, tm), lambda i: (0, i)),
            pl.BlockSpec((TOP_K, tm), lambda i: (0, i)),
        ],
        out_shape=[
            jax.ShapeDtypeStruct((m, d), F32),
            jax.ShapeDtypeStruct((m, d), F32),
            jax.ShapeDtypeStruct((TOP_K, m), jnp.int32),
            jax.ShapeDtypeStruct((TOP_K, m), F32),
        ],
        compiler_params=_params("arbitrary"),
        name="mixer_out",
    )(og, proj, proj, proj, proj, proj, x2d, proj, proj, proj_meta, proj_meta,
      wa, wb, wo, conv_w, norm_w.reshape(1, d), wr_t, b_router.reshape(ne, 1))


def _gather_kernel(act_ref, tok_ref, tok_next_ref, x_hbm, o_ref, buf_ref, sem):
    i = pl.program_id(0)
    n = o_ref.shape[0]

    def start_rows(tok, slot):
        def issue(r, carry):
            t = tok[0, 0, r]
            pltpu.make_async_copy(x_hbm.at[pl.ds(t, 1)], buf_ref.at[slot, pl.ds(r, 1)], sem.at[slot]).start()
            return carry

        lax.fori_loop(0, n, issue, 0, unroll=8)

    @pl.when(jnp.logical_and(i == 0, act_ref[0] > 0))
    def _():
        start_rows(tok_ref, 0)

    nxt = jnp.minimum(i + 1, pl.num_programs(0) - 1)

    @pl.when(jnp.logical_and(i + 1 < pl.num_programs(0), act_ref[nxt] > 0))
    def _():
        start_rows(tok_next_ref, (i + 1) % 2)

    @pl.when(act_ref[i] > 0)
    def _():
        slot = i % 2
        pltpu.make_async_copy(x_hbm.at[pl.ds(0, n)], buf_ref.at[slot], sem.at[slot]).wait()
        o_ref[...] = buf_ref[slot].astype(o_ref.dtype)

    @pl.when(act_ref[i] == 0)
    def _():
        o_ref[...] = jnp.zeros_like(o_ref)


def _gather_rows(xn, row_tok, sub_active):
    n_rows = row_tok.shape[0]
    width = xn.shape[1]
    n_sub = n_rows // EXPERT_SUB
    tok3 = row_tok.reshape(n_sub, 1, EXPERT_SUB)
    grid_spec = pltpu.PrefetchScalarGridSpec(
        num_scalar_prefetch=1,
        grid=(n_sub,),
        in_specs=[
            pl.BlockSpec((1, 1, EXPERT_SUB), lambda i, act: (i, 0, 0), memory_space=pltpu.SMEM),
            pl.BlockSpec((1, 1, EXPERT_SUB), lambda i, act: (jnp.minimum(i + 1, n_sub - 1), 0, 0),
                         memory_space=pltpu.SMEM),
            pl.BlockSpec(memory_space=pl.ANY),
        ],
        out_specs=pl.BlockSpec((EXPERT_SUB, width), lambda i, act: (i, 0)),
        scratch_shapes=[pltpu.VMEM((2, EXPERT_SUB, width), xn.dtype), pltpu.SemaphoreType.DMA((2,))],
    )
    return pl.pallas_call(
        _gather_kernel,
        grid_spec=grid_spec,
        out_shape=jax.ShapeDtypeStruct((n_rows, width), BF16),
        compiler_params=_params("arbitrary"),
        name="gather_rows",
    )(sub_active, tok3, tok3, xn)


def _expert_kernel(sbi_ref, sbe_ref, nsub_ref, jup_ref, jdn_ref, x_ref, wg_ref, wu_ref, wd_ref, bg_ref, bu_ref,
                   bd_ref, o_ref, h_ref, wgb_ref, wub_ref, wdb_ref):
    s = pl.program_id(0)
    t = pl.program_id(1)
    nsub = nsub_ref[s]
    nj = h_ref.shape[0]
    tn = wdb_ref.shape[1]
    n_sub_blocks = o_ref.shape[0] // EXPERT_SUB

    @pl.when(nsub > 0)
    def _():
        @pl.when(t < nj)
        def _():
            wgb_ref[...] = wg_ref[0].astype(BF16)
            wub_ref[...] = wu_ref[0].astype(BF16)
            for n in range(1, n_sub_blocks + 1):
                rows = n * EXPERT_SUB

                @pl.when(nsub == n)
                def _():
                    xs = x_ref[0:rows, :]
                    g = _dot(xs, wgb_ref[...]) + bg_ref[0]
                    u = _dot(xs, wub_ref[...]) + bu_ref[0]
                    g = jnp.minimum(g, SWIGLU_LIMIT)
                    u = jnp.clip(u, -SWIGLU_LIMIT, SWIGLU_LIMIT)
                    h_ref[t, 0:rows, :] = ((u + 1.0) * (g * jax.nn.sigmoid(SWIGLU_ALPHA * g))).astype(BF16)

        @pl.when(t >= nj)
        def _():
            wdb_ref[...] = wd_ref[0].astype(BF16)
            for n in range(1, n_sub_blocks + 1):
                rows = n * EXPERT_SUB

                @pl.when(nsub == n)
                def _():
                    hid = jnp.concatenate([h_ref[j, 0:rows, :] for j in range(nj)], axis=1)
                    y = _dot(hid, wdb_ref[...]) + bd_ref[0]
                    for c in range(o_ref.shape[1] // tn):
                        @pl.when(t - nj == c)
                        def _():
                            o_ref[0:rows, c * tn:(c + 1) * tn] = y

    for b in range(n_sub_blocks):
        @pl.when(jnp.logical_and(b >= nsub, t == 0))
        def _():
            o_ref[b * EXPERT_SUB:(b + 1) * EXPERT_SUB, :] = jnp.zeros((EXPERT_SUB, o_ref.shape[1]), o_ref.dtype)


def _experts(x_rows, w_up, b_up, w_down, b_down, sb_idx, sb_expert, sb_nsub):
    n_rows, d = x_rows.shape
    ne, _, ff2 = w_up.shape
    ff = ff2 // 2
    tf = EXPERT_FF_TILE
    tn = EXPERT_FF_TILE
    nj = ff // tf
    nc = d // tn
    n_sb = n_rows // EXPERT_ROWS
    steps = jnp.arange(nj + nc, dtype=jnp.int32)[None, :]
    active = sb_nsub[:, None] > 0
    jup = jnp.where(active, jnp.minimum(steps, nj - 1), nj - 1).astype(jnp.int32)
    jdn = jnp.where(active, jnp.maximum(steps - nj, 0), nc - 1).astype(jnp.int32)
    grid_spec = pltpu.PrefetchScalarGridSpec(
        num_scalar_prefetch=5,
        grid=(n_sb, nj + nc),
        in_specs=[
            pl.BlockSpec((EXPERT_ROWS, d), lambda s, t, sbi, sbe, ns, ju, jd: (sbi[s], 0)),
            pl.BlockSpec((1, d, tf), lambda s, t, sbi, sbe, ns, ju, jd: (sbe[s], 0, ju[s, t])),
            pl.BlockSpec((1, d, tf), lambda s, t, sbi, sbe, ns, ju, jd: (sbe[s], 0, nj + ju[s, t])),
            pl.BlockSpec((1, ff, tn), lambda s, t, sbi, sbe, ns, ju, jd: (sbe[s], 0, jd[s, t])),
            pl.BlockSpec((1, 1, tf), lambda s, t, sbi, sbe, ns, ju, jd: (sbe[s], 0, ju[s, t])),
            pl.BlockSpec((1, 1, tf), lambda s, t, sbi, sbe, ns, ju, jd: (sbe[s], 0, nj + ju[s, t])),
            pl.BlockSpec((1, 1, tn), lambda s, t, sbi, sbe, ns, ju, jd: (sbe[s], 0, jd[s, t])),
        ],
        out_specs=pl.BlockSpec((EXPERT_ROWS, d), lambda s, t, sbi, sbe, ns, ju, jd: (s, 0)),
        scratch_shapes=[
            pltpu.VMEM((nj, EXPERT_ROWS, tf), BF16),
            pltpu.VMEM((d, tf), BF16),
            pltpu.VMEM((d, tf), BF16),
            pltpu.VMEM((ff, tn), BF16),
        ],
    )
    return pl.pallas_call(
        _expert_kernel,
        grid_spec=grid_spec,
        out_shape=jax.ShapeDtypeStruct((n_rows, d), F32),
        compiler_params=_params("arbitrary", "arbitrary"),
        name="experts",
    )(sb_idx, sb_expert, sb_nsub, jup, jdn, x_rows, w_up, w_up, w_down,
      b_up.reshape(ne, 1, ff2), b_up.reshape(ne, 1, ff2), b_down.reshape(ne, 1, d))


def _combine_kernel(pos_ref, pos_next_ref, gate_ref, h1_ref, fw_ref, y_hbm, o_ref, buf_ref, sem):
    i = pl.program_id(0)
    tt = h1_ref.shape[0]

    def start_rows(pos, slot):
        def issue(r, carry):
            for k in range(TOP_K):
                p = pos[0, 0, k * tt + r]
                pltpu.make_async_copy(y_hbm.at[pl.ds(p, 1)], buf_ref.at[slot, k, pl.ds(r, 1)], sem.at[slot]).start()
            return carry

        lax.fori_loop(0, tt, issue, 0, unroll=4)

    @pl.when(i == 0)
    def _():
        start_rows(pos_ref, 0)

    @pl.when(i + 1 < pl.num_programs(0))
    def _():
        start_rows(pos_next_ref, (i + 1) % 2)

    slot = i % 2
    for k in range(TOP_K):
        pltpu.make_async_copy(y_hbm.at[pl.ds(0, tt)], buf_ref.at[slot, k], sem.at[slot]).wait()

    gate = gate_ref[...]
    gpad = jnp.concatenate([gate, jnp.zeros((tt - TOP_K, tt), F32)], axis=0)
    gcol = gpad.T
    acc = h1_ref[...]
    for k in range(TOP_K):
        acc = acc + gcol[:, k:k + 1] * buf_ref[slot, k]
    ms = jnp.mean(acc * acc, axis=-1, keepdims=True)
    o_ref[...] = acc * lax.rsqrt(ms + EPS) * fw_ref[...]


def _combine(y_rows, pos, gate, h1, final_w):
    m, d = h1.shape
    tt = COMBINE_ROWS
    nt = m // tt
    pos_t = pos.reshape(TOP_K, nt, tt).transpose(1, 0, 2).reshape(nt, 1, TOP_K * tt)
    return pl.pallas_call(
        _combine_kernel,
        grid=(nt,),
        in_specs=[
            pl.BlockSpec((1, 1, TOP_K * tt), lambda i: (i, 0, 0), memory_space=pltpu.SMEM),
            pl.BlockSpec((1, 1, TOP_K * tt), lambda i: (jnp.minimum(i + 1, nt - 1), 0, 0), memory_space=pltpu.SMEM),
            pl.BlockSpec((TOP_K, tt), lambda i: (0, i)),
            pl.BlockSpec((tt, d), lambda i: (i, 0)),
            pl.BlockSpec((1, d), lambda i: (0, 0)),
            pl.BlockSpec(memory_space=pl.ANY),
        ],
        out_specs=pl.BlockSpec((tt, d), lambda i: (i, 0)),
        out_shape=jax.ShapeDtypeStruct((m, d), F32),
        scratch_shapes=[pltpu.VMEM((2, TOP_K, tt, d), F32), pltpu.SemaphoreType.DMA((2,))],
        compiler_params=_params("arbitrary"),
        name="combine",
    )(pos_t, pos_t, gate, h1, final_w.reshape(1, d), y_rows)


def _routing_tables(idx, m):
    n_assign = TOP_K * m
    flat_e = idx.reshape(n_assign)
    onehot = flat_e[:, None] == jnp.arange(N_EXPERTS, dtype=jnp.int32)[None, :]
    csum = jnp.cumsum(onehot.astype(jnp.int32), axis=0)
    rank = jnp.sum(jnp.where(onehot, csum - 1, 0), axis=1)
    counts = csum[-1]
    padded = (counts + EXPERT_ROWS - 1) // EXPERT_ROWS * EXPERT_ROWS
    pad_end = jnp.cumsum(padded)
    pad_start = pad_end - padded
    pos = (jnp.sum(jnp.where(onehot, pad_start[None, :], 0), axis=1) + rank).astype(jnp.int32)
    n_sb = -(-(n_assign + N_EXPERTS * (EXPERT_ROWS - 1)) // EXPERT_ROWS)
    n_rows = n_sb * EXPERT_ROWS
    tok = (jnp.arange(n_assign, dtype=jnp.int32) % m).astype(jnp.int32)
    row_tok = jnp.zeros((n_rows,), jnp.int32).at[pos].set(tok)
    sb_start = jnp.arange(n_sb, dtype=jnp.int32) * EXPERT_ROWS
    sb_e = jnp.minimum(jnp.sum((pad_end[None, :] <= sb_start[:, None]).astype(jnp.int32), axis=1), N_EXPERTS - 1)
    valid = jnp.clip(pad_start[sb_e] + counts[sb_e] - sb_start, 0, EXPERT_ROWS)
    sb_nsub = ((valid + EXPERT_SUB - 1) // EXPERT_SUB).astype(jnp.int32)
    n_act = (pad_end[-1] // EXPERT_ROWS).astype(jnp.int32)
    sb_idx = jnp.minimum(jnp.arange(n_sb, dtype=jnp.int32), n_act - 1)
    sb_expert = sb_e[sb_idx]
    per = EXPERT_ROWS // EXPERT_SUB
    sub_active = (jnp.arange(n_sb * per, dtype=jnp.int32) % per < jnp.repeat(sb_nsub, per)).astype(jnp.int32)
    return pos, row_tok, sb_idx, sb_expert, sb_nsub, sub_active


def _main_tiles(d):
    w = HEADS * HEAD_DIM
    sc = d // 2
    sizes = (w, w, w, w, sc, sc, sc, d, d)
    acts = (1, None, 0, 1, 0, 0, 0, 2, 2)
    order = (0, 2, 3, 4, 5, 6, 7, 8)
    starts = np.concatenate([[0], np.cumsum(sizes)])
    cols, codes = [], []
    for seg in order:
        assert sizes[seg] % COL_TILE == 0 and starts[seg] % COL_TILE == 0
        for t in range(sizes[seg] // COL_TILE):
            cols.append(int(starts[seg]) // COL_TILE + t)
            codes.append(acts[seg])
    return cols, codes, int(starts[1]) // COL_TILE


def _layer(x2d, meta, bsz, seq, norm_mix_w, w_in, lb_logits, g_norm_w, w_hgrn_out, conv_w, w_conv_out, w_o,
           norm_ffn_w, w_router, b_router, w_up, b_up, w_down, b_down, final_norm_w):
    m, d = x2d.shape
    cols, codes, fcol = _main_tiles(d)
    consts = _hgrn_constants(CHUNK)

    xn_meta = _prenorm(meta, norm_mix_w, N_META)
    proj_meta = _inproj(xn_meta, w_in, cols, codes, N_META)
    lf_meta, kk_meta = _fgate(xn_meta, w_in, lb_logits, fcol, N_META)
    pad = CHUNK - N_META
    front = lambda a: jnp.pad(a, ((pad, 0), (0, 0)))
    s_zero = jnp.zeros((HEADS, HEAD_DIM, HEAD_DIM), F32)
    _, s_meta = _hgrn(front(proj_meta), front(lf_meta), front(kk_meta), g_norm_w, s_zero, consts, 1, CHUNK, CHUNK)

    xn = _prenorm(x2d, norm_mix_w, 512)
    proj = _inproj(xn, w_in, cols, codes, 1024)
    lf, kk = _fgate(xn, w_in, lb_logits, fcol, 1024)
    og, _ = _hgrn(proj, lf, kk, g_norm_w, s_meta[0], consts, bsz, seq, HGRN_ROWS)
    h1, xn_ffn, idx, gate = _mixer_out(og, proj, proj_meta, x2d, w_hgrn_out.astype(BF16), w_conv_out.astype(BF16),
                                   w_o.astype(BF16), conv_w, norm_ffn_w, w_router, b_router, seq, 256)

    pos, row_tok, sb_idx, sb_expert, sb_nsub, sub_active = _routing_tables(idx, m)
    x_rows = _gather_rows(xn_ffn, row_tok, sub_active)
    y_rows = _experts(x_rows, w_up, b_up, w_down, b_down, sb_idx, sb_expert, sb_nsub)
    return _combine(y_rows, pos, gate, h1, final_norm_w)


def kernel(x, meta_tokens, norm_mix_w, w_in, lb_logits, g_norm_w, w_hgrn_out, conv_w, w_conv_out, w_o, norm_ffn_w,
           w_router, b_router, w_up, b_up, w_down, b_down, final_norm_w):
    bsz, seq, d = x.shape
    assert norm_mix_w.shape[0] == 1, "single-layer block"
    out = _layer(x.reshape(bsz * seq, d), meta_tokens.astype(x.dtype), bsz, seq, norm_mix_w[0], w_in[0], lb_logits,
                 g_norm_w[0], w_hgrn_out[0], conv_w[0], w_conv_out[0], w_o[0], norm_ffn_w[0], w_router[0],
                 b_router[0], w_up[0], b_up[0], w_down[0], b_down[0], final_norm_w)
    return out.reshape(bsz, seq, d)
```

```python
import functools

import numpy as np
import jax
import jax.numpy as jnp
from jax import lax
from jax.experimental import pallas as pl
from jax.experimental.pallas import tpu as pltpu

F32 = jnp.float32
BF16 = jnp.bfloat16

N_META = 16
HEADS = 8
HEAD_DIM = 128
N_EXPERTS = 32
TOP_K = 4
SWIGLU_LIMIT = 7.0
SWIGLU_ALPHA = 1.702
EPS = 1e-6

CHUNK = 64
HGRN_ROWS = 512
COL_TILE = 1024
EXPERT_SUB = 256
EXPERT_SUBS = 9
EXPERT_FF_TILE = 512
EXPERT_OUT_TILE = 512
COMBINE_ROWS = 128

V7X_VMEM_LIMIT = 56 * 1024 * 1024


def _dot(a, b):
    return jnp.dot(a, b, preferred_element_type=F32)


def _dot_bt(a, b):
    return lax.dot_general(a, b, (((1,), (1,)), ((), ())), preferred_element_type=F32)


def _dot_at(a, b):
    return lax.dot_general(a, b, (((0,), (0,)), ((), ())), preferred_element_type=F32)


def _split3(x):
    hi = x.astype(BF16)
    r1 = x - hi.astype(F32)
    mid = r1.astype(BF16)
    lo = (r1 - mid.astype(F32)).astype(BF16)
    return hi, mid, lo


def _params(*sem):
    return pltpu.CompilerParams(dimension_semantics=sem, vmem_limit_bytes=V7X_VMEM_LIMIT)


def _prenorm_kernel(x_ref, w_ref, o_ref):
    x = x_ref[...]
    ms = jnp.mean(x * x, axis=-1, keepdims=True)
    o_ref[...] = (x * lax.rsqrt(ms + EPS) * w_ref[...]).astype(o_ref.dtype)


def _prenorm(x, w, tm):
    m, d = x.shape
    return pl.pallas_call(
        _prenorm_kernel,
        grid=(m // tm,),
        in_specs=[pl.BlockSpec((tm, d), lambda i: (i, 0)), pl.BlockSpec((1, d), lambda i: (0, 0))],
        out_specs=pl.BlockSpec((tm, d), lambda i: (i, 0)),
        out_shape=jax.ShapeDtypeStruct((m, d), BF16),
        compiler_params=_params("arbitrary"),
        name="prenorm",
    )(x, w.reshape(1, d))


def _inproj_kernel(col_ref, code_ref, x_ref, w_ref, o_ref, wb_ref):
    n = pl.program_id(0)

    @pl.when(pl.program_id(1) == 0)
    def _():
        wb_ref[...] = w_ref[...].astype(BF16)

    z = _dot(x_ref[...], wb_ref[...])
    code = code_ref[n]

    @pl.when(code == 0)
    def _():
        o_ref[...] = z.astype(o_ref.dtype)

    @pl.when(code == 1)
    def _():
        o_ref[...] = (z * jax.nn.sigmoid(z)).astype(o_ref.dtype)

    @pl.when(code == 2)
    def _():
        o_ref[...] = jax.nn.sigmoid(z).astype(o_ref.dtype)


def _inproj(xn, w_in, cols, codes, tm):
    m, d = xn.shape
    nt = len(cols)
    grid_spec = pltpu.PrefetchScalarGridSpec(
        num_scalar_prefetch=2,
        grid=(nt, m // tm),
        in_specs=[
            pl.BlockSpec((tm, d), lambda n, i, col, code: (i, 0)),
            pl.BlockSpec((d, COL_TILE), lambda n, i, col, code: (0, col[n])),
        ],
        out_specs=pl.BlockSpec((tm, COL_TILE), lambda n, i, col, code: (i, n)),
        scratch_shapes=[pltpu.VMEM((d, COL_TILE), BF16)],
    )
    return pl.pallas_call(
        _inproj_kernel,
        grid_spec=grid_spec,
        out_shape=jax.ShapeDtypeStruct((m, nt * COL_TILE), BF16),
        compiler_params=_params("arbitrary", "arbitrary"),
        name="inproj",
    )(jnp.asarray(cols, jnp.int32), jnp.asarray(codes, jnp.int32), xn, w_in)


def _fgate_kernel(x_ref, w_ref, lbl_ref, lf_ref, kk_ref, wb_ref):
    @pl.when(pl.program_id(0) == 0)
    def _():
        wb_ref[...] = w_ref[...].astype(BF16)

    z = _dot(x_ref[...], wb_ref[...])
    lbl = lbl_ref[...]
    e = jnp.exp(lbl - jnp.max(lbl, axis=0, keepdims=True))
    lb = e[0:1] / jnp.sum(e, axis=0, keepdims=True)
    f = lb + (1.0 - lb) * jax.nn.sigmoid(z)
    lf_ref[...] = jnp.log(f)
    kk_ref[...] = (1.0 - lb) * jax.nn.sigmoid(-z)


def _fgate(xn, w_in, lb_logits, col, tm):
    m, d = xn.shape
    r = lb_logits.shape[0]
    out = jax.ShapeDtypeStruct((m, COL_TILE), F32)
    return pl.pallas_call(
        _fgate_kernel,
        grid=(m // tm,),
        in_specs=[
            pl.BlockSpec((tm, d), lambda i: (i, 0)),
            pl.BlockSpec((d, COL_TILE), lambda i: (0, col)),
            pl.BlockSpec((r, COL_TILE), lambda i: (0, 0)),
        ],
        out_specs=[pl.BlockSpec((tm, COL_TILE), lambda i: (i, 0))] * 2,
        out_shape=[out, out],
        scratch_shapes=[pltpu.VMEM((d, COL_TILE), BF16)],
        compiler_params=_params("arbitrary"),
        name="fgate",
    )(xn, w_in, lb_logits)


def _hgrn_constants(c):
    nl = int(np.log2(c))
    assert (1 << nl) == c
    rr = np.arange(c)[:, None]
    uu = np.arange(c)[None, :]
    mats = [uu <= rr, uu > rr]
    masks, sels = [], []
    for lvl in range(nl):
        b = 1 << lvl
        start = (rr // (2 * b)) * (2 * b)
        mid = start + b - 1
        second = (rr - start) >= b
        mats.append(np.where(second, (uu > mid) & (uu <= rr), (uu > rr) & (uu <= mid)))
        masks.append(((rr // (2 * b)) == (uu // (2 * b))) & ((rr % (2 * b)) >= b) & ((uu % (2 * b)) < b))
        sels.append(np.broadcast_to(second, (c, HEAD_DIM)))
    masks.append(np.eye(c, dtype=bool))
    mall = jnp.asarray(np.concatenate(mats, 0).astype(np.float32), BF16)
    return mall, jnp.asarray(np.stack(masks).astype(np.float32)), jnp.asarray(np.stack(sels).astype(np.float32))


def _hgrn_kernel(q_ref, v_ref, g_ref, lf_ref, kk_ref, gw_ref, s0_ref, mall_ref, mask_ref, sel_ref,
                 o_ref, sfin_ref, st_ref, *, chunk, n_chunks):
    nl = mask_ref.shape[0] - 1
    step = pl.program_id(1)

    @pl.when(step == 0)
    def _():
        st_ref[...] = s0_ref[...]

    def chunk_body(ci, carry):
        rows = pl.ds(pl.multiple_of(ci * chunk, chunk), chunk)
        mall = mall_ref[...]
        for h in range(HEADS):
            cs = slice(h * HEAD_DIM, (h + 1) * HEAD_DIM)
            hi, mid, lo = _split3(lf_ref[rows, cs])
            e_all = jnp.exp(_dot(mall, hi) + _dot(mall, mid) + _dot(mall, lo))
            qb = q_ref[rows, cs]
            q = qb.astype(F32)
            k = kk_ref[rows, cs]
            v = v_ref[rows, cs]
            scores = mask_ref[nl] * _dot_bt(qb, k.astype(BF16))
            for lvl in range(nl):
                e_l = e_all[(2 + lvl) * chunk:(3 + lvl) * chunk]
                x = (jnp.where(sel_ref[lvl] > 0.5, q, k) * e_l).astype(BF16)
                scores = scores + mask_ref[lvl] * _dot_bt(x, x)
            st = st_ref[h]
            q_in = (q * e_all[0:chunk]).astype(BF16)
            o = _dot(scores.astype(BF16), v) + _dot_bt(q_in, st.astype(BF16))
            k_out = (k * e_all[chunk:2 * chunk]).astype(BF16)
            st_ref[h] = st * e_all[chunk - 1:chunk] + _dot_at(v, k_out)
            ms = jnp.mean(o * o, axis=-1, keepdims=True)
            on = o * lax.rsqrt(ms + EPS) * gw_ref[...]
            o_ref[rows, cs] = (on * g_ref[rows, cs].astype(F32)).astype(o_ref.dtype)
        return carry

    lax.fori_loop(0, n_chunks, chunk_body, 0)

    @pl.when(step == pl.num_programs(1) - 1)
    def _():
        sfin_ref[0] = st_ref[...]


def _hgrn(proj, lf, kk, g_norm_w, s0, consts, bsz, seq, rows):
    mall, masks, sels = consts
    steps = seq // rows
    w = HEADS * HEAD_DIM
    assert w == COL_TILE
    row_map = lambda col: (lambda b, s: (b * steps + s, col))
    const2 = lambda b, s: (0, 0)
    const3 = lambda b, s: (0, 0, 0)
    kern = functools.partial(_hgrn_kernel, chunk=CHUNK, n_chunks=rows // CHUNK)
    return pl.pallas_call(
        kern,
        grid=(bsz, steps),
        in_specs=[
            pl.BlockSpec((rows, w), row_map(0)),
            pl.BlockSpec((rows, w), row_map(1)),
            pl.BlockSpec((rows, w), row_map(2)),
            pl.BlockSpec((rows, w), row_map(0)),
            pl.BlockSpec((rows, w), row_map(0)),
            pl.BlockSpec((1, HEAD_DIM), const2),
            pl.BlockSpec((HEADS, HEAD_DIM, HEAD_DIM), const3),
            pl.BlockSpec(mall.shape, const2),
            pl.BlockSpec(masks.shape, const3),
            pl.BlockSpec(sels.shape, const3),
        ],
        out_specs=[
            pl.BlockSpec((rows, w), row_map(0)),
            pl.BlockSpec((1, HEADS, HEAD_DIM, HEAD_DIM), lambda b, s: (b, 0, 0, 0)),
        ],
        out_shape=[
            jax.ShapeDtypeStruct((bsz * seq, w), BF16),
            jax.ShapeDtypeStruct((bsz, HEADS, HEAD_DIM, HEAD_DIM), F32),
        ],
        scratch_shapes=[pltpu.VMEM((HEADS, HEAD_DIM, HEAD_DIM), F32)],
        compiler_params=_params("arbitrary", "arbitrary"),
        name="hgrn2",
    )(proj, proj, proj, lf, kk, g_norm_w.reshape(1, HEAD_DIM), s0, mall, masks, sels)


def _mixer_out_kernel(og_ref, scv_ref, scb_ref, scc_ref, ga_ref, gb_ref, x_ref,
                      pv_ref, pc_ref, mv_ref, mc_ref,
                      wa_ref, wb_ref, wo_ref, cw_ref, nw_ref, wr_ref, br_ref,
                      h1_ref, xn_ref, idx_ref, gate_ref, *, tiles_per_seq):
    i = pl.program_id(0)
    tm = x_ref.shape[0]
    first = (i % tiles_per_seq) == 0

    u = scc_ref[...].astype(F32) * scv_ref[...].astype(F32)
    halo_prev = pc_ref[...].astype(F32) * pv_ref[...].astype(F32)
    halo_meta = mc_ref[...].astype(F32) * mv_ref[...].astype(F32)
    halo = jnp.where(first, halo_meta, halo_prev)
    hr = halo.shape[0]
    r = lax.broadcasted_iota(jnp.int32, (tm, 1), 0)
    u1 = jnp.where(r == 0, halo[hr - 1:hr], pltpu.roll(u, 1, 0))
    u2 = jnp.where(r == 0, halo[hr - 2:hr - 1], jnp.where(r == 1, halo[hr - 1:hr], pltpu.roll(u, 2, 0)))
    conv = cw_ref[2:3] * u + cw_ref[1:2] * u1 + cw_ref[0:1] * u2
    yb_in = (scb_ref[...].astype(F32) * conv).astype(BF16)

    y_a = _dot(og_ref[...], wa_ref[...])
    y_b = _dot(yb_in, wb_ref[...])
    merged = (ga_ref[...].astype(F32) * y_a + gb_ref[...].astype(F32) * y_b).astype(BF16)
    h1 = x_ref[...] + _dot(merged, wo_ref[...])
    h1_ref[...] = h1

    ms = jnp.mean(h1 * h1, axis=-1, keepdims=True)
    xn = h1 * lax.rsqrt(ms + EPS) * nw_ref[...]
    xn_ref[...] = xn

    xh = xn.astype(BF16)
    xl = (xn - xh.astype(F32)).astype(BF16)
    wr = wr_ref[...]
    wh = wr.astype(BF16)
    wl = (wr - wh.astype(F32)).astype(BF16)
    logits = _dot_bt(wh, xh) + _dot_bt(wh, xl) + _dot_bt(wl, xh) + br_ref[...]
    ne = logits.shape[0]
    ie = lax.broadcasted_iota(jnp.int32, logits.shape, 0)
    tops, idxs = [], []
    for _ in range(TOP_K):
        mx = jnp.max(logits, axis=0, keepdims=True)
        ix = jnp.min(jnp.where(logits == mx, ie, ne), axis=0, keepdims=True)
        tops.append(mx)
        idxs.append(ix)
        logits = jnp.where(ie == ix, -jnp.inf, logits)
    es = [jnp.exp(t - tops[0]) for t in tops]
    den = es[0]
    for e in es[1:]:
        den = den + e
    gate_ref[...] = jnp.concatenate([e / den for e in es], axis=0)
    idx_ref[...] = jnp.concatenate(idxs, axis=0)


def _mixer_out(og, proj, proj_meta, x2d, wa, wb, wo, conv_w, norm_w, w_router, b_router, seq, tm):
    m, d = x2d.shape
    w = COL_TILE
    halo = proj_meta.shape[0]
    assert tm % halo == 0 and seq % tm == 0 and d == 2 * w
    ne = w_router.shape[1]
    per_halo = tm // halo
    row = lambda col: (lambda i: (i, col))
    prev = lambda col: (lambda i: (jnp.maximum(i * per_halo - 1, 0), col))
    const = lambda i: (0, 0)
    whole = lambda a: pl.BlockSpec(a.shape, const)
    wr_t = w_router.T
    kern = functools.partial(_mixer_out_kernel, tiles_per_seq=seq // tm)
    return pl.pallas_call(
        kern,
        grid=(m // tm,),
        in_specs=[
            pl.BlockSpec((tm, w), row(0)),
            pl.BlockSpec((tm, w), row(3)),
            pl.BlockSpec((tm, w), row(4)),
            pl.BlockSpec((tm, w), row(5)),
            pl.BlockSpec((tm, d), row(3)),
            pl.BlockSpec((tm, d), row(4)),
            pl.BlockSpec((tm, d), row(0)),
            pl.BlockSpec((halo, w), prev(3)),
            pl.BlockSpec((halo, w), prev(5)),
            pl.BlockSpec((halo, w), lambda i: (0, 3)),
            pl.BlockSpec((halo, w), lambda i: (0, 5)),
            whole(wa), whole(wb), whole(wo),
            pl.BlockSpec(conv_w.shape, const),
            pl.BlockSpec((1, d), const),
            pl.BlockSpec((ne, d), const),
            pl.BlockSpec((ne, 1), const),
        ],
        out_specs=[
            pl.BlockSpec((tm, d), row(0)),
            pl.BlockSpec((tm, d), row(0)),
            pl.BlockSpec((TOP_K, tm), lambda i: (0, i)),
            pl.BlockSpec((TOP_K, tm), lambda i: (0, i)),
        ],
        out_shape=[
            jax.ShapeDtypeStruct((m, d), F32),
            jax.ShapeDtypeStruct((m, d), F32),
            jax.ShapeDtypeStruct((TOP_K, m), jnp.int32),
            jax.ShapeDtypeStruct((TOP_K, m), F32),
        ],
        compiler_params=_params("arbitrary"),
        name="mixer_out",
    )(og, proj, proj, proj, proj, proj, x2d, proj, proj, proj_meta, proj_meta,
      wa, wb, wo, conv_w, norm_w.reshape(1, d), wr_t, b_router.reshape(ne, 1))


def _gather_kernel(act_ref, tok_ref, tok_next_ref, x_hbm, o_ref, buf_ref, sem):
    i = pl.program_id(0)
    n = o_ref.shape[0]

    def start_rows(tok, slot):
        def issue(r, carry):
            t = tok[0, 0, r]
            pltpu.make_async_copy(x_hbm.at[pl.ds(t, 1)], buf_ref.at[slot, pl.ds(r, 1)], sem.at[slot]).start()
            return carry

        lax.fori_loop(0, n, issue, 0, unroll=8)

    @pl.when(jnp.logical_and(i == 0, act_ref[0] > 0))
    def _():
        start_rows(tok_ref, 0)

    nxt = jnp.minimum(i + 1, pl.num_programs(0) - 1)

    @pl.when(jnp.logical_and(i + 1 < pl.num_programs(0), act_ref[nxt] > 0))
    def _():
        start_rows(tok_next_ref, (i + 1) % 2)

    @pl.when(act_ref[i] > 0)
    def _():
        slot = i % 2
        pltpu.make_async_copy(x_hbm.at[pl.ds(0, n)], buf_ref.at[slot], sem.at[slot]).wait()
        o_ref[...] = buf_ref[slot].astype(o_ref.dtype)

    @pl.when(act_ref[i] == 0)
    def _():
        o_ref[...] = jnp.zeros_like(o_ref)


def _gather_rows(xn, row_tok, sub_active):
    n_rows = row_tok.shape[0]
    width = xn.shape[1]
    n_sub = n_rows // EXPERT_SUB
    tok3 = row_tok.reshape(n_sub, 1, EXPERT_SUB)
    grid_spec = pltpu.PrefetchScalarGridSpec(
        num_scalar_prefetch=1,
        grid=(n_sub,),
        in_specs=[
            pl.BlockSpec((1, 1, EXPERT_SUB), lambda i, act: (i, 0, 0), memory_space=pltpu.SMEM),
            pl.BlockSpec((1, 1, EXPERT_SUB), lambda i, act: (jnp.minimum(i + 1, n_sub - 1), 0, 0),
                         memory_space=pltpu.SMEM),
            pl.BlockSpec(memory_space=pl.ANY),
        ],
        out_specs=pl.BlockSpec((EXPERT_SUB, width), lambda i, act: (i, 0)),
        scratch_shapes=[pltpu.VMEM((2, EXPERT_SUB, width), xn.dtype), pltpu.SemaphoreType.DMA((2,))],
    )
    return pl.pallas_call(
        _gather_kernel,
        grid_spec=grid_spec,
        out_shape=jax.ShapeDtypeStruct((n_rows, width), BF16),
        compiler_params=_params("arbitrary"),
        name="gather_rows",
    )(sub_active, tok3, tok3, xn)


def _row_chunks(nsub, chunk_fn):
    pair = 2 * EXPERT_SUB

    def body(c, carry):
        chunk_fn(pl.multiple_of(c * pair, pair), pair)
        return carry

    lax.fori_loop(0, nsub // 2, body, 0)

    @pl.when(nsub % 2 == 1)
    def _():
        chunk_fn(pl.multiple_of((nsub - 1) * EXPERT_SUB, EXPERT_SUB), EXPERT_SUB)


def _ffn_up_kernel(st_ref, ns_ref, zf_ref, se_ref, jm_ref, x_hbm, wg_ref, wu_ref, bg_ref, bu_ref, h_hbm,
                   xbuf, hbuf, wgb_ref, wub_ref, sem_x, sem_h):
    s = pl.program_id(0)
    j = pl.program_id(1)
    n_s = pl.num_programs(0)
    nj = pl.num_programs(1)
    step = s * nj + j
    subs = xbuf.shape[1] // EXPERT_SUB
    nsub = ns_ref[s]
    real = jnp.logical_and(nsub > 0, zf_ref[s] == 0)
    hs = step % 2

    def x_copy(sb, b, slot):
        row = pl.multiple_of(st_ref[sb] + b * EXPERT_SUB, EXPERT_SUB)
        return pltpu.make_async_copy(x_hbm.at[pl.ds(row, EXPERT_SUB)],
                                     xbuf.at[slot, pl.ds(b * EXPERT_SUB, EXPERT_SUB)], sem_x.at[slot])

    def h_copy(sb, jj, b, slot):
        row = pl.multiple_of(st_ref[sb] + b * EXPERT_SUB, EXPERT_SUB)
        return pltpu.make_async_copy(hbuf.at[slot, pl.ds(b * EXPERT_SUB, EXPERT_SUB)],
                                     h_hbm.at[jj, pl.ds(row, EXPERT_SUB)], sem_h.at[slot])

    def for_x_subs(sb, fn):
        for b in range(subs):
            @pl.when(jnp.logical_and(b < ns_ref[sb], zf_ref[sb] == 0))
            def _():
                fn(b)

    def for_h_subs(sb, fn):
        for b in range(subs):
            @pl.when(b < ns_ref[sb])
            def _():
                fn(b)

    @pl.when(step == 0)
    def _():
        for_x_subs(0, lambda b: x_copy(0, b, 0).start())

    @pl.when(jnp.logical_and(j == 0, s + 1 < n_s))
    def _():
        nxt = jnp.minimum(s + 1, n_s - 1)
        for_x_subs(nxt, lambda b: x_copy(nxt, b, (s + 1) % 2).start())

    @pl.when(j == 0)
    def _():
        for_x_subs(s, lambda b: x_copy(s, b, s % 2).wait())

    @pl.when(step >= 2)
    def _():
        sp = lax.div(step - 2, nj)
        jp = step - 2 - sp * nj
        for_h_subs(sp, lambda b: h_copy(sp, jp, b, hs).wait())

    @pl.when(real)
    def _():
        wgb_ref[...] = wg_ref[0].astype(BF16)
        wub_ref[...] = wu_ref[0].astype(BF16)
        xs_slot = s % 2

        def chunk(row0, nrows):
            xs = xbuf[xs_slot, pl.ds(row0, nrows), :]
            g = _dot(xs, wgb_ref[...]) + bg_ref[0]
            u = _dot(xs, wub_ref[...]) + bu_ref[0]
            g = jnp.minimum(g, SWIGLU_LIMIT)
            u = jnp.clip(u, -SWIGLU_LIMIT, SWIGLU_LIMIT)
            hbuf[hs, pl.ds(row0, nrows), :] = ((u + 1.0) * (g * jax.nn.sigmoid(SWIGLU_ALPHA * g))).astype(BF16)

        _row_chunks(nsub, chunk)

    @pl.when(zf_ref[s] == 1)
    def _():
        def zero_sub(b):
            hbuf[hs, b * EXPERT_SUB:(b + 1) * EXPERT_SUB, :] = jnp.zeros((EXPERT_SUB, hbuf.shape[2]), hbuf.dtype)

        for_h_subs(s, zero_sub)

    for_h_subs(s, lambda b: h_copy(s, j, b, hs).start())

    @pl.when(step == n_s * nj - 1)
    def _():
        sp = lax.div(step - 1, nj)
        jp = step - 1 - sp * nj
        for_h_subs(sp, lambda b: h_copy(sp, jp, b, 1 - hs).wait())
        for_h_subs(s, lambda b: h_copy(s, j, b, hs).wait())


def _ffn_down_kernel(st_ref, ns_ref, zf_ref, se_ref, cm_ref, h_hbm, wd_ref, bd_ref, y_hbm,
                     hb, ybuf, wdb_ref, sem_h, sem_y):
    s = pl.program_id(0)
    c = pl.program_id(1)
    n_s = pl.num_programs(0)
    nc = pl.num_programs(1)
    step = s * nc + c
    njh = hb.shape[1]
    subs = hb.shape[2] // EXPERT_SUB
    tn = ybuf.shape[2]
    nsub = ns_ref[s]
    real = jnp.logical_and(nsub > 0, zf_ref[s] == 0)
    ys = step % 2

    def h_copy(sb, jj, b, slot):
        row = pl.multiple_of(st_ref[sb] + b * EXPERT_SUB, EXPERT_SUB)
        return pltpu.make_async_copy(h_hbm.at[jj, pl.ds(row, EXPERT_SUB)],
                                     hb.at[slot, jj, pl.ds(b * EXPERT_SUB, EXPERT_SUB)], sem_h.at[slot])

    def y_copy(sb, cc, b, slot):
        row = pl.multiple_of(st_ref[sb] + b * EXPERT_SUB, EXPERT_SUB)
        return pltpu.make_async_copy(ybuf.at[slot, pl.ds(b * EXPERT_SUB, EXPERT_SUB)],
                                     y_hbm.at[pl.ds(row, EXPERT_SUB), pl.ds(cc * tn, tn)], sem_y.at[slot])

    def for_h_subs(sb, fn):
        for b in range(subs):
            @pl.when(jnp.logical_and(b < ns_ref[sb], zf_ref[sb] == 0))
            def _():
                for jj in range(njh):
                    fn(jj, b)

    def for_y_subs(sb, fn):
        for b in range(subs):
            @pl.when(b < ns_ref[sb])
            def _():
                fn(b)

    @pl.when(step == 0)
    def _():
        for_h_subs(0, lambda jj, b: h_copy(0, jj, b, 0).start())

    @pl.when(jnp.logical_and(c == 0, s + 1 < n_s))
    def _():
        nxt = jnp.minimum(s + 1, n_s - 1)
        for_h_subs(nxt, lambda jj, b: h_copy(nxt, jj, b, (s + 1) % 2).start())

    @pl.when(c == 0)
    def _():
        for_h_subs(s, lambda jj, b: h_copy(s, jj, b, s % 2).wait())

    @pl.when(step >= 2)
    def _():
        sp = lax.div(step - 2, nc)
        for_y_subs(sp, lambda b: y_copy(sp, 0, b, ys).wait())

    @pl.when(real)
    def _():
        wdb_ref[...] = wd_ref[0].astype(BF16)
        h_slot = s % 2

        def chunk(row0, nrows):
            hid = jnp.concatenate([hb[h_slot, jj, pl.ds(row0, nrows), :] for jj in range(njh)], axis=1)
            ybuf[ys, pl.ds(row0, nrows), :] = _dot(hid, wdb_ref[...]) + bd_ref[0]

        _row_chunks(nsub, chunk)

    @pl.when(zf_ref[s] == 1)
    def _():
        def zero_sub(b):
            ybuf[ys, b * EXPERT_SUB:(b + 1) * EXPERT_SUB, :] = jnp.zeros((EXPERT_SUB, tn), ybuf.dtype)

        for_y_subs(s, zero_sub)

    for cc in range(y_hbm.shape[1] // tn):
        @pl.when(c == cc)
        def _():
            for_y_subs(s, lambda b: y_copy(s, cc, b, ys).start())

    @pl.when(step == n_s * nc - 1)
    def _():
        sp = lax.div(step - 1, nc)
        for_y_subs(sp, lambda b: y_copy(sp, 0, b, 1 - ys).wait())
        for_y_subs(s, lambda b: y_copy(s, 0, b, ys).wait())


def _experts(x_rows, w_up, b_up, w_down, b_down, sb_start, sb_nsub, sb_zero, sb_expert):
    n_rows, d = x_rows.shape
    ne, _, ff2 = w_up.shape
    ff = ff2 // 2
    tf = EXPERT_FF_TILE
    tn = EXPERT_OUT_TILE
    nj = ff // tf
    nc = d // tn
    n_sb = sb_start.shape[0]
    rows = EXPERT_SUBS * EXPERT_SUB
    is_real = jnp.logical_and(sb_nsub > 0, sb_zero == 0)[:, None]
    jm = jnp.where(is_real, jnp.arange(nj, dtype=jnp.int32)[None, :], nj - 1).astype(jnp.int32)
    cm = jnp.where(is_real, jnp.arange(nc, dtype=jnp.int32)[None, :], nc - 1).astype(jnp.int32)
    any_spec = pl.BlockSpec(memory_space=pl.ANY)

    up_spec = pltpu.PrefetchScalarGridSpec(
        num_scalar_prefetch=5,
        grid=(n_sb, nj),
        in_specs=[
            any_spec,
            pl.BlockSpec((1, d, tf), lambda s, j, st, ns, zf, se, jm: (se[s], 0, jm[s, j])),
            pl.BlockSpec((1, d, tf), lambda s, j, st, ns, zf, se, jm: (se[s], 0, nj + jm[s, j])),
            pl.BlockSpec((1, 1, tf), lambda s, j, st, ns, zf, se, jm: (se[s], 0, jm[s, j])),
            pl.BlockSpec((1, 1, tf), lambda s, j, st, ns, zf, se, jm: (se[s], 0, nj + jm[s, j])),
        ],
        out_specs=any_spec,
        scratch_shapes=[
            pltpu.VMEM((2, rows, d), BF16),
            pltpu.VMEM((2, rows, tf), BF16),
            pltpu.VMEM((d, tf), BF16),
            pltpu.VMEM((d, tf), BF16),
            pltpu.SemaphoreType.DMA((2,)),
            pltpu.SemaphoreType.DMA((2,)),
        ],
    )
    hidden = pl.pallas_call(
        _ffn_up_kernel,
        grid_spec=up_spec,
        out_shape=jax.ShapeDtypeStruct((nj, n_rows, tf), BF16),
        compiler_params=_params("arbitrary", "arbitrary"),
        name="ffn_up",
    )(sb_start, sb_nsub, sb_zero, sb_expert, jm, x_rows, w_up, w_up,
      b_up.reshape(ne, 1, ff2), b_up.reshape(ne, 1, ff2))

    down_spec = pltpu.PrefetchScalarGridSpec(
        num_scalar_prefetch=5,
        grid=(n_sb, nc),
        in_specs=[
            any_spec,
            pl.BlockSpec((1, ff, tn), lambda s, c, st, ns, zf, se, cm: (se[s], 0, cm[s, c])),
            pl.BlockSpec((1, 1, tn), lambda s, c, st, ns, zf, se, cm: (se[s], 0, cm[s, c])),
        ],
        out_specs=any_spec,
        scratch_shapes=[
            pltpu.VMEM((2, nj, rows, tf), BF16),
            pltpu.VMEM((2, rows, tn), F32),
            pltpu.VMEM((ff, tn), BF16),
            pltpu.SemaphoreType.DMA((2,)),
            pltpu.SemaphoreType.DMA((2,)),
        ],
    )
    return pl.pallas_call(
        _ffn_down_kernel,
        grid_spec=down_spec,
        out_shape=jax.ShapeDtypeStruct((n_rows, d), F32),
        compiler_params=_params("arbitrary", "arbitrary"),
        name="ffn_down",
    )(sb_start, sb_nsub, sb_zero, sb_expert, cm, hidden, w_down, b_down.reshape(ne, 1, d))


def _combine_kernel(pos_ref, pos_next_ref, gate_ref, h1_ref, fw_ref, y_hbm, o_ref, buf_ref, sem):
    i = pl.program_id(0)
    tt = h1_ref.shape[0]

    def start_rows(pos, slot):
        def issue(r, carry):
            for k in range(TOP_K):
                p = pos[0, 0, k * tt + r]
                pltpu.make_async_copy(y_hbm.at[pl.ds(p, 1)], buf_ref.at[slot, k, pl.ds(r, 1)], sem.at[slot]).start()
            return carry

        lax.fori_loop(0, tt, issue, 0, unroll=4)

    @pl.when(i == 0)
    def _():
        start_rows(pos_ref, 0)

    @pl.when(i + 1 < pl.num_programs(0))
    def _():
        start_rows(pos_next_ref, (i + 1) % 2)

    slot = i % 2
    for k in range(TOP_K):
        pltpu.make_async_copy(y_hbm.at[pl.ds(0, tt)], buf_ref.at[slot, k], sem.at[slot]).wait()

    gate = gate_ref[...]
    gpad = jnp.concatenate([gate, jnp.zeros((tt - TOP_K, tt), F32)], axis=0)
    gcol = gpad.T
    acc = h1_ref[...]
    for k in range(TOP_K):
        acc = acc + gcol[:, k:k + 1] * buf_ref[slot, k]
    ms = jnp.mean(acc * acc, axis=-1, keepdims=True)
    o_ref[...] = acc * lax.rsqrt(ms + EPS) * fw_ref[...]


def _combine(y_rows, pos, gate, h1, final_w):
    m, d = h1.shape
    tt = COMBINE_ROWS
    nt = m // tt
    pos_t = pos.reshape(TOP_K, nt, tt).transpose(1, 0, 2).reshape(nt, 1, TOP_K * tt)
    return pl.pallas_call(
        _combine_kernel,
        grid=(nt,),
        in_specs=[
            pl.BlockSpec((1, 1, TOP_K * tt), lambda i: (i, 0, 0), memory_space=pltpu.SMEM),
            pl.BlockSpec((1, 1, TOP_K * tt), lambda i: (jnp.minimum(i + 1, nt - 1), 0, 0), memory_space=pltpu.SMEM),
            pl.BlockSpec((TOP_K, tt), lambda i: (0, i)),
            pl.BlockSpec((tt, d), lambda i: (i, 0)),
            pl.BlockSpec((1, d), lambda i: (0, 0)),
            pl.BlockSpec(memory_space=pl.ANY),
        ],
        out_specs=pl.BlockSpec((tt, d), lambda i: (i, 0)),
        out_shape=jax.ShapeDtypeStruct((m, d), F32),
        scratch_shapes=[pltpu.VMEM((2, TOP_K, tt, d), F32), pltpu.SemaphoreType.DMA((2,))],
        compiler_params=_params("arbitrary"),
        name="combine",
    )(pos_t, pos_t, gate, h1, final_w.reshape(1, d), y_rows)


def _routing_tables(idx, m):
    i32 = jnp.int32
    n_assign = TOP_K * m
    flat_e = idx.reshape(n_assign)
    onehot = flat_e[:, None] == jnp.arange(N_EXPERTS, dtype=i32)[None, :]
    csum = jnp.cumsum(onehot.astype(i32), axis=0)
    rank = jnp.sum(jnp.where(onehot, csum - 1, 0), axis=1)
    counts = csum[-1]
    padded = (counts + EXPERT_SUB - 1) // EXPERT_SUB * EXPERT_SUB
    pad_end = jnp.cumsum(padded)
    pad_start = pad_end - padded
    pos = (jnp.sum(jnp.where(onehot, pad_start[None, :], 0), axis=1) + rank).astype(i32)
    n_sub = -(-(n_assign + N_EXPERTS * (EXPERT_SUB - 1)) // EXPERT_SUB)
    n_rows = n_sub * EXPERT_SUB
    tok = (jnp.arange(n_assign, dtype=i32) % m).astype(i32)
    row_tok = jnp.zeros((n_rows,), i32).at[pos].set(tok)
    used_sub = pad_end[-1] // EXPERT_SUB
    sub_active = (jnp.arange(n_sub, dtype=i32) < used_sub).astype(i32)

    rows = EXPERT_SUBS * EXPERT_SUB
    n_sb = -(-n_sub // EXPERT_SUBS) + N_EXPERTS + 1
    nsb_e = (padded + rows - 1) // rows
    sb_cum = jnp.cumsum(nsb_e)
    total_real = sb_cum[-1]
    s = jnp.arange(n_sb, dtype=i32)
    e_s = jnp.minimum(jnp.sum((sb_cum[None, :] <= s[:, None]).astype(i32), axis=1), N_EXPERTS - 1)
    local = s - (sb_cum[e_s] - nsb_e[e_s])
    real = s < total_real
    start_real = pad_start[e_s] + local * rows
    nsub_real = jnp.clip((padded[e_s] - local * rows) // EXPERT_SUB, 0, EXPERT_SUBS)
    fill_idx = s - total_real
    start_fill = pad_end[-1] + fill_idx * rows
    nsub_fill = jnp.clip(n_sub - used_sub - fill_idx * EXPERT_SUBS, 0, EXPERT_SUBS)
    is_fill = jnp.logical_and(jnp.logical_not(real), nsub_fill > 0)
    sb_start = jnp.where(real, start_real, jnp.where(is_fill, start_fill, 0)).astype(i32)
    sb_nsub = jnp.where(real, nsub_real, jnp.where(is_fill, nsub_fill, 0)).astype(i32)
    sb_zero = is_fill.astype(i32)
    last_e = e_s[jnp.maximum(total_real - 1, 0)]
    sb_expert = jnp.where(real, e_s, last_e).astype(i32)
    return pos, row_tok, sub_active, sb_start, sb_nsub, sb_zero, sb_expert


def _main_tiles(d):
    w = HEADS * HEAD_DIM
    sc = d // 2
    sizes = (w, w, w, w, sc, sc, sc, d, d)
    acts = (1, None, 0, 1, 0, 0, 0, 2, 2)
    order = (0, 2, 3, 4, 5, 6, 7, 8)
    starts = np.concatenate([[0], np.cumsum(sizes)])
    cols, codes = [], []
    for seg in order:
        assert sizes[seg] % COL_TILE == 0 and starts[seg] % COL_TILE == 0
        for t in range(sizes[seg] // COL_TILE):
            cols.append(int(starts[seg]) // COL_TILE + t)
            codes.append(acts[seg])
    return cols, codes, int(starts[1]) // COL_TILE


def _layer(x2d, meta, bsz, seq, norm_mix_w, w_in, lb_logits, g_norm_w, w_hgrn_out, conv_w, w_conv_out, w_o,
           norm_ffn_w, w_router, b_router, w_up, b_up, w_down, b_down, final_norm_w):
    m, d = x2d.shape
    cols, codes, fcol = _main_tiles(d)
    consts = _hgrn_constants(CHUNK)

    xn_meta = _prenorm(meta, norm_mix_w, N_META)
    proj_meta = _inproj(xn_meta, w_in, cols, codes, N_META)
    lf_meta, kk_meta = _fgate(xn_meta, w_in, lb_logits, fcol, N_META)
    pad = CHUNK - N_META
    front = lambda a: jnp.pad(a, ((pad, 0), (0, 0)))
    s_zero = jnp.zeros((HEADS, HEAD_DIM, HEAD_DIM), F32)
    _, s_meta = _hgrn(front(proj_meta), front(lf_meta), front(kk_meta), g_norm_w, s_zero, consts, 1, CHUNK, CHUNK)

    xn = _prenorm(x2d, norm_mix_w, 512)
    proj = _inproj(xn, w_in, cols, codes, 1024)
    lf, kk = _fgate(xn, w_in, lb_logits, fcol, 1024)
    og, _ = _hgrn(proj, lf, kk, g_norm_w, s_meta[0], consts, bsz, seq, HGRN_ROWS)
    h1, xn_ffn, idx, gate = _mixer_out(og, proj, proj_meta, x2d, w_hgrn_out.astype(BF16), w_conv_out.astype(BF16),
                                   w_o.astype(BF16), conv_w, norm_ffn_w, w_router, b_router, seq, 256)

    pos, row_tok, sub_active, sb_start, sb_nsub, sb_zero, sb_expert = _routing_tables(idx, m)
    x_rows = _gather_rows(xn_ffn, row_tok, sub_active)
    y_rows = _experts(x_rows, w_up, b_up, w_down, b_down, sb_start, sb_nsub, sb_zero, sb_expert)
    return _combine(y_rows, pos, gate, h1, final_norm_w)


def kernel(x, meta_tokens, norm_mix_w, w_in, lb_logits, g_norm_w, w_hgrn_out, conv_w, w_conv_out, w_o, norm_ffn_w,
           w_router, b_router, w_up, b_up, w_down, b_down, final_norm_w):
    bsz, seq, d = x.shape
    assert norm_mix_w.shape[0] == 1, "single-layer block"
    out = _layer(x.reshape(bsz * seq, d), meta_tokens.astype(x.dtype), bsz, seq, norm_mix_w[0], w_in[0], lb_logits,
                 g_norm_w[0], w_hgrn_out[0], conv_w[0], w_conv_out[0], w_o[0], norm_ffn_w[0], w_router[0],
                 b_router[0], w_up[0], b_up[0], w_down[0], b_down[0], final_norm_w)
    return out.reshape(bsz, seq, d)
```

```python
import functools

import numpy as np
import jax
import jax.numpy as jnp
from jax import lax
from jax.experimental import pallas as pl
from jax.experimental.pallas import tpu as pltpu

F32 = jnp.float32
BF16 = jnp.bfloat16

N_META = 16
HEADS = 8
HEAD_DIM = 128
N_EXPERTS = 32
TOP_K = 4
SWIGLU_LIMIT = 7.0
SWIGLU_ALPHA = 1.702
EPS = 1e-6

CHUNK = 256
HGRN_BASE = 64
HGRN_ROWS = 512
COL_TILE = 1024
EXPERT_SUB = 256
EXPERT_SUBS = 9
EXPERT_FF_TILE = 512
EXPERT_OUT_TILE = 512
COMBINE_ROWS = 128

V7X_VMEM_LIMIT = 56 * 1024 * 1024


def _dot(a, b):
    return jnp.dot(a, b, preferred_element_type=F32)


def _dot_bt(a, b):
    return lax.dot_general(a, b, (((1,), (1,)), ((), ())), preferred_element_type=F32)


def _dot_at(a, b):
    return lax.dot_general(a, b, (((0,), (0,)), ((), ())), preferred_element_type=F32)


def _split3(x):
    hi = x.astype(BF16)
    r1 = x - hi.astype(F32)
    mid = r1.astype(BF16)
    lo = (r1 - mid.astype(F32)).astype(BF16)
    return hi, mid, lo


def _params(*sem):
    return pltpu.CompilerParams(dimension_semantics=sem, vmem_limit_bytes=V7X_VMEM_LIMIT)


def _prenorm_kernel(x_ref, w_ref, o_ref):
    x = x_ref[...]
    ms = jnp.mean(x * x, axis=-1, keepdims=True)
    o_ref[...] = (x * lax.rsqrt(ms + EPS) * w_ref[...]).astype(o_ref.dtype)


def _prenorm(x, w, tm):
    m, d = x.shape
    return pl.pallas_call(
        _prenorm_kernel,
        grid=(m // tm,),
        in_specs=[pl.BlockSpec((tm, d), lambda i: (i, 0)), pl.BlockSpec((1, d), lambda i: (0, 0))],
        out_specs=pl.BlockSpec((tm, d), lambda i: (i, 0)),
        out_shape=jax.ShapeDtypeStruct((m, d), BF16),
        compiler_params=_params("arbitrary"),
        name="prenorm",
    )(x, w.reshape(1, d))


def _inproj_kernel(col_ref, code_ref, x_ref, w_ref, o_ref, wb_ref):
    n = pl.program_id(0)

    @pl.when(pl.program_id(1) == 0)
    def _():
        wb_ref[...] = w_ref[...].astype(BF16)

    z = _dot(x_ref[...], wb_ref[...])
    code = code_ref[n]
    s = jax.nn.sigmoid(z)
    o_ref[...] = jnp.where(code == 0, z, jnp.where(code == 1, z * s, s)).astype(o_ref.dtype)


def _inproj(xn, w_in, cols, codes, tm):
    m, d = xn.shape
    nt = len(cols)
    grid_spec = pltpu.PrefetchScalarGridSpec(
        num_scalar_prefetch=2,
        grid=(nt, m // tm),
        in_specs=[
            pl.BlockSpec((tm, d), lambda n, i, col, code: (i, 0)),
            pl.BlockSpec((d, COL_TILE), lambda n, i, col, code: (0, col[n])),
        ],
        out_specs=pl.BlockSpec((tm, COL_TILE), lambda n, i, col, code: (i, n)),
        scratch_shapes=[pltpu.VMEM((d, COL_TILE), BF16)],
    )
    return pl.pallas_call(
        _inproj_kernel,
        grid_spec=grid_spec,
        out_shape=jax.ShapeDtypeStruct((m, nt * COL_TILE), BF16),
        compiler_params=_params("arbitrary", "arbitrary"),
        name="inproj",
    )(jnp.asarray(cols, jnp.int32), jnp.asarray(codes, jnp.int32), xn, w_in)


def _fgate_kernel(x_ref, w_ref, lbl_ref, lf_ref, kk_ref, wb_ref):
    @pl.when(pl.program_id(0) == 0)
    def _():
        wb_ref[...] = w_ref[...].astype(BF16)

    z = _dot(x_ref[...], wb_ref[...])
    lbl = lbl_ref[...]
    e = jnp.exp(lbl - jnp.max(lbl, axis=0, keepdims=True))
    lb = e[0:1] / jnp.sum(e, axis=0, keepdims=True)
    f = lb + (1.0 - lb) * jax.nn.sigmoid(z)
    lf_ref[...] = jnp.log(f)
    kk_ref[...] = (1.0 - lb) * jax.nn.sigmoid(-z)


def _fgate(xn, w_in, lb_logits, col, tm):
    m, d = xn.shape
    r = lb_logits.shape[0]
    out = jax.ShapeDtypeStruct((m, COL_TILE), F32)
    return pl.pallas_call(
        _fgate_kernel,
        grid=(m // tm,),
        in_specs=[
            pl.BlockSpec((tm, d), lambda i: (i, 0)),
            pl.BlockSpec((d, COL_TILE), lambda i: (0, col)),
            pl.BlockSpec((r, COL_TILE), lambda i: (0, 0)),
        ],
        out_specs=[pl.BlockSpec((tm, COL_TILE), lambda i: (i, 0))] * 2,
        out_shape=[out, out],
        scratch_shapes=[pltpu.VMEM((d, COL_TILE), BF16)],
        compiler_params=_params("arbitrary"),
        name="fgate",
    )(xn, w_in, lb_logits)


def _hgrn_constants(c):
    base = HGRN_BASE
    nlow = int(np.log2(base))
    nl = int(np.log2(c))
    assert (1 << nl) == c and (1 << nlow) == base and c >= base
    rr = np.arange(base)[:, None]
    uu = np.arange(base)[None, :]
    mats = [uu <= rr, uu > rr]
    sels = []
    for lvl in range(nlow):
        b = 1 << lvl
        start = (rr // (2 * b)) * (2 * b)
        mid = start + b - 1
        second = (rr - start) >= b
        mats.append(np.where(second, (uu > mid) & (uu <= rr), (uu > rr) & (uu <= mid)))
        sels.append(np.broadcast_to(second, (base, HEADS * HEAD_DIM)))
    tt = np.arange(c)[:, None]
    ss = np.arange(c)[None, :]
    masks = []
    for lvl in range(nl):
        b = 1 << lvl
        masks.append(((tt // (2 * b)) == (ss // (2 * b))) & ((tt % (2 * b)) >= b) & ((ss % (2 * b)) < b))
    masks.append(np.eye(c, dtype=bool))
    m1 = np.concatenate(mats, 0).astype(np.float32)
    mall = jnp.asarray(np.concatenate([m1, m1, m1], axis=1), BF16)
    return mall, jnp.asarray(np.stack(masks).astype(np.float32)), jnp.asarray(np.stack(sels).astype(np.float32))


def _hgrn_kernel(q_ref, v_ref, g_ref, lf_ref, kk_ref, gw_ref, s0_ref, mall_ref, mask_ref, sel_ref,
                 o_ref, sfin_ref, st_ref, *, chunk, n_chunks):
    base = HGRN_BASE
    nb = chunk // base
    nlow = sel_ref.shape[0]
    nl = mask_ref.shape[0] - 1
    step = pl.program_id(1)

    @pl.when(step == 0)
    def _():
        st_ref[...] = s0_ref[...]

    def chunk_body(ci, carry):
        r0 = pl.multiple_of(ci * chunk, chunk)
        rows = pl.ds(r0, chunk)
        qb = q_ref[rows, :]
        q = qb.astype(F32)
        k = kk_ref[rows, :]
        blk = lambda a, i: a[i * base:(i + 1) * base]

        pre, suf, e_low = [], [], []
        for i in range(nb):
            hi, mid, lo = _split3(lf_ref[pl.ds(r0 + i * base, base), :])
            args = _dot(mall_ref[...], jnp.concatenate([hi, mid, lo], axis=0))
            pre.append(args[0:base])
            suf.append(args[base:2 * base])
            e_low.append(jnp.exp(args[2 * base:]))
        tot = [p[base - 1:base] for p in pre]

        def span(lo_blk, hi_blk):
            acc = None
            for j in range(lo_blk, hi_blk):
                acc = tot[j] if acc is None else acc + tot[j]
            return acc

        def shifted(a, off):
            return a if off is None else a + off

        xs = []
        for lvl in range(nlow):
            parts = [jnp.where(sel_ref[lvl] > 0.5, blk(q, i), blk(k, i)) * blk(e_low[i], lvl) for i in range(nb)]
            xs.append(jnp.concatenate(parts, axis=0).astype(BF16))
        for lvl in range(nlow, nl):
            half = (1 << lvl) // base
            parts = []
            for i in range(nb):
                g = i % (2 * half)
                if g >= half:
                    parts.append(blk(q, i) * jnp.exp(shifted(pre[i], span(i - (g - half), i))))
                else:
                    parts.append(blk(k, i) * jnp.exp(shifted(suf[i], span(i + 1, i - g + half))))
            xs.append(jnp.concatenate(parts, axis=0).astype(BF16))
        q_in = jnp.concatenate([blk(q, i) * jnp.exp(shifted(pre[i], span(0, i))) for i in range(nb)],
                               axis=0).astype(BF16)
        k_out = jnp.concatenate([blk(k, i) * jnp.exp(shifted(suf[i], span(i + 1, nb))) for i in range(nb)],
                                axis=0).astype(BF16)
        dec = jnp.exp(span(0, nb))
        kb = k.astype(BF16)

        for h in range(HEADS):
            cs = slice(h * HEAD_DIM, (h + 1) * HEAD_DIM)
            scores = mask_ref[nl] * _dot_bt(qb[:, cs], kb[:, cs])
            for lvl in range(nl):
                x = xs[lvl][:, cs]
                scores = scores + mask_ref[lvl] * _dot_bt(x, x)
            v = v_ref[rows, cs]
            st = st_ref[h]
            o = _dot(scores.astype(BF16), v) + _dot_bt(q_in[:, cs], st.astype(BF16))
            st_ref[h] = st * dec[:, cs] + _dot_at(v, k_out[:, cs])
            ms = jnp.mean(o * o, axis=-1, keepdims=True)
            on = o * lax.rsqrt(ms + EPS) * gw_ref[...]
            o_ref[rows, cs] = (on * g_ref[rows, cs].astype(F32)).astype(o_ref.dtype)
        return carry

    lax.fori_loop(0, n_chunks, chunk_body, 0)

    @pl.when(step == pl.num_programs(1) - 1)
    def _():
        sfin_ref[0] = st_ref[...]


def _hgrn(proj, lf, kk, g_norm_w, s0, consts, bsz, seq, rows):
    mall, masks, sels = consts
    steps = seq // rows
    w = HEADS * HEAD_DIM
    assert w == COL_TILE
    row_map = lambda col: (lambda b, s: (b * steps + s, col))
    const2 = lambda b, s: (0, 0)
    const3 = lambda b, s: (0, 0, 0)
    kern = functools.partial(_hgrn_kernel, chunk=CHUNK, n_chunks=rows // CHUNK)
    return pl.pallas_call(
        kern,
        grid=(bsz, steps),
        in_specs=[
            pl.BlockSpec((rows, w), row_map(0)),
            pl.BlockSpec((rows, w), row_map(1)),
            pl.BlockSpec((rows, w), row_map(2)),
            pl.BlockSpec((rows, w), row_map(0)),
            pl.BlockSpec((rows, w), row_map(0)),
            pl.BlockSpec((1, HEAD_DIM), const2),
            pl.BlockSpec((HEADS, HEAD_DIM, HEAD_DIM), const3),
            pl.BlockSpec(mall.shape, const2),
            pl.BlockSpec(masks.shape, const3),
            pl.BlockSpec(sels.shape, const3),
        ],
        out_specs=[
            pl.BlockSpec((rows, w), row_map(0)),
            pl.BlockSpec((1, HEADS, HEAD_DIM, HEAD_DIM), lambda b, s: (b, 0, 0, 0)),
        ],
        out_shape=[
            jax.ShapeDtypeStruct((bsz * seq, w), BF16),
            jax.ShapeDtypeStruct((bsz, HEADS, HEAD_DIM, HEAD_DIM), F32),
        ],
        scratch_shapes=[pltpu.VMEM((HEADS, HEAD_DIM, HEAD_DIM), F32)],
        compiler_params=_params("arbitrary", "arbitrary"),
        name="hgrn2",
    )(proj, proj, proj, lf, kk, g_norm_w.reshape(1, HEAD_DIM), s0, mall, masks, sels)


def _mixer_out_kernel(og_ref, scv_ref, scb_ref, scc_ref, ga_ref, gb_ref, x_ref,
                      pv_ref, pc_ref, mv_ref, mc_ref,
                      wa_ref, wb_ref, wo_ref, cw_ref, nw_ref, wr_ref, br_ref,
                      h1_ref, xn_ref, idx_ref, gate_ref, *, tiles_per_seq):
    i = pl.program_id(0)
    tm = x_ref.shape[0]
    first = (i % tiles_per_seq) == 0

    u = scc_ref[...].astype(F32) * scv_ref[...].astype(F32)
    halo_prev = pc_ref[...].astype(F32) * pv_ref[...].astype(F32)
    halo_meta = mc_ref[...].astype(F32) * mv_ref[...].astype(F32)
    halo = jnp.where(first, halo_meta, halo_prev)
    hr = halo.shape[0]
    r = lax.broadcasted_iota(jnp.int32, (tm, 1), 0)
    u1 = jnp.where(r == 0, halo[hr - 1:hr], pltpu.roll(u, 1, 0))
    u2 = jnp.where(r == 0, halo[hr - 2:hr - 1], jnp.where(r == 1, halo[hr - 1:hr], pltpu.roll(u, 2, 0)))
    conv = cw_ref[2:3] * u + cw_ref[1:2] * u1 + cw_ref[0:1] * u2
    yb_in = (scb_ref[...].astype(F32) * conv).astype(BF16)

    y_a = _dot(og_ref[...], wa_ref[...])
    y_b = _dot(yb_in, wb_ref[...])
    merged = (ga_ref[...].astype(F32) * y_a + gb_ref[...].astype(F32) * y_b).astype(BF16)
    h1 = x_ref[...] + _dot(merged, wo_ref[...])
    h1_ref[...] = h1

    ms = jnp.mean(h1 * h1, axis=-1, keepdims=True)
    xn = h1 * lax.rsqrt(ms + EPS) * nw_ref[...]
    xn_ref[...] = xn

    xh = xn.astype(BF16)
    xl = (xn - xh.astype(F32)).astype(BF16)
    wr = wr_ref[...]
    wh = wr.astype(BF16)
    wl = (wr - wh.astype(F32)).astype(BF16)
    logits = _dot_bt(wh, xh) + _dot_bt(wh, xl) + _dot_bt(wl, xh) + br_ref[...]
    ne = logits.shape[0]
    ie = lax.broadcasted_iota(jnp.int32, logits.shape, 0)
    tops, idxs = [], []
    for _ in range(TOP_K):
        mx = jnp.max(logits, axis=0, keepdims=True)
        ix = jnp.min(jnp.where(logits == mx, ie, ne), axis=0, keepdims=True)
        tops.append(mx)
        idxs.append(ix)
        logits = jnp.where(ie == ix, -jnp.inf, logits)
    es = [jnp.exp(t - tops[0]) for t in tops]
    den = es[0]
    for e in es[1:]:
        den = den + e
    gate_ref[...] = jnp.concatenate([e / den for e in es], axis=0)
    idx_ref[...] = jnp.concatenate(idxs, axis=0)


def _mixer_out(og, proj, proj_meta, x2d, wa, wb, wo, conv_w, norm_w, w_router, b_router, seq, tm):
    m, d = x2d.shape
    w = COL_TILE
    halo = proj_meta.shape[0]
    assert tm % halo == 0 and seq % tm == 0 and d == 2 * w
    ne = w_router.shape[1]
    per_halo = tm // halo
    row = lambda col: (lambda i: (i, col))
    prev = lambda col: (lambda i: (jnp.maximum(i * per_halo - 1, 0), col))
    const = lambda i: (0, 0)
    whole = lambda a: pl.BlockSpec(a.shape, const)
    wr_t = w_router.T
    kern = functools.partial(_mixer_out_kernel, tiles_per_seq=seq // tm)
    return pl.pallas_call(
        kern,
        grid=(m // tm,),
        in_specs=[
            pl.BlockSpec((tm, w), row(0)),
            pl.BlockSpec((tm, w), row(3)),
            pl.BlockSpec((tm, w), row(4)),
            pl.BlockSpec((tm, w), row(5)),
            pl.BlockSpec((tm, d), row(3)),
            pl.BlockSpec((tm, d), row(4)),
            pl.BlockSpec((tm, d), row(0)),
            pl.BlockSpec((halo, w), prev(3)),
            pl.BlockSpec((halo, w), prev(5)),
            pl.BlockSpec((halo, w), lambda i: (0, 3)),
            pl.BlockSpec((halo, w), lambda i: (0, 5)),
            whole(wa), whole(wb), whole(wo),
            pl.BlockSpec(conv_w.shape, const),
            pl.BlockSpec((1, d), const),
            pl.BlockSpec((ne, d), const),
            pl.BlockSpec((ne, 1), const),
        ],
        out_specs=[
            pl.BlockSpec((tm, d), row(0)),
            pl.BlockSpec((tm, d), row(0)),
            pl.BlockSpec((TOP_K, tm), lambda i: (0, i)),
            pl.BlockSpec((TOP_K, tm), lambda i: (0, i)),
        ],
        out_shape=[
            jax.ShapeDtypeStruct((m, d), F32),
            jax.ShapeDtypeStruct((m, d), F32),
            jax.ShapeDtypeStruct((TOP_K, m), jnp.int32),
            jax.ShapeDtypeStruct((TOP_K, m), F32),
        ],
        compiler_params=_params("arbitrary"),
        name="mixer_out",
    )(og, proj, proj, proj, proj, proj, x2d, proj, proj, proj_meta, proj_meta,
      wa, wb, wo, conv_w, norm_w.reshape(1, d), wr_t, b_router.reshape(ne, 1))


def _gather_kernel(act_ref, tok_ref, tok_next_ref, x_hbm, o_ref, buf_ref, sem):
    i = pl.program_id(0)
    n = o_ref.shape[0]

    def start_rows(tok, slot):
        def issue(r, carry):
            t = tok[0, 0, r]
            pltpu.make_async_copy(x_hbm.at[pl.ds(t, 1)], buf_ref.at[slot, pl.ds(r, 1)], sem.at[slot]).start()
            return carry

        lax.fori_loop(0, n, issue, 0, unroll=8)

    @pl.when(jnp.logical_and(i == 0, act_ref[0] > 0))
    def _():
        start_rows(tok_ref, 0)

    nxt = jnp.minimum(i + 1, pl.num_programs(0) - 1)

    @pl.when(jnp.logical_and(i + 1 < pl.num_programs(0), act_ref[nxt] > 0))
    def _():
        start_rows(tok_next_ref, (i + 1) % 2)

    @pl.when(act_ref[i] > 0)
    def _():
        slot = i % 2
        pltpu.make_async_copy(x_hbm.at[pl.ds(0, n)], buf_ref.at[slot], sem.at[slot]).wait()
        o_ref[...] = buf_ref[slot].astype(o_ref.dtype)

    @pl.when(act_ref[i] == 0)
    def _():
        o_ref[...] = jnp.zeros_like(o_ref)


def _gather_rows(xn, row_tok, sub_active):
    n_rows = row_tok.shape[0]
    width = xn.shape[1]
    n_sub = n_rows // EXPERT_SUB
    tok3 = row_tok.reshape(n_sub, 1, EXPERT_SUB)
    grid_spec = pltpu.PrefetchScalarGridSpec(
        num_scalar_prefetch=1,
        grid=(n_sub,),
        in_specs=[
            pl.BlockSpec((1, 1, EXPERT_SUB), lambda i, act: (i, 0, 0), memory_space=pltpu.SMEM),
            pl.BlockSpec((1, 1, EXPERT_SUB), lambda i, act: (jnp.minimum(i + 1, n_sub - 1), 0, 0),
                         memory_space=pltpu.SMEM),
            pl.BlockSpec(memory_space=pl.ANY),
        ],
        out_specs=pl.BlockSpec((EXPERT_SUB, width), lambda i, act: (i, 0)),
        scratch_shapes=[pltpu.VMEM((2, EXPERT_SUB, width), xn.dtype), pltpu.SemaphoreType.DMA((2,))],
    )
    return pl.pallas_call(
        _gather_kernel,
        grid_spec=grid_spec,
        out_shape=jax.ShapeDtypeStruct((n_rows, width), BF16),
        compiler_params=_params("arbitrary"),
        name="gather_rows",
    )(sub_active, tok3, tok3, xn)


def _row_chunks(nsub, chunk_fn):
    pair = 2 * EXPERT_SUB

    def body(c, carry):
        chunk_fn(pl.multiple_of(c * pair, pair), pair)
        return carry

    lax.fori_loop(0, nsub // 2, body, 0)

    @pl.when(nsub % 2 == 1)
    def _():
        chunk_fn(pl.multiple_of((nsub - 1) * EXPERT_SUB, EXPERT_SUB), EXPERT_SUB)


def _ffn_up_kernel(st_ref, ns_ref, zf_ref, se_ref, jm_ref, x_hbm, wg_ref, wu_ref, bg_ref, bu_ref, h_hbm,
                   xbuf, hbuf, wgb_ref, wub_ref, sem_x, sem_h):
    s = pl.program_id(0)
    j = pl.program_id(1)
    n_s = pl.num_programs(0)
    nj = pl.num_programs(1)
    step = s * nj + j
    subs = xbuf.shape[1] // EXPERT_SUB
    nsub = ns_ref[s]
    real = jnp.logical_and(nsub > 0, zf_ref[s] == 0)
    hs = step % 2

    def x_copy(sb, b, slot):
        row = pl.multiple_of(st_ref[sb] + b * EXPERT_SUB, EXPERT_SUB)
        return pltpu.make_async_copy(x_hbm.at[pl.ds(row, EXPERT_SUB)],
                                     xbuf.at[slot, pl.ds(b * EXPERT_SUB, EXPERT_SUB)], sem_x.at[slot])

    def h_copy(sb, jj, b, slot):
        row = pl.multiple_of(st_ref[sb] + b * EXPERT_SUB, EXPERT_SUB)
        return pltpu.make_async_copy(hbuf.at[slot, pl.ds(b * EXPERT_SUB, EXPERT_SUB)],
                                     h_hbm.at[jj, pl.ds(row, EXPERT_SUB)], sem_h.at[slot])

    def for_x_subs(sb, fn):
        for b in range(subs):
            @pl.when(jnp.logical_and(b < ns_ref[sb], zf_ref[sb] == 0))
            def _():
                fn(b)

    def for_h_subs(sb, fn):
        for b in range(subs):
            @pl.when(b < ns_ref[sb])
            def _():
                fn(b)

    @pl.when(step == 0)
    def _():
        for_x_subs(0, lambda b: x_copy(0, b, 0).start())

    @pl.when(jnp.logical_and(j == 0, s + 1 < n_s))
    def _():
        nxt = jnp.minimum(s + 1, n_s - 1)
        for_x_subs(nxt, lambda b: x_copy(nxt, b, (s + 1) % 2).start())

    @pl.when(j == 0)
    def _():
        for_x_subs(s, lambda b: x_copy(s, b, s % 2).wait())

    @pl.when(step >= 2)
    def _():
        sp = lax.div(step - 2, nj)
        jp = step - 2 - sp * nj
        for_h_subs(sp, lambda b: h_copy(sp, jp, b, hs).wait())

    @pl.when(real)
    def _():
        wgb_ref[...] = wg_ref[0].astype(BF16)
        wub_ref[...] = wu_ref[0].astype(BF16)
        xs_slot = s % 2

        def chunk(row0, nrows):
            xs = xbuf[xs_slot, pl.ds(row0, nrows), :]
            g = _dot(xs, wgb_ref[...]) + bg_ref[0]
            u = _dot(xs, wub_ref[...]) + bu_ref[0]
            g = jnp.minimum(g, SWIGLU_LIMIT)
            u = jnp.clip(u, -SWIGLU_LIMIT, SWIGLU_LIMIT)
            hbuf[hs, pl.ds(row0, nrows), :] = ((u + 1.0) * (g * jax.nn.sigmoid(SWIGLU_ALPHA * g))).astype(BF16)

        _row_chunks(nsub, chunk)

    @pl.when(zf_ref[s] == 1)
    def _():
        def zero_sub(b):
            hbuf[hs, b * EXPERT_SUB:(b + 1) * EXPERT_SUB, :] = jnp.zeros((EXPERT_SUB, hbuf.shape[2]), hbuf.dtype)

        for_h_subs(s, zero_sub)

    for_h_subs(s, lambda b: h_copy(s, j, b, hs).start())

    @pl.when(step == n_s * nj - 1)
    def _():
        sp = lax.div(step - 1, nj)
        jp = step - 1 - sp * nj
        for_h_subs(sp, lambda b: h_copy(sp, jp, b, 1 - hs).wait())
        for_h_subs(s, lambda b: h_copy(s, j, b, hs).wait())


def _ffn_down_kernel(st_ref, ns_ref, zf_ref, se_ref, cm_ref, h_hbm, wd_ref, bd_ref, y_hbm,
                     hb, ybuf, wdb_ref, sem_h, sem_y):
    s = pl.program_id(0)
    c = pl.program_id(1)
    n_s = pl.num_programs(0)
    nc = pl.num_programs(1)
    step = s * nc + c
    njh = hb.shape[1]
    subs = hb.shape[2] // EXPERT_SUB
    tn = ybuf.shape[2]
    nsub = ns_ref[s]
    real = jnp.logical_and(nsub > 0, zf_ref[s] == 0)
    ys = step % 2

    def h_copy(sb, jj, b, slot):
        row = pl.multiple_of(st_ref[sb] + b * EXPERT_SUB, EXPERT_SUB)
        return pltpu.make_async_copy(h_hbm.at[jj, pl.ds(row, EXPERT_SUB)],
                                     hb.at[slot, jj, pl.ds(b * EXPERT_SUB, EXPERT_SUB)], sem_h.at[slot])

    def y_copy(sb, cc, b, slot):
        row = pl.multiple_of(st_ref[sb] + b * EXPERT_SUB, EXPERT_SUB)
        return pltpu.make_async_copy(ybuf.at[slot, pl.ds(b * EXPERT_SUB, EXPERT_SUB)],
                                     y_hbm.at[pl.ds(row, EXPERT_SUB), pl.ds(cc * tn, tn)], sem_y.at[slot])

    def for_h_subs(sb, fn):
        for b in range(subs):
            @pl.when(jnp.logical_and(b < ns_ref[sb], zf_ref[sb] == 0))
            def _():
                for jj in range(njh):
                    fn(jj, b)

    def for_y_subs(sb, fn):
        for b in range(subs):
            @pl.when(b < ns_ref[sb])
            def _():
                fn(b)

    @pl.when(step == 0)
    def _():
        for_h_subs(0, lambda jj, b: h_copy(0, jj, b, 0).start())

    @pl.when(jnp.logical_and(c == 0, s + 1 < n_s))
    def _():
        nxt = jnp.minimum(s + 1, n_s - 1)
        for_h_subs(nxt, lambda jj, b: h_copy(nxt, jj, b, (s + 1) % 2).start())

    @pl.when(c == 0)
    def _():
        for_h_subs(s, lambda jj, b: h_copy(s, jj, b, s % 2).wait())

    @pl.when(step >= 2)
    def _():
        sp = lax.div(step - 2, nc)
        for_y_subs(sp, lambda b: y_copy(sp, 0, b, ys).wait())

    @pl.when(real)
    def _():
        wdb_ref[...] = wd_ref[0].astype(BF16)
        h_slot = s % 2

        def chunk(row0, nrows):
            hid = jnp.concatenate([hb[h_slot, jj, pl.ds(row0, nrows), :] for jj in range(njh)], axis=1)
            ybuf[ys, pl.ds(row0, nrows), :] = _dot(hid, wdb_ref[...]) + bd_ref[0]

        _row_chunks(nsub, chunk)

    @pl.when(zf_ref[s] == 1)
    def _():
        def zero_sub(b):
            ybuf[ys, b * EXPERT_SUB:(b + 1) * EXPERT_SUB, :] = jnp.zeros((EXPERT_SUB, tn), ybuf.dtype)

        for_y_subs(s, zero_sub)

    for cc in range(y_hbm.shape[1] // tn):
        @pl.when(c == cc)
        def _():
            for_y_subs(s, lambda b: y_copy(s, cc, b, ys).start())

    @pl.when(step == n_s * nc - 1)
    def _():
        sp = lax.div(step - 1, nc)
        for_y_subs(sp, lambda b: y_copy(sp, 0, b, 1 - ys).wait())
        for_y_subs(s, lambda b: y_copy(s, 0, b, ys).wait())


def _experts(x_rows, w_up, b_up, w_down, b_down, sb_start, sb_nsub, sb_zero, sb_expert):
    n_rows, d = x_rows.shape
    ne, _, ff2 = w_up.shape
    ff = ff2 // 2
    tf = EXPERT_FF_TILE
    tn = EXPERT_OUT_TILE
    nj = ff // tf
    nc = d // tn
    n_sb = sb_start.shape[0]
    rows = EXPERT_SUBS * EXPERT_SUB
    is_real = jnp.logical_and(sb_nsub > 0, sb_zero == 0)[:, None]
    jm = jnp.where(is_real, jnp.arange(nj, dtype=jnp.int32)[None, :], nj - 1).astype(jnp.int32)
    cm = jnp.where(is_real, jnp.arange(nc, dtype=jnp.int32)[None, :], nc - 1).astype(jnp.int32)
    any_spec = pl.BlockSpec(memory_space=pl.ANY)

    up_spec = pltpu.PrefetchScalarGridSpec(
        num_scalar_prefetch=5,
        grid=(n_sb, nj),
        in_specs=[
            any_spec,
            pl.BlockSpec((1, d, tf), lambda s, j, st, ns, zf, se, jm: (se[s], 0, jm[s, j])),
            pl.BlockSpec((1, d, tf), lambda s, j, st, ns, zf, se, jm: (se[s], 0, nj + jm[s, j])),
            pl.BlockSpec((1, 1, tf), lambda s, j, st, ns, zf, se, jm: (se[s], 0, jm[s, j])),
            pl.BlockSpec((1, 1, tf), lambda s, j, st, ns, zf, se, jm: (se[s], 0, nj + jm[s, j])),
        ],
        out_specs=any_spec,
        scratch_shapes=[
            pltpu.VMEM((2, rows, d), BF16),
            pltpu.VMEM((2, rows, tf), BF16),
            pltpu.VMEM((d, tf), BF16),
            pltpu.VMEM((d, tf), BF16),
            pltpu.SemaphoreType.DMA((2,)),
            pltpu.SemaphoreType.DMA((2,)),
        ],
    )
    hidden = pl.pallas_call(
        _ffn_up_kernel,
        grid_spec=up_spec,
        out_shape=jax.ShapeDtypeStruct((nj, n_rows, tf), BF16),
        compiler_params=_params("arbitrary", "arbitrary"),
        name="ffn_up",
    )(sb_start, sb_nsub, sb_zero, sb_expert, jm, x_rows, w_up, w_up,
      b_up.reshape(ne, 1, ff2), b_up.reshape(ne, 1, ff2))

    down_spec = pltpu.PrefetchScalarGridSpec(
        num_scalar_prefetch=5,
        grid=(n_sb, nc),
        in_specs=[
            any_spec,
            pl.BlockSpec((1, ff, tn), lambda s, c, st, ns, zf, se, cm: (se[s], 0, cm[s, c])),
            pl.BlockSpec((1, 1, tn), lambda s, c, st, ns, zf, se, cm: (se[s], 0, cm[s, c])),
        ],
        out_specs=any_spec,
        scratch_shapes=[
            pltpu.VMEM((2, nj, rows, tf), BF16),
            pltpu.VMEM((2, rows, tn), F32),
            pltpu.VMEM((ff, tn), BF16),
            pltpu.SemaphoreType.DMA((2,)),
            pltpu.SemaphoreType.DMA((2,)),
        ],
    )
    return pl.pallas_call(
        _ffn_down_kernel,
        grid_spec=down_spec,
        out_shape=jax.ShapeDtypeStruct((n_rows, d), F32),
        compiler_params=_params("arbitrary", "arbitrary"),
        name="ffn_down",
    )(sb_start, sb_nsub, sb_zero, sb_expert, cm, hidden, w_down, b_down.reshape(ne, 1, d))


def _combine_kernel(pos_ref, pos_next_ref, gate_ref, h1_ref, fw_ref, y_hbm, o_ref, buf_ref, sem):
    i = pl.program_id(0)
    tt = h1_ref.shape[0]

    def start_rows(pos, slot):
        def issue(r, carry):
            for k in range(TOP_K):
                p = pos[0, 0, k * tt + r]
                pltpu.make_async_copy(y_hbm.at[pl.ds(p, 1)], buf_ref.at[slot, k, pl.ds(r, 1)], sem.at[slot]).start()
            return carry

        lax.fori_loop(0, tt, issue, 0, unroll=4)

    @pl.when(i == 0)
    def _():
        start_rows(pos_ref, 0)

    @pl.when(i + 1 < pl.num_programs(0))
    def _():
        start_rows(pos_next_ref, (i + 1) % 2)

    slot = i % 2
    for k in range(TOP_K):
        pltpu.make_async_copy(y_hbm.at[pl.ds(0, tt)], buf_ref.at[slot, k], sem.at[slot]).wait()

    gate = gate_ref[...]
    gpad = jnp.concatenate([gate, jnp.zeros((tt - TOP_K, tt), F32)], axis=0)
    gcol = gpad.T
    acc = h1_ref[...]
    for k in range(TOP_K):
        acc = acc + gcol[:, k:k + 1] * buf_ref[slot, k]
    ms = jnp.mean(acc * acc, axis=-1, keepdims=True)
    o_ref[...] = acc * lax.rsqrt(ms + EPS) * fw_ref[...]


def _combine(y_rows, pos, gate, h1, final_w):
    m, d = h1.shape
    tt = COMBINE_ROWS
    nt = m // tt
    pos_t = pos.reshape(TOP_K, nt, tt).transpose(1, 0, 2).reshape(nt, 1, TOP_K * tt)
    return pl.pallas_call(
        _combine_kernel,
        grid=(nt,),
        in_specs=[
            pl.BlockSpec((1, 1, TOP_K * tt), lambda i: (i, 0, 0), memory_space=pltpu.SMEM),
            pl.BlockSpec((1, 1, TOP_K * tt), lambda i: (jnp.minimum(i + 1, nt - 1), 0, 0), memory_space=pltpu.SMEM),
            pl.BlockSpec((TOP_K, tt), lambda i: (0, i)),
            pl.BlockSpec((tt, d), lambda i: (i, 0)),
            pl.BlockSpec((1, d), lambda i: (0, 0)),
            pl.BlockSpec(memory_space=pl.ANY),
        ],
        out_specs=pl.BlockSpec((tt, d), lambda i: (i, 0)),
        out_shape=jax.ShapeDtypeStruct((m, d), F32),
        scratch_shapes=[pltpu.VMEM((2, TOP_K, tt, d), F32), pltpu.SemaphoreType.DMA((2,))],
        compiler_params=_params("arbitrary"),
        name="combine",
    )(pos_t, pos_t, gate, h1, final_w.reshape(1, d), y_rows)


def _routing_tables(idx, m):
    i32 = jnp.int32
    n_assign = TOP_K * m
    flat_e = idx.reshape(n_assign)
    onehot = flat_e[:, None] == jnp.arange(N_EXPERTS, dtype=i32)[None, :]
    csum = jnp.cumsum(onehot.astype(i32), axis=0)
    rank = jnp.sum(jnp.where(onehot, csum - 1, 0), axis=1)
    counts = csum[-1]
    padded = (counts + EXPERT_SUB - 1) // EXPERT_SUB * EXPERT_SUB
    pad_end = jnp.cumsum(padded)
    pad_start = pad_end - padded
    pos = (jnp.sum(jnp.where(onehot, pad_start[None, :], 0), axis=1) + rank).astype(i32)
    n_sub = -(-(n_assign + N_EXPERTS * (EXPERT_SUB - 1)) // EXPERT_SUB)
    n_rows = n_sub * EXPERT_SUB
    tok = (jnp.arange(n_assign, dtype=i32) % m).astype(i32)
    row_tok = jnp.zeros((n_rows,), i32).at[pos].set(tok)
    used_sub = pad_end[-1] // EXPERT_SUB
    sub_active = (jnp.arange(n_sub, dtype=i32) < used_sub).astype(i32)

    rows = EXPERT_SUBS * EXPERT_SUB
    n_sb = -(-n_sub // EXPERT_SUBS) + N_EXPERTS + 1
    nsb_e = (padded + rows - 1) // rows
    sb_cum = jnp.cumsum(nsb_e)
    total_real = sb_cum[-1]
    s = jnp.arange(n_sb, dtype=i32)
    e_s = jnp.minimum(jnp.sum((sb_cum[None, :] <= s[:, None]).astype(i32), axis=1), N_EXPERTS - 1)
    local = s - (sb_cum[e_s] - nsb_e[e_s])
    real = s < total_real
    start_real = pad_start[e_s] + local * rows
    nsub_real = jnp.clip((padded[e_s] - local * rows) // EXPERT_SUB, 0, EXPERT_SUBS)
    fill_idx = s - total_real
    start_fill = pad_end[-1] + fill_idx * rows
    nsub_fill = jnp.clip(n_sub - used_sub - fill_idx * EXPERT_SUBS, 0, EXPERT_SUBS)
    is_fill = jnp.logical_and(jnp.logical_not(real), nsub_fill > 0)
    sb_start = jnp.where(real, start_real, jnp.where(is_fill, start_fill, 0)).astype(i32)
    sb_nsub = jnp.where(real, nsub_real, jnp.where(is_fill, nsub_fill, 0)).astype(i32)
    sb_zero = is_fill.astype(i32)
    last_e = e_s[jnp.maximum(total_real - 1, 0)]
    sb_expert = jnp.where(real, e_s, last_e).astype(i32)
    return pos, row_tok, sub_active, sb_start, sb_nsub, sb_zero, sb_expert


def _main_tiles(d):
    w = HEADS * HEAD_DIM
    sc = d // 2
    sizes = (w, w, w, w, sc, sc, sc, d, d)
    acts = (1, None, 0, 1, 0, 0, 0, 2, 2)
    order = (0, 2, 3, 4, 5, 6, 7, 8)
    starts = np.concatenate([[0], np.cumsum(sizes)])
    cols, codes = [], []
    for seg in order:
        assert sizes[seg] % COL_TILE == 0 and starts[seg] % COL_TILE == 0
        for t in range(sizes[seg] // COL_TILE):
            cols.append(int(starts[seg]) // COL_TILE + t)
            codes.append(acts[seg])
    return cols, codes, int(starts[1]) // COL_TILE


def _layer(x2d, meta, bsz, seq, norm_mix_w, w_in, lb_logits, g_norm_w, w_hgrn_out, conv_w, w_conv_out, w_o,
           norm_ffn_w, w_router, b_router, w_up, b_up, w_down, b_down, final_norm_w):
    m, d = x2d.shape
    cols, codes, fcol = _main_tiles(d)
    consts = _hgrn_constants(CHUNK)

    xn_meta = _prenorm(meta, norm_mix_w, N_META)
    proj_meta = _inproj(xn_meta, w_in, cols, codes, N_META)
    lf_meta, kk_meta = _fgate(xn_meta, w_in, lb_logits, fcol, N_META)
    pad = CHUNK - N_META
    front = lambda a: jnp.pad(a, ((pad, 0), (0, 0)))
    s_zero = jnp.zeros((HEADS, HEAD_DIM, HEAD_DIM), F32)
    _, s_meta = _hgrn(front(proj_meta), front(lf_meta), front(kk_meta), g_norm_w, s_zero, consts, 1, CHUNK, CHUNK)

    xn = _prenorm(x2d, norm_mix_w, 512)
    proj = _inproj(xn, w_in, cols, codes, 1024)
    lf, kk = _fgate(xn, w_in, lb_logits, fcol, 1024)
    og, _ = _hgrn(proj, lf, kk, g_norm_w, s_meta[0], consts, bsz, seq, HGRN_ROWS)
    h1, xn_ffn, idx, gate = _mixer_out(og, proj, proj_meta, x2d, w_hgrn_out.astype(BF16), w_conv_out.astype(BF16),
                                   w_o.astype(BF16), conv_w, norm_ffn_w, w_router, b_router, seq, 256)

    pos, row_tok, sub_active, sb_start, sb_nsub, sb_zero, sb_expert = _routing_tables(idx, m)
    x_rows = _gather_rows(xn_ffn, row_tok, sub_active)
    y_rows = _experts(x_rows, w_up, b_up, w_down, b_down, sb_start, sb_nsub, sb_zero, sb_expert)
    return _combine(y_rows, pos, gate, h1, final_norm_w)


def kernel(x, meta_tokens, norm_mix_w, w_in, lb_logits, g_norm_w, w_hgrn_out, conv_w, w_conv_out, w_o, norm_ffn_w,
           w_router, b_router, w_up, b_up, w_down, b_down, final_norm_w):
    bsz, seq, d = x.shape
    assert norm_mix_w.shape[0] == 1, "single-layer block"
    out = _layer(x.reshape(bsz * seq, d), meta_tokens.astype(x.dtype), bsz, seq, norm_mix_w[0], w_in[0], lb_logits,
                 g_norm_w[0], w_hgrn_out[0], conv_w[0], w_conv_out[0], w_o[0], norm_ffn_w[0], w_router[0],
                 b_router[0], w_up[0], b_up[0], w_down[0], b_down[0], final_norm_w)
    return out.reshape(bsz, seq, d)
```

```python
import functools

import numpy as np
import jax
import jax.numpy as jnp
from jax import lax
from jax.experimental import pallas as pl
from jax.experimental.pallas import tpu as pltpu

F32 = jnp.float32
BF16 = jnp.bfloat16

LANES = 128
N_META = 16
HEADS = 8
HEAD_DIM = 128
N_EXPERTS = 32
TOP_K = 4
SWIGLU_LIMIT = 7.0
SWIGLU_ALPHA = 1.702
EPS = 1e-6

CHUNK = 256
HGRN_BASE = 64
HGRN_ROWS = 512
COL_TILE = 1024
EXPERT_SUB = 256
EXPERT_SUBS = 9
EXPERT_FF_TILE = 512
EXPERT_OUT_TILE = 512
COMBINE_ROWS = 128

V7X_VMEM_LIMIT = 56 * 1024 * 1024


def _dot(a, b):
    return jnp.dot(a, b, preferred_element_type=F32)


def _dot_bt(a, b):
    return lax.dot_general(a, b, (((1,), (1,)), ((), ())), preferred_element_type=F32)


def _dot_at(a, b):
    return lax.dot_general(a, b, (((0,), (0,)), ((), ())), preferred_element_type=F32)


def _split3(x):
    hi = x.astype(BF16)
    r1 = x - hi.astype(F32)
    mid = r1.astype(BF16)
    lo = (r1 - mid.astype(F32)).astype(BF16)
    return hi, mid, lo


def _params(*sem):
    return pltpu.CompilerParams(dimension_semantics=sem, vmem_limit_bytes=V7X_VMEM_LIMIT)


def _prenorm_kernel(x_ref, w_ref, o_ref):
    x = x_ref[...]
    ms = jnp.mean(x * x, axis=-1, keepdims=True)
    o_ref[...] = (x * lax.rsqrt(ms + EPS) * w_ref[...]).astype(o_ref.dtype)


def _prenorm(x, w, tm):
    m, d = x.shape
    return pl.pallas_call(
        _prenorm_kernel,
        grid=(m // tm,),
        in_specs=[pl.BlockSpec((tm, d), lambda i: (i, 0)), pl.BlockSpec((1, d), lambda i: (0, 0))],
        out_specs=pl.BlockSpec((tm, d), lambda i: (i, 0)),
        out_shape=jax.ShapeDtypeStruct((m, d), BF16),
        compiler_params=_params("arbitrary"),
        name="prenorm",
    )(x, w.reshape(1, d))


def _inproj_kernel(col_ref, code_ref, x_ref, w_ref, o_ref, wb_ref):
    n = pl.program_id(0)

    @pl.when(pl.program_id(1) == 0)
    def _():
        wb_ref[...] = w_ref[...].astype(BF16)

    z = _dot(x_ref[...], wb_ref[...])
    code = code_ref[n]
    s = jax.nn.sigmoid(z)
    o_ref[...] = jnp.where(code == 0, z, jnp.where(code == 1, z * s, s)).astype(o_ref.dtype)


def _inproj(xn, w_in, cols, codes, tm):
    m, d = xn.shape
    nt = len(cols)
    grid_spec = pltpu.PrefetchScalarGridSpec(
        num_scalar_prefetch=2,
        grid=(nt, m // tm),
        in_specs=[
            pl.BlockSpec((tm, d), lambda n, i, col, code: (i, 0)),
            pl.BlockSpec((d, COL_TILE), lambda n, i, col, code: (0, col[n])),
        ],
        out_specs=pl.BlockSpec((tm, COL_TILE), lambda n, i, col, code: (i, n)),
        scratch_shapes=[pltpu.VMEM((d, COL_TILE), BF16)],
    )
    return pl.pallas_call(
        _inproj_kernel,
        grid_spec=grid_spec,
        out_shape=jax.ShapeDtypeStruct((m, nt * COL_TILE), BF16),
        compiler_params=_params("arbitrary", "arbitrary"),
        name="inproj",
    )(jnp.asarray(cols, jnp.int32), jnp.asarray(codes, jnp.int32), xn, w_in)


def _fgate_kernel(x_ref, w_ref, lbl_ref, lf_ref, kk_ref, wb_ref):
    @pl.when(pl.program_id(0) == 0)
    def _():
        wb_ref[...] = w_ref[...].astype(BF16)

    z = _dot(x_ref[...], wb_ref[...])
    lbl = lbl_ref[...]
    e = jnp.exp(lbl - jnp.max(lbl, axis=0, keepdims=True))
    lb = e[0:1] / jnp.sum(e, axis=0, keepdims=True)
    f = lb + (1.0 - lb) * jax.nn.sigmoid(z)
    lf_ref[...] = jnp.log(f)
    kk_ref[...] = (1.0 - lb) * jax.nn.sigmoid(-z)


def _fgate(xn, w_in, lb_logits, col, tm):
    m, d = xn.shape
    r = lb_logits.shape[0]
    out = jax.ShapeDtypeStruct((m, COL_TILE), F32)
    return pl.pallas_call(
        _fgate_kernel,
        grid=(m // tm,),
        in_specs=[
            pl.BlockSpec((tm, d), lambda i: (i, 0)),
            pl.BlockSpec((d, COL_TILE), lambda i: (0, col)),
            pl.BlockSpec((r, COL_TILE), lambda i: (0, 0)),
        ],
        out_specs=[pl.BlockSpec((tm, COL_TILE), lambda i: (i, 0))] * 2,
        out_shape=[out, out],
        scratch_shapes=[pltpu.VMEM((d, COL_TILE), BF16)],
        compiler_params=_params("arbitrary"),
        name="fgate",
    )(xn, w_in, lb_logits)


def _hgrn_constants(c):
    base = HGRN_BASE
    nlow = int(np.log2(base))
    nl = int(np.log2(c))
    assert (1 << nl) == c and (1 << nlow) == base and c >= base
    rr = np.arange(base)[:, None]
    uu = np.arange(base)[None, :]
    mats = [uu <= rr, uu > rr]
    sels = []
    for lvl in range(nlow):
        b = 1 << lvl
        start = (rr // (2 * b)) * (2 * b)
        mid = start + b - 1
        second = (rr - start) >= b
        mats.append(np.where(second, (uu > mid) & (uu <= rr), (uu > rr) & (uu <= mid)))
        sels.append(np.broadcast_to(second, (base, HEADS * HEAD_DIM)))
    tt = np.arange(c)[:, None]
    ss = np.arange(c)[None, :]
    masks = []
    for lvl in range(nl):
        b = 1 << lvl
        masks.append(((tt // (2 * b)) == (ss // (2 * b))) & ((tt % (2 * b)) >= b) & ((ss % (2 * b)) < b))
    masks.append(np.eye(c, dtype=bool))
    m1 = np.concatenate(mats, 0).astype(np.float32)
    mall = jnp.asarray(np.concatenate([m1, m1, m1], axis=1), BF16)
    return mall, jnp.asarray(np.stack(masks).astype(np.float32)), jnp.asarray(np.stack(sels).astype(np.float32))


def _hgrn_kernel(q_ref, v_ref, g_ref, lf_ref, kk_ref, gw_ref, s0_ref, mall_ref, mask_ref, sel_ref,
                 o_ref, sfin_ref, st_ref, *, chunk, n_chunks):
    base = HGRN_BASE
    nb = chunk // base
    nlow = sel_ref.shape[0]
    nl = mask_ref.shape[0] - 1
    step = pl.program_id(1)

    @pl.when(step == 0)
    def _():
        st_ref[...] = s0_ref[...]

    def chunk_body(ci, carry):
        r0 = pl.multiple_of(ci * chunk, chunk)
        rows = pl.ds(r0, chunk)
        qb = q_ref[rows, :]
        q = qb.astype(F32)
        k = kk_ref[rows, :]
        blk = lambda a, i: a[i * base:(i + 1) * base]

        pre, suf, e_low = [], [], []
        for i in range(nb):
            hi, mid, lo = _split3(lf_ref[pl.ds(r0 + i * base, base), :])
            args = _dot(mall_ref[...], jnp.concatenate([hi, mid, lo], axis=0))
            pre.append(args[0:base])
            suf.append(args[base:2 * base])
            e_low.append(jnp.exp(args[2 * base:]))
        tot = [p[base - 1:base] for p in pre]

        def span(lo_blk, hi_blk):
            acc = None
            for j in range(lo_blk, hi_blk):
                acc = tot[j] if acc is None else acc + tot[j]
            return acc

        def shifted(a, off):
            return a if off is None else a + off

        xs = []
        for lvl in range(nlow):
            parts = [jnp.where(sel_ref[lvl] > 0.5, blk(q, i), blk(k, i)) * blk(e_low[i], lvl) for i in range(nb)]
            xs.append(jnp.concatenate(parts, axis=0).astype(BF16))
        for lvl in range(nlow, nl):
            half = (1 << lvl) // base
            parts = []
            for i in range(nb):
                g = i % (2 * half)
                if g >= half:
                    parts.append(blk(q, i) * jnp.exp(shifted(pre[i], span(i - (g - half), i))))
                else:
                    parts.append(blk(k, i) * jnp.exp(shifted(suf[i], span(i + 1, i - g + half))))
            xs.append(jnp.concatenate(parts, axis=0).astype(BF16))
        q_in = jnp.concatenate([blk(q, i) * jnp.exp(shifted(pre[i], span(0, i))) for i in range(nb)],
                               axis=0).astype(BF16)
        k_out = jnp.concatenate([blk(k, i) * jnp.exp(shifted(suf[i], span(i + 1, nb))) for i in range(nb)],
                                axis=0).astype(BF16)
        dec = jnp.exp(span(0, nb))
        kb = k.astype(BF16)

        for h in range(HEADS):
            cs = slice(h * HEAD_DIM, (h + 1) * HEAD_DIM)
            scores = mask_ref[nl] * _dot_bt(qb[:, cs], kb[:, cs])
            for lvl in range(nl):
                x = xs[lvl][:, cs]
                scores = scores + mask_ref[lvl] * _dot_bt(x, x)
            v = v_ref[rows, cs]
            st = st_ref[h]
            o = _dot(scores.astype(BF16), v) + _dot_bt(q_in[:, cs], st.astype(BF16))
            st_ref[h] = st * dec[:, cs] + _dot_at(v, k_out[:, cs])
            ms = jnp.mean(o * o, axis=-1, keepdims=True)
            on = o * lax.rsqrt(ms + EPS) * gw_ref[...]
            o_ref[rows, cs] = (on * g_ref[rows, cs].astype(F32)).astype(o_ref.dtype)
        return carry

    lax.fori_loop(0, n_chunks, chunk_body, 0)

    @pl.when(step == pl.num_programs(1) - 1)
    def _():
        sfin_ref[0] = st_ref[...]


def _hgrn(proj, lf, kk, g_norm_w, s0, consts, bsz, seq, rows):
    mall, masks, sels = consts
    steps = seq // rows
    w = HEADS * HEAD_DIM
    assert w == COL_TILE
    row_map = lambda col: (lambda b, s: (b * steps + s, col))
    const2 = lambda b, s: (0, 0)
    const3 = lambda b, s: (0, 0, 0)
    kern = functools.partial(_hgrn_kernel, chunk=CHUNK, n_chunks=rows // CHUNK)
    return pl.pallas_call(
        kern,
        grid=(bsz, steps),
        in_specs=[
            pl.BlockSpec((rows, w), row_map(0)),
            pl.BlockSpec((rows, w), row_map(1)),
            pl.BlockSpec((rows, w), row_map(2)),
            pl.BlockSpec((rows, w), row_map(0)),
            pl.BlockSpec((rows, w), row_map(0)),
            pl.BlockSpec((1, HEAD_DIM), const2),
            pl.BlockSpec((HEADS, HEAD_DIM, HEAD_DIM), const3),
            pl.BlockSpec(mall.shape, const2),
            pl.BlockSpec(masks.shape, const3),
            pl.BlockSpec(sels.shape, const3),
        ],
        out_specs=[
            pl.BlockSpec((rows, w), row_map(0)),
            pl.BlockSpec((1, HEADS, HEAD_DIM, HEAD_DIM), lambda b, s: (b, 0, 0, 0)),
        ],
        out_shape=[
            jax.ShapeDtypeStruct((bsz * seq, w), BF16),
            jax.ShapeDtypeStruct((bsz, HEADS, HEAD_DIM, HEAD_DIM), F32),
        ],
        scratch_shapes=[pltpu.VMEM((HEADS, HEAD_DIM, HEAD_DIM), F32)],
        compiler_params=_params("arbitrary", "arbitrary"),
        name="hgrn2",
    )(proj, proj, proj, lf, kk, g_norm_w.reshape(1, HEAD_DIM), s0, mall, masks, sels)


def _mixer_out_kernel(og_ref, scv_ref, scb_ref, scc_ref, ga_ref, gb_ref, x_ref,
                      pv_ref, pc_ref, mv_ref, mc_ref,
                      wa_ref, wb_ref, wo_ref, cw_ref, nw_ref, wr_ref, br_ref,
                      h1_ref, xn_ref, idx_ref, gate_ref, *, tiles_per_seq):
    i = pl.program_id(0)
    tm = x_ref.shape[0]
    first = (i % tiles_per_seq) == 0

    u = scc_ref[...].astype(F32) * scv_ref[...].astype(F32)
    halo_prev = pc_ref[...].astype(F32) * pv_ref[...].astype(F32)
    halo_meta = mc_ref[...].astype(F32) * mv_ref[...].astype(F32)
    halo = jnp.where(first, halo_meta, halo_prev)
    hr = halo.shape[0]
    r = lax.broadcasted_iota(jnp.int32, (tm, 1), 0)
    u1 = jnp.where(r == 0, halo[hr - 1:hr], pltpu.roll(u, 1, 0))
    u2 = jnp.where(r == 0, halo[hr - 2:hr - 1], jnp.where(r == 1, halo[hr - 1:hr], pltpu.roll(u, 2, 0)))
    conv = cw_ref[2:3] * u + cw_ref[1:2] * u1 + cw_ref[0:1] * u2
    yb_in = (scb_ref[...].astype(F32) * conv).astype(BF16)

    y_a = _dot(og_ref[...], wa_ref[...])
    y_b = _dot(yb_in, wb_ref[...])
    merged = (ga_ref[...].astype(F32) * y_a + gb_ref[...].astype(F32) * y_b).astype(BF16)
    h1 = x_ref[...] + _dot(merged, wo_ref[...])
    h1_ref[...] = h1

    ms = jnp.mean(h1 * h1, axis=-1, keepdims=True)
    xn = h1 * lax.rsqrt(ms + EPS) * nw_ref[...]
    xn_ref[...] = xn.astype(BF16).reshape(xn_ref.shape)

    xh = xn.astype(BF16)
    xl = (xn - xh.astype(F32)).astype(BF16)
    wr = wr_ref[...]
    wh = wr.astype(BF16)
    wl = (wr - wh.astype(F32)).astype(BF16)
    logits = _dot_bt(wh, xh) + _dot_bt(wh, xl) + _dot_bt(wl, xh) + br_ref[...]
    ne = logits.shape[0]
    ie = lax.broadcasted_iota(jnp.int32, logits.shape, 0)
    tops, idxs = [], []
    for _ in range(TOP_K):
        mx = jnp.max(logits, axis=0, keepdims=True)
        ix = jnp.min(jnp.where(logits == mx, ie, ne), axis=0, keepdims=True)
        tops.append(mx)
        idxs.append(ix)
        logits = jnp.where(ie == ix, -jnp.inf, logits)
    es = [jnp.exp(t - tops[0]) for t in tops]
    den = es[0]
    for e in es[1:]:
        den = den + e
    gate_ref[...] = jnp.concatenate([e / den for e in es], axis=0)
    idx_ref[...] = jnp.concatenate(idxs, axis=0)


def _mixer_out(og, proj, proj_meta, x2d, wa, wb, wo, conv_w, norm_w, w_router, b_router, seq, tm):
    m, d = x2d.shape
    w = COL_TILE
    halo = proj_meta.shape[0]
    assert tm % halo == 0 and seq % tm == 0 and d == 2 * w
    ne = w_router.shape[1]
    per_halo = tm // halo
    row = lambda col: (lambda i: (i, col))
    prev = lambda col: (lambda i: (jnp.maximum(i * per_halo - 1, 0), col))
    const = lambda i: (0, 0)
    whole = lambda a: pl.BlockSpec(a.shape, const)
    wr_t = w_router.T
    kern = functools.partial(_mixer_out_kernel, tiles_per_seq=seq // tm)
    return pl.pallas_call(
        kern,
        grid=(m // tm,),
        in_specs=[
            pl.BlockSpec((tm, w), row(0)),
            pl.BlockSpec((tm, w), row(3)),
            pl.BlockSpec((tm, w), row(4)),
            pl.BlockSpec((tm, w), row(5)),
            pl.BlockSpec((tm, d), row(3)),
            pl.BlockSpec((tm, d), row(4)),
            pl.BlockSpec((tm, d), row(0)),
            pl.BlockSpec((halo, w), prev(3)),
            pl.BlockSpec((halo, w), prev(5)),
            pl.BlockSpec((halo, w), lambda i: (0, 3)),
            pl.BlockSpec((halo, w), lambda i: (0, 5)),
            whole(wa), whole(wb), whole(wo),
            pl.BlockSpec(conv_w.shape, const),
            pl.BlockSpec((1, d), const),
            pl.BlockSpec((ne, d), const),
            pl.BlockSpec((ne, 1), const),
        ],
        out_specs=[
            pl.BlockSpec((tm, d), row(0)),
            pl.BlockSpec((tm, d // LANES, LANES), lambda i: (i, 0, 0)),
            pl.BlockSpec((TOP_K, tm), lambda i: (0, i)),
            pl.BlockSpec((TOP_K, tm), lambda i: (0, i)),
        ],
        out_shape=[
            jax.ShapeDtypeStruct((m, d), F32),
            jax.ShapeDtypeStruct((m, d // LANES, LANES), BF16),
            jax.ShapeDtypeStruct((TOP_K, m), jnp.int32),
            jax.ShapeDtypeStruct((TOP_K, m), F32),
        ],
        compiler_params=_params("arbitrary"),
        name="mixer_out",
    )(og, proj, proj, proj, proj, proj, x2d, proj, proj, proj_meta, proj_meta,
      wa, wb, wo, conv_w, norm_w.reshape(1, d), wr_t, b_router.reshape(ne, 1))


def _dispatch_kernel(cnt_ref, pst_ref, used_ref, pos_ref, x_hbm, o_hbm, zbuf, sem, zsem):
    i = pl.program_id(0)
    n_steps = pl.num_programs(0)
    tt = pos_ref.shape[2] // TOP_K
    n_sub = o_hbm.shape[0] // EXPERT_SUB
    bits = [1 << b for b in reversed(range(EXPERT_SUB.bit_length() - 1))]

    def issue(r, carry):
        t = i * tt + r
        for k in range(TOP_K):
            p = pos_ref[0, 0, k * tt + r]
            pltpu.make_async_copy(x_hbm.at[pl.ds(t, 1)], o_hbm.at[pl.ds(p, 1)], sem.at[i % 2]).start()
        return carry

    lax.fori_loop(0, tt, issue, 0, unroll=4)

    def tokens_done(slot):
        n = TOP_K * tt
        return pltpu.make_async_copy(x_hbm.at[pl.ds(0, n)], o_hbm.at[pl.ds(0, n)], sem.at[slot])

    def zero_copies(fn):
        for e in range(cnt_ref.shape[0]):
            npad = (-cnt_ref[e]) & (EXPERT_SUB - 1)
            base = pst_ref[e] + cnt_ref[e]
            for bit in bits:
                @pl.when((npad & bit) != 0)
                def _():
                    row = base + (npad & ~(2 * bit - 1))
                    fn(pltpu.make_async_copy(zbuf.at[pl.ds(0, bit)], o_hbm.at[pl.ds(row, bit)], zsem))
        for j in range(cnt_ref.shape[0]):
            blk = used_ref[0] + j

            @pl.when(blk < n_sub)
            def _():
                row = pl.multiple_of(blk * EXPERT_SUB, EXPERT_SUB)
                fn(pltpu.make_async_copy(zbuf, o_hbm.at[pl.ds(row, EXPERT_SUB)], zsem))

    @pl.when(i == 0)
    def _():
        zbuf[...] = jnp.zeros_like(zbuf)
        zero_copies(lambda c: c.start())

    @pl.when(i > 0)
    def _():
        tokens_done((i + 1) % 2).wait()

    @pl.when(i == n_steps - 1)
    def _():
        tokens_done(i % 2).wait()
        zero_copies(lambda c: c.wait())


def _dispatch(xn3, pos_t, counts, pad_start, used_sub, n_rows):
    m, sub, lanes = xn3.shape
    nt = pos_t.shape[0]
    any_spec = pl.BlockSpec(memory_space=pl.ANY)
    grid_spec = pltpu.PrefetchScalarGridSpec(
        num_scalar_prefetch=3,
        grid=(nt,),
        in_specs=[
            pl.BlockSpec((1, 1, pos_t.shape[2]), lambda i, c, p, u: (i, 0, 0), memory_space=pltpu.SMEM),
            any_spec,
        ],
        out_specs=any_spec,
        scratch_shapes=[pltpu.VMEM((EXPERT_SUB, sub, lanes), xn3.dtype), pltpu.SemaphoreType.DMA((2,)),
                        pltpu.SemaphoreType.DMA(())],
    )
    return pl.pallas_call(
        _dispatch_kernel,
        grid_spec=grid_spec,
        out_shape=jax.ShapeDtypeStruct((n_rows, sub, lanes), xn3.dtype),
        compiler_params=_params("arbitrary"),
        name="dispatch",
    )(counts, pad_start, used_sub.reshape(1), pos_t, xn3)


def _row_chunks(nsub, chunk_fn):
    pair = 2 * EXPERT_SUB

    def body(c, carry):
        chunk_fn(pl.multiple_of(c * pair, pair), pair)
        return carry

    lax.fori_loop(0, nsub // 2, body, 0)

    @pl.when(nsub % 2 == 1)
    def _():
        chunk_fn(pl.multiple_of((nsub - 1) * EXPERT_SUB, EXPERT_SUB), EXPERT_SUB)


def _ffn_up_kernel(st_ref, ns_ref, zf_ref, se_ref, jm_ref, x_hbm, wg_ref, wu_ref, bg_ref, bu_ref, h_hbm,
                   xstage, xbuf, hbuf, wgb_ref, wub_ref, sem_x, sem_h):
    s = pl.program_id(0)
    j = pl.program_id(1)
    n_s = pl.num_programs(0)
    nj = pl.num_programs(1)
    step = s * nj + j
    subs = xbuf.shape[0] // EXPERT_SUB
    nsub = ns_ref[s]
    real = jnp.logical_and(nsub > 0, zf_ref[s] == 0)
    hs = step % 2

    def x_copy(sb, b):
        row = pl.multiple_of(st_ref[sb] + b * EXPERT_SUB, EXPERT_SUB)
        return pltpu.make_async_copy(x_hbm.at[pl.ds(row, EXPERT_SUB)],
                                     xstage.at[pl.ds(b * EXPERT_SUB, EXPERT_SUB)], sem_x)

    def h_copy(sb, jj, b, slot):
        row = pl.multiple_of(st_ref[sb] + b * EXPERT_SUB, EXPERT_SUB)
        return pltpu.make_async_copy(hbuf.at[slot, pl.ds(b * EXPERT_SUB, EXPERT_SUB)],
                                     h_hbm.at[jj, pl.ds(row, EXPERT_SUB)], sem_h.at[slot])

    def for_x_subs(sb, fn):
        for b in range(subs):
            @pl.when(jnp.logical_and(b < ns_ref[sb], zf_ref[sb] == 0))
            def _():
                fn(b)

    def for_h_subs(sb, fn):
        for b in range(subs):
            @pl.when(b < ns_ref[sb])
            def _():
                fn(b)

    @pl.when(step == 0)
    def _():
        for_x_subs(0, lambda b: x_copy(0, b).start())

    @pl.when(j == 0)
    def _():
        def to_row_major(b):
            rows = slice(b * EXPERT_SUB, (b + 1) * EXPERT_SUB)
            xbuf[rows, :] = xstage[rows].reshape(EXPERT_SUB, xbuf.shape[1])

        for_x_subs(s, lambda b: x_copy(s, b).wait())
        for_x_subs(s, to_row_major)

    @pl.when(jnp.logical_and(j == 0, s + 1 < n_s))
    def _():
        nxt = jnp.minimum(s + 1, n_s - 1)
        for_x_subs(nxt, lambda b: x_copy(nxt, b).start())

    @pl.when(step >= 2)
    def _():
        sp = lax.div(step - 2, nj)
        jp = step - 2 - sp * nj
        for_h_subs(sp, lambda b: h_copy(sp, jp, b, hs).wait())

    @pl.when(real)
    def _():
        wgb_ref[...] = wg_ref[0].astype(BF16)
        wub_ref[...] = wu_ref[0].astype(BF16)

        def chunk(row0, nrows):
            xs = xbuf[pl.ds(row0, nrows), :]
            g = _dot(xs, wgb_ref[...]) + bg_ref[0]
            u = _dot(xs, wub_ref[...]) + bu_ref[0]
            g = jnp.minimum(g, SWIGLU_LIMIT)
            u = jnp.clip(u, -SWIGLU_LIMIT, SWIGLU_LIMIT)
            hbuf[hs, pl.ds(row0, nrows), :] = ((u + 1.0) * (g * jax.nn.sigmoid(SWIGLU_ALPHA * g))).astype(BF16)

        _row_chunks(nsub, chunk)

    @pl.when(zf_ref[s] == 1)
    def _():
        def zero_sub(b):
            hbuf[hs, b * EXPERT_SUB:(b + 1) * EXPERT_SUB, :] = jnp.zeros((EXPERT_SUB, hbuf.shape[2]), hbuf.dtype)

        for_h_subs(s, zero_sub)

    for_h_subs(s, lambda b: h_copy(s, j, b, hs).start())

    @pl.when(step == n_s * nj - 1)
    def _():
        sp = lax.div(step - 1, nj)
        jp = step - 1 - sp * nj
        for_h_subs(sp, lambda b: h_copy(sp, jp, b, 1 - hs).wait())
        for_h_subs(s, lambda b: h_copy(s, j, b, hs).wait())


def _ffn_down_kernel(st_ref, ns_ref, zf_ref, se_ref, cm_ref, h_hbm, wd_ref, bd_ref, y_hbm,
                     hb, ybuf, wdb_ref, sem_h, sem_y):
    s = pl.program_id(0)
    c = pl.program_id(1)
    n_s = pl.num_programs(0)
    nc = pl.num_programs(1)
    step = s * nc + c
    njh = hb.shape[1]
    subs = hb.shape[2] // EXPERT_SUB
    tn = ybuf.shape[2]
    nsub = ns_ref[s]
    real = jnp.logical_and(nsub > 0, zf_ref[s] == 0)
    ys = step % 2

    def h_copy(sb, jj, b, slot):
        row = pl.multiple_of(st_ref[sb] + b * EXPERT_SUB, EXPERT_SUB)
        return pltpu.make_async_copy(h_hbm.at[jj, pl.ds(row, EXPERT_SUB)],
                                     hb.at[slot, jj, pl.ds(b * EXPERT_SUB, EXPERT_SUB)], sem_h.at[slot])

    def y_copy(sb, cc, b, slot):
        row = pl.multiple_of(st_ref[sb] + b * EXPERT_SUB, EXPERT_SUB)
        return pltpu.make_async_copy(ybuf.at[slot, pl.ds(b * EXPERT_SUB, EXPERT_SUB)],
                                     y_hbm.at[pl.ds(row, EXPERT_SUB), pl.ds(cc * tn, tn)], sem_y.at[slot])

    def for_h_subs(sb, fn):
        for b in range(subs):
            @pl.when(jnp.logical_and(b < ns_ref[sb], zf_ref[sb] == 0))
            def _():
                for jj in range(njh):
                    fn(jj, b)

    def for_y_subs(sb, fn):
        for b in range(subs):
            @pl.when(b < ns_ref[sb])
            def _():
                fn(b)

    @pl.when(step == 0)
    def _():
        for_h_subs(0, lambda jj, b: h_copy(0, jj, b, 0).start())

    @pl.when(jnp.logical_and(c == 0, s + 1 < n_s))
    def _():
        nxt = jnp.minimum(s + 1, n_s - 1)
        for_h_subs(nxt, lambda jj, b: h_copy(nxt, jj, b, (s + 1) % 2).start())

    @pl.when(c == 0)
    def _():
        for_h_subs(s, lambda jj, b: h_copy(s, jj, b, s % 2).wait())

    @pl.when(step >= 2)
    def _():
        sp = lax.div(step - 2, nc)
        for_y_subs(sp, lambda b: y_copy(sp, 0, b, ys).wait())

    @pl.when(real)
    def _():
        wdb_ref[...] = wd_ref[0].astype(BF16)
        h_slot = s % 2

        def chunk(row0, nrows):
            hid = jnp.concatenate([hb[h_slot, jj, pl.ds(row0, nrows), :] for jj in range(njh)], axis=1)
            ybuf[ys, pl.ds(row0, nrows), :] = _dot(hid, wdb_ref[...]) + bd_ref[0]

        _row_chunks(nsub, chunk)

    @pl.when(zf_ref[s] == 1)
    def _():
        def zero_sub(b):
            ybuf[ys, b * EXPERT_SUB:(b + 1) * EXPERT_SUB, :] = jnp.zeros((EXPERT_SUB, tn), ybuf.dtype)

        for_y_subs(s, zero_sub)

    for cc in range(y_hbm.shape[1] // tn):
        @pl.when(c == cc)
        def _():
            for_y_subs(s, lambda b: y_copy(s, cc, b, ys).start())

    @pl.when(step == n_s * nc - 1)
    def _():
        sp = lax.div(step - 1, nc)
        for_y_subs(sp, lambda b: y_copy(sp, 0, b, 1 - ys).wait())
        for_y_subs(s, lambda b: y_copy(s, 0, b, ys).wait())


def _experts(x_rows, w_up, b_up, w_down, b_down, sb_start, sb_nsub, sb_zero, sb_expert):
    n_rows, x_sub, x_lanes = x_rows.shape
    d = x_sub * x_lanes
    ne, _, ff2 = w_up.shape
    ff = ff2 // 2
    tf = EXPERT_FF_TILE
    tn = EXPERT_OUT_TILE
    nj = ff // tf
    nc = d // tn
    n_sb = sb_start.shape[0]
    rows = EXPERT_SUBS * EXPERT_SUB
    is_real = jnp.logical_and(sb_nsub > 0, sb_zero == 0)[:, None]
    jm = jnp.where(is_real, jnp.arange(nj, dtype=jnp.int32)[None, :], nj - 1).astype(jnp.int32)
    cm = jnp.where(is_real, jnp.arange(nc, dtype=jnp.int32)[None, :], nc - 1).astype(jnp.int32)
    any_spec = pl.BlockSpec(memory_space=pl.ANY)

    up_spec = pltpu.PrefetchScalarGridSpec(
        num_scalar_prefetch=5,
        grid=(n_sb, nj),
        in_specs=[
            any_spec,
            pl.BlockSpec((1, d, tf), lambda s, j, st, ns, zf, se, jm: (se[s], 0, jm[s, j])),
            pl.BlockSpec((1, d, tf), lambda s, j, st, ns, zf, se, jm: (se[s], 0, nj + jm[s, j])),
            pl.BlockSpec((1, 1, tf), lambda s, j, st, ns, zf, se, jm: (se[s], 0, jm[s, j])),
            pl.BlockSpec((1, 1, tf), lambda s, j, st, ns, zf, se, jm: (se[s], 0, nj + jm[s, j])),
        ],
        out_specs=any_spec,
        scratch_shapes=[
            pltpu.VMEM((rows, x_sub, x_lanes), BF16),
            pltpu.VMEM((rows, d), BF16),
            pltpu.VMEM((2, rows, tf), BF16),
            pltpu.VMEM((d, tf), BF16),
            pltpu.VMEM((d, tf), BF16),
            pltpu.SemaphoreType.DMA(()),
            pltpu.SemaphoreType.DMA((2,)),
        ],
    )
    hidden = pl.pallas_call(
        _ffn_up_kernel,
        grid_spec=up_spec,
        out_shape=jax.ShapeDtypeStruct((nj, n_rows, tf), BF16),
        compiler_params=_params("arbitrary", "arbitrary"),
        name="ffn_up",
    )(sb_start, sb_nsub, sb_zero, sb_expert, jm, x_rows, w_up, w_up,
      b_up.reshape(ne, 1, ff2), b_up.reshape(ne, 1, ff2))

    down_spec = pltpu.PrefetchScalarGridSpec(
        num_scalar_prefetch=5,
        grid=(n_sb, nc),
        in_specs=[
            any_spec,
            pl.BlockSpec((1, ff, tn), lambda s, c, st, ns, zf, se, cm: (se[s], 0, cm[s, c])),
            pl.BlockSpec((1, 1, tn), lambda s, c, st, ns, zf, se, cm: (se[s], 0, cm[s, c])),
        ],
        out_specs=any_spec,
        scratch_shapes=[
            pltpu.VMEM((2, nj, rows, tf), BF16),
            pltpu.VMEM((2, rows, tn), F32),
            pltpu.VMEM((ff, tn), BF16),
            pltpu.SemaphoreType.DMA((2,)),
            pltpu.SemaphoreType.DMA((2,)),
        ],
    )
    return pl.pallas_call(
        _ffn_down_kernel,
        grid_spec=down_spec,
        out_shape=jax.ShapeDtypeStruct((n_rows, d), F32),
        compiler_params=_params("arbitrary", "arbitrary"),
        name="ffn_down",
    )(sb_start, sb_nsub, sb_zero, sb_expert, cm, hidden, w_down, b_down.reshape(ne, 1, d))


def _combine_kernel(pos_ref, pos_next_ref, gate_ref, h1_ref, fw_ref, y_hbm, o_ref, buf_ref, sem):
    i = pl.program_id(0)
    tt = h1_ref.shape[0]

    def start_rows(pos, slot):
        def issue(r, carry):
            for k in range(TOP_K):
                p = pos[0, 0, k * tt + r]
                pltpu.make_async_copy(y_hbm.at[pl.ds(p, 1)], buf_ref.at[slot, k, pl.ds(r, 1)], sem.at[slot]).start()
            return carry

        lax.fori_loop(0, tt, issue, 0, unroll=4)

    @pl.when(i == 0)
    def _():
        start_rows(pos_ref, 0)

    @pl.when(i + 1 < pl.num_programs(0))
    def _():
        start_rows(pos_next_ref, (i + 1) % 2)

    slot = i % 2
    for k in range(TOP_K):
        pltpu.make_async_copy(y_hbm.at[pl.ds(0, tt)], buf_ref.at[slot, k], sem.at[slot]).wait()

    gate = gate_ref[...]
    gpad = jnp.concatenate([gate, jnp.zeros((tt - TOP_K, tt), F32)], axis=0)
    gcol = gpad.T
    acc = h1_ref[...]
    for k in range(TOP_K):
        acc = acc + gcol[:, k:k + 1] * buf_ref[slot, k]
    ms = jnp.mean(acc * acc, axis=-1, keepdims=True)
    o_ref[...] = acc * lax.rsqrt(ms + EPS) * fw_ref[...]


def _pos_tiles(pos, m):
    tt = COMBINE_ROWS
    nt = m // tt
    return pos.reshape(TOP_K, nt, tt).transpose(1, 0, 2).reshape(nt, 1, TOP_K * tt)


def _combine(y_rows, pos_t, gate, h1, final_w):
    m, d = h1.shape
    tt = COMBINE_ROWS
    nt = m // tt
    return pl.pallas_call(
        _combine_kernel,
        grid=(nt,),
        in_specs=[
            pl.BlockSpec((1, 1, TOP_K * tt), lambda i: (i, 0, 0), memory_space=pltpu.SMEM),
            pl.BlockSpec((1, 1, TOP_K * tt), lambda i: (jnp.minimum(i + 1, nt - 1), 0, 0), memory_space=pltpu.SMEM),
            pl.BlockSpec((TOP_K, tt), lambda i: (0, i)),
            pl.BlockSpec((tt, d), lambda i: (i, 0)),
            pl.BlockSpec((1, d), lambda i: (0, 0)),
            pl.BlockSpec(memory_space=pl.ANY),
        ],
        out_specs=pl.BlockSpec((tt, d), lambda i: (i, 0)),
        out_shape=jax.ShapeDtypeStruct((m, d), F32),
        scratch_shapes=[pltpu.VMEM((2, TOP_K, tt, d), F32), pltpu.SemaphoreType.DMA((2,))],
        compiler_params=_params("arbitrary"),
        name="combine",
    )(pos_t, pos_t, gate, h1, final_w.reshape(1, d), y_rows)


def _routing_tables(idx, m):
    i32 = jnp.int32
    n_assign = TOP_K * m
    flat_e = idx.reshape(n_assign)
    onehot = flat_e[:, None] == jnp.arange(N_EXPERTS, dtype=i32)[None, :]
    csum = jnp.cumsum(onehot.astype(i32), axis=0)
    rank = jnp.sum(jnp.where(onehot, csum - 1, 0), axis=1)
    counts = csum[-1]
    padded = (counts + EXPERT_SUB - 1) // EXPERT_SUB * EXPERT_SUB
    pad_end = jnp.cumsum(padded)
    pad_start = pad_end - padded
    pos = (jnp.sum(jnp.where(onehot, pad_start[None, :], 0), axis=1) + rank).astype(i32)
    n_sub = -(-(n_assign + N_EXPERTS * (EXPERT_SUB - 1)) // EXPERT_SUB)
    n_rows = n_sub * EXPERT_SUB
    used_sub = pad_end[-1] // EXPERT_SUB

    rows = EXPERT_SUBS * EXPERT_SUB
    n_sb = -(-n_sub // EXPERT_SUBS) + N_EXPERTS + 1
    nsb_e = (padded + rows - 1) // rows
    sb_cum = jnp.cumsum(nsb_e)
    total_real = sb_cum[-1]
    s = jnp.arange(n_sb, dtype=i32)
    e_s = jnp.minimum(jnp.sum((sb_cum[None, :] <= s[:, None]).astype(i32), axis=1), N_EXPERTS - 1)
    local = s - (sb_cum[e_s] - nsb_e[e_s])
    real = s < total_real
    start_real = pad_start[e_s] + local * rows
    nsub_real = jnp.clip((padded[e_s] - local * rows) // EXPERT_SUB, 0, EXPERT_SUBS)
    fill_idx = s - total_real
    start_fill = pad_end[-1] + fill_idx * rows
    nsub_fill = jnp.clip(n_sub - used_sub - fill_idx * EXPERT_SUBS, 0, EXPERT_SUBS)
    is_fill = jnp.logical_and(jnp.logical_not(real), nsub_fill > 0)
    sb_start = jnp.where(real, start_real, jnp.where(is_fill, start_fill, 0)).astype(i32)
    sb_nsub = jnp.where(real, nsub_real, jnp.where(is_fill, nsub_fill, 0)).astype(i32)
    sb_zero = is_fill.astype(i32)
    last_e = e_s[jnp.maximum(total_real - 1, 0)]
    sb_expert = jnp.where(real, e_s, last_e).astype(i32)
    return pos, counts.astype(i32), pad_start.astype(i32), used_sub.astype(i32), n_rows, sb_start, sb_nsub, sb_zero, sb_expert


def _main_tiles(d):
    w = HEADS * HEAD_DIM
    sc = d // 2
    sizes = (w, w, w, w, sc, sc, sc, d, d)
    acts = (1, None, 0, 1, 0, 0, 0, 2, 2)
    order = (0, 2, 3, 4, 5, 6, 7, 8)
    starts = np.concatenate([[0], np.cumsum(sizes)])
    cols, codes = [], []
    for seg in order:
        assert sizes[seg] % COL_TILE == 0 and starts[seg] % COL_TILE == 0
        for t in range(sizes[seg] // COL_TILE):
            cols.append(int(starts[seg]) // COL_TILE + t)
            codes.append(acts[seg])
    return cols, codes, int(starts[1]) // COL_TILE


def _layer(x2d, meta, bsz, seq, norm_mix_w, w_in, lb_logits, g_norm_w, w_hgrn_out, conv_w, w_conv_out, w_o,
           norm_ffn_w, w_router, b_router, w_up, b_up, w_down, b_down, final_norm_w):
    m, d = x2d.shape
    cols, codes, fcol = _main_tiles(d)
    consts = _hgrn_constants(CHUNK)

    xn_meta = _prenorm(meta, norm_mix_w, N_META)
    proj_meta = _inproj(xn_meta, w_in, cols, codes, N_META)
    lf_meta, kk_meta = _fgate(xn_meta, w_in, lb_logits, fcol, N_META)
    pad = CHUNK - N_META
    front = lambda a: jnp.pad(a, ((pad, 0), (0, 0)))
    s_zero = jnp.zeros((HEADS, HEAD_DIM, HEAD_DIM), F32)
    _, s_meta = _hgrn(front(proj_meta), front(lf_meta), front(kk_meta), g_norm_w, s_zero, consts, 1, CHUNK, CHUNK)

    xn = _prenorm(x2d, norm_mix_w, 512)
    proj = _inproj(xn, w_in, cols, codes, 1024)
    lf, kk = _fgate(xn, w_in, lb_logits, fcol, 1024)
    og, _ = _hgrn(proj, lf, kk, g_norm_w, s_meta[0], consts, bsz, seq, HGRN_ROWS)
    h1, xn_ffn, idx, gate = _mixer_out(og, proj, proj_meta, x2d, w_hgrn_out.astype(BF16), w_conv_out.astype(BF16),
                                   w_o.astype(BF16), conv_w, norm_ffn_w, w_router, b_router, seq, 256)

    pos, counts, pad_start, used_sub, n_rows, sb_start, sb_nsub, sb_zero, sb_expert = _routing_tables(idx, m)
    pos_t = _pos_tiles(pos, m)
    x_rows = _dispatch(xn_ffn, pos_t, counts, pad_start, used_sub, n_rows)
    y_rows = _experts(x_rows, w_up, b_up, w_down, b_down, sb_start, sb_nsub, sb_zero, sb_expert)
    return _combine(y_rows, pos_t, gate, h1, final_norm_w)


def kernel(x, meta_tokens, norm_mix_w, w_in, lb_logits, g_norm_w, w_hgrn_out, conv_w, w_conv_out, w_o, norm_ffn_w,
           w_router, b_router, w_up, b_up, w_down, b_down, final_norm_w):
    bsz, seq, d = x.shape
    assert norm_mix_w.shape[0] == 1, "single-layer block"
    out = _layer(x.reshape(bsz * seq, d), meta_tokens.astype(x.dtype), bsz, seq, norm_mix_w[0], w_in[0], lb_logits,
                 g_norm_w[0], w_hgrn_out[0], conv_w[0], w_conv_out[0], w_o[0], norm_ffn_w[0], w_router[0],
                 b_router[0], w_up[0], b_up[0], w_down[0], b_down[0], final_norm_w)
    return out.reshape(bsz, seq, d)
```

```python
import functools

import numpy as np
import jax
import jax.numpy as jnp
from jax import lax
from jax.experimental import pallas as pl
from jax.experimental.pallas import tpu as pltpu

F32 = jnp.float32
BF16 = jnp.bfloat16

LANES = 128
N_META = 16
HEADS = 8
HEAD_DIM = 128
N_EXPERTS = 32
TOP_K = 4
SWIGLU_LIMIT = 7.0
SWIGLU_ALPHA = 1.702
EPS = 1e-6

CHUNK = 256
HGRN_BASE = 64
HGRN_ROWS = 512
COL_TILE = 1024
EXPERT_SUB = 256
EXPERT_SUBS = 9
EXPERT_FF_TILE = 512
EXPERT_OUT_TILE = 512
COMBINE_ROWS = 128

V7X_VMEM_LIMIT = 56 * 1024 * 1024


def _dot(a, b):
    return jnp.dot(a, b, preferred_element_type=F32)


def _dot_bt(a, b):
    return lax.dot_general(a, b, (((1,), (1,)), ((), ())), preferred_element_type=F32)


def _dot_at(a, b):
    return lax.dot_general(a, b, (((0,), (0,)), ((), ())), preferred_element_type=F32)


def _split3(x):
    hi = x.astype(BF16)
    r1 = x - hi.astype(F32)
    mid = r1.astype(BF16)
    lo = (r1 - mid.astype(F32)).astype(BF16)
    return hi, mid, lo


def _params(*sem):
    return pltpu.CompilerParams(dimension_semantics=sem, vmem_limit_bytes=V7X_VMEM_LIMIT)


def _prenorm_kernel(x_ref, w_ref, o_ref):
    x = x_ref[...]
    ms = jnp.mean(x * x, axis=-1, keepdims=True)
    o_ref[...] = (x * lax.rsqrt(ms + EPS) * w_ref[...]).astype(o_ref.dtype)


def _prenorm(x, w, tm):
    m, d = x.shape
    return pl.pallas_call(
        _prenorm_kernel,
        grid=(m // tm,),
        in_specs=[pl.BlockSpec((tm, d), lambda i: (i, 0)), pl.BlockSpec((1, d), lambda i: (0, 0))],
        out_specs=pl.BlockSpec((tm, d), lambda i: (i, 0)),
        out_shape=jax.ShapeDtypeStruct((m, d), BF16),
        compiler_params=_params("arbitrary"),
        name="prenorm",
    )(x, w.reshape(1, d))


def _inproj_kernel(col_ref, code_ref, x_ref, w_ref, o_ref, wb_ref):
    n = pl.program_id(0)

    @pl.when(pl.program_id(1) == 0)
    def _():
        wb_ref[...] = w_ref[...].astype(BF16)

    z = _dot(x_ref[...], wb_ref[...])
    code = code_ref[n]
    s = jax.nn.sigmoid(z)
    o_ref[...] = jnp.where(code == 0, z, jnp.where(code == 1, z * s, s)).astype(o_ref.dtype)


def _inproj(xn, w_in, cols, codes, tm):
    m, d = xn.shape
    nt = len(cols)
    grid_spec = pltpu.PrefetchScalarGridSpec(
        num_scalar_prefetch=2,
        grid=(nt, m // tm),
        in_specs=[
            pl.BlockSpec((tm, d), lambda n, i, col, code: (i, 0)),
            pl.BlockSpec((d, COL_TILE), lambda n, i, col, code: (0, col[n])),
        ],
        out_specs=pl.BlockSpec((tm, COL_TILE), lambda n, i, col, code: (i, n)),
        scratch_shapes=[pltpu.VMEM((d, COL_TILE), BF16)],
    )
    return pl.pallas_call(
        _inproj_kernel,
        grid_spec=grid_spec,
        out_shape=jax.ShapeDtypeStruct((m, nt * COL_TILE), BF16),
        compiler_params=_params("arbitrary", "arbitrary"),
        name="inproj",
    )(jnp.asarray(cols, jnp.int32), jnp.asarray(codes, jnp.int32), xn, w_in)


def _fgate_kernel(x_ref, w_ref, lbl_ref, lf_ref, kk_ref, wb_ref):
    @pl.when(pl.program_id(0) == 0)
    def _():
        wb_ref[...] = w_ref[...].astype(BF16)

    z = _dot(x_ref[...], wb_ref[...])
    lbl = lbl_ref[...]
    e = jnp.exp(lbl - jnp.max(lbl, axis=0, keepdims=True))
    lb = e[0:1] / jnp.sum(e, axis=0, keepdims=True)
    f = lb + (1.0 - lb) * jax.nn.sigmoid(z)
    lf_ref[...] = jnp.log(f)
    kk_ref[...] = (1.0 - lb) * jax.nn.sigmoid(-z)


def _fgate(xn, w_in, lb_logits, col, tm):
    m, d = xn.shape
    r = lb_logits.shape[0]
    out = jax.ShapeDtypeStruct((m, COL_TILE), F32)
    return pl.pallas_call(
        _fgate_kernel,
        grid=(m // tm,),
        in_specs=[
            pl.BlockSpec((tm, d), lambda i: (i, 0)),
            pl.BlockSpec((d, COL_TILE), lambda i: (0, col)),
            pl.BlockSpec((r, COL_TILE), lambda i: (0, 0)),
        ],
        out_specs=[pl.BlockSpec((tm, COL_TILE), lambda i: (i, 0))] * 2,
        out_shape=[out, out],
        scratch_shapes=[pltpu.VMEM((d, COL_TILE), BF16)],
        compiler_params=_params("arbitrary"),
        name="fgate",
    )(xn, w_in, lb_logits)


def _hgrn_constants(c):
    base = HGRN_BASE
    nlow = int(np.log2(base))
    nl = int(np.log2(c))
    assert (1 << nl) == c and (1 << nlow) == base and c >= base
    rr = np.arange(base)[:, None]
    uu = np.arange(base)[None, :]
    mats = [uu <= rr, uu > rr]
    sels = []
    for lvl in range(nlow):
        b = 1 << lvl
        start = (rr // (2 * b)) * (2 * b)
        mid = start + b - 1
        second = (rr - start) >= b
        mats.append(np.where(second, (uu > mid) & (uu <= rr), (uu > rr) & (uu <= mid)))
        sels.append(np.broadcast_to(second, (base, HEADS * HEAD_DIM)))
    tt = np.arange(c)[:, None]
    ss = np.arange(c)[None, :]
    masks = []
    for lvl in range(nl):
        b = 1 << lvl
        masks.append(((tt // (2 * b)) == (ss // (2 * b))) & ((tt % (2 * b)) >= b) & ((ss % (2 * b)) < b))
    masks.append(np.eye(c, dtype=bool))
    m1 = np.concatenate(mats, 0).astype(np.float32)
    mall = jnp.asarray(np.concatenate([m1, m1, m1], axis=1), BF16)
    return mall, jnp.asarray(np.stack(masks).astype(np.float32)), jnp.asarray(np.stack(sels).astype(np.float32))


def _hgrn_kernel(q_ref, v_ref, g_ref, lf_ref, kk_ref, gw_ref, s0_ref, mall_ref, mask_ref, sel_ref,
                 o_ref, sfin_ref, st_ref, *, chunk, n_chunks):
    base = HGRN_BASE
    nb = chunk // base
    nlow = sel_ref.shape[0]
    nl = mask_ref.shape[0] - 1
    step = pl.program_id(1)

    @pl.when(step == 0)
    def _():
        st_ref[...] = s0_ref[...]

    def chunk_body(ci, carry):
        r0 = pl.multiple_of(ci * chunk, chunk)
        rows = pl.ds(r0, chunk)
        qb = q_ref[rows, :]
        q = qb.astype(F32)
        k = kk_ref[rows, :]
        blk = lambda a, i: a[i * base:(i + 1) * base]

        pre, suf, e_low = [], [], []
        for i in range(nb):
            hi, mid, lo = _split3(lf_ref[pl.ds(r0 + i * base, base), :])
            args = _dot(mall_ref[...], jnp.concatenate([hi, mid, lo], axis=0))
            pre.append(args[0:base])
            suf.append(args[base:2 * base])
            e_low.append(jnp.exp(args[2 * base:]))
        tot = [p[base - 1:base] for p in pre]

        def span(lo_blk, hi_blk):
            acc = None
            for j in range(lo_blk, hi_blk):
                acc = tot[j] if acc is None else acc + tot[j]
            return acc

        def shifted(a, off):
            return a if off is None else a + off

        xs = []
        for lvl in range(nlow):
            parts = [jnp.where(sel_ref[lvl] > 0.5, blk(q, i), blk(k, i)) * blk(e_low[i], lvl) for i in range(nb)]
            xs.append(jnp.concatenate(parts, axis=0).astype(BF16))
        for lvl in range(nlow, nl):
            half = (1 << lvl) // base
            parts = []
            for i in range(nb):
                g = i % (2 * half)
                if g >= half:
                    parts.append(blk(q, i) * jnp.exp(shifted(pre[i], span(i - (g - half), i))))
                else:
                    parts.append(blk(k, i) * jnp.exp(shifted(suf[i], span(i + 1, i - g + half))))
            xs.append(jnp.concatenate(parts, axis=0).astype(BF16))
        q_in = jnp.concatenate([blk(q, i) * jnp.exp(shifted(pre[i], span(0, i))) for i in range(nb)],
                               axis=0).astype(BF16)
        k_out = jnp.concatenate([blk(k, i) * jnp.exp(shifted(suf[i], span(i + 1, nb))) for i in range(nb)],
                                axis=0).astype(BF16)
        dec = jnp.exp(span(0, nb))
        kb = k.astype(BF16)

        for h in range(HEADS):
            cs = slice(h * HEAD_DIM, (h + 1) * HEAD_DIM)
            scores = mask_ref[nl] * _dot_bt(qb[:, cs], kb[:, cs])
            for lvl in range(nl):
                x = xs[lvl][:, cs]
                scores = scores + mask_ref[lvl] * _dot_bt(x, x)
            v = v_ref[rows, cs]
            st = st_ref[h]
            o = _dot(scores.astype(BF16), v) + _dot_bt(q_in[:, cs], st.astype(BF16))
            st_ref[h] = st * dec[:, cs] + _dot_at(v, k_out[:, cs])
            ms = jnp.mean(o * o, axis=-1, keepdims=True)
            on = o * lax.rsqrt(ms + EPS) * gw_ref[...]
            o_ref[rows, cs] = (on * g_ref[rows, cs].astype(F32)).astype(o_ref.dtype)
        return carry

    lax.fori_loop(0, n_chunks, chunk_body, 0)

    @pl.when(step == pl.num_programs(1) - 1)
    def _():
        sfin_ref[0] = st_ref[...]


def _hgrn(proj, lf, kk, g_norm_w, s0, consts, bsz, seq, rows):
    mall, masks, sels = consts
    steps = seq // rows
    w = HEADS * HEAD_DIM
    assert w == COL_TILE
    row_map = lambda col: (lambda b, s: (b * steps + s, col))
    const2 = lambda b, s: (0, 0)
    const3 = lambda b, s: (0, 0, 0)
    kern = functools.partial(_hgrn_kernel, chunk=CHUNK, n_chunks=rows // CHUNK)
    return pl.pallas_call(
        kern,
        grid=(bsz, steps),
        in_specs=[
            pl.BlockSpec((rows, w), row_map(0)),
            pl.BlockSpec((rows, w), row_map(1)),
            pl.BlockSpec((rows, w), row_map(2)),
            pl.BlockSpec((rows, w), row_map(0)),
            pl.BlockSpec((rows, w), row_map(0)),
            pl.BlockSpec((1, HEAD_DIM), const2),
            pl.BlockSpec((HEADS, HEAD_DIM, HEAD_DIM), const3),
            pl.BlockSpec(mall.shape, const2),
            pl.BlockSpec(masks.shape, const3),
            pl.BlockSpec(sels.shape, const3),
        ],
        out_specs=[
            pl.BlockSpec((rows, w), row_map(0)),
            pl.BlockSpec((1, HEADS, HEAD_DIM, HEAD_DIM), lambda b, s: (b, 0, 0, 0)),
        ],
        out_shape=[
            jax.ShapeDtypeStruct((bsz * seq, w), BF16),
            jax.ShapeDtypeStruct((bsz, HEADS, HEAD_DIM, HEAD_DIM), F32),
        ],
        scratch_shapes=[pltpu.VMEM((HEADS, HEAD_DIM, HEAD_DIM), F32)],
        compiler_params=_params("arbitrary", "arbitrary"),
        name="hgrn2",
    )(proj, proj, proj, lf, kk, g_norm_w.reshape(1, HEAD_DIM), s0, mall, masks, sels)


def _mixer_out_kernel(og_ref, scv_ref, scb_ref, scc_ref, ga_ref, gb_ref, x_ref,
                      pv_ref, pc_ref, mv_ref, mc_ref,
                      wa_ref, wb_ref, wo_ref, cw_ref, nw_ref, wr_ref, br_ref,
                      h1_ref, xn_ref, idx_ref, gate_ref, *, tiles_per_seq):
    i = pl.program_id(0)
    tm = x_ref.shape[0]
    first = (i % tiles_per_seq) == 0

    u = scc_ref[...].astype(F32) * scv_ref[...].astype(F32)
    halo_prev = pc_ref[...].astype(F32) * pv_ref[...].astype(F32)
    halo_meta = mc_ref[...].astype(F32) * mv_ref[...].astype(F32)
    halo = jnp.where(first, halo_meta, halo_prev)
    hr = halo.shape[0]
    r = lax.broadcasted_iota(jnp.int32, (tm, 1), 0)
    u1 = jnp.where(r == 0, halo[hr - 1:hr], pltpu.roll(u, 1, 0))
    u2 = jnp.where(r == 0, halo[hr - 2:hr - 1], jnp.where(r == 1, halo[hr - 1:hr], pltpu.roll(u, 2, 0)))
    conv = cw_ref[2:3] * u + cw_ref[1:2] * u1 + cw_ref[0:1] * u2
    yb_in = (scb_ref[...].astype(F32) * conv).astype(BF16)

    y_a = _dot(og_ref[...], wa_ref[...])
    y_b = _dot(yb_in, wb_ref[...])
    merged = (ga_ref[...].astype(F32) * y_a + gb_ref[...].astype(F32) * y_b).astype(BF16)
    h1 = x_ref[...] + _dot(merged, wo_ref[...])
    h1_ref[...] = h1

    ms = jnp.mean(h1 * h1, axis=-1, keepdims=True)
    xn = h1 * lax.rsqrt(ms + EPS) * nw_ref[...]
    xn_ref[...] = xn.astype(BF16).reshape(xn_ref.shape)

    xh = xn.astype(BF16)
    xl = (xn - xh.astype(F32)).astype(BF16)
    wr = wr_ref[...]
    wh = wr.astype(BF16)
    wl = (wr - wh.astype(F32)).astype(BF16)
    logits = _dot_bt(wh, xh) + _dot_bt(wh, xl) + _dot_bt(wl, xh) + br_ref[...]
    ne = logits.shape[0]
    ie = lax.broadcasted_iota(jnp.int32, logits.shape, 0)
    tops, idxs = [], []
    for _ in range(TOP_K):
        mx = jnp.max(logits, axis=0, keepdims=True)
        ix = jnp.min(jnp.where(logits == mx, ie, ne), axis=0, keepdims=True)
        tops.append(mx)
        idxs.append(ix)
        logits = jnp.where(ie == ix, -jnp.inf, logits)
    es = [jnp.exp(t - tops[0]) for t in tops]
    den = es[0]
    for e in es[1:]:
        den = den + e
    gate_ref[...] = jnp.concatenate([e / den for e in es], axis=0)
    idx_ref[...] = jnp.concatenate(idxs, axis=0)


def _mixer_out(og, proj, proj_meta, x2d, wa, wb, wo, conv_w, norm_w, w_router, b_router, seq, tm):
    m, d = x2d.shape
    w = COL_TILE
    halo = proj_meta.shape[0]
    assert tm % halo == 0 and seq % tm == 0 and d == 2 * w
    ne = w_router.shape[1]
    per_halo = tm // halo
    row = lambda col: (lambda i: (i, col))
    prev = lambda col: (lambda i: (jnp.maximum(i * per_halo - 1, 0), col))
    const = lambda i: (0, 0)
    whole = lambda a: pl.BlockSpec(a.shape, const)
    wr_t = w_router.T
    kern = functools.partial(_mixer_out_kernel, tiles_per_seq=seq // tm)
    return pl.pallas_call(
        kern,
        grid=(m // tm,),
        in_specs=[
            pl.BlockSpec((tm, w), row(0)),
            pl.BlockSpec((tm, w), row(3)),
            pl.BlockSpec((tm, w), row(4)),
            pl.BlockSpec((tm, w), row(5)),
            pl.BlockSpec((tm, d), row(3)),
            pl.BlockSpec((tm, d), row(4)),
            pl.BlockSpec((tm, d), row(0)),
            pl.BlockSpec((halo, w), prev(3)),
            pl.BlockSpec((halo, w), prev(5)),
            pl.BlockSpec((halo, w), lambda i: (0, 3)),
            pl.BlockSpec((halo, w), lambda i: (0, 5)),
            whole(wa), whole(wb), whole(wo),
            pl.BlockSpec(conv_w.shape, const),
            pl.BlockSpec((1, d), const),
            pl.BlockSpec((ne, d), const),
            pl.BlockSpec((ne, 1), const),
        ],
        out_specs=[
            pl.BlockSpec((tm, d), row(0)),
            pl.BlockSpec((tm, d // LANES, LANES), lambda i: (i, 0, 0)),
            pl.BlockSpec((TOP_K, tm), lambda i: (0, i)),
            pl.BlockSpec((TOP_K, tm), lambda i: (0, i)),
        ],
        out_shape=[
            jax.ShapeDtypeStruct((m, d), F32),
            jax.ShapeDtypeStruct((m, d // LANES, LANES), BF16),
            jax.ShapeDtypeStruct((TOP_K, m), jnp.int32),
            jax.ShapeDtypeStruct((TOP_K, m), F32),
        ],
        compiler_params=_params("arbitrary"),
        name="mixer_out",
    )(og, proj, proj, proj, proj, proj, x2d, proj, proj, proj_meta, proj_meta,
      wa, wb, wo, conv_w, norm_w.reshape(1, d), wr_t, b_router.reshape(ne, 1))


def _dispatch_kernel(cnt_ref, pst_ref, used_ref, pos_ref, x_ref, o_hbm, xbuf, zbuf, sem, zsem):
    i = pl.program_id(0)
    n_steps = pl.num_programs(0)
    tt = x_ref.shape[0]
    n_sub = o_hbm.shape[0] // EXPERT_SUB
    bits = [1 << b for b in reversed(range(EXPERT_SUB.bit_length() - 1))]
    slot = i % 2

    xbuf[slot] = x_ref[...]

    def issue(r, carry):
        for k in range(TOP_K):
            p = pos_ref[0, 0, k * tt + r]
            pltpu.make_async_copy(xbuf.at[slot, pl.ds(r, 1)], o_hbm.at[pl.ds(p, 1)], sem.at[slot]).start()
        return carry

    lax.fori_loop(0, tt, issue, 0, unroll=4)

    def tokens_done(s):
        return [pltpu.make_async_copy(xbuf.at[s], o_hbm.at[pl.ds(0, tt)], sem.at[s]) for _ in range(TOP_K)]

    def zero_copies(fn):
        for e in range(cnt_ref.shape[0]):
            npad = (-cnt_ref[e]) & (EXPERT_SUB - 1)
            base = pst_ref[e] + cnt_ref[e]
            for bit in bits:
                @pl.when((npad & bit) != 0)
                def _():
                    row = base + (npad & ~(2 * bit - 1))
                    fn(pltpu.make_async_copy(zbuf.at[pl.ds(0, bit)], o_hbm.at[pl.ds(row, bit)], zsem))
        for j in range(cnt_ref.shape[0]):
            blk = used_ref[0] + j

            @pl.when(blk < n_sub)
            def _():
                row = pl.multiple_of(blk * EXPERT_SUB, EXPERT_SUB)
                fn(pltpu.make_async_copy(zbuf, o_hbm.at[pl.ds(row, EXPERT_SUB)], zsem))

    @pl.when(i == 0)
    def _():
        zbuf[...] = jnp.zeros_like(zbuf)
        zero_copies(lambda c: c.start())

    @pl.when(i > 0)
    def _():
        for c in tokens_done(1 - slot):
            c.wait()

    @pl.when(i == n_steps - 1)
    def _():
        for c in tokens_done(slot):
            c.wait()
        zero_copies(lambda c: c.wait())


def _dispatch(xn3, pos_t, counts, pad_start, used_sub, n_rows):
    m, sub, lanes = xn3.shape
    nt = pos_t.shape[0]
    tt = m // nt
    grid_spec = pltpu.PrefetchScalarGridSpec(
        num_scalar_prefetch=3,
        grid=(nt,),
        in_specs=[
            pl.BlockSpec((1, 1, pos_t.shape[2]), lambda i, c, p, u: (i, 0, 0), memory_space=pltpu.SMEM),
            pl.BlockSpec((tt, sub, lanes), lambda i, c, p, u: (i, 0, 0)),
        ],
        out_specs=pl.BlockSpec(memory_space=pl.ANY),
        scratch_shapes=[pltpu.VMEM((2, tt, sub, lanes), xn3.dtype), pltpu.VMEM((EXPERT_SUB, sub, lanes), xn3.dtype),
                        pltpu.SemaphoreType.DMA((2,)), pltpu.SemaphoreType.DMA(())],
    )
    return pl.pallas_call(
        _dispatch_kernel,
        grid_spec=grid_spec,
        out_shape=jax.ShapeDtypeStruct((n_rows, sub, lanes), xn3.dtype),
        compiler_params=_params("arbitrary"),
        name="dispatch",
    )(counts, pad_start, used_sub.reshape(1), pos_t, xn3)


def _row_chunks(nsub, chunk_fn):
    pair = 2 * EXPERT_SUB

    def body(c, carry):
        chunk_fn(pl.multiple_of(c * pair, pair), pair)
        return carry

    lax.fori_loop(0, nsub // 2, body, 0)

    @pl.when(nsub % 2 == 1)
    def _():
        chunk_fn(pl.multiple_of((nsub - 1) * EXPERT_SUB, EXPERT_SUB), EXPERT_SUB)


def _ffn_up_kernel(st_ref, ns_ref, zf_ref, se_ref, jm_ref, x_hbm, wg_ref, wu_ref, bg_ref, bu_ref, h_hbm,
                   xstage, xbuf, hbuf, wgb_ref, wub_ref, sem_x, sem_h):
    s = pl.program_id(0)
    j = pl.program_id(1)
    n_s = pl.num_programs(0)
    nj = pl.num_programs(1)
    step = s * nj + j
    subs = xbuf.shape[0] // EXPERT_SUB
    nsub = ns_ref[s]
    real = jnp.logical_and(nsub > 0, zf_ref[s] == 0)
    hs = step % 2

    def x_copy(sb, b):
        row = pl.multiple_of(st_ref[sb] + b * EXPERT_SUB, EXPERT_SUB)
        return pltpu.make_async_copy(x_hbm.at[pl.ds(row, EXPERT_SUB)],
                                     xstage.at[pl.ds(b * EXPERT_SUB, EXPERT_SUB)], sem_x)

    def h_copy(sb, jj, b, slot):
        row = pl.multiple_of(st_ref[sb] + b * EXPERT_SUB, EXPERT_SUB)
        return pltpu.make_async_copy(hbuf.at[slot, pl.ds(b * EXPERT_SUB, EXPERT_SUB)],
                                     h_hbm.at[jj, pl.ds(row, EXPERT_SUB)], sem_h.at[slot])

    def for_x_subs(sb, fn):
        for b in range(subs):
            @pl.when(jnp.logical_and(b < ns_ref[sb], zf_ref[sb] == 0))
            def _():
                fn(b)

    def for_h_subs(sb, fn):
        for b in range(subs):
            @pl.when(b < ns_ref[sb])
            def _():
                fn(b)

    @pl.when(step == 0)
    def _():
        for_x_subs(0, lambda b: x_copy(0, b).start())

    @pl.when(j == 0)
    def _():
        def to_row_major(b):
            rows = slice(b * EXPERT_SUB, (b + 1) * EXPERT_SUB)
            xbuf[rows, :] = xstage[rows].reshape(EXPERT_SUB, xbuf.shape[1])

        for_x_subs(s, lambda b: x_copy(s, b).wait())
        for_x_subs(s, to_row_major)

    @pl.when(jnp.logical_and(j == 0, s + 1 < n_s))
    def _():
        nxt = jnp.minimum(s + 1, n_s - 1)
        for_x_subs(nxt, lambda b: x_copy(nxt, b).start())

    @pl.when(step >= 2)
    def _():
        sp = lax.div(step - 2, nj)
        jp = step - 2 - sp * nj
        for_h_subs(sp, lambda b: h_copy(sp, jp, b, hs).wait())

    @pl.when(real)
    def _():
        wgb_ref[...] = wg_ref[0].astype(BF16)
        wub_ref[...] = wu_ref[0].astype(BF16)

        def chunk(row0, nrows):
            xs = xbuf[pl.ds(row0, nrows), :]
            g = _dot(xs, wgb_ref[...]) + bg_ref[0]
            u = _dot(xs, wub_ref[...]) + bu_ref[0]
            g = jnp.minimum(g, SWIGLU_LIMIT)
            u = jnp.clip(u, -SWIGLU_LIMIT, SWIGLU_LIMIT)
            hbuf[hs, pl.ds(row0, nrows), :] = ((u + 1.0) * (g * jax.nn.sigmoid(SWIGLU_ALPHA * g))).astype(BF16)

        _row_chunks(nsub, chunk)

    @pl.when(zf_ref[s] == 1)
    def _():
        def zero_sub(b):
            hbuf[hs, b * EXPERT_SUB:(b + 1) * EXPERT_SUB, :] = jnp.zeros((EXPERT_SUB, hbuf.shape[2]), hbuf.dtype)

        for_h_subs(s, zero_sub)

    for_h_subs(s, lambda b: h_copy(s, j, b, hs).start())

    @pl.when(step == n_s * nj - 1)
    def _():
        sp = lax.div(step - 1, nj)
        jp = step - 1 - sp * nj
        for_h_subs(sp, lambda b: h_copy(sp, jp, b, 1 - hs).wait())
        for_h_subs(s, lambda b: h_copy(s, j, b, hs).wait())


def _ffn_down_kernel(st_ref, ns_ref, zf_ref, se_ref, cm_ref, h_hbm, wd_ref, bd_ref, y_hbm,
                     hb, ybuf, wdb_ref, sem_h, sem_y):
    s = pl.program_id(0)
    c = pl.program_id(1)
    n_s = pl.num_programs(0)
    nc = pl.num_programs(1)
    step = s * nc + c
    njh = hb.shape[1]
    subs = hb.shape[2] // EXPERT_SUB
    tn = ybuf.shape[2]
    nsub = ns_ref[s]
    real = jnp.logical_and(nsub > 0, zf_ref[s] == 0)
    ys = step % 2

    def h_copy(sb, jj, b, slot):
        row = pl.multiple_of(st_ref[sb] + b * EXPERT_SUB, EXPERT_SUB)
        return pltpu.make_async_copy(h_hbm.at[jj, pl.ds(row, EXPERT_SUB)],
                                     hb.at[slot, jj, pl.ds(b * EXPERT_SUB, EXPERT_SUB)], sem_h.at[slot])

    def y_copy(sb, cc, b, slot):
        row = pl.multiple_of(st_ref[sb] + b * EXPERT_SUB, EXPERT_SUB)
        return pltpu.make_async_copy(ybuf.at[slot, pl.ds(b * EXPERT_SUB, EXPERT_SUB)],
                                     y_hbm.at[pl.ds(row, EXPERT_SUB), pl.ds(cc * tn, tn)], sem_y.at[slot])

    def for_h_subs(sb, fn):
        for b in range(subs):
            @pl.when(jnp.logical_and(b < ns_ref[sb], zf_ref[sb] == 0))
            def _():
                for jj in range(njh):
                    fn(jj, b)

    def for_y_subs(sb, fn):
        for b in range(subs):
            @pl.when(b < ns_ref[sb])
            def _():
                fn(b)

    @pl.when(step == 0)
    def _():
        for_h_subs(0, lambda jj, b: h_copy(0, jj, b, 0).start())

    @pl.when(jnp.logical_and(c == 0, s + 1 < n_s))
    def _():
        nxt = jnp.minimum(s + 1, n_s - 1)
        for_h_subs(nxt, lambda jj, b: h_copy(nxt, jj, b, (s + 1) % 2).start())

    @pl.when(c == 0)
    def _():
        for_h_subs(s, lambda jj, b: h_copy(s, jj, b, s % 2).wait())

    @pl.when(step >= 2)
    def _():
        sp = lax.div(step - 2, nc)
        for_y_subs(sp, lambda b: y_copy(sp, 0, b, ys).wait())

    @pl.when(real)
    def _():
        wdb_ref[...] = wd_ref[0].astype(BF16)
        h_slot = s % 2

        def chunk(row0, nrows):
            hid = jnp.concatenate([hb[h_slot, jj, pl.ds(row0, nrows), :] for jj in range(njh)], axis=1)
            ybuf[ys, pl.ds(row0, nrows), :] = _dot(hid, wdb_ref[...]) + bd_ref[0]

        _row_chunks(nsub, chunk)

    @pl.when(zf_ref[s] == 1)
    def _():
        def zero_sub(b):
            ybuf[ys, b * EXPERT_SUB:(b + 1) * EXPERT_SUB, :] = jnp.zeros((EXPERT_SUB, tn), ybuf.dtype)

        for_y_subs(s, zero_sub)

    for cc in range(y_hbm.shape[1] // tn):
        @pl.when(c == cc)
        def _():
            for_y_subs(s, lambda b: y_copy(s, cc, b, ys).start())

    @pl.when(step == n_s * nc - 1)
    def _():
        sp = lax.div(step - 1, nc)
        for_y_subs(sp, lambda b: y_copy(sp, 0, b, 1 - ys).wait())
        for_y_subs(s, lambda b: y_copy(s, 0, b, ys).wait())


def _experts(x_rows, w_up, b_up, w_down, b_down, sb_start, sb_nsub, sb_zero, sb_expert):
    n_rows, x_sub, x_lanes = x_rows.shape
    d = x_sub * x_lanes
    ne, _, ff2 = w_up.shape
    ff = ff2 // 2
    tf = EXPERT_FF_TILE
    tn = EXPERT_OUT_TILE
    nj = ff // tf
    nc = d // tn
    n_sb = sb_start.shape[0]
    rows = EXPERT_SUBS * EXPERT_SUB
    is_real = jnp.logical_and(sb_nsub > 0, sb_zero == 0)[:, None]
    jm = jnp.where(is_real, jnp.arange(nj, dtype=jnp.int32)[None, :], nj - 1).astype(jnp.int32)
    cm = jnp.where(is_real, jnp.arange(nc, dtype=jnp.int32)[None, :], nc - 1).astype(jnp.int32)
    any_spec = pl.BlockSpec(memory_space=pl.ANY)

    up_spec = pltpu.PrefetchScalarGridSpec(
        num_scalar_prefetch=5,
        grid=(n_sb, nj),
        in_specs=[
            any_spec,
            pl.BlockSpec((1, d, tf), lambda s, j, st, ns, zf, se, jm: (se[s], 0, jm[s, j])),
            pl.BlockSpec((1, d, tf), lambda s, j, st, ns, zf, se, jm: (se[s], 0, nj + jm[s, j])),
            pl.BlockSpec((1, 1, tf), lambda s, j, st, ns, zf, se, jm: (se[s], 0, jm[s, j])),
            pl.BlockSpec((1, 1, tf), lambda s, j, st, ns, zf, se, jm: (se[s], 0, nj + jm[s, j])),
        ],
        out_specs=any_spec,
        scratch_shapes=[
            pltpu.VMEM((rows, x_sub, x_lanes), BF16),
            pltpu.VMEM((rows, d), BF16),
            pltpu.VMEM((2, rows, tf), BF16),
            pltpu.VMEM((d, tf), BF16),
            pltpu.VMEM((d, tf), BF16),
            pltpu.SemaphoreType.DMA(()),
            pltpu.SemaphoreType.DMA((2,)),
        ],
    )
    hidden = pl.pallas_call(
        _ffn_up_kernel,
        grid_spec=up_spec,
        out_shape=jax.ShapeDtypeStruct((nj, n_rows, tf), BF16),
        compiler_params=_params("arbitrary", "arbitrary"),
        name="ffn_up",
    )(sb_start, sb_nsub, sb_zero, sb_expert, jm, x_rows, w_up, w_up,
      b_up.reshape(ne, 1, ff2), b_up.reshape(ne, 1, ff2))

    down_spec = pltpu.PrefetchScalarGridSpec(
        num_scalar_prefetch=5,
        grid=(n_sb, nc),
        in_specs=[
            any_spec,
            pl.BlockSpec((1, ff, tn), lambda s, c, st, ns, zf, se, cm: (se[s], 0, cm[s, c])),
            pl.BlockSpec((1, 1, tn), lambda s, c, st, ns, zf, se, cm: (se[s], 0, cm[s, c])),
        ],
        out_specs=any_spec,
        scratch_shapes=[
            pltpu.VMEM((2, nj, rows, tf), BF16),
            pltpu.VMEM((2, rows, tn), F32),
            pltpu.VMEM((ff, tn), BF16),
            pltpu.SemaphoreType.DMA((2,)),
            pltpu.SemaphoreType.DMA((2,)),
        ],
    )
    return pl.pallas_call(
        _ffn_down_kernel,
        grid_spec=down_spec,
        out_shape=jax.ShapeDtypeStruct((n_rows, d), F32),
        compiler_params=_params("arbitrary", "arbitrary"),
        name="ffn_down",
    )(sb_start, sb_nsub, sb_zero, sb_expert, cm, hidden, w_down, b_down.reshape(ne, 1, d))


def _combine_kernel(pos_ref, pos_next_ref, gate_ref, h1_ref, fw_ref, y_hbm, o_ref, buf_ref, sem):
    i = pl.program_id(0)
    tt = h1_ref.shape[0]

    def start_rows(pos, slot):
        def issue(r, carry):
            for k in range(TOP_K):
                p = pos[0, 0, k * tt + r]
                pltpu.make_async_copy(y_hbm.at[pl.ds(p, 1)], buf_ref.at[slot, k, pl.ds(r, 1)], sem.at[slot]).start()
            return carry

        lax.fori_loop(0, tt, issue, 0, unroll=4)

    @pl.when(i == 0)
    def _():
        start_rows(pos_ref, 0)

    @pl.when(i + 1 < pl.num_programs(0))
    def _():
        start_rows(pos_next_ref, (i + 1) % 2)

    slot = i % 2
    for k in range(TOP_K):
        pltpu.make_async_copy(y_hbm.at[pl.ds(0, tt)], buf_ref.at[slot, k], sem.at[slot]).wait()

    gate = gate_ref[...]
    gpad = jnp.concatenate([gate, jnp.zeros((tt - TOP_K, tt), F32)], axis=0)
    gcol = gpad.T
    acc = h1_ref[...]
    for k in range(TOP_K):
        acc = acc + gcol[:, k:k + 1] * buf_ref[slot, k]
    ms = jnp.mean(acc * acc, axis=-1, keepdims=True)
    o_ref[...] = acc * lax.rsqrt(ms + EPS) * fw_ref[...]


def _pos_tiles(pos, m):
    tt = COMBINE_ROWS
    nt = m // tt
    return pos.reshape(TOP_K, nt, tt).transpose(1, 0, 2).reshape(nt, 1, TOP_K * tt)


def _combine(y_rows, pos_t, gate, h1, final_w):
    m, d = h1.shape
    tt = COMBINE_ROWS
    nt = m // tt
    return pl.pallas_call(
        _combine_kernel,
        grid=(nt,),
        in_specs=[
            pl.BlockSpec((1, 1, TOP_K * tt), lambda i: (i, 0, 0), memory_space=pltpu.SMEM),
            pl.BlockSpec((1, 1, TOP_K * tt), lambda i: (jnp.minimum(i + 1, nt - 1), 0, 0), memory_space=pltpu.SMEM),
            pl.BlockSpec((TOP_K, tt), lambda i: (0, i)),
            pl.BlockSpec((tt, d), lambda i: (i, 0)),
            pl.BlockSpec((1, d), lambda i: (0, 0)),
            pl.BlockSpec(memory_space=pl.ANY),
        ],
        out_specs=pl.BlockSpec((tt, d), lambda i: (i, 0)),
        out_shape=jax.ShapeDtypeStruct((m, d), F32),
        scratch_shapes=[pltpu.VMEM((2, TOP_K, tt, d), F32), pltpu.SemaphoreType.DMA((2,))],
        compiler_params=_params("arbitrary"),
        name="combine",
    )(pos_t, pos_t, gate, h1, final_w.reshape(1, d), y_rows)


def _routing_tables(idx, m):
    i32 = jnp.int32
    n_assign = TOP_K * m
    flat_e = idx.reshape(n_assign)
    onehot = flat_e[:, None] == jnp.arange(N_EXPERTS, dtype=i32)[None, :]
    csum = jnp.cumsum(onehot.astype(i32), axis=0)
    rank = jnp.sum(jnp.where(onehot, csum - 1, 0), axis=1)
    counts = csum[-1]
    padded = (counts + EXPERT_SUB - 1) // EXPERT_SUB * EXPERT_SUB
    pad_end = jnp.cumsum(padded)
    pad_start = pad_end - padded
    pos = (jnp.sum(jnp.where(onehot, pad_start[None, :], 0), axis=1) + rank).astype(i32)
    n_sub = -(-(n_assign + N_EXPERTS * (EXPERT_SUB - 1)) // EXPERT_SUB)
    n_rows = n_sub * EXPERT_SUB
    used_sub = pad_end[-1] // EXPERT_SUB

    rows = EXPERT_SUBS * EXPERT_SUB
    n_sb = -(-n_sub // EXPERT_SUBS) + N_EXPERTS + 1
    nsb_e = (padded + rows - 1) // rows
    sb_cum = jnp.cumsum(nsb_e)
    total_real = sb_cum[-1]
    s = jnp.arange(n_sb, dtype=i32)
    e_s = jnp.minimum(jnp.sum((sb_cum[None, :] <= s[:, None]).astype(i32), axis=1), N_EXPERTS - 1)
    local = s - (sb_cum[e_s] - nsb_e[e_s])
    real = s < total_real
    start_real = pad_start[e_s] + local * rows
    nsub_real = jnp.clip((padded[e_s] - local * rows) // EXPERT_SUB, 0, EXPERT_SUBS)
    fill_idx = s - total_real
    start_fill = pad_end[-1] + fill_idx * rows
    nsub_fill = jnp.clip(n_sub - used_sub - fill_idx * EXPERT_SUBS, 0, EXPERT_SUBS)
    is_fill = jnp.logical_and(jnp.logical_not(real), nsub_fill > 0)
    sb_start = jnp.where(real, start_real, jnp.where(is_fill, start_fill, 0)).astype(i32)
    sb_nsub = jnp.where(real, nsub_real, jnp.where(is_fill, nsub_fill, 0)).astype(i32)
    sb_zero = is_fill.astype(i32)
    last_e = e_s[jnp.maximum(total_real - 1, 0)]
    sb_expert = jnp.where(real, e_s, last_e).astype(i32)
    return pos, counts.astype(i32), pad_start.astype(i32), used_sub.astype(i32), n_rows, sb_start, sb_nsub, sb_zero, sb_expert


def _main_tiles(d):
    w = HEADS * HEAD_DIM
    sc = d // 2
    sizes = (w, w, w, w, sc, sc, sc, d, d)
    acts = (1, None, 0, 1, 0, 0, 0, 2, 2)
    order = (0, 2, 3, 4, 5, 6, 7, 8)
    starts = np.concatenate([[0], np.cumsum(sizes)])
    cols, codes = [], []
    for seg in order:
        assert sizes[seg] % COL_TILE == 0 and starts[seg] % COL_TILE == 0
        for t in range(sizes[seg] // COL_TILE):
            cols.append(int(starts[seg]) // COL_TILE + t)
            codes.append(acts[seg])
    return cols, codes, int(starts[1]) // COL_TILE


def _layer(x2d, meta, bsz, seq, norm_mix_w, w_in, lb_logits, g_norm_w, w_hgrn_out, conv_w, w_conv_out, w_o,
           norm_ffn_w, w_router, b_router, w_up, b_up, w_down, b_down, final_norm_w):
    m, d = x2d.shape
    cols, codes, fcol = _main_tiles(d)
    consts = _hgrn_constants(CHUNK)

    xn_meta = _prenorm(meta, norm_mix_w, N_META)
    proj_meta = _inproj(xn_meta, w_in, cols, codes, N_META)
    lf_meta, kk_meta = _fgate(xn_meta, w_in, lb_logits, fcol, N_META)
    pad = CHUNK - N_META
    front = lambda a: jnp.pad(a, ((pad, 0), (0, 0)))
    s_zero = jnp.zeros((HEADS, HEAD_DIM, HEAD_DIM), F32)
    _, s_meta = _hgrn(front(proj_meta), front(lf_meta), front(kk_meta), g_norm_w, s_zero, consts, 1, CHUNK, CHUNK)

    xn = _prenorm(x2d, norm_mix_w, 512)
    proj = _inproj(xn, w_in, cols, codes, 1024)
    lf, kk = _fgate(xn, w_in, lb_logits, fcol, 1024)
    og, _ = _hgrn(proj, lf, kk, g_norm_w, s_meta[0], consts, bsz, seq, HGRN_ROWS)
    h1, xn_ffn, idx, gate = _mixer_out(og, proj, proj_meta, x2d, w_hgrn_out.astype(BF16), w_conv_out.astype(BF16),
                                   w_o.astype(BF16), conv_w, norm_ffn_w, w_router, b_router, seq, 256)

    pos, counts, pad_start, used_sub, n_rows, sb_start, sb_nsub, sb_zero, sb_expert = _routing_tables(idx, m)
    pos_t = _pos_tiles(pos, m)
    x_rows = _dispatch(xn_ffn, pos_t, counts, pad_start, used_sub, n_rows)
    y_rows = _experts(x_rows, w_up, b_up, w_down, b_down, sb_start, sb_nsub, sb_zero, sb_expert)
    return _combine(y_rows, pos_t, gate, h1, final_norm_w)


def kernel(x, meta_tokens, norm_mix_w, w_in, lb_logits, g_norm_w, w_hgrn_out, conv_w, w_conv_out, w_o, norm_ffn_w,
           w_router, b_router, w_up, b_up, w_down, b_down, final_norm_w):
    bsz, seq, d = x.shape
    assert norm_mix_w.shape[0] == 1, "single-layer block"
    out = _layer(x.reshape(bsz * seq, d), meta_tokens.astype(x.dtype), bsz, seq, norm_mix_w[0], w_in[0], lb_logits,
                 g_norm_w[0], w_hgrn_out[0], conv_w[0], w_conv_out[0], w_o[0], norm_ffn_w[0], w_router[0],
                 b_router[0], w_up[0], b_up[0], w_down[0], b_down[0], final_norm_w)
    return out.reshape(bsz, seq, d)
```

```python
import functools

import numpy as np
import jax
import jax.numpy as jnp
from jax import lax
from jax.experimental import pallas as pl
from jax.experimental.pallas import tpu as pltpu

F32 = jnp.float32
BF16 = jnp.bfloat16

LANES = 128
N_META = 16
HEADS = 8
HEAD_DIM = 128
N_EXPERTS = 32
TOP_K = 4
SWIGLU_LIMIT = 7.0
SWIGLU_ALPHA = 1.702
EPS = 1e-6

CHUNK = 256
HGRN_BASE = 64
HGRN_ROWS = 512
COL_TILE = 1024
EXPERT_SUB = 128
EXPERT_SUBS = 18
EXPERT_CHUNK_SUBS = 4
EXPERT_FF_TILE = 512
EXPERT_OUT_TILE = 512
COMBINE_ROWS = 128

V7X_VMEM_LIMIT = 56 * 1024 * 1024


def _dot(a, b):
    return jnp.dot(a, b, preferred_element_type=F32)


def _dot_bt(a, b):
    return lax.dot_general(a, b, (((1,), (1,)), ((), ())), preferred_element_type=F32)


def _dot_at(a, b):
    return lax.dot_general(a, b, (((0,), (0,)), ((), ())), preferred_element_type=F32)


def _split3(x):
    hi = x.astype(BF16)
    r1 = x - hi.astype(F32)
    mid = r1.astype(BF16)
    lo = (r1 - mid.astype(F32)).astype(BF16)
    return hi, mid, lo


def _params(*sem):
    return pltpu.CompilerParams(dimension_semantics=sem, vmem_limit_bytes=V7X_VMEM_LIMIT)


def _prenorm_kernel(x_ref, w_ref, o_ref):
    x = x_ref[...]
    ms = jnp.mean(x * x, axis=-1, keepdims=True)
    o_ref[...] = (x * lax.rsqrt(ms + EPS) * w_ref[...]).astype(o_ref.dtype)


def _prenorm(x, w, tm):
    m, d = x.shape
    return pl.pallas_call(
        _prenorm_kernel,
        grid=(m // tm,),
        in_specs=[pl.BlockSpec((tm, d), lambda i: (i, 0)), pl.BlockSpec((1, d), lambda i: (0, 0))],
        out_specs=pl.BlockSpec((tm, d), lambda i: (i, 0)),
        out_shape=jax.ShapeDtypeStruct((m, d), BF16),
        compiler_params=_params("arbitrary"),
        name="prenorm",
    )(x, w.reshape(1, d))


def _inproj_kernel(col_ref, code_ref, x_ref, w_ref, o_ref, wb_ref):
    n = pl.program_id(0)

    @pl.when(pl.program_id(1) == 0)
    def _():
        wb_ref[...] = w_ref[...].astype(BF16)

    z = _dot(x_ref[...], wb_ref[...])
    code = code_ref[n]
    s = jax.nn.sigmoid(z)
    o_ref[...] = jnp.where(code == 0, z, jnp.where(code == 1, z * s, s)).astype(o_ref.dtype)


def _inproj(xn, w_in, cols, codes, tm):
    m, d = xn.shape
    nt = len(cols)
    grid_spec = pltpu.PrefetchScalarGridSpec(
        num_scalar_prefetch=2,
        grid=(nt, m // tm),
        in_specs=[
            pl.BlockSpec((tm, d), lambda n, i, col, code: (i, 0)),
            pl.BlockSpec((d, COL_TILE), lambda n, i, col, code: (0, col[n])),
        ],
        out_specs=pl.BlockSpec((tm, COL_TILE), lambda n, i, col, code: (i, n)),
        scratch_shapes=[pltpu.VMEM((d, COL_TILE), BF16)],
    )
    return pl.pallas_call(
        _inproj_kernel,
        grid_spec=grid_spec,
        out_shape=jax.ShapeDtypeStruct((m, nt * COL_TILE), BF16),
        compiler_params=_params("arbitrary", "arbitrary"),
        name="inproj",
    )(jnp.asarray(cols, jnp.int32), jnp.asarray(codes, jnp.int32), xn, w_in)


def _fgate_kernel(x_ref, w_ref, lbl_ref, lf_ref, kk_ref, wb_ref):
    @pl.when(pl.program_id(0) == 0)
    def _():
        wb_ref[...] = w_ref[...].astype(BF16)

    z = _dot(x_ref[...], wb_ref[...])
    lbl = lbl_ref[...]
    e = jnp.exp(lbl - jnp.max(lbl, axis=0, keepdims=True))
    lb = e[0:1] / jnp.sum(e, axis=0, keepdims=True)
    f = lb + (1.0 - lb) * jax.nn.sigmoid(z)
    lf_ref[...] = jnp.log(f)
    kk_ref[...] = (1.0 - lb) * jax.nn.sigmoid(-z)


def _fgate(xn, w_in, lb_logits, col, tm):
    m, d = xn.shape
    r = lb_logits.shape[0]
    out = jax.ShapeDtypeStruct((m, COL_TILE), F32)
    return pl.pallas_call(
        _fgate_kernel,
        grid=(m // tm,),
        in_specs=[
            pl.BlockSpec((tm, d), lambda i: (i, 0)),
            pl.BlockSpec((d, COL_TILE), lambda i: (0, col)),
            pl.BlockSpec((r, COL_TILE), lambda i: (0, 0)),
        ],
        out_specs=[pl.BlockSpec((tm, COL_TILE), lambda i: (i, 0))] * 2,
        out_shape=[out, out],
        scratch_shapes=[pltpu.VMEM((d, COL_TILE), BF16)],
        compiler_params=_params("arbitrary"),
        name="fgate",
    )(xn, w_in, lb_logits)


def _hgrn_constants(c):
    base = HGRN_BASE
    nlow = int(np.log2(base))
    nl = int(np.log2(c))
    assert (1 << nl) == c and (1 << nlow) == base and c >= base
    rr = np.arange(base)[:, None]
    uu = np.arange(base)[None, :]
    mats = [uu <= rr, uu > rr]
    sels = []
    for lvl in range(nlow):
        b = 1 << lvl
        start = (rr // (2 * b)) * (2 * b)
        mid = start + b - 1
        second = (rr - start) >= b
        mats.append(np.where(second, (uu > mid) & (uu <= rr), (uu > rr) & (uu <= mid)))
        sels.append(np.broadcast_to(second, (base, HEADS * HEAD_DIM)))
    tt = np.arange(c)[:, None]
    ss = np.arange(c)[None, :]
    masks = []
    for lvl in range(nl):
        b = 1 << lvl
        masks.append(((tt // (2 * b)) == (ss // (2 * b))) & ((tt % (2 * b)) >= b) & ((ss % (2 * b)) < b))
    masks.append(np.eye(c, dtype=bool))
    m1 = np.concatenate(mats, 0).astype(np.float32)
    mall = jnp.asarray(np.concatenate([m1, m1, m1], axis=1), BF16)
    return mall, jnp.asarray(np.stack(masks).astype(np.float32)), jnp.asarray(np.stack(sels).astype(np.float32))


def _hgrn_kernel(q_ref, v_ref, g_ref, lf_ref, kk_ref, gw_ref, s0_ref, mall_ref, mask_ref, sel_ref,
                 o_ref, sfin_ref, st_ref, *, chunk, n_chunks):
    base = HGRN_BASE
    nb = chunk // base
    nlow = sel_ref.shape[0]
    nl = mask_ref.shape[0] - 1
    step = pl.program_id(1)

    @pl.when(step == 0)
    def _():
        st_ref[...] = s0_ref[...]

    def chunk_body(ci, carry):
        r0 = pl.multiple_of(ci * chunk, chunk)
        rows = pl.ds(r0, chunk)
        qb = q_ref[rows, :]
        q = qb.astype(F32)
        k = kk_ref[rows, :]
        blk = lambda a, i: a[i * base:(i + 1) * base]

        pre, suf, e_low = [], [], []
        for i in range(nb):
            hi, mid, lo = _split3(lf_ref[pl.ds(r0 + i * base, base), :])
            args = _dot(mall_ref[...], jnp.concatenate([hi, mid, lo], axis=0))
            pre.append(args[0:base])
            suf.append(args[base:2 * base])
            e_low.append(jnp.exp(args[2 * base:]))
        tot = [p[base - 1:base] for p in pre]

        def span(lo_blk, hi_blk):
            acc = None
            for j in range(lo_blk, hi_blk):
                acc = tot[j] if acc is None else acc + tot[j]
            return acc

        def shifted(a, off):
            return a if off is None else a + off

        xs = []
        for lvl in range(nlow):
            parts = [jnp.where(sel_ref[lvl] > 0.5, blk(q, i), blk(k, i)) * blk(e_low[i], lvl) for i in range(nb)]
            xs.append(jnp.concatenate(parts, axis=0).astype(BF16))
        for lvl in range(nlow, nl):
            half = (1 << lvl) // base
            parts = []
            for i in range(nb):
                g = i % (2 * half)
                if g >= half:
                    parts.append(blk(q, i) * jnp.exp(shifted(pre[i], span(i - (g - half), i))))
                else:
                    parts.append(blk(k, i) * jnp.exp(shifted(suf[i], span(i + 1, i - g + half))))
            xs.append(jnp.concatenate(parts, axis=0).astype(BF16))
        q_in = jnp.concatenate([blk(q, i) * jnp.exp(shifted(pre[i], span(0, i))) for i in range(nb)],
                               axis=0).astype(BF16)
        k_out = jnp.concatenate([blk(k, i) * jnp.exp(shifted(suf[i], span(i + 1, nb))) for i in range(nb)],
                                axis=0).astype(BF16)
        dec = jnp.exp(span(0, nb))
        kb = k.astype(BF16)

        for h in range(HEADS):
            cs = slice(h * HEAD_DIM, (h + 1) * HEAD_DIM)
            scores = mask_ref[nl] * _dot_bt(qb[:, cs], kb[:, cs])
            for lvl in range(nl):
                x = xs[lvl][:, cs]
                scores = scores + mask_ref[lvl] * _dot_bt(x, x)
            v = v_ref[rows, cs]
            st = st_ref[h]
            o = _dot(scores.astype(BF16), v) + _dot_bt(q_in[:, cs], st.astype(BF16))
            st_ref[h] = st * dec[:, cs] + _dot_at(v, k_out[:, cs])
            ms = jnp.mean(o * o, axis=-1, keepdims=True)
            on = o * lax.rsqrt(ms + EPS) * gw_ref[...]
            o_ref[rows, cs] = (on * g_ref[rows, cs].astype(F32)).astype(o_ref.dtype)
        return carry

    lax.fori_loop(0, n_chunks, chunk_body, 0)

    @pl.when(step == pl.num_programs(1) - 1)
    def _():
        sfin_ref[0] = st_ref[...]


def _hgrn(proj, lf, kk, g_norm_w, s0, consts, bsz, seq, rows):
    mall, masks, sels = consts
    steps = seq // rows
    w = HEADS * HEAD_DIM
    assert w == COL_TILE
    row_map = lambda col: (lambda b, s: (b * steps + s, col))
    const2 = lambda b, s: (0, 0)
    const3 = lambda b, s: (0, 0, 0)
    kern = functools.partial(_hgrn_kernel, chunk=CHUNK, n_chunks=rows // CHUNK)
    return pl.pallas_call(
        kern,
        grid=(bsz, steps),
        in_specs=[
            pl.BlockSpec((rows, w), row_map(0)),
            pl.BlockSpec((rows, w), row_map(1)),
            pl.BlockSpec((rows, w), row_map(2)),
            pl.BlockSpec((rows, w), row_map(0)),
            pl.BlockSpec((rows, w), row_map(0)),
            pl.BlockSpec((1, HEAD_DIM), const2),
            pl.BlockSpec((HEADS, HEAD_DIM, HEAD_DIM), const3),
            pl.BlockSpec(mall.shape, const2),
            pl.BlockSpec(masks.shape, const3),
            pl.BlockSpec(sels.shape, const3),
        ],
        out_specs=[
            pl.BlockSpec((rows, w), row_map(0)),
            pl.BlockSpec((1, HEADS, HEAD_DIM, HEAD_DIM), lambda b, s: (b, 0, 0, 0)),
        ],
        out_shape=[
            jax.ShapeDtypeStruct((bsz * seq, w), BF16),
            jax.ShapeDtypeStruct((bsz, HEADS, HEAD_DIM, HEAD_DIM), F32),
        ],
        scratch_shapes=[pltpu.VMEM((HEADS, HEAD_DIM, HEAD_DIM), F32)],
        compiler_params=_params("arbitrary", "arbitrary"),
        name="hgrn2",
    )(proj, proj, proj, lf, kk, g_norm_w.reshape(1, HEAD_DIM), s0, mall, masks, sels)


def _mixer_out_kernel(og_ref, scv_ref, scb_ref, scc_ref, ga_ref, gb_ref, x_ref,
                      pv_ref, pc_ref, mv_ref, mc_ref,
                      wa_ref, wb_ref, wo_ref, cw_ref, nw_ref, wr_ref, br_ref,
                      h1_ref, xn_ref, idx_ref, gate_ref, hbuf, *, tiles_per_seq, n_tiles):
    i = pl.program_id(0)
    tm = x_ref.shape[0]
    tile = jnp.minimum(i, n_tiles - 1)
    first = (tile % tiles_per_seq) == 0

    @pl.when(i == 0)
    def _():
        hbuf[1] = jnp.zeros(hbuf.shape[1:], hbuf.dtype)

    hp = hbuf[(i + 1) % 2]
    ms = jnp.mean(hp * hp, axis=-1, keepdims=True)
    xn = hp * lax.rsqrt(ms + EPS) * nw_ref[...]
    xn_ref[...] = xn.astype(BF16).reshape(xn_ref.shape)

    xh = xn.astype(BF16)
    xl = (xn - xh.astype(F32)).astype(BF16)
    wr = wr_ref[...]
    wh = wr.astype(BF16)
    wl = (wr - wh.astype(F32)).astype(BF16)
    logits = _dot_bt(wh, xh) + _dot_bt(wh, xl) + _dot_bt(wl, xh) + br_ref[...]
    ne = logits.shape[0]
    ie = lax.broadcasted_iota(jnp.int32, logits.shape, 0)
    tops, idxs = [], []
    for _ in range(TOP_K):
        mx = jnp.max(logits, axis=0, keepdims=True)
        ix = jnp.min(jnp.where(logits == mx, ie, ne), axis=0, keepdims=True)
        tops.append(mx)
        idxs.append(ix)
        logits = jnp.where(ie == ix, -jnp.inf, logits)
    es = [jnp.exp(t - tops[0]) for t in tops]
    den = es[0]
    for e in es[1:]:
        den = den + e
    gate_ref[...] = jnp.concatenate([e / den for e in es], axis=0)
    idx_ref[...] = jnp.concatenate(idxs, axis=0)

    u = scc_ref[...].astype(F32) * scv_ref[...].astype(F32)
    halo_prev = pc_ref[...].astype(F32) * pv_ref[...].astype(F32)
    halo_meta = mc_ref[...].astype(F32) * mv_ref[...].astype(F32)
    halo = jnp.where(first, halo_meta, halo_prev)
    hr = halo.shape[0]
    r = lax.broadcasted_iota(jnp.int32, (tm, 1), 0)
    u1 = jnp.where(r == 0, halo[hr - 1:hr], pltpu.roll(u, 1, 0))
    u2 = jnp.where(r == 0, halo[hr - 2:hr - 1], jnp.where(r == 1, halo[hr - 1:hr], pltpu.roll(u, 2, 0)))
    conv = cw_ref[2:3] * u + cw_ref[1:2] * u1 + cw_ref[0:1] * u2
    yb_in = (scb_ref[...].astype(F32) * conv).astype(BF16)

    y_a = _dot(og_ref[...], wa_ref[...])
    y_b = _dot(yb_in, wb_ref[...])
    merged = (ga_ref[...].astype(F32) * y_a + gb_ref[...].astype(F32) * y_b).astype(BF16)
    h1 = x_ref[...] + _dot(merged, wo_ref[...])
    h1_ref[...] = h1
    hbuf[i % 2] = h1


def _mixer_out(og, proj, proj_meta, x2d, wa, wb, wo, conv_w, norm_w, w_router, b_router, seq, tm):
    m, d = x2d.shape
    w = COL_TILE
    halo = proj_meta.shape[0]
    assert tm % halo == 0 and seq % tm == 0 and d == 2 * w
    ne = w_router.shape[1]
    per_halo = tm // halo
    n_tiles = m // tm
    cur = lambda i: jnp.minimum(i, n_tiles - 1)
    done = lambda i: jnp.maximum(i - 1, 0)
    row = lambda col: (lambda i: (cur(i), col))
    prev = lambda col: (lambda i: (jnp.maximum(cur(i) * per_halo - 1, 0), col))
    const = lambda i: (0, 0)
    whole = lambda a: pl.BlockSpec(a.shape, const)
    wr_t = w_router.T
    kern = functools.partial(_mixer_out_kernel, tiles_per_seq=seq // tm, n_tiles=n_tiles)
    return pl.pallas_call(
        kern,
        grid=(n_tiles + 1,),
        in_specs=[
            pl.BlockSpec((tm, w), row(0)),
            pl.BlockSpec((tm, w), row(3)),
            pl.BlockSpec((tm, w), row(4)),
            pl.BlockSpec((tm, w), row(5)),
            pl.BlockSpec((tm, d), row(3)),
            pl.BlockSpec((tm, d), row(4)),
            pl.BlockSpec((tm, d), row(0)),
            pl.BlockSpec((halo, w), prev(3)),
            pl.BlockSpec((halo, w), prev(5)),
            pl.BlockSpec((halo, w), lambda i: (0, 3)),
            pl.BlockSpec((halo, w), lambda i: (0, 5)),
            whole(wa), whole(wb), whole(wo),
            pl.BlockSpec(conv_w.shape, const),
            pl.BlockSpec((1, d), const),
            pl.BlockSpec((ne, d), const),
            pl.BlockSpec((ne, 1), const),
        ],
        out_specs=[
            pl.BlockSpec((tm, d), row(0)),
            pl.BlockSpec((tm, d // LANES, LANES), lambda i: (done(i), 0, 0)),
            pl.BlockSpec((TOP_K, tm), lambda i: (0, done(i))),
            pl.BlockSpec((TOP_K, tm), lambda i: (0, done(i))),
        ],
        out_shape=[
            jax.ShapeDtypeStruct((m, d), F32),
            jax.ShapeDtypeStruct((m, d // LANES, LANES), BF16),
            jax.ShapeDtypeStruct((TOP_K, m), jnp.int32),
            jax.ShapeDtypeStruct((TOP_K, m), F32),
        ],
        scratch_shapes=[pltpu.VMEM((2, tm, d), F32)],
        compiler_params=_params("arbitrary"),
        name="mixer_out",
    )(og, proj, proj, proj, proj, proj, x2d, proj, proj, proj_meta, proj_meta,
      wa, wb, wo, conv_w, norm_w.reshape(1, d), wr_t, b_router.reshape(ne, 1))


def _dispatch_kernel(cnt_ref, pst_ref, used_ref, pos_ref, x_ref, o_hbm, xbuf, zbuf, sem, zsem):
    i = pl.program_id(0)
    n_steps = pl.num_programs(0)
    tt = x_ref.shape[0]
    n_sub = o_hbm.shape[0] // EXPERT_SUB
    bits = [1 << b for b in reversed(range(EXPERT_SUB.bit_length() - 1))]
    slot = i % 2

    xbuf[slot] = x_ref[...]

    def issue(r, carry):
        for k in range(TOP_K):
            p = pos_ref[0, 0, k * tt + r]
            pltpu.make_async_copy(xbuf.at[slot, pl.ds(r, 1)], o_hbm.at[pl.ds(p, 1)], sem.at[slot]).start()
        return carry

    lax.fori_loop(0, tt, issue, 0, unroll=4)

    def tokens_done(s):
        return [pltpu.make_async_copy(xbuf.at[s], o_hbm.at[pl.ds(0, tt)], sem.at[s]) for _ in range(TOP_K)]

    def zero_copies(fn):
        for e in range(cnt_ref.shape[0]):
            npad = (-cnt_ref[e]) & (EXPERT_SUB - 1)
            base = pst_ref[e] + cnt_ref[e]
            for bit in bits:
                @pl.when((npad & bit) != 0)
                def _():
                    row = base + (npad & ~(2 * bit - 1))
                    fn(pltpu.make_async_copy(zbuf.at[pl.ds(0, bit)], o_hbm.at[pl.ds(row, bit)], zsem))
        for j in range(cnt_ref.shape[0]):
            blk = used_ref[0] + j

            @pl.when(blk < n_sub)
            def _():
                row = pl.multiple_of(blk * EXPERT_SUB, EXPERT_SUB)
                fn(pltpu.make_async_copy(zbuf, o_hbm.at[pl.ds(row, EXPERT_SUB)], zsem))

    @pl.when(i == 0)
    def _():
        zbuf[...] = jnp.zeros_like(zbuf)
        zero_copies(lambda c: c.start())

    @pl.when(i > 0)
    def _():
        for c in tokens_done(1 - slot):
            c.wait()

    @pl.when(i == n_steps - 1)
    def _():
        for c in tokens_done(slot):
            c.wait()
        zero_copies(lambda c: c.wait())


def _dispatch(xn3, pos_t, counts, pad_start, used_sub, n_rows):
    m, sub, lanes = xn3.shape
    nt = pos_t.shape[0]
    tt = m // nt
    grid_spec = pltpu.PrefetchScalarGridSpec(
        num_scalar_prefetch=3,
        grid=(nt,),
        in_specs=[
            pl.BlockSpec((1, 1, pos_t.shape[2]), lambda i, c, p, u: (i, 0, 0), memory_space=pltpu.SMEM),
            pl.BlockSpec((tt, sub, lanes), lambda i, c, p, u: (i, 0, 0)),
        ],
        out_specs=pl.BlockSpec(memory_space=pl.ANY),
        scratch_shapes=[pltpu.VMEM((2, tt, sub, lanes), xn3.dtype), pltpu.VMEM((EXPERT_SUB, sub, lanes), xn3.dtype),
                        pltpu.SemaphoreType.DMA((2,)), pltpu.SemaphoreType.DMA(())],
    )
    return pl.pallas_call(
        _dispatch_kernel,
        grid_spec=grid_spec,
        out_shape=jax.ShapeDtypeStruct((n_rows, sub, lanes), xn3.dtype),
        compiler_params=_params("arbitrary"),
        name="dispatch",
    )(counts, pad_start, used_sub.reshape(1), pos_t, xn3)


def _row_chunks(nsub, chunk_fn):
    per = EXPERT_CHUNK_SUBS
    big = per * EXPERT_SUB
    n_big = lax.div(nsub, per)

    def body(c, carry):
        chunk_fn(pl.multiple_of(c * big, big), big)
        return carry

    lax.fori_loop(0, n_big, body, 0)
    rem = nsub - n_big * per
    bit = per // 2
    while bit >= 1:
        done = rem & ~(2 * bit - 1)

        @pl.when((rem & bit) != 0)
        def _():
            chunk_fn(pl.multiple_of((n_big * per + done) * EXPERT_SUB, EXPERT_SUB), bit * EXPERT_SUB)

        bit //= 2


def _ffn_up_kernel(st_ref, ns_ref, zf_ref, se_ref, jm_ref, x_hbm, wg_ref, wu_ref, bg_ref, bu_ref, h_hbm,
                   xstage, xbuf, hbuf, wgb_ref, wub_ref, sem_x, sem_h):
    s = pl.program_id(0)
    j = pl.program_id(1)
    n_s = pl.num_programs(0)
    nj = pl.num_programs(1)
    step = s * nj + j
    subs = xbuf.shape[0] // EXPERT_SUB
    nsub = ns_ref[s]
    real = jnp.logical_and(nsub > 0, zf_ref[s] == 0)
    hs = step % 2

    def x_copy(sb, b):
        row = pl.multiple_of(st_ref[sb] + b * EXPERT_SUB, EXPERT_SUB)
        return pltpu.make_async_copy(x_hbm.at[pl.ds(row, EXPERT_SUB)],
                                     xstage.at[pl.ds(b * EXPERT_SUB, EXPERT_SUB)], sem_x)

    def h_copy(sb, jj, b, slot):
        row = pl.multiple_of(st_ref[sb] + b * EXPERT_SUB, EXPERT_SUB)
        return pltpu.make_async_copy(hbuf.at[slot, pl.ds(b * EXPERT_SUB, EXPERT_SUB)],
                                     h_hbm.at[jj, pl.ds(row, EXPERT_SUB)], sem_h.at[slot])

    def for_x_subs(sb, fn):
        for b in range(subs):
            @pl.when(jnp.logical_and(b < ns_ref[sb], zf_ref[sb] == 0))
            def _():
                fn(b)

    def for_h_subs(sb, fn):
        for b in range(subs):
            @pl.when(b < ns_ref[sb])
            def _():
                fn(b)

    @pl.when(step == 0)
    def _():
        for_x_subs(0, lambda b: x_copy(0, b).start())

    @pl.when(j == 0)
    def _():
        def to_row_major(b):
            rows = slice(b * EXPERT_SUB, (b + 1) * EXPERT_SUB)
            xbuf[rows, :] = xstage[rows].reshape(EXPERT_SUB, xbuf.shape[1])

        for_x_subs(s, lambda b: x_copy(s, b).wait())
        for_x_subs(s, to_row_major)

    @pl.when(jnp.logical_and(j == 0, s + 1 < n_s))
    def _():
        nxt = jnp.minimum(s + 1, n_s - 1)
        for_x_subs(nxt, lambda b: x_copy(nxt, b).start())

    @pl.when(step >= 2)
    def _():
        sp = lax.div(step - 2, nj)
        jp = step - 2 - sp * nj
        for_h_subs(sp, lambda b: h_copy(sp, jp, b, hs).wait())

    @pl.when(real)
    def _():
        wgb_ref[...] = wg_ref[0].astype(BF16)
        wub_ref[...] = wu_ref[0].astype(BF16)

        def chunk(row0, nrows):
            xs = xbuf[pl.ds(row0, nrows), :]
            g = _dot(xs, wgb_ref[...]) + bg_ref[0]
            u = _dot(xs, wub_ref[...]) + bu_ref[0]
            g = jnp.minimum(g, SWIGLU_LIMIT)
            u = jnp.clip(u, -SWIGLU_LIMIT, SWIGLU_LIMIT)
            hbuf[hs, pl.ds(row0, nrows), :] = ((u + 1.0) * (g * jax.nn.sigmoid(SWIGLU_ALPHA * g))).astype(BF16)

        _row_chunks(nsub, chunk)

    @pl.when(zf_ref[s] == 1)
    def _():
        def zero_sub(b):
            hbuf[hs, b * EXPERT_SUB:(b + 1) * EXPERT_SUB, :] = jnp.zeros((EXPERT_SUB, hbuf.shape[2]), hbuf.dtype)

        for_h_subs(s, zero_sub)

    for_h_subs(s, lambda b: h_copy(s, j, b, hs).start())

    @pl.when(step == n_s * nj - 1)
    def _():
        sp = lax.div(step - 1, nj)
        jp = step - 1 - sp * nj
        for_h_subs(sp, lambda b: h_copy(sp, jp, b, 1 - hs).wait())
        for_h_subs(s, lambda b: h_copy(s, j, b, hs).wait())


def _ffn_down_kernel(st_ref, ns_ref, zf_ref, se_ref, cm_ref, h_hbm, wd_ref, bd_ref, y_hbm,
                     hb, ybuf, wdb_ref, sem_h, sem_y):
    s = pl.program_id(0)
    c = pl.program_id(1)
    n_s = pl.num_programs(0)
    nc = pl.num_programs(1)
    step = s * nc + c
    njh = hb.shape[1]
    subs = hb.shape[2] // EXPERT_SUB
    tn = ybuf.shape[2]
    nsub = ns_ref[s]
    real = jnp.logical_and(nsub > 0, zf_ref[s] == 0)
    ys = step % 2

    def h_copy(sb, jj, b, slot):
        row = pl.multiple_of(st_ref[sb] + b * EXPERT_SUB, EXPERT_SUB)
        return pltpu.make_async_copy(h_hbm.at[jj, pl.ds(row, EXPERT_SUB)],
                                     hb.at[slot, jj, pl.ds(b * EXPERT_SUB, EXPERT_SUB)], sem_h.at[slot])

    def y_copy(sb, cc, b, slot):
        row = pl.multiple_of(st_ref[sb] + b * EXPERT_SUB, EXPERT_SUB)
        return pltpu.make_async_copy(ybuf.at[slot, pl.ds(b * EXPERT_SUB, EXPERT_SUB)],
                                     y_hbm.at[pl.ds(row, EXPERT_SUB), pl.ds(cc * tn, tn)], sem_y.at[slot])

    def for_h_subs(sb, fn):
        for b in range(subs):
            @pl.when(jnp.logical_and(b < ns_ref[sb], zf_ref[sb] == 0))
            def _():
                for jj in range(njh):
                    fn(jj, b)

    def for_y_subs(sb, fn):
        for b in range(subs):
            @pl.when(b < ns_ref[sb])
            def _():
                fn(b)

    @pl.when(step == 0)
    def _():
        for_h_subs(0, lambda jj, b: h_copy(0, jj, b, 0).start())

    @pl.when(jnp.logical_and(c == 0, s + 1 < n_s))
    def _():
        nxt = jnp.minimum(s + 1, n_s - 1)
        for_h_subs(nxt, lambda jj, b: h_copy(nxt, jj, b, (s + 1) % 2).start())

    @pl.when(c == 0)
    def _():
        for_h_subs(s, lambda jj, b: h_copy(s, jj, b, s % 2).wait())

    @pl.when(step >= 2)
    def _():
        sp = lax.div(step - 2, nc)
        for_y_subs(sp, lambda b: y_copy(sp, 0, b, ys).wait())

    @pl.when(real)
    def _():
        wdb_ref[...] = wd_ref[0].astype(BF16)
        h_slot = s % 2

        def chunk(row0, nrows):
            hid = jnp.concatenate([hb[h_slot, jj, pl.ds(row0, nrows), :] for jj in range(njh)], axis=1)
            ybuf[ys, pl.ds(row0, nrows), :] = _dot(hid, wdb_ref[...]) + bd_ref[0]

        _row_chunks(nsub, chunk)

    @pl.when(zf_ref[s] == 1)
    def _():
        def zero_sub(b):
            ybuf[ys, b * EXPERT_SUB:(b + 1) * EXPERT_SUB, :] = jnp.zeros((EXPERT_SUB, tn), ybuf.dtype)

        for_y_subs(s, zero_sub)

    for cc in range(y_hbm.shape[1] // tn):
        @pl.when(c == cc)
        def _():
            for_y_subs(s, lambda b: y_copy(s, cc, b, ys).start())

    @pl.when(step == n_s * nc - 1)
    def _():
        sp = lax.div(step - 1, nc)
        for_y_subs(sp, lambda b: y_copy(sp, 0, b, 1 - ys).wait())
        for_y_subs(s, lambda b: y_copy(s, 0, b, ys).wait())


def _experts(x_rows, w_up, b_up, w_down, b_down, sb_start, sb_nsub, sb_zero, sb_expert):
    n_rows, x_sub, x_lanes = x_rows.shape
    d = x_sub * x_lanes
    ne, _, ff2 = w_up.shape
    ff = ff2 // 2
    tf = EXPERT_FF_TILE
    tn = EXPERT_OUT_TILE
    nj = ff // tf
    nc = d // tn
    n_sb = sb_start.shape[0]
    rows = EXPERT_SUBS * EXPERT_SUB
    is_real = jnp.logical_and(sb_nsub > 0, sb_zero == 0)[:, None]
    jm = jnp.where(is_real, jnp.arange(nj, dtype=jnp.int32)[None, :], nj - 1).astype(jnp.int32)
    cm = jnp.where(is_real, jnp.arange(nc, dtype=jnp.int32)[None, :], nc - 1).astype(jnp.int32)
    any_spec = pl.BlockSpec(memory_space=pl.ANY)

    up_spec = pltpu.PrefetchScalarGridSpec(
        num_scalar_prefetch=5,
        grid=(n_sb, nj),
        in_specs=[
            any_spec,
            pl.BlockSpec((1, d, tf), lambda s, j, st, ns, zf, se, jm: (se[s], 0, jm[s, j])),
            pl.BlockSpec((1, d, tf), lambda s, j, st, ns, zf, se, jm: (se[s], 0, nj + jm[s, j])),
            pl.BlockSpec((1, 1, tf), lambda s, j, st, ns, zf, se, jm: (se[s], 0, jm[s, j])),
            pl.BlockSpec((1, 1, tf), lambda s, j, st, ns, zf, se, jm: (se[s], 0, nj + jm[s, j])),
        ],
        out_specs=any_spec,
        scratch_shapes=[
            pltpu.VMEM((rows, x_sub, x_lanes), BF16),
            pltpu.VMEM((rows, d), BF16),
            pltpu.VMEM((2, rows, tf), BF16),
            pltpu.VMEM((d, tf), BF16),
            pltpu.VMEM((d, tf), BF16),
            pltpu.SemaphoreType.DMA(()),
            pltpu.SemaphoreType.DMA((2,)),
        ],
    )
    hidden = pl.pallas_call(
        _ffn_up_kernel,
        grid_spec=up_spec,
        out_shape=jax.ShapeDtypeStruct((nj, n_rows, tf), BF16),
        compiler_params=_params("arbitrary", "arbitrary"),
        name="ffn_up",
    )(sb_start, sb_nsub, sb_zero, sb_expert, jm, x_rows, w_up, w_up,
      b_up.reshape(ne, 1, ff2), b_up.reshape(ne, 1, ff2))

    down_spec = pltpu.PrefetchScalarGridSpec(
        num_scalar_prefetch=5,
        grid=(n_sb, nc),
        in_specs=[
            any_spec,
            pl.BlockSpec((1, ff, tn), lambda s, c, st, ns, zf, se, cm: (se[s], 0, cm[s, c])),
            pl.BlockSpec((1, 1, tn), lambda s, c, st, ns, zf, se, cm: (se[s], 0, cm[s, c])),
        ],
        out_specs=any_spec,
        scratch_shapes=[
            pltpu.VMEM((2, nj, rows, tf), BF16),
            pltpu.VMEM((2, rows, tn), F32),
            pltpu.VMEM((ff, tn), BF16),
            pltpu.SemaphoreType.DMA((2,)),
            pltpu.SemaphoreType.DMA((2,)),
        ],
    )
    return pl.pallas_call(
        _ffn_down_kernel,
        grid_spec=down_spec,
        out_shape=jax.ShapeDtypeStruct((n_rows, d), F32),
        compiler_params=_params("arbitrary", "arbitrary"),
        name="ffn_down",
    )(sb_start, sb_nsub, sb_zero, sb_expert, cm, hidden, w_down, b_down.reshape(ne, 1, d))


def _combine_kernel(pos_ref, pos_next_ref, gate_ref, h1_ref, fw_ref, y_hbm, o_ref, buf_ref, sem):
    i = pl.program_id(0)
    tt = h1_ref.shape[0]

    def start_rows(pos, slot):
        def issue(r, carry):
            for k in range(TOP_K):
                p = pos[0, 0, k * tt + r]
                pltpu.make_async_copy(y_hbm.at[pl.ds(p, 1)], buf_ref.at[slot, k, pl.ds(r, 1)], sem.at[slot]).start()
            return carry

        lax.fori_loop(0, tt, issue, 0, unroll=4)

    @pl.when(i == 0)
    def _():
        start_rows(pos_ref, 0)

    @pl.when(i + 1 < pl.num_programs(0))
    def _():
        start_rows(pos_next_ref, (i + 1) % 2)

    slot = i % 2
    for k in range(TOP_K):
        pltpu.make_async_copy(y_hbm.at[pl.ds(0, tt)], buf_ref.at[slot, k], sem.at[slot]).wait()

    gate = gate_ref[...]
    gpad = jnp.concatenate([gate, jnp.zeros((tt - TOP_K, tt), F32)], axis=0)
    gcol = gpad.T
    acc = h1_ref[...]
    for k in range(TOP_K):
        acc = acc + gcol[:, k:k + 1] * buf_ref[slot, k]
    ms = jnp.mean(acc * acc, axis=-1, keepdims=True)
    o_ref[...] = acc * lax.rsqrt(ms + EPS) * fw_ref[...]


def _pos_tiles(pos, m):
    tt = COMBINE_ROWS
    nt = m // tt
    return pos.reshape(TOP_K, nt, tt).transpose(1, 0, 2).reshape(nt, 1, TOP_K * tt)


def _combine(y_rows, pos_t, gate, h1, final_w):
    m, d = h1.shape
    tt = COMBINE_ROWS
    nt = m // tt
    return pl.pallas_call(
        _combine_kernel,
        grid=(nt,),
        in_specs=[
            pl.BlockSpec((1, 1, TOP_K * tt), lambda i: (i, 0, 0), memory_space=pltpu.SMEM),
            pl.BlockSpec((1, 1, TOP_K * tt), lambda i: (jnp.minimum(i + 1, nt - 1), 0, 0), memory_space=pltpu.SMEM),
            pl.BlockSpec((TOP_K, tt), lambda i: (0, i)),
            pl.BlockSpec((tt, d), lambda i: (i, 0)),
            pl.BlockSpec((1, d), lambda i: (0, 0)),
            pl.BlockSpec(memory_space=pl.ANY),
        ],
        out_specs=pl.BlockSpec((tt, d), lambda i: (i, 0)),
        out_shape=jax.ShapeDtypeStruct((m, d), F32),
        scratch_shapes=[pltpu.VMEM((2, TOP_K, tt, d), F32), pltpu.SemaphoreType.DMA((2,))],
        compiler_params=_params("arbitrary"),
        name="combine",
    )(pos_t, pos_t, gate, h1, final_w.reshape(1, d), y_rows)


def _routing_tables(idx, m):
    i32 = jnp.int32
    n_assign = TOP_K * m
    flat_e = idx.reshape(n_assign)
    onehot = flat_e[:, None] == jnp.arange(N_EXPERTS, dtype=i32)[None, :]
    csum = jnp.cumsum(onehot.astype(i32), axis=0)
    rank = jnp.sum(jnp.where(onehot, csum - 1, 0), axis=1)
    counts = csum[-1]
    padded = (counts + EXPERT_SUB - 1) // EXPERT_SUB * EXPERT_SUB
    pad_end = jnp.cumsum(padded)
    pad_start = pad_end - padded
    pos = (jnp.sum(jnp.where(onehot, pad_start[None, :], 0), axis=1) + rank).astype(i32)
    n_sub = -(-(n_assign + N_EXPERTS * (EXPERT_SUB - 1)) // EXPERT_SUB)
    n_rows = n_sub * EXPERT_SUB
    used_sub = pad_end[-1] // EXPERT_SUB

    rows = EXPERT_SUBS * EXPERT_SUB
    n_sb = -(-n_sub // EXPERT_SUBS) + N_EXPERTS + 1
    nsb_e = (padded + rows - 1) // rows
    sb_cum = jnp.cumsum(nsb_e)
    total_real = sb_cum[-1]
    s = jnp.arange(n_sb, dtype=i32)
    e_s = jnp.minimum(jnp.sum((sb_cum[None, :] <= s[:, None]).astype(i32), axis=1), N_EXPERTS - 1)
    local = s - (sb_cum[e_s] - nsb_e[e_s])
    real = s < total_real
    start_real = pad_start[e_s] + local * rows
    nsub_real = jnp.clip((padded[e_s] - local * rows) // EXPERT_SUB, 0, EXPERT_SUBS)
    fill_idx = s - total_real
    start_fill = pad_end[-1] + fill_idx * rows
    nsub_fill = jnp.clip(n_sub - used_sub - fill_idx * EXPERT_SUBS, 0, EXPERT_SUBS)
    is_fill = jnp.logical_and(jnp.logical_not(real), nsub_fill > 0)
    sb_start = jnp.where(real, start_real, jnp.where(is_fill, start_fill, 0)).astype(i32)
    sb_nsub = jnp.where(real, nsub_real, jnp.where(is_fill, nsub_fill, 0)).astype(i32)
    sb_zero = is_fill.astype(i32)
    last_e = e_s[jnp.maximum(total_real - 1, 0)]
    sb_expert = jnp.where(real, e_s, last_e).astype(i32)
    return pos, counts.astype(i32), pad_start.astype(i32), used_sub.astype(i32), n_rows, sb_start, sb_nsub, sb_zero, sb_expert


def _main_tiles(d):
    w = HEADS * HEAD_DIM
    sc = d // 2
    sizes = (w, w, w, w, sc, sc, sc, d, d)
    acts = (1, None, 0, 1, 0, 0, 0, 2, 2)
    order = (0, 2, 3, 4, 5, 6, 7, 8)
    starts = np.concatenate([[0], np.cumsum(sizes)])
    cols, codes = [], []
    for seg in order:
        assert sizes[seg] % COL_TILE == 0 and starts[seg] % COL_TILE == 0
        for t in range(sizes[seg] // COL_TILE):
            cols.append(int(starts[seg]) // COL_TILE + t)
            codes.append(acts[seg])
    return cols, codes, int(starts[1]) // COL_TILE


def _layer(x2d, meta, bsz, seq, norm_mix_w, w_in, lb_logits, g_norm_w, w_hgrn_out, conv_w, w_conv_out, w_o,
           norm_ffn_w, w_router, b_router, w_up, b_up, w_down, b_down, final_norm_w):
    m, d = x2d.shape
    cols, codes, fcol = _main_tiles(d)
    consts = _hgrn_constants(CHUNK)

    xn_meta = _prenorm(meta, norm_mix_w, N_META)
    proj_meta = _inproj(xn_meta, w_in, cols, codes, N_META)
    lf_meta, kk_meta = _fgate(xn_meta, w_in, lb_logits, fcol, N_META)
    pad = CHUNK - N_META
    front = lambda a: jnp.pad(a, ((pad, 0), (0, 0)))
    s_zero = jnp.zeros((HEADS, HEAD_DIM, HEAD_DIM), F32)
    _, s_meta = _hgrn(front(proj_meta), front(lf_meta), front(kk_meta), g_norm_w, s_zero, consts, 1, CHUNK, CHUNK)

    xn = _prenorm(x2d, norm_mix_w, 512)
    proj = _inproj(xn, w_in, cols, codes, 1024)
    lf, kk = _fgate(xn, w_in, lb_logits, fcol, 1024)
    og, _ = _hgrn(proj, lf, kk, g_norm_w, s_meta[0], consts, bsz, seq, HGRN_ROWS)
    h1, xn_ffn, idx, gate = _mixer_out(og, proj, proj_meta, x2d, w_hgrn_out.astype(BF16), w_conv_out.astype(BF16),
                                   w_o.astype(BF16), conv_w, norm_ffn_w, w_router, b_router, seq, 256)

    pos, counts, pad_start, used_sub, n_rows, sb_start, sb_nsub, sb_zero, sb_expert = _routing_tables(idx, m)
    pos_t = _pos_tiles(pos, m)
    x_rows = _dispatch(xn_ffn, pos_t, counts, pad_start, used_sub, n_rows)
    y_rows = _experts(x_rows, w_up, b_up, w_down, b_down, sb_start, sb_nsub, sb_zero, sb_expert)
    return _combine(y_rows, pos_t, gate, h1, final_norm_w)


def kernel(x, meta_tokens, norm_mix_w, w_in, lb_logits, g_norm_w, w_hgrn_out, conv_w, w_conv_out, w_o, norm_ffn_w,
           w_router, b_router, w_up, b_up, w_down, b_down, final_norm_w):
    bsz, seq, d = x.shape
    assert norm_mix_w.shape[0] == 1, "single-layer block"
    out = _layer(x.reshape(bsz * seq, d), meta_tokens.astype(x.dtype), bsz, seq, norm_mix_w[0], w_in[0], lb_logits,
                 g_norm_w[0], w_hgrn_out[0], conv_w[0], w_conv_out[0], w_o[0], norm_ffn_w[0], w_router[0],
                 b_router[0], w_up[0], b_up[0], w_down[0], b_down[0], final_norm_w)
    return out.reshape(bsz, seq, d)
```

```python
import functools

import numpy as np
import jax
import jax.numpy as jnp
from jax import lax
from jax.experimental import pallas as pl
from jax.experimental.pallas import tpu as pltpu

F32 = jnp.float32
BF16 = jnp.bfloat16

LANES = 128
N_META = 16
HEADS = 8
HEAD_DIM = 128
N_EXPERTS = 32
TOP_K = 4
SWIGLU_LIMIT = 7.0
SWIGLU_ALPHA = 1.702
EPS = 1e-6

CHUNK = 256
HGRN_BASE = 64
HGRN_ROWS = 512
COL_TILE = 1024
EXPERT_SUB = 128
EXPERT_SUBS = 18
EXPERT_CHUNK_SUBS = 8
EXPERT_FF_TILE = 512
EXPERT_OUT_TILE = 512
COMBINE_ROWS = 128

V7X_VMEM_LIMIT = 56 * 1024 * 1024


def _dot(a, b):
    return jnp.dot(a, b, preferred_element_type=F32)


def _dot_bt(a, b):
    return lax.dot_general(a, b, (((1,), (1,)), ((), ())), preferred_element_type=F32)


def _dot_at(a, b):
    return lax.dot_general(a, b, (((0,), (0,)), ((), ())), preferred_element_type=F32)


def _split3(x):
    hi = x.astype(BF16)
    r1 = x - hi.astype(F32)
    mid = r1.astype(BF16)
    lo = (r1 - mid.astype(F32)).astype(BF16)
    return hi, mid, lo


def _params(*sem):
    return pltpu.CompilerParams(dimension_semantics=sem, vmem_limit_bytes=V7X_VMEM_LIMIT)


def _prenorm_kernel(x_ref, w_ref, o_ref):
    x = x_ref[...]
    ms = jnp.mean(x * x, axis=-1, keepdims=True)
    o_ref[...] = (x * lax.rsqrt(ms + EPS) * w_ref[...]).astype(o_ref.dtype)


def _prenorm(x, w, tm):
    m, d = x.shape
    return pl.pallas_call(
        _prenorm_kernel,
        grid=(m // tm,),
        in_specs=[pl.BlockSpec((tm, d), lambda i: (i, 0)), pl.BlockSpec((1, d), lambda i: (0, 0))],
        out_specs=pl.BlockSpec((tm, d), lambda i: (i, 0)),
        out_shape=jax.ShapeDtypeStruct((m, d), BF16),
        compiler_params=_params("arbitrary"),
        name="prenorm",
    )(x, w.reshape(1, d))


def _activate(z, code):
    s = jax.nn.sigmoid(z)
    return jnp.where(code == 0, z, jnp.where(code == 1, z * s, s))


def _inproj_kernel(col_ref, code_ref, x_ref, xm_ref, w_ref, o_ref, om_ref, wb_ref):
    code = code_ref[pl.program_id(0)]

    @pl.when(pl.program_id(1) == 0)
    def _():
        wb_ref[...] = w_ref[...].astype(BF16)
        om_ref[...] = _activate(_dot(xm_ref[...], wb_ref[...]), code).astype(om_ref.dtype)

    o_ref[...] = _activate(_dot(x_ref[...], wb_ref[...]), code).astype(o_ref.dtype)


def _inproj(xn, xn_meta, w_in, cols, codes, tm):
    m, d = xn.shape
    mm = xn_meta.shape[0]
    nt = len(cols)
    grid_spec = pltpu.PrefetchScalarGridSpec(
        num_scalar_prefetch=2,
        grid=(nt, m // tm),
        in_specs=[
            pl.BlockSpec((tm, d), lambda n, i, col, code: (i, 0)),
            pl.BlockSpec((mm, d), lambda n, i, col, code: (0, 0)),
            pl.BlockSpec((d, COL_TILE), lambda n, i, col, code: (0, col[n])),
        ],
        out_specs=[
            pl.BlockSpec((tm, COL_TILE), lambda n, i, col, code: (i, n)),
            pl.BlockSpec((mm, COL_TILE), lambda n, i, col, code: (0, n)),
        ],
        scratch_shapes=[pltpu.VMEM((d, COL_TILE), BF16)],
    )
    return pl.pallas_call(
        _inproj_kernel,
        grid_spec=grid_spec,
        out_shape=[jax.ShapeDtypeStruct((m, nt * COL_TILE), BF16), jax.ShapeDtypeStruct((mm, nt * COL_TILE), BF16)],
        compiler_params=_params("arbitrary", "arbitrary"),
        name="inproj",
    )(jnp.asarray(cols, jnp.int32), jnp.asarray(codes, jnp.int32), xn, xn_meta, w_in)


def _fgate_kernel(x_ref, xm_ref, w_ref, lbl_ref, lf_ref, kk_ref, lfm_ref, kkm_ref, wb_ref):
    lbl = lbl_ref[...]
    e = jnp.exp(lbl - jnp.max(lbl, axis=0, keepdims=True))
    lb = e[0:1] / jnp.sum(e, axis=0, keepdims=True)

    def gate(x, lf_out, kk_out):
        z = _dot(x, wb_ref[...])
        lf_out[...] = jnp.log(lb + (1.0 - lb) * jax.nn.sigmoid(z))
        kk_out[...] = (1.0 - lb) * jax.nn.sigmoid(-z)

    @pl.when(pl.program_id(0) == 0)
    def _():
        wb_ref[...] = w_ref[...].astype(BF16)
        gate(xm_ref[...], lfm_ref, kkm_ref)

    gate(x_ref[...], lf_ref, kk_ref)


def _fgate(xn, xn_meta, w_in, lb_logits, col, tm):
    m, d = xn.shape
    mm = xn_meta.shape[0]
    r = lb_logits.shape[0]
    out = jax.ShapeDtypeStruct((m, COL_TILE), F32)
    out_meta = jax.ShapeDtypeStruct((mm, COL_TILE), F32)
    return pl.pallas_call(
        _fgate_kernel,
        grid=(m // tm,),
        in_specs=[
            pl.BlockSpec((tm, d), lambda i: (i, 0)),
            pl.BlockSpec((mm, d), lambda i: (0, 0)),
            pl.BlockSpec((d, COL_TILE), lambda i: (0, col)),
            pl.BlockSpec((r, COL_TILE), lambda i: (0, 0)),
        ],
        out_specs=[pl.BlockSpec((tm, COL_TILE), lambda i: (i, 0))] * 2
        + [pl.BlockSpec((mm, COL_TILE), lambda i: (0, 0))] * 2,
        out_shape=[out, out, out_meta, out_meta],
        scratch_shapes=[pltpu.VMEM((d, COL_TILE), BF16)],
        compiler_params=_params("arbitrary"),
        name="fgate",
    )(xn, xn_meta, w_in, lb_logits)


def _hgrn_constants(c):
    base = HGRN_BASE
    nlow = int(np.log2(base))
    nl = int(np.log2(c))
    assert (1 << nl) == c and (1 << nlow) == base and c >= base
    rr = np.arange(base)[:, None]
    uu = np.arange(base)[None, :]
    mats = [uu <= rr, uu > rr]
    sels = []
    for lvl in range(nlow):
        b = 1 << lvl
        start = (rr // (2 * b)) * (2 * b)
        mid = start + b - 1
        second = (rr - start) >= b
        mats.append(np.where(second, (uu > mid) & (uu <= rr), (uu > rr) & (uu <= mid)))
        sels.append(np.broadcast_to(second, (base, HEADS * HEAD_DIM)))
    tt = np.arange(c)[:, None]
    ss = np.arange(c)[None, :]
    masks = []
    for lvl in range(nl):
        b = 1 << lvl
        masks.append(((tt // (2 * b)) == (ss // (2 * b))) & ((tt % (2 * b)) >= b) & ((ss % (2 * b)) < b))
    masks.append(np.eye(c, dtype=bool))
    m1 = np.concatenate(mats, 0).astype(np.float32)
    mall = jnp.asarray(np.concatenate([m1, m1, m1], axis=1), BF16)
    return mall, jnp.asarray(np.stack(masks).astype(np.float32)), jnp.asarray(np.stack(sels).astype(np.float32))


def _hgrn_kernel(q_ref, v_ref, g_ref, lf_ref, kk_ref, gw_ref, s0_ref, mall_ref, mask_ref, sel_ref,
                 o_ref, sfin_ref, st_ref, *, chunk, n_chunks):
    base = HGRN_BASE
    nb = chunk // base
    nlow = sel_ref.shape[0]
    nl = mask_ref.shape[0] - 1
    step = pl.program_id(1)

    @pl.when(step == 0)
    def _():
        st_ref[...] = s0_ref[...]

    def chunk_body(ci, carry):
        r0 = pl.multiple_of(ci * chunk, chunk)
        rows = pl.ds(r0, chunk)
        qb = q_ref[rows, :]
        q = qb.astype(F32)
        k = kk_ref[rows, :]
        blk = lambda a, i: a[i * base:(i + 1) * base]

        pre, suf, e_low = [], [], []
        for i in range(nb):
            hi, mid, lo = _split3(lf_ref[pl.ds(r0 + i * base, base), :])
            args = _dot(mall_ref[...], jnp.concatenate([hi, mid, lo], axis=0))
            pre.append(args[0:base])
            suf.append(args[base:2 * base])
            e_low.append(jnp.exp(args[2 * base:]))
        tot = [p[base - 1:base] for p in pre]

        def span(lo_blk, hi_blk):
            acc = None
            for j in range(lo_blk, hi_blk):
                acc = tot[j] if acc is None else acc + tot[j]
            return acc

        def shifted(a, off):
            return a if off is None else a + off

        xs = []
        for lvl in range(nlow):
            parts = [jnp.where(sel_ref[lvl] > 0.5, blk(q, i), blk(k, i)) * blk(e_low[i], lvl) for i in range(nb)]
            xs.append(jnp.concatenate(parts, axis=0).astype(BF16))
        for lvl in range(nlow, nl):
            half = (1 << lvl) // base
            parts = []
            for i in range(nb):
                g = i % (2 * half)
                if g >= half:
                    parts.append(blk(q, i) * jnp.exp(shifted(pre[i], span(i - (g - half), i))))
                else:
                    parts.append(blk(k, i) * jnp.exp(shifted(suf[i], span(i + 1, i - g + half))))
            xs.append(jnp.concatenate(parts, axis=0).astype(BF16))
        q_in = jnp.concatenate([blk(q, i) * jnp.exp(shifted(pre[i], span(0, i))) for i in range(nb)],
                               axis=0).astype(BF16)
        k_out = jnp.concatenate([blk(k, i) * jnp.exp(shifted(suf[i], span(i + 1, nb))) for i in range(nb)],
                                axis=0).astype(BF16)
        dec = jnp.exp(span(0, nb))
        kb = k.astype(BF16)

        for h in range(HEADS):
            cs = slice(h * HEAD_DIM, (h + 1) * HEAD_DIM)
            scores = mask_ref[nl] * _dot_bt(qb[:, cs], kb[:, cs])
            for lvl in range(nl):
                x = xs[lvl][:, cs]
                scores = scores + mask_ref[lvl] * _dot_bt(x, x)
            v = v_ref[rows, cs]
            st = st_ref[h]
            o = _dot(scores.astype(BF16), v) + _dot_bt(q_in[:, cs], st.astype(BF16))
            st_ref[h] = st * dec[:, cs] + _dot_at(v, k_out[:, cs])
            ms = jnp.mean(o * o, axis=-1, keepdims=True)
            on = o * lax.rsqrt(ms + EPS) * gw_ref[...]
            o_ref[rows, cs] = (on * g_ref[rows, cs].astype(F32)).astype(o_ref.dtype)
        return carry

    lax.fori_loop(0, n_chunks, chunk_body, 0)

    @pl.when(step == pl.num_programs(1) - 1)
    def _():
        sfin_ref[0] = st_ref[...]


def _hgrn(proj, lf, kk, g_norm_w, s0, consts, bsz, seq, rows):
    mall, masks, sels = consts
    steps = seq // rows
    w = HEADS * HEAD_DIM
    assert w == COL_TILE
    row_map = lambda col: (lambda b, s: (b * steps + s, col))
    const2 = lambda b, s: (0, 0)
    const3 = lambda b, s: (0, 0, 0)
    kern = functools.partial(_hgrn_kernel, chunk=CHUNK, n_chunks=rows // CHUNK)
    return pl.pallas_call(
        kern,
        grid=(bsz, steps),
        in_specs=[
            pl.BlockSpec((rows, w), row_map(0)),
            pl.BlockSpec((rows, w), row_map(1)),
            pl.BlockSpec((rows, w), row_map(2)),
            pl.BlockSpec((rows, w), row_map(0)),
            pl.BlockSpec((rows, w), row_map(0)),
            pl.BlockSpec((1, HEAD_DIM), const2),
            pl.BlockSpec((HEADS, HEAD_DIM, HEAD_DIM), const3),
            pl.BlockSpec(mall.shape, const2),
            pl.BlockSpec(masks.shape, const3),
            pl.BlockSpec(sels.shape, const3),
        ],
        out_specs=[
            pl.BlockSpec((rows, w), row_map(0)),
            pl.BlockSpec((1, HEADS, HEAD_DIM, HEAD_DIM), lambda b, s: (b, 0, 0, 0)),
        ],
        out_shape=[
            jax.ShapeDtypeStruct((bsz * seq, w), BF16),
            jax.ShapeDtypeStruct((bsz, HEADS, HEAD_DIM, HEAD_DIM), F32),
        ],
        scratch_shapes=[pltpu.VMEM((HEADS, HEAD_DIM, HEAD_DIM), F32)],
        compiler_params=_params("arbitrary", "arbitrary"),
        name="hgrn2",
    )(proj, proj, proj, lf, kk, g_norm_w.reshape(1, HEAD_DIM), s0, mall, masks, sels)


def _mixer_out_kernel(og_ref, scv_ref, scb_ref, scc_ref, ga_ref, gb_ref, x_ref,
                      pv_ref, pc_ref, mv_ref, mc_ref,
                      wa_ref, wb_ref, wo_ref, cw_ref, nw_ref, wr_ref, br_ref,
                      h1_ref, xn_ref, idx_ref, gate_ref, hbuf, *, tiles_per_seq, n_tiles):
    i = pl.program_id(0)
    tm = x_ref.shape[0]
    tile = jnp.minimum(i, n_tiles - 1)
    first = (tile % tiles_per_seq) == 0

    @pl.when(i == 0)
    def _():
        hbuf[1] = jnp.zeros(hbuf.shape[1:], hbuf.dtype)

    hp = hbuf[(i + 1) % 2]
    ms = jnp.mean(hp * hp, axis=-1, keepdims=True)
    xn = hp * lax.rsqrt(ms + EPS) * nw_ref[...]
    xn_ref[...] = xn.astype(BF16).reshape(xn_ref.shape)

    xh = xn.astype(BF16)
    xl = (xn - xh.astype(F32)).astype(BF16)
    wr = wr_ref[...]
    wh = wr.astype(BF16)
    wl = (wr - wh.astype(F32)).astype(BF16)
    logits = _dot_bt(wh, xh) + _dot_bt(wh, xl) + _dot_bt(wl, xh) + br_ref[...]
    ne = logits.shape[0]
    ie = lax.broadcasted_iota(jnp.int32, logits.shape, 0)
    tops, idxs = [], []
    for _ in range(TOP_K):
        mx = jnp.max(logits, axis=0, keepdims=True)
        ix = jnp.min(jnp.where(logits == mx, ie, ne), axis=0, keepdims=True)
        tops.append(mx)
        idxs.append(ix)
        logits = jnp.where(ie == ix, -jnp.inf, logits)
    es = [jnp.exp(t - tops[0]) for t in tops]
    den = es[0]
    for e in es[1:]:
        den = den + e
    gate_ref[...] = jnp.concatenate([e / den for e in es], axis=0)
    idx_ref[...] = jnp.concatenate(idxs, axis=0)

    u = scc_ref[...].astype(F32) * scv_ref[...].astype(F32)
    halo_prev = pc_ref[...].astype(F32) * pv_ref[...].astype(F32)
    halo_meta = mc_ref[...].astype(F32) * mv_ref[...].astype(F32)
    halo = jnp.where(first, halo_meta, halo_prev)
    hr = halo.shape[0]
    r = lax.broadcasted_iota(jnp.int32, (tm, 1), 0)
    u1 = jnp.where(r == 0, halo[hr - 1:hr], pltpu.roll(u, 1, 0))
    u2 = jnp.where(r == 0, halo[hr - 2:hr - 1], jnp.where(r == 1, halo[hr - 1:hr], pltpu.roll(u, 2, 0)))
    conv = cw_ref[2:3] * u + cw_ref[1:2] * u1 + cw_ref[0:1] * u2
    yb_in = (scb_ref[...].astype(F32) * conv).astype(BF16)

    y_a = _dot(og_ref[...], wa_ref[...])
    y_b = _dot(yb_in, wb_ref[...])
    merged = (ga_ref[...].astype(F32) * y_a + gb_ref[...].astype(F32) * y_b).astype(BF16)
    h1 = x_ref[...] + _dot(merged, wo_ref[...])
    h1_ref[...] = h1
    hbuf[i % 2] = h1


def _mixer_out(og, proj, proj_meta, x2d, wa, wb, wo, conv_w, norm_w, w_router, b_router, seq, tm):
    m, d = x2d.shape
    w = COL_TILE
    halo = proj_meta.shape[0]
    assert tm % halo == 0 and seq % tm == 0 and d == 2 * w
    ne = w_router.shape[1]
    per_halo = tm // halo
    n_tiles = m // tm
    cur = lambda i: jnp.minimum(i, n_tiles - 1)
    done = lambda i: jnp.maximum(i - 1, 0)
    row = lambda col: (lambda i: (cur(i), col))
    prev = lambda col: (lambda i: (jnp.maximum(cur(i) * per_halo - 1, 0), col))
    const = lambda i: (0, 0)
    whole = lambda a: pl.BlockSpec(a.shape, const)
    wr_t = w_router.T
    kern = functools.partial(_mixer_out_kernel, tiles_per_seq=seq // tm, n_tiles=n_tiles)
    return pl.pallas_call(
        kern,
        grid=(n_tiles + 1,),
        in_specs=[
            pl.BlockSpec((tm, w), row(0)),
            pl.BlockSpec((tm, w), row(3)),
            pl.BlockSpec((tm, w), row(4)),
            pl.BlockSpec((tm, w), row(5)),
            pl.BlockSpec((tm, d), row(3)),
            pl.BlockSpec((tm, d), row(4)),
            pl.BlockSpec((tm, d), row(0)),
            pl.BlockSpec((halo, w), prev(3)),
            pl.BlockSpec((halo, w), prev(5)),
            pl.BlockSpec((halo, w), lambda i: (0, 3)),
            pl.BlockSpec((halo, w), lambda i: (0, 5)),
            whole(wa), whole(wb), whole(wo),
            pl.BlockSpec(conv_w.shape, const),
            pl.BlockSpec((1, d), const),
            pl.BlockSpec((ne, d), const),
            pl.BlockSpec((ne, 1), const),
        ],
        out_specs=[
            pl.BlockSpec((tm, d), row(0)),
            pl.BlockSpec((tm, d // LANES, LANES), lambda i: (done(i), 0, 0)),
            pl.BlockSpec((TOP_K, tm), lambda i: (0, done(i))),
            pl.BlockSpec((TOP_K, tm), lambda i: (0, done(i))),
        ],
        out_shape=[
            jax.ShapeDtypeStruct((m, d), F32),
            jax.ShapeDtypeStruct((m, d // LANES, LANES), BF16),
            jax.ShapeDtypeStruct((TOP_K, m), jnp.int32),
            jax.ShapeDtypeStruct((TOP_K, m), F32),
        ],
        scratch_shapes=[pltpu.VMEM((2, tm, d), F32)],
        compiler_params=_params("arbitrary"),
        name="mixer_out",
    )(og, proj, proj, proj, proj, proj, x2d, proj, proj, proj_meta, proj_meta,
      wa, wb, wo, conv_w, norm_w.reshape(1, d), wr_t, b_router.reshape(ne, 1))


def _dispatch_kernel(cnt_ref, pst_ref, used_ref, pos_ref, x_ref, o_hbm, xbuf, zbuf, sem, zsem):
    i = pl.program_id(0)
    n_steps = pl.num_programs(0)
    tt = x_ref.shape[0]
    n_sub = o_hbm.shape[0] // EXPERT_SUB
    bits = [1 << b for b in reversed(range(EXPERT_SUB.bit_length() - 1))]
    slot = i % 2

    xbuf[slot] = x_ref[...]

    def issue(r, carry):
        for k in range(TOP_K):
            p = pos_ref[0, 0, k * tt + r]
            pltpu.make_async_copy(xbuf.at[slot, pl.ds(r, 1)], o_hbm.at[pl.ds(p, 1)], sem.at[slot]).start()
        return carry

    lax.fori_loop(0, tt, issue, 0, unroll=4)

    def tokens_done(s):
        return [pltpu.make_async_copy(xbuf.at[s], o_hbm.at[pl.ds(0, tt)], sem.at[s]) for _ in range(TOP_K)]

    def zero_copies(fn):
        for e in range(cnt_ref.shape[0]):
            npad = (-cnt_ref[e]) & (EXPERT_SUB - 1)
            base = pst_ref[e] + cnt_ref[e]
            for bit in bits:
                @pl.when((npad & bit) != 0)
                def _():
                    row = base + (npad & ~(2 * bit - 1))
                    fn(pltpu.make_async_copy(zbuf.at[pl.ds(0, bit)], o_hbm.at[pl.ds(row, bit)], zsem))
        for j in range(cnt_ref.shape[0]):
            blk = used_ref[0] + j

            @pl.when(blk < n_sub)
            def _():
                row = pl.multiple_of(blk * EXPERT_SUB, EXPERT_SUB)
                fn(pltpu.make_async_copy(zbuf, o_hbm.at[pl.ds(row, EXPERT_SUB)], zsem))

    @pl.when(i == 0)
    def _():
        zbuf[...] = jnp.zeros_like(zbuf)
        zero_copies(lambda c: c.start())

    @pl.when(i > 0)
    def _():
        for c in tokens_done(1 - slot):
            c.wait()

    @pl.when(i == n_steps - 1)
    def _():
        for c in tokens_done(slot):
            c.wait()
        zero_copies(lambda c: c.wait())


def _dispatch(xn3, pos_t, counts, pad_start, used_sub, n_rows):
    m, sub, lanes = xn3.shape
    nt = pos_t.shape[0]
    tt = m // nt
    grid_spec = pltpu.PrefetchScalarGridSpec(
        num_scalar_prefetch=3,
        grid=(nt,),
        in_specs=[
            pl.BlockSpec((1, 1, pos_t.shape[2]), lambda i, c, p, u: (i, 0, 0), memory_space=pltpu.SMEM),
            pl.BlockSpec((tt, sub, lanes), lambda i, c, p, u: (i, 0, 0)),
        ],
        out_specs=pl.BlockSpec(memory_space=pl.ANY),
        scratch_shapes=[pltpu.VMEM((2, tt, sub, lanes), xn3.dtype), pltpu.VMEM((EXPERT_SUB, sub, lanes), xn3.dtype),
                        pltpu.SemaphoreType.DMA((2,)), pltpu.SemaphoreType.DMA(())],
    )
    return pl.pallas_call(
        _dispatch_kernel,
        grid_spec=grid_spec,
        out_shape=jax.ShapeDtypeStruct((n_rows, sub, lanes), xn3.dtype),
        compiler_params=_params("arbitrary"),
        name="dispatch",
    )(counts, pad_start, used_sub.reshape(1), pos_t, xn3)


def _row_chunks(nsub, chunk_fn):
    per = EXPERT_CHUNK_SUBS
    big = per * EXPERT_SUB
    n_big = lax.div(nsub, per)

    def body(c, carry):
        chunk_fn(pl.multiple_of(c * big, big), big)
        return carry

    lax.fori_loop(0, n_big, body, 0)
    rem = nsub - n_big * per
    bit = per // 2
    while bit >= 1:
        done = rem & ~(2 * bit - 1)

        @pl.when((rem & bit) != 0)
        def _():
            chunk_fn(pl.multiple_of((n_big * per + done) * EXPERT_SUB, EXPERT_SUB), bit * EXPERT_SUB)

        bit //= 2


def _ffn_up_kernel(st_ref, ns_ref, zf_ref, se_ref, jm_ref, x_hbm, wg_ref, wu_ref, bg_ref, bu_ref, h_hbm,
                   xstage, xbuf, hbuf, wgb_ref, wub_ref, sem_x, sem_h):
    s = pl.program_id(0)
    j = pl.program_id(1)
    n_s = pl.num_programs(0)
    nj = pl.num_programs(1)
    step = s * nj + j
    subs = xbuf.shape[0] // EXPERT_SUB
    nsub = ns_ref[s]
    real = jnp.logical_and(nsub > 0, zf_ref[s] == 0)
    hs = step % 2

    def x_copy(sb, b):
        row = pl.multiple_of(st_ref[sb] + b * EXPERT_SUB, EXPERT_SUB)
        return pltpu.make_async_copy(x_hbm.at[pl.ds(row, EXPERT_SUB)],
                                     xstage.at[pl.ds(b * EXPERT_SUB, EXPERT_SUB)], sem_x)

    def h_copy(sb, jj, b, slot):
        row = pl.multiple_of(st_ref[sb] + b * EXPERT_SUB, EXPERT_SUB)
        return pltpu.make_async_copy(hbuf.at[slot, pl.ds(b * EXPERT_SUB, EXPERT_SUB)],
                                     h_hbm.at[jj, pl.ds(row, EXPERT_SUB)], sem_h.at[slot])

    def for_x_subs(sb, fn):
        for b in range(subs):
            @pl.when(jnp.logical_and(b < ns_ref[sb], zf_ref[sb] == 0))
            def _():
                fn(b)

    def for_h_subs(sb, fn):
        for b in range(subs):
            @pl.when(b < ns_ref[sb])
            def _():
                fn(b)

    @pl.when(step == 0)
    def _():
        for_x_subs(0, lambda b: x_copy(0, b).start())

    @pl.when(j == 0)
    def _():
        def to_row_major(b):
            rows = slice(b * EXPERT_SUB, (b + 1) * EXPERT_SUB)
            xbuf[rows, :] = xstage[rows].reshape(EXPERT_SUB, xbuf.shape[1])

        for_x_subs(s, lambda b: x_copy(s, b).wait())
        for_x_subs(s, to_row_major)

    @pl.when(jnp.logical_and(j == 0, s + 1 < n_s))
    def _():
        nxt = jnp.minimum(s + 1, n_s - 1)
        for_x_subs(nxt, lambda b: x_copy(nxt, b).start())

    @pl.when(step >= 2)
    def _():
        sp = lax.div(step - 2, nj)
        jp = step - 2 - sp * nj
        for_h_subs(sp, lambda b: h_copy(sp, jp, b, hs).wait())

    @pl.when(real)
    def _():
        wgb_ref[...] = wg_ref[0].astype(BF16)
        wub_ref[...] = wu_ref[0].astype(BF16)

        def chunk(row0, nrows):
            xs = xbuf[pl.ds(row0, nrows), :]
            g = _dot(xs, wgb_ref[...]) + bg_ref[0]
            u = _dot(xs, wub_ref[...]) + bu_ref[0]
            g = jnp.minimum(g, SWIGLU_LIMIT)
            u = jnp.clip(u, -SWIGLU_LIMIT, SWIGLU_LIMIT)
            hbuf[hs, pl.ds(row0, nrows), :] = ((u + 1.0) * (g * jax.nn.sigmoid(SWIGLU_ALPHA * g))).astype(BF16)

        _row_chunks(nsub, chunk)

    @pl.when(zf_ref[s] == 1)
    def _():
        def zero_sub(b):
            hbuf[hs, b * EXPERT_SUB:(b + 1) * EXPERT_SUB, :] = jnp.zeros((EXPERT_SUB, hbuf.shape[2]), hbuf.dtype)

        for_h_subs(s, zero_sub)

    for_h_subs(s, lambda b: h_copy(s, j, b, hs).start())

    @pl.when(step == n_s * nj - 1)
    def _():
        sp = lax.div(step - 1, nj)
        jp = step - 1 - sp * nj
        for_h_subs(sp, lambda b: h_copy(sp, jp, b, 1 - hs).wait())
        for_h_subs(s, lambda b: h_copy(s, j, b, hs).wait())


def _ffn_down_kernel(st_ref, ns_ref, zf_ref, se_ref, cm_ref, h_hbm, wd_ref, bd_ref, y_hbm,
                     hb, ybuf, wdb_ref, sem_h, sem_y):
    s = pl.program_id(0)
    c = pl.program_id(1)
    n_s = pl.num_programs(0)
    nc = pl.num_programs(1)
    step = s * nc + c
    njh = hb.shape[1]
    subs = hb.shape[2] // EXPERT_SUB
    tn = ybuf.shape[2]
    nsub = ns_ref[s]
    real = jnp.logical_and(nsub > 0, zf_ref[s] == 0)
    ys = step % 2

    def h_copy(sb, jj, b, slot):
        row = pl.multiple_of(st_ref[sb] + b * EXPERT_SUB, EXPERT_SUB)
        return pltpu.make_async_copy(h_hbm.at[jj, pl.ds(row, EXPERT_SUB)],
                                     hb.at[slot, jj, pl.ds(b * EXPERT_SUB, EXPERT_SUB)], sem_h.at[slot])

    def y_copy(sb, cc, b, slot):
        row = pl.multiple_of(st_ref[sb] + b * EXPERT_SUB, EXPERT_SUB)
        return pltpu.make_async_copy(ybuf.at[slot, pl.ds(b * EXPERT_SUB, EXPERT_SUB)],
                                     y_hbm.at[pl.ds(row, EXPERT_SUB), pl.ds(cc * tn, tn)], sem_y.at[slot])

    def for_h_subs(sb, fn):
        for b in range(subs):
            @pl.when(jnp.logical_and(b < ns_ref[sb], zf_ref[sb] == 0))
            def _():
                for jj in range(njh):
                    fn(jj, b)

    def for_y_subs(sb, fn):
        for b in range(subs):
            @pl.when(b < ns_ref[sb])
            def _():
                fn(b)

    @pl.when(step == 0)
    def _():
        for_h_subs(0, lambda jj, b: h_copy(0, jj, b, 0).start())

    @pl.when(jnp.logical_and(c == 0, s + 1 < n_s))
    def _():
        nxt = jnp.minimum(s + 1, n_s - 1)
        for_h_subs(nxt, lambda jj, b: h_copy(nxt, jj, b, (s + 1) % 2).start())

    @pl.when(c == 0)
    def _():
        for_h_subs(s, lambda jj, b: h_copy(s, jj, b, s % 2).wait())

    @pl.when(step >= 2)
    def _():
        sp = lax.div(step - 2, nc)
        for_y_subs(sp, lambda b: y_copy(sp, 0, b, ys).wait())

    @pl.when(real)
    def _():
        wdb_ref[...] = wd_ref[0].astype(BF16)
        h_slot = s % 2

        def chunk(row0, nrows):
            hid = jnp.concatenate([hb[h_slot, jj, pl.ds(row0, nrows), :] for jj in range(njh)], axis=1)
            ybuf[ys, pl.ds(row0, nrows), :] = _dot(hid, wdb_ref[...]) + bd_ref[0]

        _row_chunks(nsub, chunk)

    @pl.when(zf_ref[s] == 1)
    def _():
        def zero_sub(b):
            ybuf[ys, b * EXPERT_SUB:(b + 1) * EXPERT_SUB, :] = jnp.zeros((EXPERT_SUB, tn), ybuf.dtype)

        for_y_subs(s, zero_sub)

    for cc in range(y_hbm.shape[1] // tn):
        @pl.when(c == cc)
        def _():
            for_y_subs(s, lambda b: y_copy(s, cc, b, ys).start())

    @pl.when(step == n_s * nc - 1)
    def _():
        sp = lax.div(step - 1, nc)
        for_y_subs(sp, lambda b: y_copy(sp, 0, b, 1 - ys).wait())
        for_y_subs(s, lambda b: y_copy(s, 0, b, ys).wait())


def _experts(x_rows, w_up, b_up, w_down, b_down, sb_start, sb_nsub, sb_zero, sb_expert):
    n_rows, x_sub, x_lanes = x_rows.shape
    d = x_sub * x_lanes
    ne, _, ff2 = w_up.shape
    ff = ff2 // 2
    tf = EXPERT_FF_TILE
    tn = EXPERT_OUT_TILE
    nj = ff // tf
    nc = d // tn
    n_sb = jnp.sum((sb_nsub > 0).astype(jnp.int32))
    rows = EXPERT_SUBS * EXPERT_SUB
    is_real = jnp.logical_and(sb_nsub > 0, sb_zero == 0)[:, None]
    jm = jnp.where(is_real, jnp.arange(nj, dtype=jnp.int32)[None, :], nj - 1).astype(jnp.int32)
    cm = jnp.where(is_real, jnp.arange(nc, dtype=jnp.int32)[None, :], nc - 1).astype(jnp.int32)
    any_spec = pl.BlockSpec(memory_space=pl.ANY)

    up_spec = pltpu.PrefetchScalarGridSpec(
        num_scalar_prefetch=5,
        grid=(n_sb, nj),
        in_specs=[
            any_spec,
            pl.BlockSpec((1, d, tf), lambda s, j, st, ns, zf, se, jm: (se[s], 0, jm[s, j])),
            pl.BlockSpec((1, d, tf), lambda s, j, st, ns, zf, se, jm: (se[s], 0, nj + jm[s, j])),
            pl.BlockSpec((1, 1, tf), lambda s, j, st, ns, zf, se, jm: (se[s], 0, jm[s, j])),
            pl.BlockSpec((1, 1, tf), lambda s, j, st, ns, zf, se, jm: (se[s], 0, nj + jm[s, j])),
        ],
        out_specs=any_spec,
        scratch_shapes=[
            pltpu.VMEM((rows, x_sub, x_lanes), BF16),
            pltpu.VMEM((rows, d), BF16),
            pltpu.VMEM((2, rows, tf), BF16),
            pltpu.VMEM((d, tf), BF16),
            pltpu.VMEM((d, tf), BF16),
            pltpu.SemaphoreType.DMA(()),
            pltpu.SemaphoreType.DMA((2,)),
        ],
    )
    hidden = pl.pallas_call(
        _ffn_up_kernel,
        grid_spec=up_spec,
        out_shape=jax.ShapeDtypeStruct((nj, n_rows, tf), BF16),
        compiler_params=_params("arbitrary", "arbitrary"),
        name="ffn_up",
    )(sb_start, sb_nsub, sb_zero, sb_expert, jm, x_rows, w_up, w_up,
      b_up.reshape(ne, 1, ff2), b_up.reshape(ne, 1, ff2))

    down_spec = pltpu.PrefetchScalarGridSpec(
        num_scalar_prefetch=5,
        grid=(n_sb, nc),
        in_specs=[
            any_spec,
            pl.BlockSpec((1, ff, tn), lambda s, c, st, ns, zf, se, cm: (se[s], 0, cm[s, c])),
            pl.BlockSpec((1, 1, tn), lambda s, c, st, ns, zf, se, cm: (se[s], 0, cm[s, c])),
        ],
        out_specs=any_spec,
        scratch_shapes=[
            pltpu.VMEM((2, nj, rows, tf), BF16),
            pltpu.VMEM((2, rows, tn), F32),
            pltpu.VMEM((ff, tn), BF16),
            pltpu.SemaphoreType.DMA((2,)),
            pltpu.SemaphoreType.DMA((2,)),
        ],
    )
    return pl.pallas_call(
        _ffn_down_kernel,
        grid_spec=down_spec,
        out_shape=jax.ShapeDtypeStruct((n_rows, d), F32),
        compiler_params=_params("arbitrary", "arbitrary"),
        name="ffn_down",
    )(sb_start, sb_nsub, sb_zero, sb_expert, cm, hidden, w_down, b_down.reshape(ne, 1, d))


def _combine_kernel(pos_ref, pos_next_ref, gate_ref, h1_ref, fw_ref, y_hbm, o_ref, buf_ref, sem):
    i = pl.program_id(0)
    tt = h1_ref.shape[0]

    def start_rows(pos, slot):
        def issue(r, carry):
            for k in range(TOP_K):
                p = pos[0, 0, k * tt + r]
                pltpu.make_async_copy(y_hbm.at[pl.ds(p, 1)], buf_ref.at[slot, k, pl.ds(r, 1)], sem.at[slot]).start()
            return carry

        lax.fori_loop(0, tt, issue, 0, unroll=4)

    @pl.when(i == 0)
    def _():
        start_rows(pos_ref, 0)

    @pl.when(i + 1 < pl.num_programs(0))
    def _():
        start_rows(pos_next_ref, (i + 1) % 2)

    slot = i % 2
    for k in range(TOP_K):
        pltpu.make_async_copy(y_hbm.at[pl.ds(0, tt)], buf_ref.at[slot, k], sem.at[slot]).wait()

    gate = gate_ref[...]
    gpad = jnp.concatenate([gate, jnp.zeros((tt - TOP_K, tt), F32)], axis=0)
    gcol = gpad.T
    acc = h1_ref[...]
    for k in range(TOP_K):
        acc = acc + gcol[:, k:k + 1] * buf_ref[slot, k]
    ms = jnp.mean(acc * acc, axis=-1, keepdims=True)
    o_ref[...] = acc * lax.rsqrt(ms + EPS) * fw_ref[...]


def _pos_tiles(pos, m):
    tt = COMBINE_ROWS
    nt = m // tt
    return pos.reshape(TOP_K, nt, tt).transpose(1, 0, 2).reshape(nt, 1, TOP_K * tt)


def _combine(y_rows, pos_t, gate, h1, final_w):
    m, d = h1.shape
    tt = COMBINE_ROWS
    nt = m // tt
    return pl.pallas_call(
        _combine_kernel,
        grid=(nt,),
        in_specs=[
            pl.BlockSpec((1, 1, TOP_K * tt), lambda i: (i, 0, 0), memory_space=pltpu.SMEM),
            pl.BlockSpec((1, 1, TOP_K * tt), lambda i: (jnp.minimum(i + 1, nt - 1), 0, 0), memory_space=pltpu.SMEM),
            pl.BlockSpec((TOP_K, tt), lambda i: (0, i)),
            pl.BlockSpec((tt, d), lambda i: (i, 0)),
            pl.BlockSpec((1, d), lambda i: (0, 0)),
            pl.BlockSpec(memory_space=pl.ANY),
        ],
        out_specs=pl.BlockSpec((tt, d), lambda i: (i, 0)),
        out_shape=jax.ShapeDtypeStruct((m, d), F32),
        scratch_shapes=[pltpu.VMEM((2, TOP_K, tt, d), F32), pltpu.SemaphoreType.DMA((2,))],
        compiler_params=_params("arbitrary"),
        name="combine",
    )(pos_t, pos_t, gate, h1, final_w.reshape(1, d), y_rows)


def _routing_tables(idx, m):
    i32 = jnp.int32
    n_assign = TOP_K * m
    flat_e = idx.reshape(n_assign)
    onehot = flat_e[:, None] == jnp.arange(N_EXPERTS, dtype=i32)[None, :]
    csum = jnp.cumsum(onehot.astype(i32), axis=0)
    rank = jnp.sum(jnp.where(onehot, csum - 1, 0), axis=1)
    counts = csum[-1]
    padded = (counts + EXPERT_SUB - 1) // EXPERT_SUB * EXPERT_SUB
    pad_end = jnp.cumsum(padded)
    pad_start = pad_end - padded
    pos = (jnp.sum(jnp.where(onehot, pad_start[None, :], 0), axis=1) + rank).astype(i32)
    n_sub = -(-(n_assign + N_EXPERTS * (EXPERT_SUB - 1)) // EXPERT_SUB)
    n_rows = n_sub * EXPERT_SUB
    used_sub = pad_end[-1] // EXPERT_SUB

    rows = EXPERT_SUBS * EXPERT_SUB
    n_sb = -(-n_sub // EXPERT_SUBS) + N_EXPERTS + 1
    nsb_e = (padded + rows - 1) // rows
    sb_cum = jnp.cumsum(nsb_e)
    total_real = sb_cum[-1]
    s = jnp.arange(n_sb, dtype=i32)
    e_s = jnp.minimum(jnp.sum((sb_cum[None, :] <= s[:, None]).astype(i32), axis=1), N_EXPERTS - 1)
    local = s - (sb_cum[e_s] - nsb_e[e_s])
    real = s < total_real
    start_real = pad_start[e_s] + local * rows
    nsub_real = jnp.clip((padded[e_s] - local * rows) // EXPERT_SUB, 0, EXPERT_SUBS)
    fill_idx = s - total_real
    start_fill = pad_end[-1] + fill_idx * rows
    nsub_fill = jnp.clip(n_sub - used_sub - fill_idx * EXPERT_SUBS, 0, EXPERT_SUBS)
    is_fill = jnp.logical_and(jnp.logical_not(real), nsub_fill > 0)
    sb_start = jnp.where(real, start_real, jnp.where(is_fill, start_fill, 0)).astype(i32)
    sb_nsub = jnp.where(real, nsub_real, jnp.where(is_fill, nsub_fill, 0)).astype(i32)
    sb_zero = is_fill.astype(i32)
    last_e = e_s[jnp.maximum(total_real - 1, 0)]
    sb_expert = jnp.where(real, e_s, last_e).astype(i32)
    return pos, counts.astype(i32), pad_start.astype(i32), used_sub.astype(i32), n_rows, sb_start, sb_nsub, sb_zero, sb_expert


def _main_tiles(d):
    w = HEADS * HEAD_DIM
    sc = d // 2
    sizes = (w, w, w, w, sc, sc, sc, d, d)
    acts = (1, None, 0, 1, 0, 0, 0, 2, 2)
    order = (0, 2, 3, 4, 5, 6, 7, 8)
    starts = np.concatenate([[0], np.cumsum(sizes)])
    cols, codes = [], []
    for seg in order:
        assert sizes[seg] % COL_TILE == 0 and starts[seg] % COL_TILE == 0
        for t in range(sizes[seg] // COL_TILE):
            cols.append(int(starts[seg]) // COL_TILE + t)
            codes.append(acts[seg])
    return cols, codes, int(starts[1]) // COL_TILE


def _layer(x2d, meta, bsz, seq, norm_mix_w, w_in, lb_logits, g_norm_w, w_hgrn_out, conv_w, w_conv_out, w_o,
           norm_ffn_w, w_router, b_router, w_up, b_up, w_down, b_down, final_norm_w):
    m, d = x2d.shape
    cols, codes, fcol = _main_tiles(d)
    consts = _hgrn_constants(CHUNK)

    xn_meta = _prenorm(meta, norm_mix_w, N_META)
    xn = _prenorm(x2d, norm_mix_w, 512)
    proj, proj_meta = _inproj(xn, xn_meta, w_in, cols, codes, 1024)
    lf, kk, lf_meta, kk_meta = _fgate(xn, xn_meta, w_in, lb_logits, fcol, 1024)

    pad = CHUNK - N_META
    front = lambda a: jnp.pad(a, ((pad, 0), (0, 0)))
    s_zero = jnp.zeros((HEADS, HEAD_DIM, HEAD_DIM), F32)
    _, s_meta = _hgrn(front(proj_meta), front(lf_meta), front(kk_meta), g_norm_w, s_zero, consts, 1, CHUNK, CHUNK)
    og, _ = _hgrn(proj, lf, kk, g_norm_w, s_meta[0], consts, bsz, seq, HGRN_ROWS)
    h1, xn_ffn, idx, gate = _mixer_out(og, proj, proj_meta, x2d, w_hgrn_out.astype(BF16), w_conv_out.astype(BF16),
                                   w_o.astype(BF16), conv_w, norm_ffn_w, w_router, b_router, seq, 256)

    pos, counts, pad_start, used_sub, n_rows, sb_start, sb_nsub, sb_zero, sb_expert = _routing_tables(idx, m)
    pos_t = _pos_tiles(pos, m)
    x_rows = _dispatch(xn_ffn, pos_t, counts, pad_start, used_sub, n_rows)
    y_rows = _experts(x_rows, w_up, b_up, w_down, b_down, sb_start, sb_nsub, sb_zero, sb_expert)
    return _combine(y_rows, pos_t, gate, h1, final_norm_w)


def kernel(x, meta_tokens, norm_mix_w, w_in, lb_logits, g_norm_w, w_hgrn_out, conv_w, w_conv_out, w_o, norm_ffn_w,
           w_router, b_router, w_up, b_up, w_down, b_down, final_norm_w):
    bsz, seq, d = x.shape
    assert norm_mix_w.shape[0] == 1, "single-layer block"
    out = _layer(x.reshape(bsz * seq, d), meta_tokens.astype(x.dtype), bsz, seq, norm_mix_w[0], w_in[0], lb_logits,
                 g_norm_w[0], w_hgrn_out[0], conv_w[0], w_conv_out[0], w_o[0], norm_ffn_w[0], w_router[0],
                 b_router[0], w_up[0], b_up[0], w_down[0], b_down[0], final_norm_w)
    return out.reshape(bsz, seq, d)
```

```python
import functools

import numpy as np
import jax
import jax.numpy as jnp
from jax import lax
from jax.experimental import pallas as pl
from jax.experimental.pallas import tpu as pltpu

F32 = jnp.float32
BF16 = jnp.bfloat16

LANES = 128
N_META = 16
HEADS = 8
HEAD_DIM = 128
N_EXPERTS = 32
TOP_K = 4
SWIGLU_LIMIT = 7.0
SWIGLU_ALPHA = 1.702
EPS = 1e-6

CHUNK = 256
HGRN_BASE = 64
HGRN_ROWS = 512
COL_TILE = 1024
EXPERT_SUB = 128
EXPERT_SUBS = 18
EXPERT_CHUNK_SUBS = 8
EXPERT_FF_TILE = 512
EXPERT_OUT_TILE = 512
COMBINE_ROWS = 128

V7X_VMEM_LIMIT = 56 * 1024 * 1024


def _dot(a, b):
    return jnp.dot(a, b, preferred_element_type=F32)


def _dot_bt(a, b):
    return lax.dot_general(a, b, (((1,), (1,)), ((), ())), preferred_element_type=F32)


def _dot_at(a, b):
    return lax.dot_general(a, b, (((0,), (0,)), ((), ())), preferred_element_type=F32)


def _split3(x):
    hi = x.astype(BF16)
    r1 = x - hi.astype(F32)
    mid = r1.astype(BF16)
    lo = (r1 - mid.astype(F32)).astype(BF16)
    return hi, mid, lo


def _params(*sem):
    return pltpu.CompilerParams(dimension_semantics=sem, vmem_limit_bytes=V7X_VMEM_LIMIT)


def _activate(z, code):
    s = jax.nn.sigmoid(z)
    return jnp.where(code == 0, z, jnp.where(code == 1, z * s, s))


def _inproj_kernel(col_ref, code_ref, x_ref, xm_ref, w_ref, o_ref, om_ref, wb_ref):
    code = code_ref[pl.program_id(0)]

    @pl.when(pl.program_id(1) == 0)
    def _():
        wb_ref[...] = w_ref[...].astype(BF16)
        om_ref[...] = _activate(_dot(xm_ref[...], wb_ref[...]), code).astype(om_ref.dtype)

    o_ref[...] = _activate(_dot(x_ref[...], wb_ref[...]), code).astype(o_ref.dtype)


def _inproj(xn, xn_meta, w_in, cols, codes, tm):
    m, d = xn.shape
    mm = xn_meta.shape[0]
    nt = len(cols)
    grid_spec = pltpu.PrefetchScalarGridSpec(
        num_scalar_prefetch=2,
        grid=(nt, m // tm),
        in_specs=[
            pl.BlockSpec((tm, d), lambda n, i, col, code: (i, 0)),
            pl.BlockSpec((mm, d), lambda n, i, col, code: (0, 0)),
            pl.BlockSpec((d, COL_TILE), lambda n, i, col, code: (0, col[n])),
        ],
        out_specs=[
            pl.BlockSpec((tm, COL_TILE), lambda n, i, col, code: (i, n)),
            pl.BlockSpec((mm, COL_TILE), lambda n, i, col, code: (0, n)),
        ],
        scratch_shapes=[pltpu.VMEM((d, COL_TILE), BF16)],
    )
    return pl.pallas_call(
        _inproj_kernel,
        grid_spec=grid_spec,
        out_shape=[jax.ShapeDtypeStruct((m, nt * COL_TILE), BF16), jax.ShapeDtypeStruct((mm, nt * COL_TILE), BF16)],
        compiler_params=_params("arbitrary", "arbitrary"),
        name="inproj",
    )(jnp.asarray(cols, jnp.int32), jnp.asarray(codes, jnp.int32), xn, xn_meta, w_in)


def _fgate_kernel(x_ref, xm_ref, nw_ref, w_ref, lbl_ref, xn_ref, lf_ref, kk_ref, xnm_ref, lfm_ref, kkm_ref, wb_ref):
    lbl = lbl_ref[...]
    e = jnp.exp(lbl - jnp.max(lbl, axis=0, keepdims=True))
    lb = e[0:1] / jnp.sum(e, axis=0, keepdims=True)

    def gate(x, xn_out, lf_out, kk_out):
        ms = jnp.mean(x * x, axis=-1, keepdims=True)
        xn = (x * lax.rsqrt(ms + EPS) * nw_ref[...]).astype(BF16)
        xn_out[...] = xn
        z = _dot(xn, wb_ref[...])
        lf_out[...] = jnp.log(lb + (1.0 - lb) * jax.nn.sigmoid(z))
        kk_out[...] = (1.0 - lb) * jax.nn.sigmoid(-z)

    @pl.when(pl.program_id(0) == 0)
    def _():
        wb_ref[...] = w_ref[...].astype(BF16)
        gate(xm_ref[...], xnm_ref, lfm_ref, kkm_ref)

    gate(x_ref[...], xn_ref, lf_ref, kk_ref)


def _fgate(x, x_meta, norm_w, w_in, lb_logits, col, tm):
    m, d = x.shape
    mm = x_meta.shape[0]
    r = lb_logits.shape[0]
    out = jax.ShapeDtypeStruct((m, COL_TILE), F32)
    out_meta = jax.ShapeDtypeStruct((mm, COL_TILE), F32)
    return pl.pallas_call(
        _fgate_kernel,
        grid=(m // tm,),
        in_specs=[
            pl.BlockSpec((tm, d), lambda i: (i, 0)),
            pl.BlockSpec((mm, d), lambda i: (0, 0)),
            pl.BlockSpec((1, d), lambda i: (0, 0)),
            pl.BlockSpec((d, COL_TILE), lambda i: (0, col)),
            pl.BlockSpec((r, COL_TILE), lambda i: (0, 0)),
        ],
        out_specs=[pl.BlockSpec((tm, d), lambda i: (i, 0))] + [pl.BlockSpec((tm, COL_TILE), lambda i: (i, 0))] * 2
        + [pl.BlockSpec((mm, d), lambda i: (0, 0))] + [pl.BlockSpec((mm, COL_TILE), lambda i: (0, 0))] * 2,
        out_shape=[jax.ShapeDtypeStruct((m, d), BF16), out, out,
                   jax.ShapeDtypeStruct((mm, d), BF16), out_meta, out_meta],
        scratch_shapes=[pltpu.VMEM((d, COL_TILE), BF16)],
        compiler_params=_params("arbitrary"),
        name="norm_fgate",
    )(x, x_meta, norm_w.reshape(1, d), w_in, lb_logits)


def _hgrn_constants(c):
    base = HGRN_BASE
    nlow = int(np.log2(base))
    nl = int(np.log2(c))
    assert (1 << nl) == c and (1 << nlow) == base and c >= base
    rr = np.arange(base)[:, None]
    uu = np.arange(base)[None, :]
    mats = [uu <= rr, uu > rr]
    sels = []
    for lvl in range(nlow):
        b = 1 << lvl
        start = (rr // (2 * b)) * (2 * b)
        mid = start + b - 1
        second = (rr - start) >= b
        mats.append(np.where(second, (uu > mid) & (uu <= rr), (uu > rr) & (uu <= mid)))
        sels.append(np.broadcast_to(second, (base, HEADS * HEAD_DIM)))
    tt = np.arange(c)[:, None]
    ss = np.arange(c)[None, :]
    masks = []
    for lvl in range(nl):
        b = 1 << lvl
        masks.append(((tt // (2 * b)) == (ss // (2 * b))) & ((tt % (2 * b)) >= b) & ((ss % (2 * b)) < b))
    masks.append(np.eye(c, dtype=bool))
    m1 = np.concatenate(mats, 0).astype(np.float32)
    mall = jnp.asarray(np.concatenate([m1, m1, m1], axis=1), BF16)
    return mall, jnp.asarray(np.stack(masks).astype(np.float32)), jnp.asarray(np.stack(sels).astype(np.float32))


def _hgrn_kernel(q_ref, v_ref, g_ref, lf_ref, kk_ref, gw_ref, s0_ref, mall_ref, mask_ref, sel_ref,
                 o_ref, sfin_ref, st_ref, *, chunk, n_chunks):
    base = HGRN_BASE
    nb = chunk // base
    nlow = sel_ref.shape[0]
    nl = mask_ref.shape[0] - 1
    step = pl.program_id(1)

    @pl.when(step == 0)
    def _():
        st_ref[...] = s0_ref[...]

    def chunk_body(ci, carry):
        r0 = pl.multiple_of(ci * chunk, chunk)
        rows = pl.ds(r0, chunk)
        qb = q_ref[rows, :]
        q = qb.astype(F32)
        k = kk_ref[rows, :]
        blk = lambda a, i: a[i * base:(i + 1) * base]

        pre, suf, e_low = [], [], []
        for i in range(nb):
            hi, mid, lo = _split3(lf_ref[pl.ds(r0 + i * base, base), :])
            args = _dot(mall_ref[...], jnp.concatenate([hi, mid, lo], axis=0))
            pre.append(args[0:base])
            suf.append(args[base:2 * base])
            e_low.append(jnp.exp(args[2 * base:]))
        tot = [p[base - 1:base] for p in pre]

        def span(lo_blk, hi_blk):
            acc = None
            for j in range(lo_blk, hi_blk):
                acc = tot[j] if acc is None else acc + tot[j]
            return acc

        def shifted(a, off):
            return a if off is None else a + off

        xs = []
        for lvl in range(nlow):
            parts = [jnp.where(sel_ref[lvl] > 0.5, blk(q, i), blk(k, i)) * blk(e_low[i], lvl) for i in range(nb)]
            xs.append(jnp.concatenate(parts, axis=0).astype(BF16))
        for lvl in range(nlow, nl):
            half = (1 << lvl) // base
            parts = []
            for i in range(nb):
                g = i % (2 * half)
                if g >= half:
                    parts.append(blk(q, i) * jnp.exp(shifted(pre[i], span(i - (g - half), i))))
                else:
                    parts.append(blk(k, i) * jnp.exp(shifted(suf[i], span(i + 1, i - g + half))))
            xs.append(jnp.concatenate(parts, axis=0).astype(BF16))
        q_in = jnp.concatenate([blk(q, i) * jnp.exp(shifted(pre[i], span(0, i))) for i in range(nb)],
                               axis=0).astype(BF16)
        k_out = jnp.concatenate([blk(k, i) * jnp.exp(shifted(suf[i], span(i + 1, nb))) for i in range(nb)],
                                axis=0).astype(BF16)
        dec = jnp.exp(span(0, nb))
        kb = k.astype(BF16)

        for h in range(HEADS):
            cs = slice(h * HEAD_DIM, (h + 1) * HEAD_DIM)
            scores = mask_ref[nl] * _dot_bt(qb[:, cs], kb[:, cs])
            for lvl in range(nl):
                x = xs[lvl][:, cs]
                scores = scores + mask_ref[lvl] * _dot_bt(x, x)
            v = v_ref[rows, cs]
            st = st_ref[h]
            o = _dot(scores.astype(BF16), v) + _dot_bt(q_in[:, cs], st.astype(BF16))
            st_ref[h] = st * dec[:, cs] + _dot_at(v, k_out[:, cs])
            ms = jnp.mean(o * o, axis=-1, keepdims=True)
            on = o * lax.rsqrt(ms + EPS) * gw_ref[...]
            o_ref[rows, cs] = (on * g_ref[rows, cs].astype(F32)).astype(o_ref.dtype)
        return carry

    lax.fori_loop(0, n_chunks, chunk_body, 0)

    @pl.when(step == pl.num_programs(1) - 1)
    def _():
        sfin_ref[0] = st_ref[...]


def _hgrn(proj, lf, kk, g_norm_w, s0, consts, bsz, seq, rows):
    mall, masks, sels = consts
    steps = seq // rows
    w = HEADS * HEAD_DIM
    assert w == COL_TILE
    row_map = lambda col: (lambda b, s: (b * steps + s, col))
    const2 = lambda b, s: (0, 0)
    const3 = lambda b, s: (0, 0, 0)
    kern = functools.partial(_hgrn_kernel, chunk=CHUNK, n_chunks=rows // CHUNK)
    return pl.pallas_call(
        kern,
        grid=(bsz, steps),
        in_specs=[
            pl.BlockSpec((rows, w), row_map(0)),
            pl.BlockSpec((rows, w), row_map(1)),
            pl.BlockSpec((rows, w), row_map(2)),
            pl.BlockSpec((rows, w), row_map(0)),
            pl.BlockSpec((rows, w), row_map(0)),
            pl.BlockSpec((1, HEAD_DIM), const2),
            pl.BlockSpec((HEADS, HEAD_DIM, HEAD_DIM), const3),
            pl.BlockSpec(mall.shape, const2),
            pl.BlockSpec(masks.shape, const3),
            pl.BlockSpec(sels.shape, const3),
        ],
        out_specs=[
            pl.BlockSpec((rows, w), row_map(0)),
            pl.BlockSpec((1, HEADS, HEAD_DIM, HEAD_DIM), lambda b, s: (b, 0, 0, 0)),
        ],
        out_shape=[
            jax.ShapeDtypeStruct((bsz * seq, w), BF16),
            jax.ShapeDtypeStruct((bsz, HEADS, HEAD_DIM, HEAD_DIM), F32),
        ],
        scratch_shapes=[pltpu.VMEM((HEADS, HEAD_DIM, HEAD_DIM), F32)],
        compiler_params=_params("arbitrary", "arbitrary"),
        name="hgrn2",
    )(proj, proj, proj, lf, kk, g_norm_w.reshape(1, HEAD_DIM), s0, mall, masks, sels)


def _mixer_out_kernel(og_ref, scv_ref, scb_ref, scc_ref, ga_ref, gb_ref, x_ref,
                      pv_ref, pc_ref, mv_ref, mc_ref,
                      wa_ref, wb_ref, wo_ref, cw_ref, nw_ref, wr_ref, br_ref,
                      h1_ref, xn_ref, idx_ref, gate_ref, hbuf, *, tiles_per_seq, n_tiles):
    i = pl.program_id(0)
    tm = x_ref.shape[0]
    tile = jnp.minimum(i, n_tiles - 1)
    first = (tile % tiles_per_seq) == 0

    @pl.when(i == 0)
    def _():
        hbuf[1] = jnp.zeros(hbuf.shape[1:], hbuf.dtype)

    hp = hbuf[(i + 1) % 2]
    ms = jnp.mean(hp * hp, axis=-1, keepdims=True)
    xn = hp * lax.rsqrt(ms + EPS) * nw_ref[...]
    xn_ref[...] = xn.astype(BF16).reshape(xn_ref.shape)

    xh = xn.astype(BF16)
    xl = (xn - xh.astype(F32)).astype(BF16)
    wr = wr_ref[...]
    wh = wr.astype(BF16)
    wl = (wr - wh.astype(F32)).astype(BF16)
    logits = _dot_bt(wh, xh) + _dot_bt(wh, xl) + _dot_bt(wl, xh) + br_ref[...]
    ne = logits.shape[0]
    ie = lax.broadcasted_iota(jnp.int32, logits.shape, 0)
    tops, idxs = [], []
    for _ in range(TOP_K):
        mx = jnp.max(logits, axis=0, keepdims=True)
        ix = jnp.min(jnp.where(logits == mx, ie, ne), axis=0, keepdims=True)
        tops.append(mx)
        idxs.append(ix)
        logits = jnp.where(ie == ix, -jnp.inf, logits)
    es = [jnp.exp(t - tops[0]) for t in tops]
    den = es[0]
    for e in es[1:]:
        den = den + e
    gate_ref[...] = jnp.concatenate([e / den for e in es], axis=0)
    idx_ref[...] = jnp.concatenate(idxs, axis=0)

    u = scc_ref[...].astype(F32) * scv_ref[...].astype(F32)
    halo_prev = pc_ref[...].astype(F32) * pv_ref[...].astype(F32)
    halo_meta = mc_ref[...].astype(F32) * mv_ref[...].astype(F32)
    halo = jnp.where(first, halo_meta, halo_prev)
    hr = halo.shape[0]
    r = lax.broadcasted_iota(jnp.int32, (tm, 1), 0)
    u1 = jnp.where(r == 0, halo[hr - 1:hr], pltpu.roll(u, 1, 0))
    u2 = jnp.where(r == 0, halo[hr - 2:hr - 1], jnp.where(r == 1, halo[hr - 1:hr], pltpu.roll(u, 2, 0)))
    conv = cw_ref[2:3] * u + cw_ref[1:2] * u1 + cw_ref[0:1] * u2
    yb_in = (scb_ref[...].astype(F32) * conv).astype(BF16)

    y_a = _dot(og_ref[...], wa_ref[...])
    y_b = _dot(yb_in, wb_ref[...])
    merged = (ga_ref[...].astype(F32) * y_a + gb_ref[...].astype(F32) * y_b).astype(BF16)
    h1 = x_ref[...] + _dot(merged, wo_ref[...])
    h1_ref[...] = h1
    hbuf[i % 2] = h1


def _mixer_out(og, proj, proj_meta, x2d, wa, wb, wo, conv_w, norm_w, w_router, b_router, seq, tm):
    m, d = x2d.shape
    w = COL_TILE
    halo = proj_meta.shape[0]
    assert tm % halo == 0 and seq % tm == 0 and d == 2 * w
    ne = w_router.shape[1]
    per_halo = tm // halo
    n_tiles = m // tm
    cur = lambda i: jnp.minimum(i, n_tiles - 1)
    done = lambda i: jnp.maximum(i - 1, 0)
    row = lambda col: (lambda i: (cur(i), col))
    prev = lambda col: (lambda i: (jnp.maximum(cur(i) * per_halo - 1, 0), col))
    const = lambda i: (0, 0)
    whole = lambda a: pl.BlockSpec(a.shape, const)
    wr_t = w_router.T
    kern = functools.partial(_mixer_out_kernel, tiles_per_seq=seq // tm, n_tiles=n_tiles)
    return pl.pallas_call(
        kern,
        grid=(n_tiles + 1,),
        in_specs=[
            pl.BlockSpec((tm, w), row(0)),
            pl.BlockSpec((tm, w), row(3)),
            pl.BlockSpec((tm, w), row(4)),
            pl.BlockSpec((tm, w), row(5)),
            pl.BlockSpec((tm, d), row(3)),
            pl.BlockSpec((tm, d), row(4)),
            pl.BlockSpec((tm, d), row(0)),
            pl.BlockSpec((halo, w), prev(3)),
            pl.BlockSpec((halo, w), prev(5)),
            pl.BlockSpec((halo, w), lambda i: (0, 3)),
            pl.BlockSpec((halo, w), lambda i: (0, 5)),
            whole(wa), whole(wb), whole(wo),
            pl.BlockSpec(conv_w.shape, const),
            pl.BlockSpec((1, d), const),
            pl.BlockSpec((ne, d), const),
            pl.BlockSpec((ne, 1), const),
        ],
        out_specs=[
            pl.BlockSpec((tm, d), row(0)),
            pl.BlockSpec((tm, d // LANES, LANES), lambda i: (done(i), 0, 0)),
            pl.BlockSpec((TOP_K, tm), lambda i: (0, done(i))),
            pl.BlockSpec((TOP_K, tm), lambda i: (0, done(i))),
        ],
        out_shape=[
            jax.ShapeDtypeStruct((m, d), F32),
            jax.ShapeDtypeStruct((m, d // LANES, LANES), BF16),
            jax.ShapeDtypeStruct((TOP_K, m), jnp.int32),
            jax.ShapeDtypeStruct((TOP_K, m), F32),
        ],
        scratch_shapes=[pltpu.VMEM((2, tm, d), F32)],
        compiler_params=_params("arbitrary"),
        name="mixer_out",
    )(og, proj, proj, proj, proj, proj, x2d, proj, proj, proj_meta, proj_meta,
      wa, wb, wo, conv_w, norm_w.reshape(1, d), wr_t, b_router.reshape(ne, 1))


def _dispatch_kernel(cnt_ref, pst_ref, used_ref, pos_ref, x_ref, o_hbm, xbuf, zbuf, sem, zsem):
    i = pl.program_id(0)
    n_steps = pl.num_programs(0)
    tt = x_ref.shape[0]
    n_sub = o_hbm.shape[0] // EXPERT_SUB
    bits = [1 << b for b in reversed(range(EXPERT_SUB.bit_length() - 1))]
    slot = i % 2

    xbuf[slot] = x_ref[...]

    def issue(r, carry):
        for k in range(TOP_K):
            p = pos_ref[0, 0, k * tt + r]
            pltpu.make_async_copy(xbuf.at[slot, pl.ds(r, 1)], o_hbm.at[pl.ds(p, 1)], sem.at[slot]).start()
        return carry

    lax.fori_loop(0, tt, issue, 0, unroll=4)

    def tokens_done(s):
        return [pltpu.make_async_copy(xbuf.at[s], o_hbm.at[pl.ds(0, tt)], sem.at[s]) for _ in range(TOP_K)]

    def zero_copies(fn):
        for e in range(cnt_ref.shape[0]):
            npad = (-cnt_ref[e]) & (EXPERT_SUB - 1)
            base = pst_ref[e] + cnt_ref[e]
            for bit in bits:
                @pl.when((npad & bit) != 0)
                def _():
                    row = base + (npad & ~(2 * bit - 1))
                    fn(pltpu.make_async_copy(zbuf.at[pl.ds(0, bit)], o_hbm.at[pl.ds(row, bit)], zsem))
        for j in range(cnt_ref.shape[0]):
            blk = used_ref[0] + j

            @pl.when(blk < n_sub)
            def _():
                row = pl.multiple_of(blk * EXPERT_SUB, EXPERT_SUB)
                fn(pltpu.make_async_copy(zbuf, o_hbm.at[pl.ds(row, EXPERT_SUB)], zsem))

    @pl.when(i == 0)
    def _():
        zbuf[...] = jnp.zeros_like(zbuf)
        zero_copies(lambda c: c.start())

    @pl.when(i > 0)
    def _():
        for c in tokens_done(1 - slot):
            c.wait()

    @pl.when(i == n_steps - 1)
    def _():
        for c in tokens_done(slot):
            c.wait()
        zero_copies(lambda c: c.wait())


def _dispatch(xn3, pos_t, counts, pad_start, used_sub, n_rows):
    m, sub, lanes = xn3.shape
    nt = pos_t.shape[0]
    tt = m // nt
    grid_spec = pltpu.PrefetchScalarGridSpec(
        num_scalar_prefetch=3,
        grid=(nt,),
        in_specs=[
            pl.BlockSpec((1, 1, pos_t.shape[2]), lambda i, c, p, u: (i, 0, 0), memory_space=pltpu.SMEM),
            pl.BlockSpec((tt, sub, lanes), lambda i, c, p, u: (i, 0, 0)),
        ],
        out_specs=pl.BlockSpec(memory_space=pl.ANY),
        scratch_shapes=[pltpu.VMEM((2, tt, sub, lanes), xn3.dtype), pltpu.VMEM((EXPERT_SUB, sub, lanes), xn3.dtype),
                        pltpu.SemaphoreType.DMA((2,)), pltpu.SemaphoreType.DMA(())],
    )
    return pl.pallas_call(
        _dispatch_kernel,
        grid_spec=grid_spec,
        out_shape=jax.ShapeDtypeStruct((n_rows, sub, lanes), xn3.dtype),
        compiler_params=_params("arbitrary"),
        name="dispatch",
    )(counts, pad_start, used_sub.reshape(1), pos_t, xn3)


def _row_chunks(nsub, chunk_fn):
    per = EXPERT_CHUNK_SUBS
    big = per * EXPERT_SUB
    n_big = lax.div(nsub, per)

    def body(c, carry):
        chunk_fn(pl.multiple_of(c * big, big), big)
        return carry

    lax.fori_loop(0, n_big, body, 0)
    rem = nsub - n_big * per
    bit = per // 2
    while bit >= 1:
        done = rem & ~(2 * bit - 1)

        @pl.when((rem & bit) != 0)
        def _():
            chunk_fn(pl.multiple_of((n_big * per + done) * EXPERT_SUB, EXPERT_SUB), bit * EXPERT_SUB)

        bit //= 2


def _ffn_up_kernel(st_ref, ns_ref, zf_ref, se_ref, jm_ref, x_hbm, wg_ref, wu_ref, bg_ref, bu_ref, h_hbm,
                   xstage, xbuf, hbuf, wgb_ref, wub_ref, sem_x, sem_h):
    s = pl.program_id(0)
    j = pl.program_id(1)
    n_s = pl.num_programs(0)
    nj = pl.num_programs(1)
    step = s * nj + j
    subs = xbuf.shape[0] // EXPERT_SUB
    nsub = ns_ref[s]
    real = jnp.logical_and(nsub > 0, zf_ref[s] == 0)
    hs = step % 2

    def x_copy(sb, b):
        row = pl.multiple_of(st_ref[sb] + b * EXPERT_SUB, EXPERT_SUB)
        return pltpu.make_async_copy(x_hbm.at[pl.ds(row, EXPERT_SUB)],
                                     xstage.at[pl.ds(b * EXPERT_SUB, EXPERT_SUB)], sem_x)

    def h_copy(sb, jj, b, slot):
        row = pl.multiple_of(st_ref[sb] + b * EXPERT_SUB, EXPERT_SUB)
        return pltpu.make_async_copy(hbuf.at[slot, pl.ds(b * EXPERT_SUB, EXPERT_SUB)],
                                     h_hbm.at[jj, pl.ds(row, EXPERT_SUB)], sem_h.at[slot])

    def for_x_subs(sb, fn):
        for b in range(subs):
            @pl.when(jnp.logical_and(b < ns_ref[sb], zf_ref[sb] == 0))
            def _():
                fn(b)

    def for_h_subs(sb, fn):
        for b in range(subs):
            @pl.when(b < ns_ref[sb])
            def _():
                fn(b)

    @pl.when(step == 0)
    def _():
        for_x_subs(0, lambda b: x_copy(0, b).start())

    @pl.when(j == 0)
    def _():
        def to_row_major(b):
            rows = slice(b * EXPERT_SUB, (b + 1) * EXPERT_SUB)
            xbuf[rows, :] = xstage[rows].reshape(EXPERT_SUB, xbuf.shape[1])

        for_x_subs(s, lambda b: x_copy(s, b).wait())
        for_x_subs(s, to_row_major)

    @pl.when(jnp.logical_and(j == 0, s + 1 < n_s))
    def _():
        nxt = jnp.minimum(s + 1, n_s - 1)
        for_x_subs(nxt, lambda b: x_copy(nxt, b).start())

    @pl.when(step >= 2)
    def _():
        sp = lax.div(step - 2, nj)
        jp = step - 2 - sp * nj
        for_h_subs(sp, lambda b: h_copy(sp, jp, b, hs).wait())

    @pl.when(real)
    def _():
        wgb_ref[...] = wg_ref[0].astype(BF16)
        wub_ref[...] = wu_ref[0].astype(BF16)

        def chunk(row0, nrows):
            xs = xbuf[pl.ds(row0, nrows), :]
            g = _dot(xs, wgb_ref[...]) + bg_ref[0]
            u = _dot(xs, wub_ref[...]) + bu_ref[0]
            g = jnp.minimum(g, SWIGLU_LIMIT)
            u = jnp.clip(u, -SWIGLU_LIMIT, SWIGLU_LIMIT)
            hbuf[hs, pl.ds(row0, nrows), :] = ((u + 1.0) * (g * jax.nn.sigmoid(SWIGLU_ALPHA * g))).astype(BF16)

        _row_chunks(nsub, chunk)

    @pl.when(zf_ref[s] == 1)
    def _():
        def zero_sub(b):
            hbuf[hs, b * EXPERT_SUB:(b + 1) * EXPERT_SUB, :] = jnp.zeros((EXPERT_SUB, hbuf.shape[2]), hbuf.dtype)

        for_h_subs(s, zero_sub)

    for_h_subs(s, lambda b: h_copy(s, j, b, hs).start())

    @pl.when(step == n_s * nj - 1)
    def _():
        sp = lax.div(step - 1, nj)
        jp = step - 1 - sp * nj
        for_h_subs(sp, lambda b: h_copy(sp, jp, b, 1 - hs).wait())
        for_h_subs(s, lambda b: h_copy(s, j, b, hs).wait())


def _ffn_down_kernel(st_ref, ns_ref, zf_ref, se_ref, cm_ref, h_hbm, wd_ref, bd_ref, y_hbm,
                     hb, yrow, ytile, wdb_ref, sem_h, sem_y):
    s = pl.program_id(0)
    c = pl.program_id(1)
    n_s = pl.num_programs(0)
    nc = pl.num_programs(1)
    step = s * nc + c
    njh = hb.shape[1]
    subs = hb.shape[2] // EXPERT_SUB
    tn = wdb_ref.shape[1]
    nsub = ns_ref[s]
    real = jnp.logical_and(nsub > 0, zf_ref[s] == 0)

    def h_copy(sb, jj, b, slot):
        row = pl.multiple_of(st_ref[sb] + b * EXPERT_SUB, EXPERT_SUB)
        return pltpu.make_async_copy(h_hbm.at[jj, pl.ds(row, EXPERT_SUB)],
                                     hb.at[slot, jj, pl.ds(b * EXPERT_SUB, EXPERT_SUB)], sem_h.at[slot])

    def y_copy(sb, b):
        row = pl.multiple_of(st_ref[sb] + b * EXPERT_SUB, EXPERT_SUB)
        return pltpu.make_async_copy(ytile.at[pl.ds(b * EXPERT_SUB, EXPERT_SUB)],
                                     y_hbm.at[pl.ds(row, EXPERT_SUB)], sem_y)

    def for_h_subs(sb, fn):
        for b in range(subs):
            @pl.when(jnp.logical_and(b < ns_ref[sb], zf_ref[sb] == 0))
            def _():
                for jj in range(njh):
                    fn(jj, b)

    def for_y_subs(sb, fn):
        for b in range(subs):
            @pl.when(b < ns_ref[sb])
            def _():
                fn(b)

    @pl.when(step == 0)
    def _():
        for_h_subs(0, lambda jj, b: h_copy(0, jj, b, 0).start())

    @pl.when(jnp.logical_and(c == 0, s + 1 < n_s))
    def _():
        nxt = jnp.minimum(s + 1, n_s - 1)
        for_h_subs(nxt, lambda jj, b: h_copy(nxt, jj, b, (s + 1) % 2).start())

    @pl.when(c == 0)
    def _():
        for_h_subs(s, lambda jj, b: h_copy(s, jj, b, s % 2).wait())

    @pl.when(real)
    def _():
        wdb_ref[...] = wd_ref[0].astype(BF16)
        h_slot = s % 2
        for cc in range(yrow.shape[1] // tn):
            @pl.when(c == cc)
            def _():
                def chunk(row0, nrows):
                    hid = jnp.concatenate([hb[h_slot, jj, pl.ds(row0, nrows), :] for jj in range(njh)], axis=1)
                    y = _dot(hid, wdb_ref[...]) + bd_ref[0]
                    yrow[pl.ds(row0, nrows), cc * tn:(cc + 1) * tn] = y.astype(yrow.dtype)

                _row_chunks(nsub, chunk)

    @pl.when(c == nc - 1)
    def _():
        @pl.when(s > 0)
        def _():
            sp = jnp.maximum(s - 1, 0)
            for_y_subs(sp, lambda b: y_copy(sp, b).wait())

        sub_shape = (EXPERT_SUB,) + ytile.shape[1:]

        def retile(b):
            rows = slice(b * EXPERT_SUB, (b + 1) * EXPERT_SUB)
            ytile[rows] = yrow[rows, :].reshape(sub_shape)

        def zero_sub(b):
            ytile[b * EXPERT_SUB:(b + 1) * EXPERT_SUB] = jnp.zeros(sub_shape, ytile.dtype)

        @pl.when(real)
        def _():
            for_y_subs(s, retile)

        @pl.when(zf_ref[s] == 1)
        def _():
            for_y_subs(s, zero_sub)

        for_y_subs(s, lambda b: y_copy(s, b).start())

        @pl.when(s == n_s - 1)
        def _():
            for_y_subs(s, lambda b: y_copy(s, b).wait())


def _experts(x_rows, w_up, b_up, w_down, b_down, sb_start, sb_nsub, sb_zero, sb_expert):
    n_rows, x_sub, x_lanes = x_rows.shape
    d = x_sub * x_lanes
    ne, _, ff2 = w_up.shape
    ff = ff2 // 2
    tf = EXPERT_FF_TILE
    tn = EXPERT_OUT_TILE
    nj = ff // tf
    nc = d // tn
    n_sb = jnp.sum((sb_nsub > 0).astype(jnp.int32))
    rows = EXPERT_SUBS * EXPERT_SUB
    is_real = jnp.logical_and(sb_nsub > 0, sb_zero == 0)[:, None]
    jm = jnp.where(is_real, jnp.arange(nj, dtype=jnp.int32)[None, :], nj - 1).astype(jnp.int32)
    cm = jnp.where(is_real, jnp.arange(nc, dtype=jnp.int32)[None, :], nc - 1).astype(jnp.int32)
    any_spec = pl.BlockSpec(memory_space=pl.ANY)

    up_spec = pltpu.PrefetchScalarGridSpec(
        num_scalar_prefetch=5,
        grid=(n_sb, nj),
        in_specs=[
            any_spec,
            pl.BlockSpec((1, d, tf), lambda s, j, st, ns, zf, se, jm: (se[s], 0, jm[s, j])),
            pl.BlockSpec((1, d, tf), lambda s, j, st, ns, zf, se, jm: (se[s], 0, nj + jm[s, j])),
            pl.BlockSpec((1, 1, tf), lambda s, j, st, ns, zf, se, jm: (se[s], 0, jm[s, j])),
            pl.BlockSpec((1, 1, tf), lambda s, j, st, ns, zf, se, jm: (se[s], 0, nj + jm[s, j])),
        ],
        out_specs=any_spec,
        scratch_shapes=[
            pltpu.VMEM((rows, x_sub, x_lanes), BF16),
            pltpu.VMEM((rows, d), BF16),
            pltpu.VMEM((2, rows, tf), BF16),
            pltpu.VMEM((d, tf), BF16),
            pltpu.VMEM((d, tf), BF16),
            pltpu.SemaphoreType.DMA(()),
            pltpu.SemaphoreType.DMA((2,)),
        ],
    )
    hidden = pl.pallas_call(
        _ffn_up_kernel,
        grid_spec=up_spec,
        out_shape=jax.ShapeDtypeStruct((nj, n_rows, tf), BF16),
        compiler_params=_params("arbitrary", "arbitrary"),
        name="ffn_up",
    )(sb_start, sb_nsub, sb_zero, sb_expert, jm, x_rows, w_up, w_up,
      b_up.reshape(ne, 1, ff2), b_up.reshape(ne, 1, ff2))

    down_spec = pltpu.PrefetchScalarGridSpec(
        num_scalar_prefetch=5,
        grid=(n_sb, nc),
        in_specs=[
            any_spec,
            pl.BlockSpec((1, ff, tn), lambda s, c, st, ns, zf, se, cm: (se[s], 0, cm[s, c])),
            pl.BlockSpec((1, 1, tn), lambda s, c, st, ns, zf, se, cm: (se[s], 0, cm[s, c])),
        ],
        out_specs=any_spec,
        scratch_shapes=[
            pltpu.VMEM((2, nj, rows, tf), BF16),
            pltpu.VMEM((rows, d), BF16),
            pltpu.VMEM((rows, x_sub, x_lanes), BF16),
            pltpu.VMEM((ff, tn), BF16),
            pltpu.SemaphoreType.DMA((2,)),
            pltpu.SemaphoreType.DMA(()),
        ],
    )
    return pl.pallas_call(
        _ffn_down_kernel,
        grid_spec=down_spec,
        out_shape=jax.ShapeDtypeStruct((n_rows, x_sub, x_lanes), BF16),
        compiler_params=_params("arbitrary", "arbitrary"),
        name="ffn_down",
    )(sb_start, sb_nsub, sb_zero, sb_expert, cm, hidden, w_down, b_down.reshape(ne, 1, d))


def _combine_kernel(pos_ref, pos_next_ref, gate_ref, h1_ref, fw_ref, y_hbm, o_ref, buf_ref, sem):
    i = pl.program_id(0)
    tt = h1_ref.shape[0]

    def start_rows(pos, slot):
        def issue(r, carry):
            for k in range(TOP_K):
                p = pos[0, 0, k * tt + r]
                pltpu.make_async_copy(y_hbm.at[pl.ds(p, 1)], buf_ref.at[slot, k, pl.ds(r, 1)], sem.at[slot]).start()
            return carry

        lax.fori_loop(0, tt, issue, 0, unroll=4)

    @pl.when(i == 0)
    def _():
        start_rows(pos_ref, 0)

    @pl.when(i + 1 < pl.num_programs(0))
    def _():
        start_rows(pos_next_ref, (i + 1) % 2)

    slot = i % 2
    for k in range(TOP_K):
        pltpu.make_async_copy(y_hbm.at[pl.ds(0, tt)], buf_ref.at[slot, k], sem.at[slot]).wait()

    gate = gate_ref[...]
    gpad = jnp.concatenate([gate, jnp.zeros((tt - TOP_K, tt), F32)], axis=0)
    gcol = gpad.T
    acc = h1_ref[...]
    for k in range(TOP_K):
        acc = acc + gcol[:, k:k + 1] * buf_ref[slot, k].reshape(acc.shape).astype(F32)
    ms = jnp.mean(acc * acc, axis=-1, keepdims=True)
    o_ref[...] = acc * lax.rsqrt(ms + EPS) * fw_ref[...]


def _pos_tiles(pos, m):
    tt = COMBINE_ROWS
    nt = m // tt
    return pos.reshape(TOP_K, nt, tt).transpose(1, 0, 2).reshape(nt, 1, TOP_K * tt)


def _combine(y_rows, pos_t, gate, h1, final_w):
    m, d = h1.shape
    tt = COMBINE_ROWS
    nt = m // tt
    return pl.pallas_call(
        _combine_kernel,
        grid=(nt,),
        in_specs=[
            pl.BlockSpec((1, 1, TOP_K * tt), lambda i: (i, 0, 0), memory_space=pltpu.SMEM),
            pl.BlockSpec((1, 1, TOP_K * tt), lambda i: (jnp.minimum(i + 1, nt - 1), 0, 0), memory_space=pltpu.SMEM),
            pl.BlockSpec((TOP_K, tt), lambda i: (0, i)),
            pl.BlockSpec((tt, d), lambda i: (i, 0)),
            pl.BlockSpec((1, d), lambda i: (0, 0)),
            pl.BlockSpec(memory_space=pl.ANY),
        ],
        out_specs=pl.BlockSpec((tt, d), lambda i: (i, 0)),
        out_shape=jax.ShapeDtypeStruct((m, d), F32),
        scratch_shapes=[pltpu.VMEM((2, TOP_K, tt) + y_rows.shape[1:], y_rows.dtype), pltpu.SemaphoreType.DMA((2,))],
        compiler_params=_params("arbitrary"),
        name="combine",
    )(pos_t, pos_t, gate, h1, final_w.reshape(1, d), y_rows)


def _routing_tables(idx, m):
    i32 = jnp.int32
    n_assign = TOP_K * m
    flat_e = idx.reshape(n_assign)
    onehot = flat_e[:, None] == jnp.arange(N_EXPERTS, dtype=i32)[None, :]
    csum = jnp.cumsum(onehot.astype(i32), axis=0)
    rank = jnp.sum(jnp.where(onehot, csum - 1, 0), axis=1)
    counts = csum[-1]
    padded = (counts + EXPERT_SUB - 1) // EXPERT_SUB * EXPERT_SUB
    pad_end = jnp.cumsum(padded)
    pad_start = pad_end - padded
    pos = (jnp.sum(jnp.where(onehot, pad_start[None, :], 0), axis=1) + rank).astype(i32)
    n_sub = -(-(n_assign + N_EXPERTS * (EXPERT_SUB - 1)) // EXPERT_SUB)
    n_rows = n_sub * EXPERT_SUB
    used_sub = pad_end[-1] // EXPERT_SUB

    rows = EXPERT_SUBS * EXPERT_SUB
    n_sb = -(-n_sub // EXPERT_SUBS) + N_EXPERTS + 1
    nsb_e = (padded + rows - 1) // rows
    sb_cum = jnp.cumsum(nsb_e)
    total_real = sb_cum[-1]
    s = jnp.arange(n_sb, dtype=i32)
    e_s = jnp.minimum(jnp.sum((sb_cum[None, :] <= s[:, None]).astype(i32), axis=1), N_EXPERTS - 1)
    local = s - (sb_cum[e_s] - nsb_e[e_s])
    real = s < total_real
    start_real = pad_start[e_s] + local * rows
    nsub_real = jnp.clip((padded[e_s] - local * rows) // EXPERT_SUB, 0, EXPERT_SUBS)
    fill_idx = s - total_real
    start_fill = pad_end[-1] + fill_idx * rows
    nsub_fill = jnp.clip(n_sub - used_sub - fill_idx * EXPERT_SUBS, 0, EXPERT_SUBS)
    is_fill = jnp.logical_and(jnp.logical_not(real), nsub_fill > 0)
    sb_start = jnp.where(real, start_real, jnp.where(is_fill, start_fill, 0)).astype(i32)
    sb_nsub = jnp.where(real, nsub_real, jnp.where(is_fill, nsub_fill, 0)).astype(i32)
    sb_zero = is_fill.astype(i32)
    last_e = e_s[jnp.maximum(total_real - 1, 0)]
    sb_expert = jnp.where(real, e_s, last_e).astype(i32)
    return pos, counts.astype(i32), pad_start.astype(i32), used_sub.astype(i32), n_rows, sb_start, sb_nsub, sb_zero, sb_expert


def _main_tiles(d):
    w = HEADS * HEAD_DIM
    sc = d // 2
    sizes = (w, w, w, w, sc, sc, sc, d, d)
    acts = (1, None, 0, 1, 0, 0, 0, 2, 2)
    order = (0, 2, 3, 4, 5, 6, 7, 8)
    starts = np.concatenate([[0], np.cumsum(sizes)])
    cols, codes = [], []
    for seg in order:
        assert sizes[seg] % COL_TILE == 0 and starts[seg] % COL_TILE == 0
        for t in range(sizes[seg] // COL_TILE):
            cols.append(int(starts[seg]) // COL_TILE + t)
            codes.append(acts[seg])
    return cols, codes, int(starts[1]) // COL_TILE


def _layer(x2d, meta, bsz, seq, norm_mix_w, w_in, lb_logits, g_norm_w, w_hgrn_out, conv_w, w_conv_out, w_o,
           norm_ffn_w, w_router, b_router, w_up, b_up, w_down, b_down, final_norm_w):
    m, d = x2d.shape
    cols, codes, fcol = _main_tiles(d)
    consts = _hgrn_constants(CHUNK)

    xn, lf, kk, xn_meta, lf_meta, kk_meta = _fgate(x2d, meta, norm_mix_w, w_in, lb_logits, fcol, 512)
    proj, proj_meta = _inproj(xn, xn_meta, w_in, cols, codes, 1024)

    pad = CHUNK - N_META
    front = lambda a: jnp.pad(a, ((pad, 0), (0, 0)))
    s_zero = jnp.zeros((HEADS, HEAD_DIM, HEAD_DIM), F32)
    _, s_meta = _hgrn(front(proj_meta), front(lf_meta), front(kk_meta), g_norm_w, s_zero, consts, 1, CHUNK, CHUNK)
    og, _ = _hgrn(proj, lf, kk, g_norm_w, s_meta[0], consts, bsz, seq, HGRN_ROWS)
    h1, xn_ffn, idx, gate = _mixer_out(og, proj, proj_meta, x2d, w_hgrn_out.astype(BF16), w_conv_out.astype(BF16),
                                   w_o.astype(BF16), conv_w, norm_ffn_w, w_router, b_router, seq, 256)

    pos, counts, pad_start, used_sub, n_rows, sb_start, sb_nsub, sb_zero, sb_expert = _routing_tables(idx, m)
    pos_t = _pos_tiles(pos, m)
    x_rows = _dispatch(xn_ffn, pos_t, counts, pad_start, used_sub, n_rows)
    y_rows = _experts(x_rows, w_up, b_up, w_down, b_down, sb_start, sb_nsub, sb_zero, sb_expert)
    return _combine(y_rows, pos_t, gate, h1, final_norm_w)


def kernel(x, meta_tokens, norm_mix_w, w_in, lb_logits, g_norm_w, w_hgrn_out, conv_w, w_conv_out, w_o, norm_ffn_w,
           w_router, b_router, w_up, b_up, w_down, b_down, final_norm_w):
    bsz, seq, d = x.shape
    assert norm_mix_w.shape[0] == 1, "single-layer block"
    out = _layer(x.reshape(bsz * seq, d), meta_tokens.astype(x.dtype), bsz, seq, norm_mix_w[0], w_in[0], lb_logits,
                 g_norm_w[0], w_hgrn_out[0], conv_w[0], w_conv_out[0], w_o[0], norm_ffn_w[0], w_router[0],
                 b_router[0], w_up[0], b_up[0], w_down[0], b_down[0], final_norm_w)
    return out.reshape(bsz, seq, d)
```

```python
import functools

import numpy as np
import jax
import jax.numpy as jnp
from jax import lax
from jax.experimental import pallas as pl
from jax.experimental.pallas import tpu as pltpu

F32 = jnp.float32
BF16 = jnp.bfloat16

LANES = 128
N_META = 16
HEADS = 8
HEAD_DIM = 128
N_EXPERTS = 32
TOP_K = 4
SWIGLU_LIMIT = 7.0
SWIGLU_ALPHA = 1.702
EPS = 1e-6

CHUNK = 256
HGRN_BASE = 64
HGRN_ROWS = 512
COL_TILE = 1024
EXPERT_SUB = 128
EXPERT_SUBS = 18
EXPERT_CHUNK_SUBS = 8
EXPERT_FF_TILE = 512
EXPERT_OUT_TILE = 512
COMBINE_ROWS = 128

V7X_VMEM_LIMIT = 56 * 1024 * 1024


def _dot(a, b):
    return jnp.dot(a, b, preferred_element_type=F32)


def _dot_bt(a, b):
    return lax.dot_general(a, b, (((1,), (1,)), ((), ())), preferred_element_type=F32)


def _dot_at(a, b):
    return lax.dot_general(a, b, (((0,), (0,)), ((), ())), preferred_element_type=F32)


def _split3(x):
    hi = x.astype(BF16)
    r1 = x - hi.astype(F32)
    mid = r1.astype(BF16)
    lo = (r1 - mid.astype(F32)).astype(BF16)
    return hi, mid, lo


def _params(*sem):
    return pltpu.CompilerParams(dimension_semantics=sem, vmem_limit_bytes=V7X_VMEM_LIMIT)


def _activate(z, code):
    s = jax.nn.sigmoid(z)
    return jnp.where(code == 0, z, jnp.where(code == 1, z * s, s))


def _inproj_kernel(col_ref, code_ref, x_ref, xm_ref, w_ref, o_ref, om_ref, wb_ref):
    code = code_ref[pl.program_id(0)]

    @pl.when(pl.program_id(1) == 0)
    def _():
        wb_ref[...] = w_ref[...].astype(BF16)
        om_ref[...] = _activate(_dot(xm_ref[...], wb_ref[...]), code).astype(om_ref.dtype)

    o_ref[...] = _activate(_dot(x_ref[...], wb_ref[...]), code).astype(o_ref.dtype)


def _inproj(xn, xn_meta, w_in, cols, codes, tm):
    m, d = xn.shape
    mm = xn_meta.shape[0]
    nt = len(cols)
    grid_spec = pltpu.PrefetchScalarGridSpec(
        num_scalar_prefetch=2,
        grid=(nt, m // tm),
        in_specs=[
            pl.BlockSpec((tm, d), lambda n, i, col, code: (i, 0)),
            pl.BlockSpec((mm, d), lambda n, i, col, code: (0, 0)),
            pl.BlockSpec((d, COL_TILE), lambda n, i, col, code: (0, col[n])),
        ],
        out_specs=[
            pl.BlockSpec((tm, COL_TILE), lambda n, i, col, code: (i, n)),
            pl.BlockSpec((mm, COL_TILE), lambda n, i, col, code: (0, n)),
        ],
        scratch_shapes=[pltpu.VMEM((d, COL_TILE), BF16)],
    )
    return pl.pallas_call(
        _inproj_kernel,
        grid_spec=grid_spec,
        out_shape=[jax.ShapeDtypeStruct((m, nt * COL_TILE), BF16), jax.ShapeDtypeStruct((mm, nt * COL_TILE), BF16)],
        compiler_params=_params("arbitrary", "arbitrary"),
        name="inproj",
    )(jnp.asarray(cols, jnp.int32), jnp.asarray(codes, jnp.int32), xn, xn_meta, w_in)


def _fgate_kernel(x_ref, xm_ref, nw_ref, w_ref, lbl_ref, xn_ref, lf_ref, kk_ref, xnm_ref, lfm_ref, kkm_ref, wb_ref):
    lbl = lbl_ref[...]
    e = jnp.exp(lbl - jnp.max(lbl, axis=0, keepdims=True))
    lb = e[0:1] / jnp.sum(e, axis=0, keepdims=True)

    def gate(x, xn_out, lf_out, kk_out):
        ms = jnp.mean(x * x, axis=-1, keepdims=True)
        xn = (x * lax.rsqrt(ms + EPS) * nw_ref[...]).astype(BF16)
        xn_out[...] = xn
        z = _dot(xn, wb_ref[...])
        lf_out[...] = jnp.log(lb + (1.0 - lb) * jax.nn.sigmoid(z))
        kk_out[...] = (1.0 - lb) * jax.nn.sigmoid(-z)

    @pl.when(pl.program_id(0) == 0)
    def _():
        wb_ref[...] = w_ref[...].astype(BF16)
        gate(xm_ref[...], xnm_ref, lfm_ref, kkm_ref)

    gate(x_ref[...], xn_ref, lf_ref, kk_ref)


def _fgate(x, x_meta, norm_w, w_in, lb_logits, col, tm):
    m, d = x.shape
    mm = x_meta.shape[0]
    r = lb_logits.shape[0]
    out = jax.ShapeDtypeStruct((m, COL_TILE), F32)
    out_meta = jax.ShapeDtypeStruct((mm, COL_TILE), F32)
    return pl.pallas_call(
        _fgate_kernel,
        grid=(m // tm,),
        in_specs=[
            pl.BlockSpec((tm, d), lambda i: (i, 0)),
            pl.BlockSpec((mm, d), lambda i: (0, 0)),
            pl.BlockSpec((1, d), lambda i: (0, 0)),
            pl.BlockSpec((d, COL_TILE), lambda i: (0, col)),
            pl.BlockSpec((r, COL_TILE), lambda i: (0, 0)),
        ],
        out_specs=[pl.BlockSpec((tm, d), lambda i: (i, 0))] + [pl.BlockSpec((tm, COL_TILE), lambda i: (i, 0))] * 2
        + [pl.BlockSpec((mm, d), lambda i: (0, 0))] + [pl.BlockSpec((mm, COL_TILE), lambda i: (0, 0))] * 2,
        out_shape=[jax.ShapeDtypeStruct((m, d), BF16), out, out,
                   jax.ShapeDtypeStruct((mm, d), BF16), out_meta, out_meta],
        scratch_shapes=[pltpu.VMEM((d, COL_TILE), BF16)],
        compiler_params=_params("arbitrary"),
        name="norm_fgate",
    )(x, x_meta, norm_w.reshape(1, d), w_in, lb_logits)


def _hgrn_constants(c):
    base = HGRN_BASE
    nlow = int(np.log2(base))
    nl = int(np.log2(c))
    assert (1 << nl) == c and (1 << nlow) == base and c >= base
    rr = np.arange(base)[:, None]
    uu = np.arange(base)[None, :]
    mats = [uu <= rr, uu > rr]
    sels = []
    for lvl in range(nlow):
        b = 1 << lvl
        start = (rr // (2 * b)) * (2 * b)
        mid = start + b - 1
        second = (rr - start) >= b
        mats.append(np.where(second, (uu > mid) & (uu <= rr), (uu > rr) & (uu <= mid)))
        sels.append(np.broadcast_to(second, (base, HEADS * HEAD_DIM)))
    tt = np.arange(c)[:, None]
    ss = np.arange(c)[None, :]
    masks = []
    for lvl in range(nl):
        b = 1 << lvl
        masks.append(((tt // (2 * b)) == (ss // (2 * b))) & ((tt % (2 * b)) >= b) & ((ss % (2 * b)) < b))
    masks.append(np.eye(c, dtype=bool))
    m1 = np.concatenate(mats, 0).astype(np.float32)
    mall = jnp.asarray(np.concatenate([m1, m1, m1], axis=1), BF16)
    return mall, jnp.asarray(np.stack(masks).astype(np.float32)), jnp.asarray(np.stack(sels).astype(np.float32))


def _hgrn_kernel(q_ref, v_ref, g_ref, lf_ref, kk_ref, gw_ref, s0_ref, mall_ref, mask_ref, sel_ref,
                 o_ref, sfin_ref, st_ref, *, chunk, n_chunks):
    base = HGRN_BASE
    nb = chunk // base
    nlow = sel_ref.shape[0]
    nl = mask_ref.shape[0] - 1
    step = pl.program_id(1)

    @pl.when(step == 0)
    def _():
        st_ref[...] = s0_ref[...]

    def chunk_body(ci, carry):
        r0 = pl.multiple_of(ci * chunk, chunk)
        rows = pl.ds(r0, chunk)
        qb = q_ref[rows, :]
        q = qb.astype(F32)
        k = kk_ref[rows, :]
        blk = lambda a, i: a[i * base:(i + 1) * base]

        pre, suf, e_low = [], [], []
        for i in range(nb):
            hi, mid, lo = _split3(lf_ref[pl.ds(r0 + i * base, base), :])
            args = _dot(mall_ref[...], jnp.concatenate([hi, mid, lo], axis=0))
            pre.append(args[0:base])
            suf.append(args[base:2 * base])
            e_low.append(jnp.exp(args[2 * base:]))
        tot = [p[base - 1:base] for p in pre]

        def span(lo_blk, hi_blk):
            acc = None
            for j in range(lo_blk, hi_blk):
                acc = tot[j] if acc is None else acc + tot[j]
            return acc

        def shifted(a, off):
            return a if off is None else a + off

        xs = []
        for lvl in range(nlow):
            parts = [jnp.where(sel_ref[lvl] > 0.5, blk(q, i), blk(k, i)) * blk(e_low[i], lvl) for i in range(nb)]
            xs.append(jnp.concatenate(parts, axis=0).astype(BF16))
        for lvl in range(nlow, nl):
            half = (1 << lvl) // base
            parts = []
            for i in range(nb):
                g = i % (2 * half)
                if g >= half:
                    parts.append(blk(q, i) * jnp.exp(shifted(pre[i], span(i - (g - half), i))))
                else:
                    parts.append(blk(k, i) * jnp.exp(shifted(suf[i], span(i + 1, i - g + half))))
            xs.append(jnp.concatenate(parts, axis=0).astype(BF16))
        q_in = jnp.concatenate([blk(q, i) * jnp.exp(shifted(pre[i], span(0, i))) for i in range(nb)],
                               axis=0).astype(BF16)
        k_out = jnp.concatenate([blk(k, i) * jnp.exp(shifted(suf[i], span(i + 1, nb))) for i in range(nb)],
                                axis=0).astype(BF16)
        dec = jnp.exp(span(0, nb))
        kb = k.astype(BF16)

        for h in range(HEADS):
            cs = slice(h * HEAD_DIM, (h + 1) * HEAD_DIM)
            scores = mask_ref[nl] * _dot_bt(qb[:, cs], kb[:, cs])
            for lvl in range(nl):
                x = xs[lvl][:, cs]
                scores = scores + mask_ref[lvl] * _dot_bt(x, x)
            v = v_ref[rows, cs]
            st = st_ref[h]
            o = _dot(scores.astype(BF16), v) + _dot_bt(q_in[:, cs], st.astype(BF16))
            st_ref[h] = st * dec[:, cs] + _dot_at(v, k_out[:, cs])
            ms = jnp.mean(o * o, axis=-1, keepdims=True)
            on = o * lax.rsqrt(ms + EPS) * gw_ref[...]
            o_ref[rows, cs] = (on * g_ref[rows, cs].astype(F32)).astype(o_ref.dtype)
        return carry

    lax.fori_loop(0, n_chunks, chunk_body, 0)

    @pl.when(step == pl.num_programs(1) - 1)
    def _():
        sfin_ref[0] = st_ref[...]


def _hgrn(proj, lf, kk, g_norm_w, s0, consts, bsz, seq, rows):
    mall, masks, sels = consts
    steps = seq // rows
    w = HEADS * HEAD_DIM
    assert w == COL_TILE
    row_map = lambda col: (lambda b, s: (b * steps + s, col))
    const2 = lambda b, s: (0, 0)
    const3 = lambda b, s: (0, 0, 0)
    kern = functools.partial(_hgrn_kernel, chunk=CHUNK, n_chunks=rows // CHUNK)
    return pl.pallas_call(
        kern,
        grid=(bsz, steps),
        in_specs=[
            pl.BlockSpec((rows, w), row_map(0)),
            pl.BlockSpec((rows, w), row_map(1)),
            pl.BlockSpec((rows, w), row_map(2)),
            pl.BlockSpec((rows, w), row_map(0)),
            pl.BlockSpec((rows, w), row_map(0)),
            pl.BlockSpec((1, HEAD_DIM), const2),
            pl.BlockSpec((HEADS, HEAD_DIM, HEAD_DIM), const3),
            pl.BlockSpec(mall.shape, const2),
            pl.BlockSpec(masks.shape, const3),
            pl.BlockSpec(sels.shape, const3),
        ],
        out_specs=[
            pl.BlockSpec((rows, w), row_map(0)),
            pl.BlockSpec((1, HEADS, HEAD_DIM, HEAD_DIM), lambda b, s: (b, 0, 0, 0)),
        ],
        out_shape=[
            jax.ShapeDtypeStruct((bsz * seq, w), BF16),
            jax.ShapeDtypeStruct((bsz, HEADS, HEAD_DIM, HEAD_DIM), F32),
        ],
        scratch_shapes=[pltpu.VMEM((HEADS, HEAD_DIM, HEAD_DIM), F32)],
        compiler_params=_params("arbitrary", "arbitrary"),
        name="hgrn2",
    )(proj, proj, proj, lf, kk, g_norm_w.reshape(1, HEAD_DIM), s0, mall, masks, sels)


def _mixer_out_kernel(og_ref, scv_ref, scb_ref, scc_ref, ga_ref, gb_ref, x_ref,
                      pv_ref, pc_ref, mv_ref, mc_ref,
                      wa_ref, wb_ref, wo_ref, cw_ref, nw_ref, wr_ref, br_ref,
                      h1_ref, xn_ref, idx_ref, gate_ref, hbuf, *, tiles_per_seq, n_tiles):
    i = pl.program_id(0)
    tm = x_ref.shape[0]
    tile = jnp.minimum(i, n_tiles - 1)
    first = (tile % tiles_per_seq) == 0

    @pl.when(i == 0)
    def _():
        hbuf[1] = jnp.zeros(hbuf.shape[1:], hbuf.dtype)

    hp = hbuf[(i + 1) % 2]
    ms = jnp.mean(hp * hp, axis=-1, keepdims=True)
    xn = hp * lax.rsqrt(ms + EPS) * nw_ref[...]
    xn_ref[...] = xn.astype(BF16).reshape(xn_ref.shape)

    xh = xn.astype(BF16)
    xl = (xn - xh.astype(F32)).astype(BF16)
    wr = wr_ref[...]
    wh = wr.astype(BF16)
    wl = (wr - wh.astype(F32)).astype(BF16)
    logits = _dot_bt(wh, xh) + _dot_bt(wh, xl) + _dot_bt(wl, xh) + br_ref[...]
    ne = logits.shape[0]
    ie = lax.broadcasted_iota(jnp.int32, logits.shape, 0)
    tops, idxs = [], []
    for _ in range(TOP_K):
        mx = jnp.max(logits, axis=0, keepdims=True)
        ix = jnp.min(jnp.where(logits == mx, ie, ne), axis=0, keepdims=True)
        tops.append(mx)
        idxs.append(ix)
        logits = jnp.where(ie == ix, -jnp.inf, logits)
    es = [jnp.exp(t - tops[0]) for t in tops]
    den = es[0]
    for e in es[1:]:
        den = den + e
    gate_ref[...] = jnp.concatenate([e / den for e in es], axis=0)
    idx_ref[...] = jnp.concatenate(idxs, axis=0)

    u = scc_ref[...].astype(F32) * scv_ref[...].astype(F32)
    halo_prev = pc_ref[...].astype(F32) * pv_ref[...].astype(F32)
    halo_meta = mc_ref[...].astype(F32) * mv_ref[...].astype(F32)
    halo = jnp.where(first, halo_meta, halo_prev)
    hr = halo.shape[0]
    r = lax.broadcasted_iota(jnp.int32, (tm, 1), 0)
    u1 = jnp.where(r == 0, halo[hr - 1:hr], pltpu.roll(u, 1, 0))
    u2 = jnp.where(r == 0, halo[hr - 2:hr - 1], jnp.where(r == 1, halo[hr - 1:hr], pltpu.roll(u, 2, 0)))
    conv = cw_ref[2:3] * u + cw_ref[1:2] * u1 + cw_ref[0:1] * u2
    yb_in = (scb_ref[...].astype(F32) * conv).astype(BF16)

    y_a = _dot(og_ref[...], wa_ref[...])
    y_b = _dot(yb_in, wb_ref[...])
    merged = (ga_ref[...].astype(F32) * y_a + gb_ref[...].astype(F32) * y_b).astype(BF16)
    h1 = x_ref[...] + _dot(merged, wo_ref[...])
    h1_ref[...] = h1
    hbuf[i % 2] = h1


def _mixer_out(og, proj, proj_meta, x2d, wa, wb, wo, conv_w, norm_w, w_router, b_router, seq, tm):
    m, d = x2d.shape
    w = COL_TILE
    halo = proj_meta.shape[0]
    assert tm % halo == 0 and seq % tm == 0 and d == 2 * w
    ne = w_router.shape[1]
    per_halo = tm // halo
    n_tiles = m // tm
    cur = lambda i: jnp.minimum(i, n_tiles - 1)
    done = lambda i: jnp.maximum(i - 1, 0)
    row = lambda col: (lambda i: (cur(i), col))
    prev = lambda col: (lambda i: (jnp.maximum(cur(i) * per_halo - 1, 0), col))
    const = lambda i: (0, 0)
    whole = lambda a: pl.BlockSpec(a.shape, const)
    wr_t = w_router.T
    kern = functools.partial(_mixer_out_kernel, tiles_per_seq=seq // tm, n_tiles=n_tiles)
    return pl.pallas_call(
        kern,
        grid=(n_tiles + 1,),
        in_specs=[
            pl.BlockSpec((tm, w), row(0)),
            pl.BlockSpec((tm, w), row(3)),
            pl.BlockSpec((tm, w), row(4)),
            pl.BlockSpec((tm, w), row(5)),
            pl.BlockSpec((tm, d), row(3)),
            pl.BlockSpec((tm, d), row(4)),
            pl.BlockSpec((tm, d), row(0)),
            pl.BlockSpec((halo, w), prev(3)),
            pl.BlockSpec((halo, w), prev(5)),
            pl.BlockSpec((halo, w), lambda i: (0, 3)),
            pl.BlockSpec((halo, w), lambda i: (0, 5)),
            whole(wa), whole(wb), whole(wo),
            pl.BlockSpec(conv_w.shape, const),
            pl.BlockSpec((1, d), const),
            pl.BlockSpec((ne, d), const),
            pl.BlockSpec((ne, 1), const),
        ],
        out_specs=[
            pl.BlockSpec((tm, d), row(0)),
            pl.BlockSpec((tm, d // LANES, LANES), lambda i: (done(i), 0, 0)),
            pl.BlockSpec((TOP_K, tm), lambda i: (0, done(i))),
            pl.BlockSpec((TOP_K, tm), lambda i: (0, done(i))),
        ],
        out_shape=[
            jax.ShapeDtypeStruct((m, d), F32),
            jax.ShapeDtypeStruct((m, d // LANES, LANES), BF16),
            jax.ShapeDtypeStruct((TOP_K, m), jnp.int32),
            jax.ShapeDtypeStruct((TOP_K, m), F32),
        ],
        scratch_shapes=[pltpu.VMEM((2, tm, d), F32)],
        compiler_params=_params("arbitrary"),
        name="mixer_out",
    )(og, proj, proj, proj, proj, proj, x2d, proj, proj, proj_meta, proj_meta,
      wa, wb, wo, conv_w, norm_w.reshape(1, d), wr_t, b_router.reshape(ne, 1))


def _dispatch_kernel(cnt_ref, pst_ref, used_ref, pos_ref, x_ref, o_hbm, xbuf, zbuf, sem, zsem):
    i = pl.program_id(0)
    n_steps = pl.num_programs(0)
    tt = x_ref.shape[0]
    n_sub = o_hbm.shape[0] // EXPERT_SUB
    bits = [1 << b for b in reversed(range(EXPERT_SUB.bit_length() - 1))]
    slot = i % 2

    xbuf[slot] = x_ref[...]

    def issue(r, carry):
        for k in range(TOP_K):
            p = pos_ref[0, 0, k * tt + r]
            pltpu.make_async_copy(xbuf.at[slot, pl.ds(r, 1)], o_hbm.at[pl.ds(p, 1)], sem.at[slot]).start()
        return carry

    lax.fori_loop(0, tt, issue, 0, unroll=4)

    def tokens_done(s):
        return [pltpu.make_async_copy(xbuf.at[s], o_hbm.at[pl.ds(0, tt)], sem.at[s]) for _ in range(TOP_K)]

    def zero_copies(fn):
        for e in range(cnt_ref.shape[0]):
            npad = (-cnt_ref[e]) & (EXPERT_SUB - 1)
            base = pst_ref[e] + cnt_ref[e]
            for bit in bits:
                @pl.when((npad & bit) != 0)
                def _():
                    row = base + (npad & ~(2 * bit - 1))
                    fn(pltpu.make_async_copy(zbuf.at[pl.ds(0, bit)], o_hbm.at[pl.ds(row, bit)], zsem))
        for j in range(cnt_ref.shape[0]):
            blk = used_ref[0] + j

            @pl.when(blk < n_sub)
            def _():
                row = pl.multiple_of(blk * EXPERT_SUB, EXPERT_SUB)
                fn(pltpu.make_async_copy(zbuf, o_hbm.at[pl.ds(row, EXPERT_SUB)], zsem))

    @pl.when(i == 0)
    def _():
        zbuf[...] = jnp.zeros_like(zbuf)
        zero_copies(lambda c: c.start())

    @pl.when(i > 0)
    def _():
        for c in tokens_done(1 - slot):
            c.wait()

    @pl.when(i == n_steps - 1)
    def _():
        for c in tokens_done(slot):
            c.wait()
        zero_copies(lambda c: c.wait())


def _dispatch(xn3, pos_t, counts, pad_start, used_sub, n_rows):
    m, sub, lanes = xn3.shape
    nt = pos_t.shape[0]
    tt = m // nt
    grid_spec = pltpu.PrefetchScalarGridSpec(
        num_scalar_prefetch=3,
        grid=(nt,),
        in_specs=[
            pl.BlockSpec((1, 1, pos_t.shape[2]), lambda i, c, p, u: (i, 0, 0), memory_space=pltpu.SMEM),
            pl.BlockSpec((tt, sub, lanes), lambda i, c, p, u: (i, 0, 0)),
        ],
        out_specs=pl.BlockSpec(memory_space=pl.ANY),
        scratch_shapes=[pltpu.VMEM((2, tt, sub, lanes), xn3.dtype), pltpu.VMEM((EXPERT_SUB, sub, lanes), xn3.dtype),
                        pltpu.SemaphoreType.DMA((2,)), pltpu.SemaphoreType.DMA(())],
    )
    return pl.pallas_call(
        _dispatch_kernel,
        grid_spec=grid_spec,
        out_shape=jax.ShapeDtypeStruct((n_rows, sub, lanes), xn3.dtype),
        compiler_params=_params("arbitrary"),
        name="dispatch",
    )(counts, pad_start, used_sub.reshape(1), pos_t, xn3)


def _row_chunks(nsub, chunk_fn):
    per = EXPERT_CHUNK_SUBS
    big = per * EXPERT_SUB
    n_big = lax.div(nsub, per)

    def body(c, carry):
        chunk_fn(c * per, pl.multiple_of(c * big, big), big)
        return carry

    lax.fori_loop(0, n_big, body, 0)
    rem = nsub - n_big * per
    bit = per // 2
    while bit >= 1:
        done = rem & ~(2 * bit - 1)

        @pl.when((rem & bit) != 0)
        def _():
            sub0 = n_big * per + done
            chunk_fn(sub0, pl.multiple_of(sub0 * EXPERT_SUB, EXPERT_SUB), bit * EXPERT_SUB)

        bit //= 2


def _ffn_up_kernel(st_ref, ns_ref, zf_ref, se_ref, jm_ref, x_hbm, wg_ref, wu_ref, bg_ref, bu_ref, h_hbm,
                   xstage, xbuf, hbuf, wgb_ref, wub_ref, sem_x, sem_h):
    s = pl.program_id(0)
    j = pl.program_id(1)
    n_s = pl.num_programs(0)
    nj = pl.num_programs(1)
    step = s * nj + j
    subs = xbuf.shape[0] // EXPERT_SUB
    nsub = ns_ref[s]
    real = jnp.logical_and(nsub > 0, zf_ref[s] == 0)
    hs = step % 2

    def x_copy(sb, b):
        row = pl.multiple_of(st_ref[sb] + b * EXPERT_SUB, EXPERT_SUB)
        return pltpu.make_async_copy(x_hbm.at[pl.ds(row, EXPERT_SUB)],
                                     xstage.at[b], sem_x)

    def h_copy(sb, jj, b, slot):
        row = pl.multiple_of(st_ref[sb] + b * EXPERT_SUB, EXPERT_SUB)
        return pltpu.make_async_copy(hbuf.at[slot, pl.ds(b * EXPERT_SUB, EXPERT_SUB)],
                                     h_hbm.at[jj, pl.ds(row, EXPERT_SUB)], sem_h.at[slot])

    def for_x_subs(sb, fn):
        for b in range(subs):
            @pl.when(jnp.logical_and(b < ns_ref[sb], zf_ref[sb] == 0))
            def _():
                fn(b)

    def for_h_subs(sb, fn):
        for b in range(subs):
            @pl.when(b < ns_ref[sb])
            def _():
                fn(b)

    @pl.when(step == 0)
    def _():
        for_x_subs(0, lambda b: x_copy(0, b).start())

    @pl.when(j == 0)
    def _():
        for_x_subs(s, lambda b: x_copy(s, b).wait())

    @pl.when(step >= 2)
    def _():
        sp = lax.div(step - 2, nj)
        jp = step - 2 - sp * nj
        for_h_subs(sp, lambda b: h_copy(sp, jp, b, hs).wait())

    @pl.when(real)
    def _():
        wgb_ref[...] = wg_ref[0].astype(BF16)
        wub_ref[...] = wu_ref[0].astype(BF16)

        def make_chunk(first_tile):
            def chunk(sub0, row0, nrows):
                if first_tile:
                    xs = jnp.concatenate([xstage[sub0 + t].reshape(EXPERT_SUB, xbuf.shape[1])
                                          for t in range(nrows // EXPERT_SUB)], axis=0)
                    xbuf[pl.ds(row0, nrows), :] = xs
                else:
                    xs = xbuf[pl.ds(row0, nrows), :]
                g = _dot(xs, wgb_ref[...]) + bg_ref[0]
                u = _dot(xs, wub_ref[...]) + bu_ref[0]
                g = jnp.minimum(g, SWIGLU_LIMIT)
                u = jnp.clip(u, -SWIGLU_LIMIT, SWIGLU_LIMIT)
                hbuf[hs, pl.ds(row0, nrows), :] = ((u + 1.0) * (g * jax.nn.sigmoid(SWIGLU_ALPHA * g))).astype(BF16)

            return chunk

        @pl.when(j == 0)
        def _():
            _row_chunks(nsub, make_chunk(True))

        @pl.when(j > 0)
        def _():
            _row_chunks(nsub, make_chunk(False))

    @pl.when(jnp.logical_and(j == 0, s + 1 < n_s))
    def _():
        nxt = jnp.minimum(s + 1, n_s - 1)
        for_x_subs(nxt, lambda b: x_copy(nxt, b).start())

    @pl.when(zf_ref[s] == 1)
    def _():
        def zero_sub(b):
            hbuf[hs, b * EXPERT_SUB:(b + 1) * EXPERT_SUB, :] = jnp.zeros((EXPERT_SUB, hbuf.shape[2]), hbuf.dtype)

        for_h_subs(s, zero_sub)

    for_h_subs(s, lambda b: h_copy(s, j, b, hs).start())

    @pl.when(step == n_s * nj - 1)
    def _():
        sp = lax.div(step - 1, nj)
        jp = step - 1 - sp * nj
        for_h_subs(sp, lambda b: h_copy(sp, jp, b, 1 - hs).wait())
        for_h_subs(s, lambda b: h_copy(s, j, b, hs).wait())


def _ffn_down_kernel(st_ref, ns_ref, zf_ref, se_ref, cm_ref, h_hbm, wd_ref, bd_ref, y_hbm,
                     hb, yrow, ytile, wdb_ref, sem_h, sem_y):
    s = pl.program_id(0)
    c = pl.program_id(1)
    n_s = pl.num_programs(0)
    nc = pl.num_programs(1)
    step = s * nc + c
    njh = hb.shape[1]
    subs = hb.shape[2] // EXPERT_SUB
    tn = wdb_ref.shape[1]
    nsub = ns_ref[s]
    real = jnp.logical_and(nsub > 0, zf_ref[s] == 0)

    def h_copy(sb, jj, b, slot):
        row = pl.multiple_of(st_ref[sb] + b * EXPERT_SUB, EXPERT_SUB)
        return pltpu.make_async_copy(h_hbm.at[jj, pl.ds(row, EXPERT_SUB)],
                                     hb.at[slot, jj, pl.ds(b * EXPERT_SUB, EXPERT_SUB)], sem_h.at[slot])

    def y_copy(sb, b):
        row = pl.multiple_of(st_ref[sb] + b * EXPERT_SUB, EXPERT_SUB)
        return pltpu.make_async_copy(ytile.at[b],
                                     y_hbm.at[pl.ds(row, EXPERT_SUB)], sem_y)

    def for_h_subs(sb, fn):
        for b in range(subs):
            @pl.when(jnp.logical_and(b < ns_ref[sb], zf_ref[sb] == 0))
            def _():
                for jj in range(njh):
                    fn(jj, b)

    def for_y_subs(sb, fn):
        for b in range(subs):
            @pl.when(b < ns_ref[sb])
            def _():
                fn(b)

    @pl.when(step == 0)
    def _():
        for_h_subs(0, lambda jj, b: h_copy(0, jj, b, 0).start())

    @pl.when(jnp.logical_and(c == 0, s + 1 < n_s))
    def _():
        nxt = jnp.minimum(s + 1, n_s - 1)
        for_h_subs(nxt, lambda jj, b: h_copy(nxt, jj, b, (s + 1) % 2).start())

    @pl.when(c == 0)
    def _():
        for_h_subs(s, lambda jj, b: h_copy(s, jj, b, s % 2).wait())

    @pl.when(jnp.logical_and(c == nc - 1, s > 0))
    def _():
        sp = jnp.maximum(s - 1, 0)
        for_y_subs(sp, lambda b: y_copy(sp, b).wait())

    @pl.when(real)
    def _():
        wdb_ref[...] = wd_ref[0].astype(BF16)
        h_slot = s % 2
        n_tiles = yrow.shape[1] // tn
        for cc in range(n_tiles):
            @pl.when(c == cc)
            def _():
                def chunk(sub0, row0, nrows):
                    hid = jnp.concatenate([hb[h_slot, jj, pl.ds(row0, nrows), :] for jj in range(njh)], axis=1)
                    y = (_dot(hid, wdb_ref[...]) + bd_ref[0]).astype(yrow.dtype)
                    if cc < n_tiles - 1:
                        yrow[pl.ds(row0, nrows), cc * tn:(cc + 1) * tn] = y
                    else:
                        full = jnp.concatenate([yrow[pl.ds(row0, nrows), 0:cc * tn], y], axis=1)
                        for t in range(nrows // EXPERT_SUB):
                            ytile[sub0 + t] = full[t * EXPERT_SUB:(t + 1) * EXPERT_SUB].reshape(ytile.shape[1:])

                _row_chunks(nsub, chunk)

    @pl.when(c == nc - 1)
    def _():
        def zero_sub(b):
            ytile[b] = jnp.zeros(ytile.shape[1:], ytile.dtype)

        @pl.when(zf_ref[s] == 1)
        def _():
            for_y_subs(s, zero_sub)

        for_y_subs(s, lambda b: y_copy(s, b).start())

        @pl.when(s == n_s - 1)
        def _():
            for_y_subs(s, lambda b: y_copy(s, b).wait())


def _experts(x_rows, w_up, b_up, w_down, b_down, sb_start, sb_nsub, sb_zero, sb_expert):
    n_rows, x_sub, x_lanes = x_rows.shape
    d = x_sub * x_lanes
    ne, _, ff2 = w_up.shape
    ff = ff2 // 2
    tf = EXPERT_FF_TILE
    tn = EXPERT_OUT_TILE
    nj = ff // tf
    nc = d // tn
    n_sb = jnp.sum((sb_nsub > 0).astype(jnp.int32))
    rows = EXPERT_SUBS * EXPERT_SUB
    is_real = jnp.logical_and(sb_nsub > 0, sb_zero == 0)[:, None]
    jm = jnp.where(is_real, jnp.arange(nj, dtype=jnp.int32)[None, :], nj - 1).astype(jnp.int32)
    cm = jnp.where(is_real, jnp.arange(nc, dtype=jnp.int32)[None, :], nc - 1).astype(jnp.int32)
    any_spec = pl.BlockSpec(memory_space=pl.ANY)

    up_spec = pltpu.PrefetchScalarGridSpec(
        num_scalar_prefetch=5,
        grid=(n_sb, nj),
        in_specs=[
            any_spec,
            pl.BlockSpec((1, d, tf), lambda s, j, st, ns, zf, se, jm: (se[s], 0, jm[s, j])),
            pl.BlockSpec((1, d, tf), lambda s, j, st, ns, zf, se, jm: (se[s], 0, nj + jm[s, j])),
            pl.BlockSpec((1, 1, tf), lambda s, j, st, ns, zf, se, jm: (se[s], 0, jm[s, j])),
            pl.BlockSpec((1, 1, tf), lambda s, j, st, ns, zf, se, jm: (se[s], 0, nj + jm[s, j])),
        ],
        out_specs=any_spec,
        scratch_shapes=[
            pltpu.VMEM((EXPERT_SUBS, EXPERT_SUB, x_sub, x_lanes), BF16),
            pltpu.VMEM((rows, d), BF16),
            pltpu.VMEM((2, rows, tf), BF16),
            pltpu.VMEM((d, tf), BF16),
            pltpu.VMEM((d, tf), BF16),
            pltpu.SemaphoreType.DMA(()),
            pltpu.SemaphoreType.DMA((2,)),
        ],
    )
    hidden = pl.pallas_call(
        _ffn_up_kernel,
        grid_spec=up_spec,
        out_shape=jax.ShapeDtypeStruct((nj, n_rows, tf), BF16),
        compiler_params=_params("arbitrary", "arbitrary"),
        name="ffn_up",
    )(sb_start, sb_nsub, sb_zero, sb_expert, jm, x_rows, w_up, w_up,
      b_up.reshape(ne, 1, ff2), b_up.reshape(ne, 1, ff2))

    down_spec = pltpu.PrefetchScalarGridSpec(
        num_scalar_prefetch=5,
        grid=(n_sb, nc),
        in_specs=[
            any_spec,
            pl.BlockSpec((1, ff, tn), lambda s, c, st, ns, zf, se, cm: (se[s], 0, cm[s, c])),
            pl.BlockSpec((1, 1, tn), lambda s, c, st, ns, zf, se, cm: (se[s], 0, cm[s, c])),
        ],
        out_specs=any_spec,
        scratch_shapes=[
            pltpu.VMEM((2, nj, rows, tf), BF16),
            pltpu.VMEM((rows, d), BF16),
            pltpu.VMEM((EXPERT_SUBS, EXPERT_SUB, x_sub, x_lanes), BF16),
            pltpu.VMEM((ff, tn), BF16),
            pltpu.SemaphoreType.DMA((2,)),
            pltpu.SemaphoreType.DMA(()),
        ],
    )
    return pl.pallas_call(
        _ffn_down_kernel,
        grid_spec=down_spec,
        out_shape=jax.ShapeDtypeStruct((n_rows, x_sub, x_lanes), BF16),
        compiler_params=_params("arbitrary", "arbitrary"),
        name="ffn_down",
    )(sb_start, sb_nsub, sb_zero, sb_expert, cm, hidden, w_down, b_down.reshape(ne, 1, d))


def _combine_kernel(pos_ref, pos_next_ref, gate_ref, h1_ref, fw_ref, y_hbm, o_ref, buf_ref, sem):
    i = pl.program_id(0)
    tt = h1_ref.shape[0]

    def start_rows(pos, slot):
        def issue(r, carry):
            for k in range(TOP_K):
                p = pos[0, 0, k * tt + r]
                pltpu.make_async_copy(y_hbm.at[pl.ds(p, 1)], buf_ref.at[slot, k, pl.ds(r, 1)], sem.at[slot]).start()
            return carry

        lax.fori_loop(0, tt, issue, 0, unroll=4)

    @pl.when(i == 0)
    def _():
        start_rows(pos_ref, 0)

    @pl.when(i + 1 < pl.num_programs(0))
    def _():
        start_rows(pos_next_ref, (i + 1) % 2)

    slot = i % 2
    for k in range(TOP_K):
        pltpu.make_async_copy(y_hbm.at[pl.ds(0, tt)], buf_ref.at[slot, k], sem.at[slot]).wait()

    gate = gate_ref[...]
    gpad = jnp.concatenate([gate, jnp.zeros((tt - TOP_K, tt), F32)], axis=0)
    gcol = gpad.T
    acc = h1_ref[...]
    for k in range(TOP_K):
        acc = acc + gcol[:, k:k + 1] * buf_ref[slot, k].reshape(acc.shape).astype(F32)
    ms = jnp.mean(acc * acc, axis=-1, keepdims=True)
    o_ref[...] = acc * lax.rsqrt(ms + EPS) * fw_ref[...]


def _pos_tiles(pos, m):
    tt = COMBINE_ROWS
    nt = m // tt
    return pos.reshape(TOP_K, nt, tt).transpose(1, 0, 2).reshape(nt, 1, TOP_K * tt)


def _combine(y_rows, pos_t, gate, h1, final_w):
    m, d = h1.shape
    tt = COMBINE_ROWS
    nt = m // tt
    return pl.pallas_call(
        _combine_kernel,
        grid=(nt,),
        in_specs=[
            pl.BlockSpec((1, 1, TOP_K * tt), lambda i: (i, 0, 0), memory_space=pltpu.SMEM),
            pl.BlockSpec((1, 1, TOP_K * tt), lambda i: (jnp.minimum(i + 1, nt - 1), 0, 0), memory_space=pltpu.SMEM),
            pl.BlockSpec((TOP_K, tt), lambda i: (0, i)),
            pl.BlockSpec((tt, d), lambda i: (i, 0)),
            pl.BlockSpec((1, d), lambda i: (0, 0)),
            pl.BlockSpec(memory_space=pl.ANY),
        ],
        out_specs=pl.BlockSpec((tt, d), lambda i: (i, 0)),
        out_shape=jax.ShapeDtypeStruct((m, d), F32),
        scratch_shapes=[pltpu.VMEM((2, TOP_K, tt) + y_rows.shape[1:], y_rows.dtype), pltpu.SemaphoreType.DMA((2,))],
        compiler_params=_params("arbitrary"),
        name="combine",
    )(pos_t, pos_t, gate, h1, final_w.reshape(1, d), y_rows)


def _routing_tables(idx, m):
    i32 = jnp.int32
    n_assign = TOP_K * m
    flat_e = idx.reshape(n_assign)
    onehot = flat_e[:, None] == jnp.arange(N_EXPERTS, dtype=i32)[None, :]
    csum = jnp.cumsum(onehot.astype(i32), axis=0)
    rank = jnp.sum(jnp.where(onehot, csum - 1, 0), axis=1)
    counts = csum[-1]
    padded = (counts + EXPERT_SUB - 1) // EXPERT_SUB * EXPERT_SUB
    pad_end = jnp.cumsum(padded)
    pad_start = pad_end - padded
    pos = (jnp.sum(jnp.where(onehot, pad_start[None, :], 0), axis=1) + rank).astype(i32)
    n_sub = -(-(n_assign + N_EXPERTS * (EXPERT_SUB - 1)) // EXPERT_SUB)
    n_rows = n_sub * EXPERT_SUB
    used_sub = pad_end[-1] // EXPERT_SUB

    rows = EXPERT_SUBS * EXPERT_SUB
    n_sb = -(-n_sub // EXPERT_SUBS) + N_EXPERTS + 1
    nsb_e = (padded + rows - 1) // rows
    sb_cum = jnp.cumsum(nsb_e)
    total_real = sb_cum[-1]
    s = jnp.arange(n_sb, dtype=i32)
    e_s = jnp.minimum(jnp.sum((sb_cum[None, :] <= s[:, None]).astype(i32), axis=1), N_EXPERTS - 1)
    local = s - (sb_cum[e_s] - nsb_e[e_s])
    real = s < total_real
    start_real = pad_start[e_s] + local * rows
    nsub_real = jnp.clip((padded[e_s] - local * rows) // EXPERT_SUB, 0, EXPERT_SUBS)
    fill_idx = s - total_real
    start_fill = pad_end[-1] + fill_idx * rows
    nsub_fill = jnp.clip(n_sub - used_sub - fill_idx * EXPERT_SUBS, 0, EXPERT_SUBS)
    is_fill = jnp.logical_and(jnp.logical_not(real), nsub_fill > 0)
    sb_start = jnp.where(real, start_real, jnp.where(is_fill, start_fill, 0)).astype(i32)
    sb_nsub = jnp.where(real, nsub_real, jnp.where(is_fill, nsub_fill, 0)).astype(i32)
    sb_zero = is_fill.astype(i32)
    last_e = e_s[jnp.maximum(total_real - 1, 0)]
    sb_expert = jnp.where(real, e_s, last_e).astype(i32)
    return pos, counts.astype(i32), pad_start.astype(i32), used_sub.astype(i32), n_rows, sb_start, sb_nsub, sb_zero, sb_expert


def _main_tiles(d):
    w = HEADS * HEAD_DIM
    sc = d // 2
    sizes = (w, w, w, w, sc, sc, sc, d, d)
    acts = (1, None, 0, 1, 0, 0, 0, 2, 2)
    order = (0, 2, 3, 4, 5, 6, 7, 8)
    starts = np.concatenate([[0], np.cumsum(sizes)])
    cols, codes = [], []
    for seg in order:
        assert sizes[seg] % COL_TILE == 0 and starts[seg] % COL_TILE == 0
        for t in range(sizes[seg] // COL_TILE):
            cols.append(int(starts[seg]) // COL_TILE + t)
            codes.append(acts[seg])
    return cols, codes, int(starts[1]) // COL_TILE


def _layer(x2d, meta, bsz, seq, norm_mix_w, w_in, lb_logits, g_norm_w, w_hgrn_out, conv_w, w_conv_out, w_o,
           norm_ffn_w, w_router, b_router, w_up, b_up, w_down, b_down, final_norm_w):
    m, d = x2d.shape
    cols, codes, fcol = _main_tiles(d)
    consts = _hgrn_constants(CHUNK)

    xn, lf, kk, xn_meta, lf_meta, kk_meta = _fgate(x2d, meta, norm_mix_w, w_in, lb_logits, fcol, 512)
    proj, proj_meta = _inproj(xn, xn_meta, w_in, cols, codes, 1024)

    pad = CHUNK - N_META
    front = lambda a: jnp.pad(a, ((pad, 0), (0, 0)))
    s_zero = jnp.zeros((HEADS, HEAD_DIM, HEAD_DIM), F32)
    _, s_meta = _hgrn(front(proj_meta), front(lf_meta), front(kk_meta), g_norm_w, s_zero, consts, 1, CHUNK, CHUNK)
    og, _ = _hgrn(proj, lf, kk, g_norm_w, s_meta[0], consts, bsz, seq, HGRN_ROWS)
    h1, xn_ffn, idx, gate = _mixer_out(og, proj, proj_meta, x2d, w_hgrn_out.astype(BF16), w_conv_out.astype(BF16),
                                   w_o.astype(BF16), conv_w, norm_ffn_w, w_router, b_router, seq, 256)

    pos, counts, pad_start, used_sub, n_rows, sb_start, sb_nsub, sb_zero, sb_expert = _routing_tables(idx, m)
    pos_t = _pos_tiles(pos, m)
    x_rows = _dispatch(xn_ffn, pos_t, counts, pad_start, used_sub, n_rows)
    y_rows = _experts(x_rows, w_up, b_up, w_down, b_down, sb_start, sb_nsub, sb_zero, sb_expert)
    return _combine(y_rows, pos_t, gate, h1, final_norm_w)


def kernel(x, meta_tokens, norm_mix_w, w_in, lb_logits, g_norm_w, w_hgrn_out, conv_w, w_conv_out, w_o, norm_ffn_w,
           w_router, b_router, w_up, b_up, w_down, b_down, final_norm_w):
    bsz, seq, d = x.shape
    assert norm_mix_w.shape[0] == 1, "single-layer block"
    out = _layer(x.reshape(bsz * seq, d), meta_tokens.astype(x.dtype), bsz, seq, norm_mix_w[0], w_in[0], lb_logits,
                 g_norm_w[0], w_hgrn_out[0], conv_w[0], w_conv_out[0], w_o[0], norm_ffn_w[0], w_router[0],
                 b_router[0], w_up[0], b_up[0], w_down[0], b_down[0], final_norm_w)
    return out.reshape(bsz, seq, d)
```

```python
import functools

import numpy as np
import jax
import jax.numpy as jnp
from jax import lax
from jax.experimental import pallas as pl
from jax.experimental.pallas import tpu as pltpu

F32 = jnp.float32
BF16 = jnp.bfloat16

LANES = 128
N_META = 16
HEADS = 8
HEAD_DIM = 128
N_EXPERTS = 32
TOP_K = 4
SWIGLU_LIMIT = 7.0
SWIGLU_ALPHA = 1.702
EPS = 1e-6

CHUNK = 256
HGRN_BASE = 64
HGRN_ROWS = 512
COL_TILE = 1024
EXPERT_SUB = 128
EXPERT_SUBS = 18
EXPERT_CHUNK_SUBS = 8
EXPERT_FF_TILE = 512
EXPERT_OUT_TILE = 512
COMBINE_ROWS = 128

V7X_VMEM_LIMIT = 56 * 1024 * 1024


def _dot(a, b):
    return jnp.dot(a, b, preferred_element_type=F32)


def _dot_bt(a, b):
    return lax.dot_general(a, b, (((1,), (1,)), ((), ())), preferred_element_type=F32)


def _dot_at(a, b):
    return lax.dot_general(a, b, (((0,), (0,)), ((), ())), preferred_element_type=F32)


def _split3(x):
    hi = x.astype(BF16)
    r1 = x - hi.astype(F32)
    mid = r1.astype(BF16)
    lo = (r1 - mid.astype(F32)).astype(BF16)
    return hi, mid, lo


def _params(*sem):
    return pltpu.CompilerParams(dimension_semantics=sem, vmem_limit_bytes=V7X_VMEM_LIMIT)


def _activate(z, code):
    s = jax.nn.sigmoid(z)
    return jnp.where(code == 0, z, jnp.where(code == 1, z * s, s))


def _inproj_kernel(col_ref, code_ref, x_ref, xm_ref, w_ref, o_ref, om_ref, wb_ref):
    code = code_ref[pl.program_id(0)]

    @pl.when(pl.program_id(1) == 0)
    def _():
        wb_ref[...] = w_ref[...].astype(BF16)
        om_ref[...] = _activate(_dot(xm_ref[...], wb_ref[...]), code).astype(om_ref.dtype)

    o_ref[...] = _activate(_dot(x_ref[...], wb_ref[...]), code).astype(o_ref.dtype)


def _inproj(xn, xn_meta, w_in, cols, codes, tm):
    m, d = xn.shape
    mm = xn_meta.shape[0]
    nt = len(cols)
    grid_spec = pltpu.PrefetchScalarGridSpec(
        num_scalar_prefetch=2,
        grid=(nt, m // tm),
        in_specs=[
            pl.BlockSpec((tm, d), lambda n, i, col, code: (i, 0)),
            pl.BlockSpec((mm, d), lambda n, i, col, code: (0, 0)),
            pl.BlockSpec((d, COL_TILE), lambda n, i, col, code: (0, col[n])),
        ],
        out_specs=[
            pl.BlockSpec((tm, COL_TILE), lambda n, i, col, code: (i, n)),
            pl.BlockSpec((mm, COL_TILE), lambda n, i, col, code: (0, n)),
        ],
        scratch_shapes=[pltpu.VMEM((d, COL_TILE), BF16)],
    )
    return pl.pallas_call(
        _inproj_kernel,
        grid_spec=grid_spec,
        out_shape=[jax.ShapeDtypeStruct((m, nt * COL_TILE), BF16), jax.ShapeDtypeStruct((mm, nt * COL_TILE), BF16)],
        compiler_params=_params("arbitrary", "arbitrary"),
        name="inproj",
    )(jnp.asarray(cols, jnp.int32), jnp.asarray(codes, jnp.int32), xn, xn_meta, w_in)


def _fgate_kernel(x_ref, xm_ref, nw_ref, w_ref, lbl_ref, xn_ref, lf_ref, kk_ref, xnm_ref, lfm_ref, kkm_ref, wb_ref):
    lbl = lbl_ref[...]
    e = jnp.exp(lbl - jnp.max(lbl, axis=0, keepdims=True))
    lb = e[0:1] / jnp.sum(e, axis=0, keepdims=True)

    def gate(x, xn_out, lf_out, kk_out):
        ms = jnp.mean(x * x, axis=-1, keepdims=True)
        xn = (x * lax.rsqrt(ms + EPS) * nw_ref[...]).astype(BF16)
        xn_out[...] = xn
        z = _dot(xn, wb_ref[...])
        lf_out[...] = jnp.log(lb + (1.0 - lb) * jax.nn.sigmoid(z))
        kk_out[...] = (1.0 - lb) * jax.nn.sigmoid(-z)

    @pl.when(pl.program_id(0) == 0)
    def _():
        wb_ref[...] = w_ref[...].astype(BF16)
        gate(xm_ref[...], xnm_ref, lfm_ref, kkm_ref)

    gate(x_ref[...], xn_ref, lf_ref, kk_ref)


def _fgate(x, x_meta, norm_w, w_in, lb_logits, col, tm):
    m, d = x.shape
    mm = x_meta.shape[0]
    r = lb_logits.shape[0]
    out = jax.ShapeDtypeStruct((m, COL_TILE), F32)
    out_meta = jax.ShapeDtypeStruct((mm, COL_TILE), F32)
    return pl.pallas_call(
        _fgate_kernel,
        grid=(m // tm,),
        in_specs=[
            pl.BlockSpec((tm, d), lambda i: (i, 0)),
            pl.BlockSpec((mm, d), lambda i: (0, 0)),
            pl.BlockSpec((1, d), lambda i: (0, 0)),
            pl.BlockSpec((d, COL_TILE), lambda i: (0, col)),
            pl.BlockSpec((r, COL_TILE), lambda i: (0, 0)),
        ],
        out_specs=[pl.BlockSpec((tm, d), lambda i: (i, 0))] + [pl.BlockSpec((tm, COL_TILE), lambda i: (i, 0))] * 2
        + [pl.BlockSpec((mm, d), lambda i: (0, 0))] + [pl.BlockSpec((mm, COL_TILE), lambda i: (0, 0))] * 2,
        out_shape=[jax.ShapeDtypeStruct((m, d), BF16), out, out,
                   jax.ShapeDtypeStruct((mm, d), BF16), out_meta, out_meta],
        scratch_shapes=[pltpu.VMEM((d, COL_TILE), BF16)],
        compiler_params=_params("arbitrary"),
        name="norm_fgate",
    )(x, x_meta, norm_w.reshape(1, d), w_in, lb_logits)


def _hgrn_constants(c):
    base = HGRN_BASE
    nlow = int(np.log2(base))
    nl = int(np.log2(c))
    assert (1 << nl) == c and (1 << nlow) == base and c >= base
    rr = np.arange(base)[:, None]
    uu = np.arange(base)[None, :]
    mats = [uu <= rr, uu > rr]
    sels = []
    for lvl in range(nlow):
        b = 1 << lvl
        start = (rr // (2 * b)) * (2 * b)
        mid = start + b - 1
        second = (rr - start) >= b
        mats.append(np.where(second, (uu > mid) & (uu <= rr), (uu > rr) & (uu <= mid)))
        sels.append(np.broadcast_to(second, (base, HEADS * HEAD_DIM)))
    tt = np.arange(c)[:, None]
    ss = np.arange(c)[None, :]
    masks = []
    for lvl in range(nl):
        b = 1 << lvl
        masks.append(((tt // (2 * b)) == (ss // (2 * b))) & ((tt % (2 * b)) >= b) & ((ss % (2 * b)) < b))
    masks.append(np.eye(c, dtype=bool))
    m1 = np.concatenate(mats, 0).astype(np.float32)
    mall = jnp.asarray(np.concatenate([m1, m1, m1], axis=1), BF16)
    return mall, jnp.asarray(np.stack(masks).astype(np.float32)), jnp.asarray(np.stack(sels).astype(np.float32))


def _hgrn_kernel(q_ref, v_ref, g_ref, lf_ref, kk_ref, gw_ref, s0_ref, mall_ref, mask_ref, sel_ref,
                 o_ref, sfin_ref, st_ref, *, chunk, n_chunks):
    base = HGRN_BASE
    nb = chunk // base
    nlow = sel_ref.shape[0]
    nl = mask_ref.shape[0] - 1
    step = pl.program_id(1)

    @pl.when(step == 0)
    def _():
        st_ref[...] = s0_ref[...]

    def chunk_body(ci, carry):
        r0 = pl.multiple_of(ci * chunk, chunk)
        rows = pl.ds(r0, chunk)
        qb = q_ref[rows, :]
        q = qb.astype(F32)
        k = kk_ref[rows, :]
        blk = lambda a, i: a[i * base:(i + 1) * base]

        pre, suf, e_low = [], [], []
        for i in range(nb):
            hi, mid, lo = _split3(lf_ref[pl.ds(r0 + i * base, base), :])
            args = _dot(mall_ref[...], jnp.concatenate([hi, mid, lo], axis=0))
            pre.append(args[0:base])
            suf.append(args[base:2 * base])
            e_low.append(jnp.exp(args[2 * base:]))
        tot = [p[base - 1:base] for p in pre]

        def span(lo_blk, hi_blk):
            acc = None
            for j in range(lo_blk, hi_blk):
                acc = tot[j] if acc is None else acc + tot[j]
            return acc

        def shifted(a, off):
            return a if off is None else a + off

        xs = []
        for lvl in range(nlow):
            parts = [jnp.where(sel_ref[lvl] > 0.5, blk(q, i), blk(k, i)) * blk(e_low[i], lvl) for i in range(nb)]
            xs.append(jnp.concatenate(parts, axis=0).astype(BF16))
        for lvl in range(nlow, nl):
            half = (1 << lvl) // base
            parts = []
            for i in range(nb):
                g = i % (2 * half)
                if g >= half:
                    parts.append(blk(q, i) * jnp.exp(shifted(pre[i], span(i - (g - half), i))))
                else:
                    parts.append(blk(k, i) * jnp.exp(shifted(suf[i], span(i + 1, i - g + half))))
            xs.append(jnp.concatenate(parts, axis=0).astype(BF16))
        q_in = jnp.concatenate([blk(q, i) * jnp.exp(shifted(pre[i], span(0, i))) for i in range(nb)],
                               axis=0).astype(BF16)
        k_out = jnp.concatenate([blk(k, i) * jnp.exp(shifted(suf[i], span(i + 1, nb))) for i in range(nb)],
                                axis=0).astype(BF16)
        dec = jnp.exp(span(0, nb))
        kb = k.astype(BF16)

        for h in range(HEADS):
            cs = slice(h * HEAD_DIM, (h + 1) * HEAD_DIM)
            scores = mask_ref[nl] * _dot_bt(qb[:, cs], kb[:, cs])
            for lvl in range(nl):
                x = xs[lvl][:, cs]
                scores = scores + mask_ref[lvl] * _dot_bt(x, x)
            v = v_ref[rows, cs]
            st = st_ref[h]
            o = _dot(scores.astype(BF16), v) + _dot_bt(q_in[:, cs], st.astype(BF16))
            st_ref[h] = st * dec[:, cs] + _dot_at(v, k_out[:, cs])
            ms = jnp.mean(o * o, axis=-1, keepdims=True)
            on = o * lax.rsqrt(ms + EPS) * gw_ref[...]
            o_ref[rows, cs] = (on * g_ref[rows, cs].astype(F32)).astype(o_ref.dtype)
        return carry

    lax.fori_loop(0, n_chunks, chunk_body, 0)

    @pl.when(step == pl.num_programs(1) - 1)
    def _():
        sfin_ref[0] = st_ref[...]


def _hgrn(proj, lf, kk, g_norm_w, s0, consts, bsz, seq, rows):
    mall, masks, sels = consts
    steps = seq // rows
    w = HEADS * HEAD_DIM
    assert w == COL_TILE
    row_map = lambda col: (lambda b, s: (b * steps + s, col))
    const2 = lambda b, s: (0, 0)
    const3 = lambda b, s: (0, 0, 0)
    kern = functools.partial(_hgrn_kernel, chunk=CHUNK, n_chunks=rows // CHUNK)
    return pl.pallas_call(
        kern,
        grid=(bsz, steps),
        in_specs=[
            pl.BlockSpec((rows, w), row_map(0)),
            pl.BlockSpec((rows, w), row_map(1)),
            pl.BlockSpec((rows, w), row_map(2)),
            pl.BlockSpec((rows, w), row_map(0)),
            pl.BlockSpec((rows, w), row_map(0)),
            pl.BlockSpec((1, HEAD_DIM), const2),
            pl.BlockSpec((HEADS, HEAD_DIM, HEAD_DIM), const3),
            pl.BlockSpec(mall.shape, const2),
            pl.BlockSpec(masks.shape, const3),
            pl.BlockSpec(sels.shape, const3),
        ],
        out_specs=[
            pl.BlockSpec((rows, w), row_map(0)),
            pl.BlockSpec((1, HEADS, HEAD_DIM, HEAD_DIM), lambda b, s: (b, 0, 0, 0)),
        ],
        out_shape=[
            jax.ShapeDtypeStruct((bsz * seq, w), BF16),
            jax.ShapeDtypeStruct((bsz, HEADS, HEAD_DIM, HEAD_DIM), F32),
        ],
        scratch_shapes=[pltpu.VMEM((HEADS, HEAD_DIM, HEAD_DIM), F32)],
        compiler_params=_params("arbitrary", "arbitrary"),
        name="hgrn2",
    )(proj, proj, proj, lf, kk, g_norm_w.reshape(1, HEAD_DIM), s0, mall, masks, sels)


def _mixer_out_kernel(og_ref, scv_ref, scb_ref, scc_ref, ga_ref, gb_ref, x_ref,
                      pv_ref, pc_ref, mv_ref, mc_ref,
                      wa_ref, wb_ref, wo_ref, cw_ref, nw_ref, wr_ref, br_ref,
                      h1_ref, xn_ref, idx_ref, gate_ref, hbuf, *, tiles_per_seq, n_tiles):
    i = pl.program_id(0)
    tm = x_ref.shape[0]
    tile = jnp.minimum(i, n_tiles - 1)
    first = (tile % tiles_per_seq) == 0

    @pl.when(i == 0)
    def _():
        hbuf[1] = jnp.zeros(hbuf.shape[1:], hbuf.dtype)

    hp = hbuf[(i + 1) % 2]
    ms = jnp.mean(hp * hp, axis=-1, keepdims=True)
    xn = hp * lax.rsqrt(ms + EPS) * nw_ref[...]
    xn_ref[...] = xn.astype(BF16).reshape(xn_ref.shape)

    xh = xn.astype(BF16)
    xl = (xn - xh.astype(F32)).astype(BF16)
    wr = wr_ref[...]
    wh = wr.astype(BF16)
    wl = (wr - wh.astype(F32)).astype(BF16)
    logits = _dot_bt(wh, xh) + _dot_bt(wh, xl) + _dot_bt(wl, xh) + br_ref[...]
    ne = logits.shape[0]
    ie = lax.broadcasted_iota(jnp.int32, logits.shape, 0)
    tops, idxs = [], []
    for _ in range(TOP_K):
        mx = jnp.max(logits, axis=0, keepdims=True)
        ix = jnp.min(jnp.where(logits == mx, ie, ne), axis=0, keepdims=True)
        tops.append(mx)
        idxs.append(ix)
        logits = jnp.where(ie == ix, -jnp.inf, logits)
    es = [jnp.exp(t - tops[0]) for t in tops]
    den = es[0]
    for e in es[1:]:
        den = den + e
    gate_ref[...] = jnp.concatenate([e / den for e in es], axis=0)
    idx_ref[...] = jnp.concatenate(idxs, axis=0)

    u = scc_ref[...].astype(F32) * scv_ref[...].astype(F32)
    halo_prev = pc_ref[...].astype(F32) * pv_ref[...].astype(F32)
    halo_meta = mc_ref[...].astype(F32) * mv_ref[...].astype(F32)
    halo = jnp.where(first, halo_meta, halo_prev)
    hr = halo.shape[0]
    r = lax.broadcasted_iota(jnp.int32, (tm, 1), 0)
    u1 = jnp.where(r == 0, halo[hr - 1:hr], pltpu.roll(u, 1, 0))
    u2 = jnp.where(r == 0, halo[hr - 2:hr - 1], jnp.where(r == 1, halo[hr - 1:hr], pltpu.roll(u, 2, 0)))
    conv = cw_ref[2:3] * u + cw_ref[1:2] * u1 + cw_ref[0:1] * u2
    yb_in = (scb_ref[...].astype(F32) * conv).astype(BF16)

    y_a = _dot(og_ref[...], wa_ref[...])
    y_b = _dot(yb_in, wb_ref[...])
    merged = (ga_ref[...].astype(F32) * y_a + gb_ref[...].astype(F32) * y_b).astype(BF16)
    h1 = x_ref[...] + _dot(merged, wo_ref[...])
    h1_ref[...] = h1
    hbuf[i % 2] = h1


def _mixer_out(og, proj, proj_meta, x2d, wa, wb, wo, conv_w, norm_w, w_router, b_router, seq, tm):
    m, d = x2d.shape
    w = COL_TILE
    halo = proj_meta.shape[0]
    assert tm % halo == 0 and seq % tm == 0 and d == 2 * w
    ne = w_router.shape[1]
    per_halo = tm // halo
    n_tiles = m // tm
    cur = lambda i: jnp.minimum(i, n_tiles - 1)
    done = lambda i: jnp.maximum(i - 1, 0)
    row = lambda col: (lambda i: (cur(i), col))
    prev = lambda col: (lambda i: (jnp.maximum(cur(i) * per_halo - 1, 0), col))
    const = lambda i: (0, 0)
    whole = lambda a: pl.BlockSpec(a.shape, const)
    wr_t = w_router.T
    kern = functools.partial(_mixer_out_kernel, tiles_per_seq=seq // tm, n_tiles=n_tiles)
    return pl.pallas_call(
        kern,
        grid=(n_tiles + 1,),
        in_specs=[
            pl.BlockSpec((tm, w), row(0)),
            pl.BlockSpec((tm, w), row(3)),
            pl.BlockSpec((tm, w), row(4)),
            pl.BlockSpec((tm, w), row(5)),
            pl.BlockSpec((tm, d), row(3)),
            pl.BlockSpec((tm, d), row(4)),
            pl.BlockSpec((tm, d), row(0)),
            pl.BlockSpec((halo, w), prev(3)),
            pl.BlockSpec((halo, w), prev(5)),
            pl.BlockSpec((halo, w), lambda i: (0, 3)),
            pl.BlockSpec((halo, w), lambda i: (0, 5)),
            whole(wa), whole(wb), whole(wo),
            pl.BlockSpec(conv_w.shape, const),
            pl.BlockSpec((1, d), const),
            pl.BlockSpec((ne, d), const),
            pl.BlockSpec((ne, 1), const),
        ],
        out_specs=[
            pl.BlockSpec((tm, d), row(0)),
            pl.BlockSpec((tm, d // LANES, LANES), lambda i: (done(i), 0, 0)),
            pl.BlockSpec((TOP_K, tm), lambda i: (0, done(i))),
            pl.BlockSpec((TOP_K, tm), lambda i: (0, done(i))),
        ],
        out_shape=[
            jax.ShapeDtypeStruct((m, d), F32),
            jax.ShapeDtypeStruct((m, d // LANES, LANES), BF16),
            jax.ShapeDtypeStruct((TOP_K, m), jnp.int32),
            jax.ShapeDtypeStruct((TOP_K, m), F32),
        ],
        scratch_shapes=[pltpu.VMEM((2, tm, d), F32)],
        compiler_params=_params("arbitrary"),
        name="mixer_out",
    )(og, proj, proj, proj, proj, proj, x2d, proj, proj, proj_meta, proj_meta,
      wa, wb, wo, conv_w, norm_w.reshape(1, d), wr_t, b_router.reshape(ne, 1))


def _dispatch_kernel(cnt_ref, pst_ref, used_ref, pos_ref, x_ref, o_hbm, xbuf, zbuf, sem, zsem):
    i = pl.program_id(0)
    n_steps = pl.num_programs(0)
    tt = x_ref.shape[0]
    n_sub = o_hbm.shape[0] // EXPERT_SUB
    bits = [1 << b for b in reversed(range(EXPERT_SUB.bit_length() - 1))]
    slot = i % 2

    xbuf[slot] = x_ref[...]

    def issue(r, carry):
        for k in range(TOP_K):
            p = pos_ref[0, 0, k * tt + r]
            pltpu.make_async_copy(xbuf.at[slot, pl.ds(r, 1)], o_hbm.at[pl.ds(p, 1)],
                                  sem.at[slot]).start(priority=k % 2)
        return carry

    lax.fori_loop(0, tt, issue, 0, unroll=4)

    def tokens_done(s):
        return [pltpu.make_async_copy(xbuf.at[s], o_hbm.at[pl.ds(0, tt)], sem.at[s]) for _ in range(TOP_K)]

    def zero_copies(fn):
        for e in range(cnt_ref.shape[0]):
            npad = (-cnt_ref[e]) & (EXPERT_SUB - 1)
            base = pst_ref[e] + cnt_ref[e]
            for bit in bits:
                @pl.when((npad & bit) != 0)
                def _():
                    row = base + (npad & ~(2 * bit - 1))
                    fn(pltpu.make_async_copy(zbuf.at[pl.ds(0, bit)], o_hbm.at[pl.ds(row, bit)], zsem))
        for j in range(cnt_ref.shape[0]):
            blk = used_ref[0] + j

            @pl.when(blk < n_sub)
            def _():
                row = pl.multiple_of(blk * EXPERT_SUB, EXPERT_SUB)
                fn(pltpu.make_async_copy(zbuf, o_hbm.at[pl.ds(row, EXPERT_SUB)], zsem))

    @pl.when(i == 0)
    def _():
        zbuf[...] = jnp.zeros_like(zbuf)
        zero_copies(lambda c: c.start())

    @pl.when(i > 0)
    def _():
        for c in tokens_done(1 - slot):
            c.wait()

    @pl.when(i == n_steps - 1)
    def _():
        for c in tokens_done(slot):
            c.wait()
        zero_copies(lambda c: c.wait())


def _dispatch(xn3, pos_t, counts, pad_start, used_sub, n_rows):
    m, sub, lanes = xn3.shape
    nt = pos_t.shape[0]
    tt = m // nt
    grid_spec = pltpu.PrefetchScalarGridSpec(
        num_scalar_prefetch=3,
        grid=(nt,),
        in_specs=[
            pl.BlockSpec((1, 1, pos_t.shape[2]), lambda i, c, p, u: (i, 0, 0), memory_space=pltpu.SMEM),
            pl.BlockSpec((tt, sub, lanes), lambda i, c, p, u: (i, 0, 0)),
        ],
        out_specs=pl.BlockSpec(memory_space=pl.ANY),
        scratch_shapes=[pltpu.VMEM((2, tt, sub, lanes), xn3.dtype), pltpu.VMEM((EXPERT_SUB, sub, lanes), xn3.dtype),
                        pltpu.SemaphoreType.DMA((2,)), pltpu.SemaphoreType.DMA(())],
    )
    return pl.pallas_call(
        _dispatch_kernel,
        grid_spec=grid_spec,
        out_shape=jax.ShapeDtypeStruct((n_rows, sub, lanes), xn3.dtype),
        compiler_params=_params("arbitrary"),
        name="dispatch",
    )(counts, pad_start, used_sub.reshape(1), pos_t, xn3)


def _row_chunks(nsub, chunk_fn):
    per = EXPERT_CHUNK_SUBS
    big = per * EXPERT_SUB
    n_big = lax.div(nsub, per)

    def body(c, carry):
        chunk_fn(c * per, pl.multiple_of(c * big, big), big)
        return carry

    lax.fori_loop(0, n_big, body, 0)
    rem = nsub - n_big * per
    bit = per // 2
    while bit >= 1:
        done = rem & ~(2 * bit - 1)

        @pl.when((rem & bit) != 0)
        def _():
            sub0 = n_big * per + done
            chunk_fn(sub0, pl.multiple_of(sub0 * EXPERT_SUB, EXPERT_SUB), bit * EXPERT_SUB)

        bit //= 2


def _ffn_up_kernel(st_ref, ns_ref, zf_ref, se_ref, jm_ref, x_hbm, wg_ref, wu_ref, bg_ref, bu_ref, h_hbm,
                   xstage, xbuf, hbuf, wgb_ref, wub_ref, sem_x, sem_h):
    s = pl.program_id(0)
    j = pl.program_id(1)
    n_s = pl.num_programs(0)
    nj = pl.num_programs(1)
    step = s * nj + j
    subs = xbuf.shape[0] // EXPERT_SUB
    nsub = ns_ref[s]
    real = jnp.logical_and(nsub > 0, zf_ref[s] == 0)
    hs = step % 2

    def x_copy(sb, b):
        row = pl.multiple_of(st_ref[sb] + b * EXPERT_SUB, EXPERT_SUB)
        return pltpu.make_async_copy(x_hbm.at[pl.ds(row, EXPERT_SUB)],
                                     xstage.at[b], sem_x)

    def h_copy(sb, jj, b, slot):
        row = pl.multiple_of(st_ref[sb] + b * EXPERT_SUB, EXPERT_SUB)
        return pltpu.make_async_copy(hbuf.at[slot, pl.ds(b * EXPERT_SUB, EXPERT_SUB)],
                                     h_hbm.at[jj, pl.ds(row, EXPERT_SUB)], sem_h.at[slot])

    def for_x_subs(sb, fn):
        for b in range(subs):
            @pl.when(jnp.logical_and(b < ns_ref[sb], zf_ref[sb] == 0))
            def _():
                fn(b)

    def for_h_subs(sb, fn):
        for b in range(subs):
            @pl.when(b < ns_ref[sb])
            def _():
                fn(b)

    @pl.when(step == 0)
    def _():
        for_x_subs(0, lambda b: x_copy(0, b).start())

    @pl.when(j == 0)
    def _():
        for_x_subs(s, lambda b: x_copy(s, b).wait())

    @pl.when(step >= 2)
    def _():
        sp = lax.div(step - 2, nj)
        jp = step - 2 - sp * nj
        for_h_subs(sp, lambda b: h_copy(sp, jp, b, hs).wait())

    @pl.when(real)
    def _():
        wgb_ref[...] = wg_ref[0].astype(BF16)
        wub_ref[...] = wu_ref[0].astype(BF16)

        def make_chunk(first_tile):
            def chunk(sub0, row0, nrows):
                if first_tile:
                    xs = jnp.concatenate([xstage[sub0 + t].reshape(EXPERT_SUB, xbuf.shape[1])
                                          for t in range(nrows // EXPERT_SUB)], axis=0)
                    xbuf[pl.ds(row0, nrows), :] = xs
                else:
                    xs = xbuf[pl.ds(row0, nrows), :]
                g = _dot(xs, wgb_ref[...]) + bg_ref[0]
                u = _dot(xs, wub_ref[...]) + bu_ref[0]
                g = jnp.minimum(g, SWIGLU_LIMIT)
                u = jnp.clip(u, -SWIGLU_LIMIT, SWIGLU_LIMIT)
                hbuf[hs, pl.ds(row0, nrows), :] = ((u + 1.0) * (g * jax.nn.sigmoid(SWIGLU_ALPHA * g))).astype(BF16)

            return chunk

        @pl.when(j == 0)
        def _():
            _row_chunks(nsub, make_chunk(True))

        @pl.when(j > 0)
        def _():
            _row_chunks(nsub, make_chunk(False))

    @pl.when(jnp.logical_and(j == 0, s + 1 < n_s))
    def _():
        nxt = jnp.minimum(s + 1, n_s - 1)
        for_x_subs(nxt, lambda b: x_copy(nxt, b).start())

    @pl.when(zf_ref[s] == 1)
    def _():
        def zero_sub(b):
            hbuf[hs, b * EXPERT_SUB:(b + 1) * EXPERT_SUB, :] = jnp.zeros((EXPERT_SUB, hbuf.shape[2]), hbuf.dtype)

        for_h_subs(s, zero_sub)

    for_h_subs(s, lambda b: h_copy(s, j, b, hs).start())

    @pl.when(step == n_s * nj - 1)
    def _():
        sp = lax.div(step - 1, nj)
        jp = step - 1 - sp * nj
        for_h_subs(sp, lambda b: h_copy(sp, jp, b, 1 - hs).wait())
        for_h_subs(s, lambda b: h_copy(s, j, b, hs).wait())


def _ffn_down_kernel(st_ref, ns_ref, zf_ref, se_ref, cm_ref, h_hbm, wd_ref, bd_ref, y_hbm,
                     hb, yrow, ytile, wdb_ref, sem_h, sem_y):
    s = pl.program_id(0)
    c = pl.program_id(1)
    n_s = pl.num_programs(0)
    nc = pl.num_programs(1)
    step = s * nc + c
    njh = hb.shape[1]
    subs = hb.shape[2] // EXPERT_SUB
    tn = wdb_ref.shape[1]
    nsub = ns_ref[s]
    real = jnp.logical_and(nsub > 0, zf_ref[s] == 0)

    def h_copy(sb, jj, b, slot):
        row = pl.multiple_of(st_ref[sb] + b * EXPERT_SUB, EXPERT_SUB)
        return pltpu.make_async_copy(h_hbm.at[jj, pl.ds(row, EXPERT_SUB)],
                                     hb.at[slot, jj, pl.ds(b * EXPERT_SUB, EXPERT_SUB)], sem_h.at[slot])

    def y_copy(sb, b):
        row = pl.multiple_of(st_ref[sb] + b * EXPERT_SUB, EXPERT_SUB)
        return pltpu.make_async_copy(ytile.at[b],
                                     y_hbm.at[pl.ds(row, EXPERT_SUB)], sem_y)

    def for_h_subs(sb, fn):
        for b in range(subs):
            @pl.when(jnp.logical_and(b < ns_ref[sb], zf_ref[sb] == 0))
            def _():
                for jj in range(njh):
                    fn(jj, b)

    def for_y_subs(sb, fn):
        for b in range(subs):
            @pl.when(b < ns_ref[sb])
            def _():
                fn(b)

    @pl.when(step == 0)
    def _():
        for_h_subs(0, lambda jj, b: h_copy(0, jj, b, 0).start())

    @pl.when(jnp.logical_and(c == 0, s + 1 < n_s))
    def _():
        nxt = jnp.minimum(s + 1, n_s - 1)
        for_h_subs(nxt, lambda jj, b: h_copy(nxt, jj, b, (s + 1) % 2).start())

    @pl.when(c == 0)
    def _():
        for_h_subs(s, lambda jj, b: h_copy(s, jj, b, s % 2).wait())

    @pl.when(jnp.logical_and(c == nc - 1, s > 0))
    def _():
        sp = jnp.maximum(s - 1, 0)
        for_y_subs(sp, lambda b: y_copy(sp, b).wait())

    @pl.when(real)
    def _():
        wdb_ref[...] = wd_ref[0].astype(BF16)
        h_slot = s % 2
        n_tiles = yrow.shape[1] // tn
        for cc in range(n_tiles):
            @pl.when(c == cc)
            def _():
                def chunk(sub0, row0, nrows):
                    hid = jnp.concatenate([hb[h_slot, jj, pl.ds(row0, nrows), :] for jj in range(njh)], axis=1)
                    y = (_dot(hid, wdb_ref[...]) + bd_ref[0]).astype(yrow.dtype)
                    if cc < n_tiles - 1:
                        yrow[pl.ds(row0, nrows), cc * tn:(cc + 1) * tn] = y
                    else:
                        full = jnp.concatenate([yrow[pl.ds(row0, nrows), 0:cc * tn], y], axis=1)
                        for t in range(nrows // EXPERT_SUB):
                            ytile[sub0 + t] = full[t * EXPERT_SUB:(t + 1) * EXPERT_SUB].reshape(ytile.shape[1:])

                _row_chunks(nsub, chunk)

    @pl.when(c == nc - 1)
    def _():
        def zero_sub(b):
            ytile[b] = jnp.zeros(ytile.shape[1:], ytile.dtype)

        @pl.when(zf_ref[s] == 1)
        def _():
            for_y_subs(s, zero_sub)

        for_y_subs(s, lambda b: y_copy(s, b).start())

        @pl.when(s == n_s - 1)
        def _():
            for_y_subs(s, lambda b: y_copy(s, b).wait())


def _experts(x_rows, w_up, b_up, w_down, b_down, sb_start, sb_nsub, sb_zero, sb_expert):
    n_rows, x_sub, x_lanes = x_rows.shape
    d = x_sub * x_lanes
    ne, _, ff2 = w_up.shape
    ff = ff2 // 2
    tf = EXPERT_FF_TILE
    tn = EXPERT_OUT_TILE
    nj = ff // tf
    nc = d // tn
    n_sb = jnp.sum((sb_nsub > 0).astype(jnp.int32))
    rows = EXPERT_SUBS * EXPERT_SUB
    is_real = jnp.logical_and(sb_nsub > 0, sb_zero == 0)[:, None]
    jm = jnp.where(is_real, jnp.arange(nj, dtype=jnp.int32)[None, :], nj - 1).astype(jnp.int32)
    cm = jnp.where(is_real, jnp.arange(nc, dtype=jnp.int32)[None, :], nc - 1).astype(jnp.int32)
    any_spec = pl.BlockSpec(memory_space=pl.ANY)

    up_spec = pltpu.PrefetchScalarGridSpec(
        num_scalar_prefetch=5,
        grid=(n_sb, nj),
        in_specs=[
            any_spec,
            pl.BlockSpec((1, d, tf), lambda s, j, st, ns, zf, se, jm: (se[s], 0, jm[s, j])),
            pl.BlockSpec((1, d, tf), lambda s, j, st, ns, zf, se, jm: (se[s], 0, nj + jm[s, j])),
            pl.BlockSpec((1, 1, tf), lambda s, j, st, ns, zf, se, jm: (se[s], 0, jm[s, j])),
            pl.BlockSpec((1, 1, tf), lambda s, j, st, ns, zf, se, jm: (se[s], 0, nj + jm[s, j])),
        ],
        out_specs=any_spec,
        scratch_shapes=[
            pltpu.VMEM((EXPERT_SUBS, EXPERT_SUB, x_sub, x_lanes), BF16),
            pltpu.VMEM((rows, d), BF16),
            pltpu.VMEM((2, rows, tf), BF16),
            pltpu.VMEM((d, tf), BF16),
            pltpu.VMEM((d, tf), BF16),
            pltpu.SemaphoreType.DMA(()),
            pltpu.SemaphoreType.DMA((2,)),
        ],
    )
    hidden = pl.pallas_call(
        _ffn_up_kernel,
        grid_spec=up_spec,
        out_shape=jax.ShapeDtypeStruct((nj, n_rows, tf), BF16),
        compiler_params=_params("arbitrary", "arbitrary"),
        name="ffn_up",
    )(sb_start, sb_nsub, sb_zero, sb_expert, jm, x_rows, w_up, w_up,
      b_up.reshape(ne, 1, ff2), b_up.reshape(ne, 1, ff2))

    down_spec = pltpu.PrefetchScalarGridSpec(
        num_scalar_prefetch=5,
        grid=(n_sb, nc),
        in_specs=[
            any_spec,
            pl.BlockSpec((1, ff, tn), lambda s, c, st, ns, zf, se, cm: (se[s], 0, cm[s, c])),
            pl.BlockSpec((1, 1, tn), lambda s, c, st, ns, zf, se, cm: (se[s], 0, cm[s, c])),
        ],
        out_specs=any_spec,
        scratch_shapes=[
            pltpu.VMEM((2, nj, rows, tf), BF16),
            pltpu.VMEM((rows, d), BF16),
            pltpu.VMEM((EXPERT_SUBS, EXPERT_SUB, x_sub, x_lanes), BF16),
            pltpu.VMEM((ff, tn), BF16),
            pltpu.SemaphoreType.DMA((2,)),
            pltpu.SemaphoreType.DMA(()),
        ],
    )
    return pl.pallas_call(
        _ffn_down_kernel,
        grid_spec=down_spec,
        out_shape=jax.ShapeDtypeStruct((n_rows, x_sub, x_lanes), BF16),
        compiler_params=_params("arbitrary", "arbitrary"),
        name="ffn_down",
    )(sb_start, sb_nsub, sb_zero, sb_expert, cm, hidden, w_down, b_down.reshape(ne, 1, d))


def _combine_kernel(pos_ref, pos_next_ref, gate_ref, h1_ref, fw_ref, y_hbm, o_ref, buf_ref, sem):
    i = pl.program_id(0)
    tt = h1_ref.shape[0]

    def start_rows(pos, slot):
        def issue(r, carry):
            for k in range(TOP_K):
                p = pos[0, 0, k * tt + r]
                pltpu.make_async_copy(y_hbm.at[pl.ds(p, 1)], buf_ref.at[slot, k, pl.ds(r, 1)],
                                      sem.at[slot]).start(priority=k % 2)
            return carry

        lax.fori_loop(0, tt, issue, 0, unroll=4)

    @pl.when(i == 0)
    def _():
        start_rows(pos_ref, 0)

    @pl.when(i + 1 < pl.num_programs(0))
    def _():
        start_rows(pos_next_ref, (i + 1) % 2)

    slot = i % 2
    for k in range(TOP_K):
        pltpu.make_async_copy(y_hbm.at[pl.ds(0, tt)], buf_ref.at[slot, k], sem.at[slot]).wait()

    gate = gate_ref[...]
    gpad = jnp.concatenate([gate, jnp.zeros((tt - TOP_K, tt), F32)], axis=0)
    gcol = gpad.T
    acc = h1_ref[...]
    for k in range(TOP_K):
        acc = acc + gcol[:, k:k + 1] * buf_ref[slot, k].reshape(acc.shape).astype(F32)
    ms = jnp.mean(acc * acc, axis=-1, keepdims=True)
    o_ref[...] = acc * lax.rsqrt(ms + EPS) * fw_ref[...]


def _pos_tiles(pos, m):
    tt = COMBINE_ROWS
    nt = m // tt
    return pos.reshape(TOP_K, nt, tt).transpose(1, 0, 2).reshape(nt, 1, TOP_K * tt)


def _combine(y_rows, pos_t, gate, h1, final_w):
    m, d = h1.shape
    tt = COMBINE_ROWS
    nt = m // tt
    return pl.pallas_call(
        _combine_kernel,
        grid=(nt,),
        in_specs=[
            pl.BlockSpec((1, 1, TOP_K * tt), lambda i: (i, 0, 0), memory_space=pltpu.SMEM),
            pl.BlockSpec((1, 1, TOP_K * tt), lambda i: (jnp.minimum(i + 1, nt - 1), 0, 0), memory_space=pltpu.SMEM),
            pl.BlockSpec((TOP_K, tt), lambda i: (0, i)),
            pl.BlockSpec((tt, d), lambda i: (i, 0)),
            pl.BlockSpec((1, d), lambda i: (0, 0)),
            pl.BlockSpec(memory_space=pl.ANY),
        ],
        out_specs=pl.BlockSpec((tt, d), lambda i: (i, 0)),
        out_shape=jax.ShapeDtypeStruct((m, d), F32),
        scratch_shapes=[pltpu.VMEM((2, TOP_K, tt) + y_rows.shape[1:], y_rows.dtype), pltpu.SemaphoreType.DMA((2,))],
        compiler_params=_params("arbitrary"),
        name="combine",
    )(pos_t, pos_t, gate, h1, final_w.reshape(1, d), y_rows)


def _routing_tables(idx, m):
    i32 = jnp.int32
    n_assign = TOP_K * m
    flat_e = idx.reshape(n_assign)
    onehot = flat_e[:, None] == jnp.arange(N_EXPERTS, dtype=i32)[None, :]
    csum = jnp.cumsum(onehot.astype(i32), axis=0)
    rank = jnp.sum(jnp.where(onehot, csum - 1, 0), axis=1)
    counts = csum[-1]
    padded = (counts + EXPERT_SUB - 1) // EXPERT_SUB * EXPERT_SUB
    pad_end = jnp.cumsum(padded)
    pad_start = pad_end - padded
    pos = (jnp.sum(jnp.where(onehot, pad_start[None, :], 0), axis=1) + rank).astype(i32)
    n_sub = -(-(n_assign + N_EXPERTS * (EXPERT_SUB - 1)) // EXPERT_SUB)
    n_rows = n_sub * EXPERT_SUB
    used_sub = pad_end[-1] // EXPERT_SUB

    rows = EXPERT_SUBS * EXPERT_SUB
    n_sb = -(-n_sub // EXPERT_SUBS) + N_EXPERTS + 1
    nsb_e = (padded + rows - 1) // rows
    sb_cum = jnp.cumsum(nsb_e)
    total_real = sb_cum[-1]
    s = jnp.arange(n_sb, dtype=i32)
    e_s = jnp.minimum(jnp.sum((sb_cum[None, :] <= s[:, None]).astype(i32), axis=1), N_EXPERTS - 1)
    local = s - (sb_cum[e_s] - nsb_e[e_s])
    real = s < total_real
    start_real = pad_start[e_s] + local * rows
    nsub_real = jnp.clip((padded[e_s] - local * rows) // EXPERT_SUB, 0, EXPERT_SUBS)
    fill_idx = s - total_real
    start_fill = pad_end[-1] + fill_idx * rows
    nsub_fill = jnp.clip(n_sub - used_sub - fill_idx * EXPERT_SUBS, 0, EXPERT_SUBS)
    is_fill = jnp.logical_and(jnp.logical_not(real), nsub_fill > 0)
    sb_start = jnp.where(real, start_real, jnp.where(is_fill, start_fill, 0)).astype(i32)
    sb_nsub = jnp.where(real, nsub_real, jnp.where(is_fill, nsub_fill, 0)).astype(i32)
    sb_zero = is_fill.astype(i32)
    last_e = e_s[jnp.maximum(total_real - 1, 0)]
    sb_expert = jnp.where(real, e_s, last_e).astype(i32)
    return pos, counts.astype(i32), pad_start.astype(i32), used_sub.astype(i32), n_rows, sb_start, sb_nsub, sb_zero, sb_expert


def _main_tiles(d):
    w = HEADS * HEAD_DIM
    sc = d // 2
    sizes = (w, w, w, w, sc, sc, sc, d, d)
    acts = (1, None, 0, 1, 0, 0, 0, 2, 2)
    order = (0, 2, 3, 4, 5, 6, 7, 8)
    starts = np.concatenate([[0], np.cumsum(sizes)])
    cols, codes = [], []
    for seg in order:
        assert sizes[seg] % COL_TILE == 0 and starts[seg] % COL_TILE == 0
        for t in range(sizes[seg] // COL_TILE):
            cols.append(int(starts[seg]) // COL_TILE + t)
            codes.append(acts[seg])
    return cols, codes, int(starts[1]) // COL_TILE


def _layer(x2d, meta, bsz, seq, norm_mix_w, w_in, lb_logits, g_norm_w, w_hgrn_out, conv_w, w_conv_out, w_o,
           norm_ffn_w, w_router, b_router, w_up, b_up, w_down, b_down, final_norm_w):
    m, d = x2d.shape
    cols, codes, fcol = _main_tiles(d)
    consts = _hgrn_constants(CHUNK)

    xn, lf, kk, xn_meta, lf_meta, kk_meta = _fgate(x2d, meta, norm_mix_w, w_in, lb_logits, fcol, 512)
    proj, proj_meta = _inproj(xn, xn_meta, w_in, cols, codes, 1024)

    pad = CHUNK - N_META
    front = lambda a: jnp.pad(a, ((pad, 0), (0, 0)))
    s_zero = jnp.zeros((HEADS, HEAD_DIM, HEAD_DIM), F32)
    _, s_meta = _hgrn(front(proj_meta), front(lf_meta), front(kk_meta), g_norm_w, s_zero, consts, 1, CHUNK, CHUNK)
    og, _ = _hgrn(proj, lf, kk, g_norm_w, s_meta[0], consts, bsz, seq, HGRN_ROWS)
    h1, xn_ffn, idx, gate = _mixer_out(og, proj, proj_meta, x2d, w_hgrn_out.astype(BF16), w_conv_out.astype(BF16),
                                   w_o.astype(BF16), conv_w, norm_ffn_w, w_router, b_router, seq, 256)

    pos, counts, pad_start, used_sub, n_rows, sb_start, sb_nsub, sb_zero, sb_expert = _routing_tables(idx, m)
    pos_t = _pos_tiles(pos, m)
    x_rows = _dispatch(xn_ffn, pos_t, counts, pad_start, used_sub, n_rows)
    y_rows = _experts(x_rows, w_up, b_up, w_down, b_down, sb_start, sb_nsub, sb_zero, sb_expert)
    return _combine(y_rows, pos_t, gate, h1, final_norm_w)


def kernel(x, meta_tokens, norm_mix_w, w_in, lb_logits, g_norm_w, w_hgrn_out, conv_w, w_conv_out, w_o, norm_ffn_w,
           w_router, b_router, w_up, b_up, w_down, b_down, final_norm_w):
    bsz, seq, d = x.shape
    assert norm_mix_w.shape[0] == 1, "single-layer block"
    out = _layer(x.reshape(bsz * seq, d), meta_tokens.astype(x.dtype), bsz, seq, norm_mix_w[0], w_in[0], lb_logits,
                 g_norm_w[0], w_hgrn_out[0], conv_w[0], w_conv_out[0], w_o[0], norm_ffn_w[0], w_router[0],
                 b_router[0], w_up[0], b_up[0], w_down[0], b_down[0], final_norm_w)
    return out.reshape(bsz, seq, d)
```

```python
import functools

import numpy as np
import jax
import jax.numpy as jnp
from jax import lax
from jax.experimental import pallas as pl
from jax.experimental.pallas import tpu as pltpu

F32 = jnp.float32
BF16 = jnp.bfloat16

LANES = 128
N_META = 16
HEADS = 8
HEAD_DIM = 128
N_EXPERTS = 32
TOP_K = 4
SWIGLU_LIMIT = 7.0
SWIGLU_ALPHA = 1.702
EPS = 1e-6

CHUNK = 256
HGRN_BASE = 64
HGRN_ROWS = 512
NORM_ROWS = 512
INPROJ_ROWS = 1024
MIXER_ROWS = 256
COL_TILE = 1024
EXPERT_SUB = 128
EXPERT_SUBS = 18
EXPERT_CHUNK_SUBS = 8
EXPERT_FF_TILE = 512
EXPERT_OUT_TILE = 512
COMBINE_ROWS = 256

V7X_VMEM_BYTES = 64 * 1024 * 1024
V7X_VMEM_LIMIT = V7X_VMEM_BYTES - 8 * 1024 * 1024


def _dot(a, b):
    return jnp.dot(a, b, preferred_element_type=F32)


def _dot_bt(a, b):
    return lax.dot_general(a, b, (((1,), (1,)), ((), ())), preferred_element_type=F32)


def _dot_at(a, b):
    return lax.dot_general(a, b, (((0,), (0,)), ((), ())), preferred_element_type=F32)


def _split3(x):
    hi = x.astype(BF16)
    r1 = x - hi.astype(F32)
    mid = r1.astype(BF16)
    lo = (r1 - mid.astype(F32)).astype(BF16)
    return hi, mid, lo


def _params(*sem):
    return pltpu.CompilerParams(dimension_semantics=sem, vmem_limit_bytes=V7X_VMEM_LIMIT)


def _activate(z, code):
    s = jax.nn.sigmoid(z)
    return jnp.where(code == 0, z, jnp.where(code == 1, z * s, s))


def _inproj_kernel(col_ref, code_ref, x_ref, xm_ref, w_ref, o_ref, om_ref, wb_ref):
    code = code_ref[pl.program_id(0)]

    @pl.when(pl.program_id(1) == 0)
    def _():
        wb_ref[...] = w_ref[...].astype(BF16)
        om_ref[...] = _activate(_dot(xm_ref[...], wb_ref[...]), code).astype(om_ref.dtype)

    o_ref[...] = _activate(_dot(x_ref[...], wb_ref[...]), code).astype(o_ref.dtype)


def _inproj(xn, xn_meta, w_in, cols, codes, tm):
    m, d = xn.shape
    mm = xn_meta.shape[0]
    nt = len(cols)
    grid_spec = pltpu.PrefetchScalarGridSpec(
        num_scalar_prefetch=2,
        grid=(nt, m // tm),
        in_specs=[
            pl.BlockSpec((tm, d), lambda n, i, col, code: (i, 0)),
            pl.BlockSpec((mm, d), lambda n, i, col, code: (0, 0)),
            pl.BlockSpec((d, COL_TILE), lambda n, i, col, code: (0, col[n])),
        ],
        out_specs=[
            pl.BlockSpec((tm, COL_TILE), lambda n, i, col, code: (i, n)),
            pl.BlockSpec((mm, COL_TILE), lambda n, i, col, code: (0, n)),
        ],
        scratch_shapes=[pltpu.VMEM((d, COL_TILE), BF16)],
    )
    return pl.pallas_call(
        _inproj_kernel,
        grid_spec=grid_spec,
        out_shape=[jax.ShapeDtypeStruct((m, nt * COL_TILE), BF16), jax.ShapeDtypeStruct((mm, nt * COL_TILE), BF16)],
        compiler_params=_params("arbitrary", "arbitrary"),
        name="inproj",
    )(jnp.asarray(cols, jnp.int32), jnp.asarray(codes, jnp.int32), xn, xn_meta, w_in)


def _fgate_kernel(x_ref, xm_ref, nw_ref, w_ref, lbl_ref, xn_ref, lf_ref, kk_ref, xnm_ref, lfm_ref, kkm_ref, wb_ref):
    lbl = lbl_ref[...]
    e = jnp.exp(lbl - jnp.max(lbl, axis=0, keepdims=True))
    lb = e[0:1] / jnp.sum(e, axis=0, keepdims=True)

    def gate(x, xn_out, lf_out, kk_out):
        ms = jnp.mean(x * x, axis=-1, keepdims=True)
        xn = (x * lax.rsqrt(ms + EPS) * nw_ref[...]).astype(BF16)
        xn_out[...] = xn
        z = _dot(xn, wb_ref[...])
        lf_out[...] = jnp.log(lb + (1.0 - lb) * jax.nn.sigmoid(z))
        kk_out[...] = (1.0 - lb) * jax.nn.sigmoid(-z)

    @pl.when(pl.program_id(0) == 0)
    def _():
        wb_ref[...] = w_ref[...].astype(BF16)
        gate(xm_ref[...], xnm_ref, lfm_ref, kkm_ref)

    gate(x_ref[...], xn_ref, lf_ref, kk_ref)


def _fgate(x, x_meta, norm_w, w_in, lb_logits, col, tm):
    m, d = x.shape
    mm = x_meta.shape[0]
    r = lb_logits.shape[0]
    out = jax.ShapeDtypeStruct((m, COL_TILE), F32)
    out_meta = jax.ShapeDtypeStruct((mm, COL_TILE), F32)
    return pl.pallas_call(
        _fgate_kernel,
        grid=(m // tm,),
        in_specs=[
            pl.BlockSpec((tm, d), lambda i: (i, 0)),
            pl.BlockSpec((mm, d), lambda i: (0, 0)),
            pl.BlockSpec((1, d), lambda i: (0, 0)),
            pl.BlockSpec((d, COL_TILE), lambda i: (0, col)),
            pl.BlockSpec((r, COL_TILE), lambda i: (0, 0)),
        ],
        out_specs=[pl.BlockSpec((tm, d), lambda i: (i, 0))] + [pl.BlockSpec((tm, COL_TILE), lambda i: (i, 0))] * 2
        + [pl.BlockSpec((mm, d), lambda i: (0, 0))] + [pl.BlockSpec((mm, COL_TILE), lambda i: (0, 0))] * 2,
        out_shape=[jax.ShapeDtypeStruct((m, d), BF16), out, out,
                   jax.ShapeDtypeStruct((mm, d), BF16), out_meta, out_meta],
        scratch_shapes=[pltpu.VMEM((d, COL_TILE), BF16)],
        compiler_params=_params("arbitrary"),
        name="norm_fgate",
    )(x, x_meta, norm_w.reshape(1, d), w_in, lb_logits)


def _hgrn_constants(c):
    base = HGRN_BASE
    nlow = int(np.log2(base))
    nl = int(np.log2(c))
    assert (1 << nl) == c and (1 << nlow) == base and c >= base
    rr = np.arange(base)[:, None]
    uu = np.arange(base)[None, :]
    mats = [uu <= rr, uu > rr]
    sels = []
    for lvl in range(nlow):
        b = 1 << lvl
        start = (rr // (2 * b)) * (2 * b)
        mid = start + b - 1
        second = (rr - start) >= b
        mats.append(np.where(second, (uu > mid) & (uu <= rr), (uu > rr) & (uu <= mid)))
        sels.append(np.broadcast_to(second, (base, HEADS * HEAD_DIM)))
    tt = np.arange(c)[:, None]
    ss = np.arange(c)[None, :]
    masks = []
    for lvl in range(nl):
        b = 1 << lvl
        masks.append(((tt // (2 * b)) == (ss // (2 * b))) & ((tt % (2 * b)) >= b) & ((ss % (2 * b)) < b))
    masks.append(np.eye(c, dtype=bool))
    m1 = np.concatenate(mats, 0).astype(np.float32)
    mall = jnp.asarray(np.concatenate([m1, m1, m1], axis=1), BF16)
    return mall, jnp.asarray(np.stack(masks).astype(np.float32)), jnp.asarray(np.stack(sels).astype(np.float32))


def _hgrn_kernel(q_ref, v_ref, g_ref, lf_ref, kk_ref, gw_ref, s0_ref, mall_ref, mask_ref, sel_ref,
                 o_ref, sfin_ref, st_ref, *, chunk, n_chunks):
    base = HGRN_BASE
    nb = chunk // base
    nlow = sel_ref.shape[0]
    nl = mask_ref.shape[0] - 1
    step = pl.program_id(1)

    @pl.when(step == 0)
    def _():
        st_ref[...] = s0_ref[...]

    def chunk_body(ci, carry):
        r0 = pl.multiple_of(ci * chunk, chunk)
        rows = pl.ds(r0, chunk)
        qb = q_ref[rows, :]
        q = qb.astype(F32)
        k = kk_ref[rows, :]
        blk = lambda a, i: a[i * base:(i + 1) * base]

        pre, suf, e_low = [], [], []
        for i in range(nb):
            hi, mid, lo = _split3(lf_ref[pl.ds(r0 + i * base, base), :])
            args = _dot(mall_ref[...], jnp.concatenate([hi, mid, lo], axis=0))
            pre.append(args[0:base])
            suf.append(args[base:2 * base])
            e_low.append(jnp.exp(args[2 * base:]))
        tot = [p[base - 1:base] for p in pre]

        def span(lo_blk, hi_blk):
            acc = None
            for j in range(lo_blk, hi_blk):
                acc = tot[j] if acc is None else acc + tot[j]
            return acc

        def shifted(a, off):
            return a if off is None else a + off

        xs = []
        for lvl in range(nlow):
            parts = [jnp.where(sel_ref[lvl] > 0.5, blk(q, i), blk(k, i)) * blk(e_low[i], lvl) for i in range(nb)]
            xs.append(jnp.concatenate(parts, axis=0).astype(BF16))
        for lvl in range(nlow, nl):
            half = (1 << lvl) // base
            parts = []
            for i in range(nb):
                g = i % (2 * half)
                if g >= half:
                    parts.append(blk(q, i) * jnp.exp(shifted(pre[i], span(i - (g - half), i))))
                else:
                    parts.append(blk(k, i) * jnp.exp(shifted(suf[i], span(i + 1, i - g + half))))
            xs.append(jnp.concatenate(parts, axis=0).astype(BF16))
        q_in = jnp.concatenate([blk(q, i) * jnp.exp(shifted(pre[i], span(0, i))) for i in range(nb)],
                               axis=0).astype(BF16)
        k_out = jnp.concatenate([blk(k, i) * jnp.exp(shifted(suf[i], span(i + 1, nb))) for i in range(nb)],
                                axis=0).astype(BF16)
        dec = jnp.exp(span(0, nb))
        kb = k.astype(BF16)

        for h in range(HEADS):
            cs = slice(h * HEAD_DIM, (h + 1) * HEAD_DIM)
            scores = mask_ref[nl] * _dot_bt(qb[:, cs], kb[:, cs])
            for lvl in range(nl):
                x = xs[lvl][:, cs]
                scores = scores + mask_ref[lvl] * _dot_bt(x, x)
            v = v_ref[rows, cs]
            st = st_ref[h]
            o = _dot(scores.astype(BF16), v) + _dot_bt(q_in[:, cs], st.astype(BF16))
            st_ref[h] = st * dec[:, cs] + _dot_at(v, k_out[:, cs])
            ms = jnp.mean(o * o, axis=-1, keepdims=True)
            on = o * lax.rsqrt(ms + EPS) * gw_ref[...]
            o_ref[rows, cs] = (on * g_ref[rows, cs].astype(F32)).astype(o_ref.dtype)
        return carry

    lax.fori_loop(0, n_chunks, chunk_body, 0)

    @pl.when(step == pl.num_programs(1) - 1)
    def _():
        sfin_ref[0] = st_ref[...]


def _hgrn(proj, lf, kk, g_norm_w, s0, consts, bsz, seq, rows):
    mall, masks, sels = consts
    steps = seq // rows
    w = HEADS * HEAD_DIM
    assert w == COL_TILE
    row_map = lambda col: (lambda b, s: (b * steps + s, col))
    const2 = lambda b, s: (0, 0)
    const3 = lambda b, s: (0, 0, 0)
    kern = functools.partial(_hgrn_kernel, chunk=CHUNK, n_chunks=rows // CHUNK)
    return pl.pallas_call(
        kern,
        grid=(bsz, steps),
        in_specs=[
            pl.BlockSpec((rows, w), row_map(0)),
            pl.BlockSpec((rows, w), row_map(1)),
            pl.BlockSpec((rows, w), row_map(2)),
            pl.BlockSpec((rows, w), row_map(0)),
            pl.BlockSpec((rows, w), row_map(0)),
            pl.BlockSpec((1, HEAD_DIM), const2),
            pl.BlockSpec((HEADS, HEAD_DIM, HEAD_DIM), const3),
            pl.BlockSpec(mall.shape, const2),
            pl.BlockSpec(masks.shape, const3),
            pl.BlockSpec(sels.shape, const3),
        ],
        out_specs=[
            pl.BlockSpec((rows, w), row_map(0)),
            pl.BlockSpec((1, HEADS, HEAD_DIM, HEAD_DIM), lambda b, s: (b, 0, 0, 0)),
        ],
        out_shape=[
            jax.ShapeDtypeStruct((bsz * seq, w), BF16),
            jax.ShapeDtypeStruct((bsz, HEADS, HEAD_DIM, HEAD_DIM), F32),
        ],
        scratch_shapes=[pltpu.VMEM((HEADS, HEAD_DIM, HEAD_DIM), F32)],
        compiler_params=_params("arbitrary", "arbitrary"),
        name="hgrn2",
    )(proj, proj, proj, lf, kk, g_norm_w.reshape(1, HEAD_DIM), s0, mall, masks, sels)


def _mixer_out_kernel(og_ref, scv_ref, scb_ref, scc_ref, ga_ref, gb_ref, x_ref,
                      pv_ref, pc_ref, mv_ref, mc_ref,
                      wa_ref, wb_ref, wo_ref, cw_ref, nw_ref, wr_ref, br_ref,
                      h1_ref, xn_ref, idx_ref, gate_ref, hbuf, *, tiles_per_seq, n_tiles):
    i = pl.program_id(0)
    tm = x_ref.shape[0]
    tile = jnp.minimum(i, n_tiles - 1)
    first = (tile % tiles_per_seq) == 0

    @pl.when(i == 0)
    def _():
        hbuf[1] = jnp.zeros(hbuf.shape[1:], hbuf.dtype)

    hp = hbuf[(i + 1) % 2]
    ms = jnp.mean(hp * hp, axis=-1, keepdims=True)
    xn = hp * lax.rsqrt(ms + EPS) * nw_ref[...]
    xn_ref[...] = xn.astype(BF16).reshape(xn_ref.shape)

    xh = xn.astype(BF16)
    xl = (xn - xh.astype(F32)).astype(BF16)
    wr = wr_ref[...]
    wh = wr.astype(BF16)
    wl = (wr - wh.astype(F32)).astype(BF16)
    logits = _dot_bt(wh, xh) + _dot_bt(wh, xl) + _dot_bt(wl, xh) + br_ref[...]
    ne = logits.shape[0]
    ie = lax.broadcasted_iota(jnp.int32, logits.shape, 0)
    tops, idxs = [], []
    for _ in range(TOP_K):
        mx = jnp.max(logits, axis=0, keepdims=True)
        ix = jnp.min(jnp.where(logits == mx, ie, ne), axis=0, keepdims=True)
        tops.append(mx)
        idxs.append(ix)
        logits = jnp.where(ie == ix, -jnp.inf, logits)
    es = [jnp.exp(t - tops[0]) for t in tops]
    den = es[0]
    for e in es[1:]:
        den = den + e
    gate_ref[...] = jnp.concatenate([e / den for e in es], axis=0)
    idx_ref[...] = jnp.concatenate(idxs, axis=0)

    u = scc_ref[...].astype(F32) * scv_ref[...].astype(F32)
    halo_prev = pc_ref[...].astype(F32) * pv_ref[...].astype(F32)
    halo_meta = mc_ref[...].astype(F32) * mv_ref[...].astype(F32)
    halo = jnp.where(first, halo_meta, halo_prev)
    hr = halo.shape[0]
    r = lax.broadcasted_iota(jnp.int32, (tm, 1), 0)
    u1 = jnp.where(r == 0, halo[hr - 1:hr], pltpu.roll(u, 1, 0))
    u2 = jnp.where(r == 0, halo[hr - 2:hr - 1], jnp.where(r == 1, halo[hr - 1:hr], pltpu.roll(u, 2, 0)))
    conv = cw_ref[2:3] * u + cw_ref[1:2] * u1 + cw_ref[0:1] * u2
    yb_in = (scb_ref[...].astype(F32) * conv).astype(BF16)

    y_a = _dot(og_ref[...], wa_ref[...])
    y_b = _dot(yb_in, wb_ref[...])
    merged = (ga_ref[...].astype(F32) * y_a + gb_ref[...].astype(F32) * y_b).astype(BF16)
    h1 = x_ref[...] + _dot(merged, wo_ref[...])
    h1_ref[...] = h1
    hbuf[i % 2] = h1


def _mixer_out(og, proj, proj_meta, x2d, wa, wb, wo, conv_w, norm_w, w_router, b_router, seq, tm):
    m, d = x2d.shape
    w = COL_TILE
    halo = proj_meta.shape[0]
    assert tm % halo == 0 and seq % tm == 0 and d == 2 * w
    ne = w_router.shape[1]
    per_halo = tm // halo
    n_tiles = m // tm
    cur = lambda i: jnp.minimum(i, n_tiles - 1)
    done = lambda i: jnp.maximum(i - 1, 0)
    row = lambda col: (lambda i: (cur(i), col))
    prev = lambda col: (lambda i: (jnp.maximum(cur(i) * per_halo - 1, 0), col))
    const = lambda i: (0, 0)
    whole = lambda a: pl.BlockSpec(a.shape, const)
    wr_t = w_router.T
    kern = functools.partial(_mixer_out_kernel, tiles_per_seq=seq // tm, n_tiles=n_tiles)
    return pl.pallas_call(
        kern,
        grid=(n_tiles + 1,),
        in_specs=[
            pl.BlockSpec((tm, w), row(0)),
            pl.BlockSpec((tm, w), row(3)),
            pl.BlockSpec((tm, w), row(4)),
            pl.BlockSpec((tm, w), row(5)),
            pl.BlockSpec((tm, d), row(3)),
            pl.BlockSpec((tm, d), row(4)),
            pl.BlockSpec((tm, d), row(0)),
            pl.BlockSpec((halo, w), prev(3)),
            pl.BlockSpec((halo, w), prev(5)),
            pl.BlockSpec((halo, w), lambda i: (0, 3)),
            pl.BlockSpec((halo, w), lambda i: (0, 5)),
            whole(wa), whole(wb), whole(wo),
            pl.BlockSpec(conv_w.shape, const),
            pl.BlockSpec((1, d), const),
            pl.BlockSpec((ne, d), const),
            pl.BlockSpec((ne, 1), const),
        ],
        out_specs=[
            pl.BlockSpec((tm, d), row(0)),
            pl.BlockSpec((tm, d // LANES, LANES), lambda i: (done(i), 0, 0)),
            pl.BlockSpec((TOP_K, tm), lambda i: (0, done(i))),
            pl.BlockSpec((TOP_K, tm), lambda i: (0, done(i))),
        ],
        out_shape=[
            jax.ShapeDtypeStruct((m, d), F32),
            jax.ShapeDtypeStruct((m, d // LANES, LANES), BF16),
            jax.ShapeDtypeStruct((TOP_K, m), jnp.int32),
            jax.ShapeDtypeStruct((TOP_K, m), F32),
        ],
        scratch_shapes=[pltpu.VMEM((2, tm, d), F32)],
        compiler_params=_params("arbitrary"),
        name="mixer_out",
    )(og, proj, proj, proj, proj, proj, x2d, proj, proj, proj_meta, proj_meta,
      wa, wb, wo, conv_w, norm_w.reshape(1, d), wr_t, b_router.reshape(ne, 1))


def _dispatch_kernel(cnt_ref, pst_ref, used_ref, pos_ref, x_ref, o_hbm, xbuf, zbuf, sem, zsem):
    i = pl.program_id(0)
    n_steps = pl.num_programs(0)
    tt = x_ref.shape[0]
    n_sub = o_hbm.shape[0] // EXPERT_SUB
    bits = [1 << b for b in reversed(range(EXPERT_SUB.bit_length() - 1))]
    slot = i % 2

    xbuf[slot] = x_ref[...]

    def issue(r, carry):
        for k in range(TOP_K):
            p = pos_ref[0, 0, k * tt + r]
            pltpu.make_async_copy(xbuf.at[slot, pl.ds(r, 1)], o_hbm.at[pl.ds(p, 1)],
                                  sem.at[slot]).start(priority=k % 2)
        return carry

    lax.fori_loop(0, tt, issue, 0, unroll=4)

    def tokens_done(s):
        return [pltpu.make_async_copy(xbuf.at[s], o_hbm.at[pl.ds(0, tt)], sem.at[s]) for _ in range(TOP_K)]

    def zero_copies(fn):
        for e in range(cnt_ref.shape[0]):
            npad = (-cnt_ref[e]) & (EXPERT_SUB - 1)
            base = pst_ref[e] + cnt_ref[e]
            for bit in bits:
                @pl.when((npad & bit) != 0)
                def _():
                    row = base + (npad & ~(2 * bit - 1))
                    fn(pltpu.make_async_copy(zbuf.at[pl.ds(0, bit)], o_hbm.at[pl.ds(row, bit)], zsem))
        for j in range(cnt_ref.shape[0]):
            blk = used_ref[0] + j

            @pl.when(blk < n_sub)
            def _():
                row = pl.multiple_of(blk * EXPERT_SUB, EXPERT_SUB)
                fn(pltpu.make_async_copy(zbuf, o_hbm.at[pl.ds(row, EXPERT_SUB)], zsem))

    @pl.when(i == 0)
    def _():
        zbuf[...] = jnp.zeros_like(zbuf)
        zero_copies(lambda c: c.start())

    @pl.when(i > 0)
    def _():
        for c in tokens_done(1 - slot):
            c.wait()

    @pl.when(i == n_steps - 1)
    def _():
        for c in tokens_done(slot):
            c.wait()
        zero_copies(lambda c: c.wait())


def _dispatch(xn3, pos_t, counts, pad_start, used_sub, n_rows):
    m, sub, lanes = xn3.shape
    nt = pos_t.shape[0]
    tt = m // nt
    grid_spec = pltpu.PrefetchScalarGridSpec(
        num_scalar_prefetch=3,
        grid=(nt,),
        in_specs=[
            pl.BlockSpec((1, 1, pos_t.shape[2]), lambda i, c, p, u: (i, 0, 0), memory_space=pltpu.SMEM),
            pl.BlockSpec((tt, sub, lanes), lambda i, c, p, u: (i, 0, 0)),
        ],
        out_specs=pl.BlockSpec(memory_space=pl.ANY),
        scratch_shapes=[pltpu.VMEM((2, tt, sub, lanes), xn3.dtype), pltpu.VMEM((EXPERT_SUB, sub, lanes), xn3.dtype),
                        pltpu.SemaphoreType.DMA((2,)), pltpu.SemaphoreType.DMA(())],
    )
    return pl.pallas_call(
        _dispatch_kernel,
        grid_spec=grid_spec,
        out_shape=jax.ShapeDtypeStruct((n_rows, sub, lanes), xn3.dtype),
        compiler_params=_params("arbitrary"),
        name="dispatch",
    )(counts, pad_start, used_sub.reshape(1), pos_t, xn3)


def _row_chunks(nsub, chunk_fn):
    per = EXPERT_CHUNK_SUBS
    big = per * EXPERT_SUB
    n_big = lax.div(nsub, per)

    def body(c, carry):
        chunk_fn(c * per, pl.multiple_of(c * big, big), big)
        return carry

    lax.fori_loop(0, n_big, body, 0)
    rem = nsub - n_big * per
    bit = per // 2
    while bit >= 1:
        done = rem & ~(2 * bit - 1)

        @pl.when((rem & bit) != 0)
        def _():
            sub0 = n_big * per + done
            chunk_fn(sub0, pl.multiple_of(sub0 * EXPERT_SUB, EXPERT_SUB), bit * EXPERT_SUB)

        bit //= 2


def _ffn_up_kernel(st_ref, ns_ref, zf_ref, se_ref, jm_ref, x_hbm, wg_ref, wu_ref, bg_ref, bu_ref, h_hbm,
                   xstage, xbuf, hbuf, wgb_ref, wub_ref, sem_x, sem_h):
    s = pl.program_id(0)
    j = pl.program_id(1)
    n_s = pl.num_programs(0)
    nj = pl.num_programs(1)
    step = s * nj + j
    subs = xbuf.shape[0] // EXPERT_SUB
    nsub = ns_ref[s]
    real = jnp.logical_and(nsub > 0, zf_ref[s] == 0)
    hs = step % 2

    def x_copy(sb, b):
        row = pl.multiple_of(st_ref[sb] + b * EXPERT_SUB, EXPERT_SUB)
        return pltpu.make_async_copy(x_hbm.at[pl.ds(row, EXPERT_SUB)],
                                     xstage.at[b], sem_x)

    def h_copy(sb, jj, b, slot):
        row = pl.multiple_of(st_ref[sb] + b * EXPERT_SUB, EXPERT_SUB)
        return pltpu.make_async_copy(hbuf.at[slot, pl.ds(b * EXPERT_SUB, EXPERT_SUB)],
                                     h_hbm.at[jj, pl.ds(row, EXPERT_SUB)], sem_h.at[slot])

    def for_x_subs(sb, fn):
        for b in range(subs):
            @pl.when(jnp.logical_and(b < ns_ref[sb], zf_ref[sb] == 0))
            def _():
                fn(b)

    def for_h_subs(sb, fn):
        for b in range(subs):
            @pl.when(b < ns_ref[sb])
            def _():
                fn(b)

    @pl.when(step == 0)
    def _():
        for_x_subs(0, lambda b: x_copy(0, b).start())

    @pl.when(j == 0)
    def _():
        for_x_subs(s, lambda b: x_copy(s, b).wait())

    @pl.when(step >= 2)
    def _():
        sp = lax.div(step - 2, nj)
        jp = step - 2 - sp * nj
        for_h_subs(sp, lambda b: h_copy(sp, jp, b, hs).wait())

    @pl.when(real)
    def _():
        wgb_ref[...] = wg_ref[0].astype(BF16)
        wub_ref[...] = wu_ref[0].astype(BF16)

        def make_chunk(first_tile):
            def chunk(sub0, row0, nrows):
                if first_tile:
                    xs = jnp.concatenate([xstage[sub0 + t].reshape(EXPERT_SUB, xbuf.shape[1])
                                          for t in range(nrows // EXPERT_SUB)], axis=0)
                    xbuf[pl.ds(row0, nrows), :] = xs
                else:
                    xs = xbuf[pl.ds(row0, nrows), :]
                g = _dot(xs, wgb_ref[...]) + bg_ref[0]
                u = _dot(xs, wub_ref[...]) + bu_ref[0]
                g = jnp.minimum(g, SWIGLU_LIMIT)
                u = jnp.clip(u, -SWIGLU_LIMIT, SWIGLU_LIMIT)
                hbuf[hs, pl.ds(row0, nrows), :] = ((u + 1.0) * (g * jax.nn.sigmoid(SWIGLU_ALPHA * g))).astype(BF16)

            return chunk

        @pl.when(j == 0)
        def _():
            _row_chunks(nsub, make_chunk(True))

        @pl.when(j > 0)
        def _():
            _row_chunks(nsub, make_chunk(False))

    @pl.when(jnp.logical_and(j == 0, s + 1 < n_s))
    def _():
        nxt = jnp.minimum(s + 1, n_s - 1)
        for_x_subs(nxt, lambda b: x_copy(nxt, b).start())

    @pl.when(zf_ref[s] == 1)
    def _():
        def zero_sub(b):
            hbuf[hs, b * EXPERT_SUB:(b + 1) * EXPERT_SUB, :] = jnp.zeros((EXPERT_SUB, hbuf.shape[2]), hbuf.dtype)

        for_h_subs(s, zero_sub)

    for_h_subs(s, lambda b: h_copy(s, j, b, hs).start())

    @pl.when(step == n_s * nj - 1)
    def _():
        sp = lax.div(step - 1, nj)
        jp = step - 1 - sp * nj
        for_h_subs(sp, lambda b: h_copy(sp, jp, b, 1 - hs).wait())
        for_h_subs(s, lambda b: h_copy(s, j, b, hs).wait())


def _ffn_down_kernel(st_ref, ns_ref, zf_ref, se_ref, cm_ref, h_hbm, wd_ref, bd_ref, y_hbm,
                     hb, yrow, ytile, wdb_ref, sem_h, sem_y):
    s = pl.program_id(0)
    c = pl.program_id(1)
    n_s = pl.num_programs(0)
    nc = pl.num_programs(1)
    step = s * nc + c
    njh = hb.shape[1]
    subs = hb.shape[2] // EXPERT_SUB
    tn = wdb_ref.shape[1]
    nsub = ns_ref[s]
    real = jnp.logical_and(nsub > 0, zf_ref[s] == 0)

    def h_copy(sb, jj, b, slot):
        row = pl.multiple_of(st_ref[sb] + b * EXPERT_SUB, EXPERT_SUB)
        return pltpu.make_async_copy(h_hbm.at[jj, pl.ds(row, EXPERT_SUB)],
                                     hb.at[slot, jj, pl.ds(b * EXPERT_SUB, EXPERT_SUB)], sem_h.at[slot])

    def y_copy(sb, b):
        row = pl.multiple_of(st_ref[sb] + b * EXPERT_SUB, EXPERT_SUB)
        return pltpu.make_async_copy(ytile.at[b],
                                     y_hbm.at[pl.ds(row, EXPERT_SUB)], sem_y)

    def for_h_subs(sb, fn):
        for b in range(subs):
            @pl.when(jnp.logical_and(b < ns_ref[sb], zf_ref[sb] == 0))
            def _():
                for jj in range(njh):
                    fn(jj, b)

    def for_y_subs(sb, fn):
        for b in range(subs):
            @pl.when(b < ns_ref[sb])
            def _():
                fn(b)

    @pl.when(step == 0)
    def _():
        for_h_subs(0, lambda jj, b: h_copy(0, jj, b, 0).start())

    @pl.when(jnp.logical_and(c == 0, s + 1 < n_s))
    def _():
        nxt = jnp.minimum(s + 1, n_s - 1)
        for_h_subs(nxt, lambda jj, b: h_copy(nxt, jj, b, (s + 1) % 2).start())

    @pl.when(c == 0)
    def _():
        for_h_subs(s, lambda jj, b: h_copy(s, jj, b, s % 2).wait())

    @pl.when(jnp.logical_and(c == nc - 1, s > 0))
    def _():
        sp = jnp.maximum(s - 1, 0)
        for_y_subs(sp, lambda b: y_copy(sp, b).wait())

    @pl.when(real)
    def _():
        wdb_ref[...] = wd_ref[0].astype(BF16)
        h_slot = s % 2
        n_tiles = yrow.shape[1] // tn
        for cc in range(n_tiles):
            @pl.when(c == cc)
            def _():
                def chunk(sub0, row0, nrows):
                    hid = jnp.concatenate([hb[h_slot, jj, pl.ds(row0, nrows), :] for jj in range(njh)], axis=1)
                    y = (_dot(hid, wdb_ref[...]) + bd_ref[0]).astype(yrow.dtype)
                    if cc < n_tiles - 1:
                        yrow[pl.ds(row0, nrows), cc * tn:(cc + 1) * tn] = y
                    else:
                        full = jnp.concatenate([yrow[pl.ds(row0, nrows), 0:cc * tn], y], axis=1)
                        for t in range(nrows // EXPERT_SUB):
                            ytile[sub0 + t] = full[t * EXPERT_SUB:(t + 1) * EXPERT_SUB].reshape(ytile.shape[1:])

                _row_chunks(nsub, chunk)

    @pl.when(c == nc - 1)
    def _():
        def zero_sub(b):
            ytile[b] = jnp.zeros(ytile.shape[1:], ytile.dtype)

        @pl.when(zf_ref[s] == 1)
        def _():
            for_y_subs(s, zero_sub)

        for_y_subs(s, lambda b: y_copy(s, b).start())

        @pl.when(s == n_s - 1)
        def _():
            for_y_subs(s, lambda b: y_copy(s, b).wait())


def _experts(x_rows, w_up, b_up, w_down, b_down, sb_start, sb_nsub, sb_zero, sb_expert):
    n_rows, x_sub, x_lanes = x_rows.shape
    d = x_sub * x_lanes
    ne, _, ff2 = w_up.shape
    ff = ff2 // 2
    tf = EXPERT_FF_TILE
    tn = EXPERT_OUT_TILE
    nj = ff // tf
    nc = d // tn
    n_sb = jnp.sum((sb_nsub > 0).astype(jnp.int32))
    rows = EXPERT_SUBS * EXPERT_SUB
    is_real = jnp.logical_and(sb_nsub > 0, sb_zero == 0)[:, None]
    jm = jnp.where(is_real, jnp.arange(nj, dtype=jnp.int32)[None, :], nj - 1).astype(jnp.int32)
    cm = jnp.where(is_real, jnp.arange(nc, dtype=jnp.int32)[None, :], nc - 1).astype(jnp.int32)
    any_spec = pl.BlockSpec(memory_space=pl.ANY)

    up_spec = pltpu.PrefetchScalarGridSpec(
        num_scalar_prefetch=5,
        grid=(n_sb, nj),
        in_specs=[
            any_spec,
            pl.BlockSpec((1, d, tf), lambda s, j, st, ns, zf, se, jm: (se[s], 0, jm[s, j])),
            pl.BlockSpec((1, d, tf), lambda s, j, st, ns, zf, se, jm: (se[s], 0, nj + jm[s, j])),
            pl.BlockSpec((1, 1, tf), lambda s, j, st, ns, zf, se, jm: (se[s], 0, jm[s, j])),
            pl.BlockSpec((1, 1, tf), lambda s, j, st, ns, zf, se, jm: (se[s], 0, nj + jm[s, j])),
        ],
        out_specs=any_spec,
        scratch_shapes=[
            pltpu.VMEM((EXPERT_SUBS, EXPERT_SUB, x_sub, x_lanes), BF16),
            pltpu.VMEM((rows, d), BF16),
            pltpu.VMEM((2, rows, tf), BF16),
            pltpu.VMEM((d, tf), BF16),
            pltpu.VMEM((d, tf), BF16),
            pltpu.SemaphoreType.DMA(()),
            pltpu.SemaphoreType.DMA((2,)),
        ],
    )
    hidden = pl.pallas_call(
        _ffn_up_kernel,
        grid_spec=up_spec,
        out_shape=jax.ShapeDtypeStruct((nj, n_rows, tf), BF16),
        compiler_params=_params("arbitrary", "arbitrary"),
        name="ffn_up",
    )(sb_start, sb_nsub, sb_zero, sb_expert, jm, x_rows, w_up, w_up,
      b_up.reshape(ne, 1, ff2), b_up.reshape(ne, 1, ff2))

    down_spec = pltpu.PrefetchScalarGridSpec(
        num_scalar_prefetch=5,
        grid=(n_sb, nc),
        in_specs=[
            any_spec,
            pl.BlockSpec((1, ff, tn), lambda s, c, st, ns, zf, se, cm: (se[s], 0, cm[s, c])),
            pl.BlockSpec((1, 1, tn), lambda s, c, st, ns, zf, se, cm: (se[s], 0, cm[s, c])),
        ],
        out_specs=any_spec,
        scratch_shapes=[
            pltpu.VMEM((2, nj, rows, tf), BF16),
            pltpu.VMEM((rows, d), BF16),
            pltpu.VMEM((EXPERT_SUBS, EXPERT_SUB, x_sub, x_lanes), BF16),
            pltpu.VMEM((ff, tn), BF16),
            pltpu.SemaphoreType.DMA((2,)),
            pltpu.SemaphoreType.DMA(()),
        ],
    )
    return pl.pallas_call(
        _ffn_down_kernel,
        grid_spec=down_spec,
        out_shape=jax.ShapeDtypeStruct((n_rows, x_sub, x_lanes), BF16),
        compiler_params=_params("arbitrary", "arbitrary"),
        name="ffn_down",
    )(sb_start, sb_nsub, sb_zero, sb_expert, cm, hidden, w_down, b_down.reshape(ne, 1, d))


def _combine_kernel(pos_ref, pos_next_ref, gate_ref, h1_ref, fw_ref, y_hbm, o_ref, buf_ref, sem):
    i = pl.program_id(0)
    tt = h1_ref.shape[0]

    def start_rows(pos, slot):
        def issue(r, carry):
            for k in range(TOP_K):
                p = pos[0, 0, k * tt + r]
                pltpu.make_async_copy(y_hbm.at[pl.ds(p, 1)], buf_ref.at[slot, k, pl.ds(r, 1)],
                                      sem.at[slot]).start(priority=k % 2)
            return carry

        lax.fori_loop(0, tt, issue, 0, unroll=4)

    @pl.when(i == 0)
    def _():
        start_rows(pos_ref, 0)

    @pl.when(i + 1 < pl.num_programs(0))
    def _():
        start_rows(pos_next_ref, (i + 1) % 2)

    slot = i % 2
    for k in range(TOP_K):
        pltpu.make_async_copy(y_hbm.at[pl.ds(0, tt)], buf_ref.at[slot, k], sem.at[slot]).wait()

    gate = gate_ref[...]
    gpad = jnp.concatenate([gate, jnp.zeros((tt - TOP_K, tt), F32)], axis=0)
    gcol = gpad.T
    acc = h1_ref[...]
    for k in range(TOP_K):
        acc = acc + gcol[:, k:k + 1] * buf_ref[slot, k].reshape(acc.shape).astype(F32)
    ms = jnp.mean(acc * acc, axis=-1, keepdims=True)
    o_ref[...] = acc * lax.rsqrt(ms + EPS) * fw_ref[...]


def _pos_tiles(pos, m):
    tt = COMBINE_ROWS
    nt = m // tt
    return pos.reshape(TOP_K, nt, tt).transpose(1, 0, 2).reshape(nt, 1, TOP_K * tt)


def _combine(y_rows, pos_t, gate, h1, final_w):
    m, d = h1.shape
    tt = COMBINE_ROWS
    nt = m // tt
    return pl.pallas_call(
        _combine_kernel,
        grid=(nt,),
        in_specs=[
            pl.BlockSpec((1, 1, TOP_K * tt), lambda i: (i, 0, 0), memory_space=pltpu.SMEM),
            pl.BlockSpec((1, 1, TOP_K * tt), lambda i: (jnp.minimum(i + 1, nt - 1), 0, 0), memory_space=pltpu.SMEM),
            pl.BlockSpec((TOP_K, tt), lambda i: (0, i)),
            pl.BlockSpec((tt, d), lambda i: (i, 0)),
            pl.BlockSpec((1, d), lambda i: (0, 0)),
            pl.BlockSpec(memory_space=pl.ANY),
        ],
        out_specs=pl.BlockSpec((tt, d), lambda i: (i, 0)),
        out_shape=jax.ShapeDtypeStruct((m, d), F32),
        scratch_shapes=[pltpu.VMEM((2, TOP_K, tt) + y_rows.shape[1:], y_rows.dtype), pltpu.SemaphoreType.DMA((2,))],
        compiler_params=_params("arbitrary"),
        name="combine",
    )(pos_t, pos_t, gate, h1, final_w.reshape(1, d), y_rows)


def _routing_tables(idx, m):
    i32 = jnp.int32
    n_assign = TOP_K * m
    flat_e = idx.reshape(n_assign)
    onehot = flat_e[:, None] == jnp.arange(N_EXPERTS, dtype=i32)[None, :]
    csum = jnp.cumsum(onehot.astype(i32), axis=0)
    rank = jnp.sum(jnp.where(onehot, csum - 1, 0), axis=1)
    counts = csum[-1]
    padded = (counts + EXPERT_SUB - 1) // EXPERT_SUB * EXPERT_SUB
    pad_end = jnp.cumsum(padded)
    pad_start = pad_end - padded
    pos = (jnp.sum(jnp.where(onehot, pad_start[None, :], 0), axis=1) + rank).astype(i32)
    n_sub = -(-(n_assign + N_EXPERTS * (EXPERT_SUB - 1)) // EXPERT_SUB)
    n_rows = n_sub * EXPERT_SUB
    used_sub = pad_end[-1] // EXPERT_SUB

    rows = EXPERT_SUBS * EXPERT_SUB
    n_sb = -(-n_sub // EXPERT_SUBS) + N_EXPERTS + 1
    nsb_e = (padded + rows - 1) // rows
    sb_cum = jnp.cumsum(nsb_e)
    total_real = sb_cum[-1]
    s = jnp.arange(n_sb, dtype=i32)
    e_s = jnp.minimum(jnp.sum((sb_cum[None, :] <= s[:, None]).astype(i32), axis=1), N_EXPERTS - 1)
    local = s - (sb_cum[e_s] - nsb_e[e_s])
    real = s < total_real
    start_real = pad_start[e_s] + local * rows
    nsub_real = jnp.clip((padded[e_s] - local * rows) // EXPERT_SUB, 0, EXPERT_SUBS)
    fill_idx = s - total_real
    start_fill = pad_end[-1] + fill_idx * rows
    nsub_fill = jnp.clip(n_sub - used_sub - fill_idx * EXPERT_SUBS, 0, EXPERT_SUBS)
    is_fill = jnp.logical_and(jnp.logical_not(real), nsub_fill > 0)
    sb_start = jnp.where(real, start_real, jnp.where(is_fill, start_fill, 0)).astype(i32)
    sb_nsub = jnp.where(real, nsub_real, jnp.where(is_fill, nsub_fill, 0)).astype(i32)
    sb_zero = is_fill.astype(i32)
    last_e = e_s[jnp.maximum(total_real - 1, 0)]
    sb_expert = jnp.where(real, e_s, last_e).astype(i32)
    return pos, counts.astype(i32), pad_start.astype(i32), used_sub.astype(i32), n_rows, sb_start, sb_nsub, sb_zero, sb_expert


def _main_tiles(d):
    w = HEADS * HEAD_DIM
    sc = d // 2
    sizes = (w, w, w, w, sc, sc, sc, d, d)
    acts = (1, None, 0, 1, 0, 0, 0, 2, 2)
    order = (0, 2, 3, 4, 5, 6, 7, 8)
    starts = np.concatenate([[0], np.cumsum(sizes)])
    cols, codes = [], []
    for seg in order:
        assert sizes[seg] % COL_TILE == 0 and starts[seg] % COL_TILE == 0
        for t in range(sizes[seg] // COL_TILE):
            cols.append(int(starts[seg]) // COL_TILE + t)
            codes.append(acts[seg])
    return cols, codes, int(starts[1]) // COL_TILE


def _layer(x2d, meta, bsz, seq, norm_mix_w, w_in, lb_logits, g_norm_w, w_hgrn_out, conv_w, w_conv_out, w_o,
           norm_ffn_w, w_router, b_router, w_up, b_up, w_down, b_down, final_norm_w):
    m, d = x2d.shape
    cols, codes, fcol = _main_tiles(d)
    consts = _hgrn_constants(CHUNK)

    xn, lf, kk, xn_meta, lf_meta, kk_meta = _fgate(x2d, meta, norm_mix_w, w_in, lb_logits, fcol, NORM_ROWS)
    proj, proj_meta = _inproj(xn, xn_meta, w_in, cols, codes, INPROJ_ROWS)

    pad = CHUNK - N_META
    front = lambda a: jnp.pad(a, ((pad, 0), (0, 0)))
    s_zero = jnp.zeros((HEADS, HEAD_DIM, HEAD_DIM), F32)
    _, s_meta = _hgrn(front(proj_meta), front(lf_meta), front(kk_meta), g_norm_w, s_zero, consts, 1, CHUNK, CHUNK)
    og, _ = _hgrn(proj, lf, kk, g_norm_w, s_meta[0], consts, bsz, seq, HGRN_ROWS)
    h1, xn_ffn, idx, gate = _mixer_out(og, proj, proj_meta, x2d, w_hgrn_out.astype(BF16), w_conv_out.astype(BF16),
                                   w_o.astype(BF16), conv_w, norm_ffn_w, w_router, b_router, seq, MIXER_ROWS)

    pos, counts, pad_start, used_sub, n_rows, sb_start, sb_nsub, sb_zero, sb_expert = _routing_tables(idx, m)
    pos_t = _pos_tiles(pos, m)
    x_rows = _dispatch(xn_ffn, pos_t, counts, pad_start, used_sub, n_rows)
    y_rows = _experts(x_rows, w_up, b_up, w_down, b_down, sb_start, sb_nsub, sb_zero, sb_expert)
    return _combine(y_rows, pos_t, gate, h1, final_norm_w)


def kernel(x, meta_tokens, norm_mix_w, w_in, lb_logits, g_norm_w, w_hgrn_out, conv_w, w_conv_out, w_o, norm_ffn_w,
           w_router, b_router, w_up, b_up, w_down, b_down, final_norm_w):
    bsz, seq, d = x.shape
    assert norm_mix_w.shape[0] == 1, "single-layer block"
    out = _layer(x.reshape(bsz * seq, d), meta_tokens.astype(x.dtype), bsz, seq, norm_mix_w[0], w_in[0], lb_logits,
                 g_norm_w[0], w_hgrn_out[0], conv_w[0], w_conv_out[0], w_o[0], norm_ffn_w[0], w_router[0],
                 b_router[0], w_up[0], b_up[0], w_down[0], b_down[0], final_norm_w)
    return out.reshape(bsz, seq, d)
```

```python
import functools

import numpy as np
import jax
import jax.numpy as jnp
from jax import lax
from jax.experimental import pallas as pl
from jax.experimental.pallas import tpu as pltpu

F32 = jnp.float32
BF16 = jnp.bfloat16

LANES = 128
N_META = 16
HEADS = 8
HEAD_DIM = 128
N_EXPERTS = 32
TOP_K = 4
SWIGLU_LIMIT = 7.0
SWIGLU_ALPHA = 1.702
EPS = 1e-6

CHUNK = 256
HGRN_BASE = 64
HGRN_ROWS = 512
NORM_ROWS = 512
INPROJ_ROWS = 1024
MIXER_ROWS = 256
COL_TILE = 1024
EXPERT_SUB = 128
EXPERT_SUBS = 18
EXPERT_CHUNK_SUBS = 8
EXPERT_FF_TILE = 512
EXPERT_OUT_TILE = 512
COMBINE_ROWS = 256

V7X_VMEM_BYTES = 64 * 1024 * 1024
V7X_VMEM_LIMIT = V7X_VMEM_BYTES - 8 * 1024 * 1024


def _dot(a, b):
    return jnp.dot(a, b, preferred_element_type=F32)


def _dot_bt(a, b):
    return lax.dot_general(a, b, (((1,), (1,)), ((), ())), preferred_element_type=F32)


def _dot_at(a, b):
    return lax.dot_general(a, b, (((0,), (0,)), ((), ())), preferred_element_type=F32)


def _split3(x):
    hi = x.astype(BF16)
    r1 = x - hi.astype(F32)
    mid = r1.astype(BF16)
    lo = (r1 - mid.astype(F32)).astype(BF16)
    return hi, mid, lo


def _params(*sem):
    return pltpu.CompilerParams(dimension_semantics=sem, vmem_limit_bytes=V7X_VMEM_LIMIT)


def _activate(z, code):
    s = jax.nn.sigmoid(z)
    return jnp.where(code == 0, z, jnp.where(code == 1, z * s, s))


def _inproj_kernel(col_ref, code_ref, x_ref, xm_ref, w_ref, o_ref, om_ref, wb_ref):
    code = code_ref[pl.program_id(0)]

    @pl.when(pl.program_id(1) == 0)
    def _():
        wb_ref[...] = w_ref[...].astype(BF16)
        om_ref[...] = _activate(_dot(xm_ref[...], wb_ref[...]), code).astype(om_ref.dtype)

    o_ref[...] = _activate(_dot(x_ref[...], wb_ref[...]), code).astype(o_ref.dtype)


def _inproj(xn, xn_meta, w_in, cols, codes, tm):
    m, d = xn.shape
    mm = xn_meta.shape[0]
    nt = len(cols)
    grid_spec = pltpu.PrefetchScalarGridSpec(
        num_scalar_prefetch=2,
        grid=(nt, m // tm),
        in_specs=[
            pl.BlockSpec((tm, d), lambda n, i, col, code: (i, 0)),
            pl.BlockSpec((mm, d), lambda n, i, col, code: (0, 0)),
            pl.BlockSpec((d, COL_TILE), lambda n, i, col, code: (0, col[n])),
        ],
        out_specs=[
            pl.BlockSpec((tm, COL_TILE), lambda n, i, col, code: (i, n)),
            pl.BlockSpec((mm, COL_TILE), lambda n, i, col, code: (0, n)),
        ],
        scratch_shapes=[pltpu.VMEM((d, COL_TILE), BF16)],
    )
    return pl.pallas_call(
        _inproj_kernel,
        grid_spec=grid_spec,
        out_shape=[jax.ShapeDtypeStruct((m, nt * COL_TILE), BF16), jax.ShapeDtypeStruct((mm, nt * COL_TILE), BF16)],
        compiler_params=_params("arbitrary", "arbitrary"),
        name="inproj",
    )(jnp.asarray(cols, jnp.int32), jnp.asarray(codes, jnp.int32), xn, xn_meta, w_in)


def _fgate_kernel(x_ref, xm_ref, nw_ref, w_ref, lbl_ref, xn_ref, lf_ref, kk_ref, xnm_ref, lfm_ref, kkm_ref, wb_ref):
    lbl = lbl_ref[...]
    e = jnp.exp(lbl - jnp.max(lbl, axis=0, keepdims=True))
    lb = e[0:1] / jnp.sum(e, axis=0, keepdims=True)

    def gate(x, xn_out, lf_out, kk_out):
        ms = jnp.mean(x * x, axis=-1, keepdims=True)
        xn = (x * lax.rsqrt(ms + EPS) * nw_ref[...]).astype(BF16)
        xn_out[...] = xn
        z = _dot(xn, wb_ref[...])
        lf_out[...] = jnp.log(lb + (1.0 - lb) * jax.nn.sigmoid(z))
        kk_out[...] = (1.0 - lb) * jax.nn.sigmoid(-z)

    @pl.when(pl.program_id(0) == 0)
    def _():
        wb_ref[...] = w_ref[...].astype(BF16)
        gate(xm_ref[...], xnm_ref, lfm_ref, kkm_ref)

    gate(x_ref[...], xn_ref, lf_ref, kk_ref)


def _fgate(x, x_meta, norm_w, w_in, lb_logits, col, tm):
    m, d = x.shape
    mm = x_meta.shape[0]
    r = lb_logits.shape[0]
    out = jax.ShapeDtypeStruct((m, COL_TILE), F32)
    out_meta = jax.ShapeDtypeStruct((mm, COL_TILE), F32)
    return pl.pallas_call(
        _fgate_kernel,
        grid=(m // tm,),
        in_specs=[
            pl.BlockSpec((tm, d), lambda i: (i, 0)),
            pl.BlockSpec((mm, d), lambda i: (0, 0)),
            pl.BlockSpec((1, d), lambda i: (0, 0)),
            pl.BlockSpec((d, COL_TILE), lambda i: (0, col)),
            pl.BlockSpec((r, COL_TILE), lambda i: (0, 0)),
        ],
        out_specs=[pl.BlockSpec((tm, d), lambda i: (i, 0))] + [pl.BlockSpec((tm, COL_TILE), lambda i: (i, 0))] * 2
        + [pl.BlockSpec((mm, d), lambda i: (0, 0))] + [pl.BlockSpec((mm, COL_TILE), lambda i: (0, 0))] * 2,
        out_shape=[jax.ShapeDtypeStruct((m, d), BF16), out, out,
                   jax.ShapeDtypeStruct((mm, d), BF16), out_meta, out_meta],
        scratch_shapes=[pltpu.VMEM((d, COL_TILE), BF16)],
        compiler_params=_params("arbitrary"),
        name="norm_fgate",
    )(x, x_meta, norm_w.reshape(1, d), w_in, lb_logits)


def _hgrn_constants(c):
    base = HGRN_BASE
    nlow = int(np.log2(base))
    nl = int(np.log2(c))
    assert (1 << nl) == c and (1 << nlow) == base and c >= base
    rr = np.arange(base)[:, None]
    uu = np.arange(base)[None, :]
    mats = [uu <= rr, uu > rr]
    sels = []
    for lvl in range(nlow):
        b = 1 << lvl
        start = (rr // (2 * b)) * (2 * b)
        mid = start + b - 1
        second = (rr - start) >= b
        mats.append(np.where(second, (uu > mid) & (uu <= rr), (uu > rr) & (uu <= mid)))
        sels.append(np.broadcast_to(second, (base, HEADS * HEAD_DIM)))
    tt = np.arange(c)[:, None]
    ss = np.arange(c)[None, :]
    masks = []
    for lvl in range(nl):
        b = 1 << lvl
        masks.append(((tt // (2 * b)) == (ss // (2 * b))) & ((tt % (2 * b)) >= b) & ((ss % (2 * b)) < b))
    masks.append(np.eye(c, dtype=bool))
    m1 = np.concatenate(mats, 0).astype(np.float32)
    mall = jnp.asarray(np.concatenate([m1, m1, m1], axis=1), BF16)
    return mall, jnp.asarray(np.stack(masks).astype(np.float32)), jnp.asarray(np.stack(sels).astype(np.float32))


def _hgrn_kernel(q_ref, v_ref, g_ref, lf_ref, kk_ref, gw_ref, s0_ref, mall_ref, mask_ref, sel_ref,
                 o_ref, sfin_ref, st_ref, *, chunk, n_chunks):
    base = HGRN_BASE
    nb = chunk // base
    nlow = sel_ref.shape[0]
    nl = mask_ref.shape[0] - 1
    step = pl.program_id(1)

    @pl.when(step == 0)
    def _():
        st_ref[...] = s0_ref[...]

    def chunk_body(ci, carry):
        r0 = pl.multiple_of(ci * chunk, chunk)
        rows = pl.ds(r0, chunk)
        qb = q_ref[rows, :]
        q = qb.astype(F32)
        k = kk_ref[rows, :]
        blk = lambda a, i: a[i * base:(i + 1) * base]

        pre, suf, e_low = [], [], []
        for i in range(nb):
            hi, mid, lo = _split3(lf_ref[pl.ds(r0 + i * base, base), :])
            args = _dot(mall_ref[...], jnp.concatenate([hi, mid, lo], axis=0))
            pre.append(args[0:base])
            suf.append(args[base:2 * base])
            e_low.append(jnp.exp(args[2 * base:]))
        tot = [p[base - 1:base] for p in pre]

        def span(lo_blk, hi_blk):
            acc = None
            for j in range(lo_blk, hi_blk):
                acc = tot[j] if acc is None else acc + tot[j]
            return acc

        def shifted(a, off):
            return a if off is None else a + off

        xs = []
        for lvl in range(nlow):
            parts = [jnp.where(sel_ref[lvl] > 0.5, blk(q, i), blk(k, i)) * blk(e_low[i], lvl) for i in range(nb)]
            xs.append(jnp.concatenate(parts, axis=0).astype(BF16))
        for lvl in range(nlow, nl):
            half = (1 << lvl) // base
            parts = []
            for i in range(nb):
                g = i % (2 * half)
                if g >= half:
                    parts.append(blk(q, i) * jnp.exp(shifted(pre[i], span(i - (g - half), i))))
                else:
                    parts.append(blk(k, i) * jnp.exp(shifted(suf[i], span(i + 1, i - g + half))))
            xs.append(jnp.concatenate(parts, axis=0).astype(BF16))
        q_in = jnp.concatenate([blk(q, i) * jnp.exp(shifted(pre[i], span(0, i))) for i in range(nb)],
                               axis=0).astype(BF16)
        k_out = jnp.concatenate([blk(k, i) * jnp.exp(shifted(suf[i], span(i + 1, nb))) for i in range(nb)],
                                axis=0).astype(BF16)
        dec = jnp.exp(span(0, nb))
        kb = k.astype(BF16)

        for h in range(HEADS):
            cs = slice(h * HEAD_DIM, (h + 1) * HEAD_DIM)
            scores = mask_ref[nl] * _dot_bt(qb[:, cs], kb[:, cs])
            for lvl in range(nl):
                x = xs[lvl][:, cs]
                scores = scores + mask_ref[lvl] * _dot_bt(x, x)
            v = v_ref[rows, cs]
            st = st_ref[h]
            o = _dot(scores.astype(BF16), v) + _dot_bt(q_in[:, cs], st.astype(BF16))
            st_ref[h] = st * dec[:, cs] + _dot_at(v, k_out[:, cs])
            ms = jnp.mean(o * o, axis=-1, keepdims=True)
            on = o * lax.rsqrt(ms + EPS) * gw_ref[...]
            o_ref[rows, cs] = (on * g_ref[rows, cs].astype(F32)).astype(o_ref.dtype)
        return carry

    lax.fori_loop(0, n_chunks, chunk_body, 0)

    @pl.when(step == pl.num_programs(1) - 1)
    def _():
        sfin_ref[0] = st_ref[...]


def _hgrn(proj, lf, kk, g_norm_w, s0, consts, bsz, seq, rows):
    mall, masks, sels = consts
    steps = seq // rows
    w = HEADS * HEAD_DIM
    assert w == COL_TILE
    row_map = lambda col: (lambda b, s: (b * steps + s, col))
    const2 = lambda b, s: (0, 0)
    const3 = lambda b, s: (0, 0, 0)
    kern = functools.partial(_hgrn_kernel, chunk=CHUNK, n_chunks=rows // CHUNK)
    return pl.pallas_call(
        kern,
        grid=(bsz, steps),
        in_specs=[
            pl.BlockSpec((rows, w), row_map(0)),
            pl.BlockSpec((rows, w), row_map(1)),
            pl.BlockSpec((rows, w), row_map(2)),
            pl.BlockSpec((rows, w), row_map(0)),
            pl.BlockSpec((rows, w), row_map(0)),
            pl.BlockSpec((1, HEAD_DIM), const2),
            pl.BlockSpec((HEADS, HEAD_DIM, HEAD_DIM), const3),
            pl.BlockSpec(mall.shape, const2),
            pl.BlockSpec(masks.shape, const3),
            pl.BlockSpec(sels.shape, const3),
        ],
        out_specs=[
            pl.BlockSpec((rows, w), row_map(0)),
            pl.BlockSpec((1, HEADS, HEAD_DIM, HEAD_DIM), lambda b, s: (b, 0, 0, 0)),
        ],
        out_shape=[
            jax.ShapeDtypeStruct((bsz * seq, w), BF16),
            jax.ShapeDtypeStruct((bsz, HEADS, HEAD_DIM, HEAD_DIM), F32),
        ],
        scratch_shapes=[pltpu.VMEM((HEADS, HEAD_DIM, HEAD_DIM), F32)],
        compiler_params=_params("arbitrary", "arbitrary"),
        name="hgrn2",
    )(proj, proj, proj, lf, kk, g_norm_w.reshape(1, HEAD_DIM), s0, mall, masks, sels)


def _mixer_out_kernel(og_ref, scv_ref, scb_ref, scc_ref, ga_ref, gb_ref, x_ref,
                      pv_ref, pc_ref, mv_ref, mc_ref,
                      wa_ref, wb_ref, wo_ref, cw_ref, nw_ref, wr_ref, br_ref,
                      h1_ref, xn_ref, idx_ref, gate_ref, hbuf, *, tiles_per_seq, n_tiles):
    i = pl.program_id(0)
    tm = x_ref.shape[0]
    tile = jnp.minimum(i, n_tiles - 1)
    first = (tile % tiles_per_seq) == 0

    @pl.when(i == 0)
    def _():
        hbuf[1] = jnp.zeros(hbuf.shape[1:], hbuf.dtype)

    hp = hbuf[(i + 1) % 2]
    ms = jnp.mean(hp * hp, axis=-1, keepdims=True)
    xn = hp * lax.rsqrt(ms + EPS) * nw_ref[...]
    xn_ref[...] = xn.astype(BF16).reshape(xn_ref.shape)

    xh = xn.astype(BF16)
    xl = (xn - xh.astype(F32)).astype(BF16)
    wr = wr_ref[...]
    wh = wr.astype(BF16)
    wl = (wr - wh.astype(F32)).astype(BF16)
    logits = _dot_bt(wh, xh) + _dot_bt(wh, xl) + _dot_bt(wl, xh) + br_ref[...]
    ne = logits.shape[0]
    ie = lax.broadcasted_iota(jnp.int32, logits.shape, 0)
    tops, idxs = [], []
    for _ in range(TOP_K):
        mx = jnp.max(logits, axis=0, keepdims=True)
        ix = jnp.min(jnp.where(logits == mx, ie, ne), axis=0, keepdims=True)
        tops.append(mx)
        idxs.append(ix)
        logits = jnp.where(ie == ix, -jnp.inf, logits)
    es = [jnp.exp(t - tops[0]) for t in tops]
    den = es[0]
    for e in es[1:]:
        den = den + e
    gate_ref[...] = jnp.concatenate([e / den for e in es], axis=0)
    idx_ref[...] = jnp.concatenate(idxs, axis=0)

    u = scc_ref[...].astype(F32) * scv_ref[...].astype(F32)
    halo_prev = pc_ref[...].astype(F32) * pv_ref[...].astype(F32)
    halo_meta = mc_ref[...].astype(F32) * mv_ref[...].astype(F32)
    halo = jnp.where(first, halo_meta, halo_prev)
    hr = halo.shape[0]
    r = lax.broadcasted_iota(jnp.int32, (tm, 1), 0)
    u1 = jnp.where(r == 0, halo[hr - 1:hr], pltpu.roll(u, 1, 0))
    u2 = jnp.where(r == 0, halo[hr - 2:hr - 1], jnp.where(r == 1, halo[hr - 1:hr], pltpu.roll(u, 2, 0)))
    conv = cw_ref[2:3] * u + cw_ref[1:2] * u1 + cw_ref[0:1] * u2
    yb_in = (scb_ref[...].astype(F32) * conv).astype(BF16)

    y_a = _dot(og_ref[...], wa_ref[...])
    y_b = _dot(yb_in, wb_ref[...])
    merged = (ga_ref[...].astype(F32) * y_a + gb_ref[...].astype(F32) * y_b).astype(BF16)
    h1 = x_ref[...] + _dot(merged, wo_ref[...])
    h1_ref[...] = h1
    hbuf[i % 2] = h1


def _mixer_out(og, proj, proj_meta, x2d, wa, wb, wo, conv_w, norm_w, w_router, b_router, seq, tm):
    m, d = x2d.shape
    w = COL_TILE
    halo = proj_meta.shape[0]
    assert tm % halo == 0 and seq % tm == 0 and d == 2 * w
    ne = w_router.shape[1]
    per_halo = tm // halo
    n_tiles = m // tm
    cur = lambda i: jnp.minimum(i, n_tiles - 1)
    done = lambda i: jnp.maximum(i - 1, 0)
    row = lambda col: (lambda i: (cur(i), col))
    prev = lambda col: (lambda i: (jnp.maximum(cur(i) * per_halo - 1, 0), col))
    const = lambda i: (0, 0)
    whole = lambda a: pl.BlockSpec(a.shape, const)
    wr_t = w_router.T
    kern = functools.partial(_mixer_out_kernel, tiles_per_seq=seq // tm, n_tiles=n_tiles)
    return pl.pallas_call(
        kern,
        grid=(n_tiles + 1,),
        in_specs=[
            pl.BlockSpec((tm, w), row(0)),
            pl.BlockSpec((tm, w), row(3)),
            pl.BlockSpec((tm, w), row(4)),
            pl.BlockSpec((tm, w), row(5)),
            pl.BlockSpec((tm, d), row(3)),
            pl.BlockSpec((tm, d), row(4)),
            pl.BlockSpec((tm, d), row(0)),
            pl.BlockSpec((halo, w), prev(3)),
            pl.BlockSpec((halo, w), prev(5)),
            pl.BlockSpec((halo, w), lambda i: (0, 3)),
            pl.BlockSpec((halo, w), lambda i: (0, 5)),
            whole(wa), whole(wb), whole(wo),
            pl.BlockSpec(conv_w.shape, const),
            pl.BlockSpec((1, d), const),
            pl.BlockSpec((ne, d), const),
            pl.BlockSpec((ne, 1), const),
        ],
        out_specs=[
            pl.BlockSpec((tm, d), row(0)),
            pl.BlockSpec((tm, d // LANES, LANES), lambda i: (done(i), 0, 0)),
            pl.BlockSpec((TOP_K, tm), lambda i: (0, done(i))),
            pl.BlockSpec((TOP_K, tm), lambda i: (0, done(i))),
        ],
        out_shape=[
            jax.ShapeDtypeStruct((m, d), F32),
            jax.ShapeDtypeStruct((m, d // LANES, LANES), BF16),
            jax.ShapeDtypeStruct((TOP_K, m), jnp.int32),
            jax.ShapeDtypeStruct((TOP_K, m), F32),
        ],
        scratch_shapes=[pltpu.VMEM((2, tm, d), F32)],
        compiler_params=_params("arbitrary"),
        name="mixer_out",
    )(og, proj, proj, proj, proj, proj, x2d, proj, proj, proj_meta, proj_meta,
      wa, wb, wo, conv_w, norm_w.reshape(1, d), wr_t, b_router.reshape(ne, 1))


def _dispatch_kernel(cnt_ref, pst_ref, used_ref, pos_ref, x_ref, o_hbm, xbuf, zbuf, sem, zsem):
    i = pl.program_id(0)
    n_steps = pl.num_programs(0)
    tt = x_ref.shape[0]
    n_sub = o_hbm.shape[0] // EXPERT_SUB
    bits = [1 << b for b in reversed(range(EXPERT_SUB.bit_length() - 1))]
    slot = i % 2

    xbuf[slot] = x_ref[...]

    def issue(r, carry):
        for k in range(TOP_K):
            p = pos_ref[0, 0, k * tt + r]
            pltpu.make_async_copy(xbuf.at[slot, pl.ds(r, 1)], o_hbm.at[pl.ds(p, 1)],
                                  sem.at[slot]).start(priority=k % 2)
        return carry

    lax.fori_loop(0, tt, issue, 0, unroll=4)

    def tokens_done(s):
        return [pltpu.make_async_copy(xbuf.at[s], o_hbm.at[pl.ds(0, tt)], sem.at[s]) for _ in range(TOP_K)]

    def zero_copies(fn):
        for e in range(cnt_ref.shape[0]):
            npad = (-cnt_ref[e]) & (EXPERT_SUB - 1)
            base = pst_ref[e] + cnt_ref[e]
            for bit in bits:
                @pl.when((npad & bit) != 0)
                def _():
                    row = base + (npad & ~(2 * bit - 1))
                    fn(pltpu.make_async_copy(zbuf.at[pl.ds(0, bit)], o_hbm.at[pl.ds(row, bit)], zsem))
        for j in range(cnt_ref.shape[0]):
            blk = used_ref[0] + j

            @pl.when(blk < n_sub)
            def _():
                row = pl.multiple_of(blk * EXPERT_SUB, EXPERT_SUB)
                fn(pltpu.make_async_copy(zbuf, o_hbm.at[pl.ds(row, EXPERT_SUB)], zsem))

    @pl.when(i == 0)
    def _():
        zbuf[...] = jnp.zeros_like(zbuf)
        zero_copies(lambda c: c.start())

    @pl.when(i > 0)
    def _():
        for c in tokens_done(1 - slot):
            c.wait()

    @pl.when(i == n_steps - 1)
    def _():
        for c in tokens_done(slot):
            c.wait()
        zero_copies(lambda c: c.wait())


def _dispatch(xn3, pos_t, counts, pad_start, used_sub, n_rows):
    m, sub, lanes = xn3.shape
    nt = pos_t.shape[0]
    tt = m // nt
    grid_spec = pltpu.PrefetchScalarGridSpec(
        num_scalar_prefetch=3,
        grid=(nt,),
        in_specs=[
            pl.BlockSpec((1, 1, pos_t.shape[2]), lambda i, c, p, u: (i, 0, 0), memory_space=pltpu.SMEM),
            pl.BlockSpec((tt, sub, lanes), lambda i, c, p, u: (i, 0, 0)),
        ],
        out_specs=pl.BlockSpec(memory_space=pl.ANY),
        scratch_shapes=[pltpu.VMEM((2, tt, sub, lanes), xn3.dtype), pltpu.VMEM((EXPERT_SUB, sub, lanes), xn3.dtype),
                        pltpu.SemaphoreType.DMA((2,)), pltpu.SemaphoreType.DMA(())],
    )
    return pl.pallas_call(
        _dispatch_kernel,
        grid_spec=grid_spec,
        out_shape=jax.ShapeDtypeStruct((n_rows, sub, lanes), xn3.dtype),
        compiler_params=_params("arbitrary"),
        name="dispatch",
    )(counts, pad_start, used_sub.reshape(1), pos_t, xn3)


def _row_chunks(nsub, chunk_fn):
    per = EXPERT_CHUNK_SUBS
    big = per * EXPERT_SUB
    n_big = lax.div(nsub, per)

    def body(c, carry):
        chunk_fn(c * per, pl.multiple_of(c * big, big), big)
        return carry

    lax.fori_loop(0, n_big, body, 0)
    rem = nsub - n_big * per
    bit = per // 2
    while bit >= 1:
        done = rem & ~(2 * bit - 1)

        @pl.when((rem & bit) != 0)
        def _():
            sub0 = n_big * per + done
            chunk_fn(sub0, pl.multiple_of(sub0 * EXPERT_SUB, EXPERT_SUB), bit * EXPERT_SUB)

        bit //= 2


def _sub_pieces(n, fn):
    bit = 1 << (EXPERT_SUBS.bit_length() - 1)
    while bit >= 1:
        @pl.when((n & bit) != 0)
        def _():
            fn(n & ~(2 * bit - 1), bit)

        bit //= 2


def _ffn_up_kernel(st_ref, ns_ref, zf_ref, se_ref, jm_ref, x_hbm, wg_ref, wu_ref, bg_ref, bu_ref, h_hbm,
                   xstage, xbuf, hbuf, wgb_ref, wub_ref, sem_x, sem_h):
    s = pl.program_id(0)
    j = pl.program_id(1)
    n_s = pl.num_programs(0)
    nj = pl.num_programs(1)
    step = s * nj + j
    nsub = ns_ref[s]
    real = jnp.logical_and(nsub > 0, zf_ref[s] == 0)
    hs = step % 2

    def x_copy(sb, first, count):
        return pltpu.make_async_copy(x_hbm.at[pl.ds(st_ref[sb] + first, count)],
                                     xstage.at[pl.ds(first, count)], sem_x)

    def h_copy(sb, jj, first, count, slot):
        row = pl.multiple_of((st_ref[sb] + first) * EXPERT_SUB, EXPERT_SUB)
        return pltpu.make_async_copy(
            hbuf.at[slot, pl.ds(pl.multiple_of(first * EXPERT_SUB, EXPERT_SUB), count * EXPERT_SUB)],
            h_hbm.at[jj, pl.ds(row, count * EXPERT_SUB)], sem_h.at[slot])

    def for_x_subs(sb, fn):
        _sub_pieces(ns_ref[sb] * (1 - zf_ref[sb]), fn)

    def for_h_subs(sb, fn):
        _sub_pieces(ns_ref[sb], fn)

    @pl.when(step == 0)
    def _():
        for_x_subs(0, lambda first, count: x_copy(0, first, count).start())

    @pl.when(j == 0)
    def _():
        for_x_subs(s, lambda first, count: x_copy(s, first, count).wait())

    @pl.when(step >= 2)
    def _():
        sp = lax.div(step - 2, nj)
        jp = step - 2 - sp * nj
        for_h_subs(sp, lambda first, count: h_copy(sp, jp, first, count, hs).wait())

    @pl.when(real)
    def _():
        wgb_ref[...] = wg_ref[0].astype(BF16)
        wub_ref[...] = wu_ref[0].astype(BF16)

        def make_chunk(first_tile):
            def chunk(sub0, row0, nrows):
                if first_tile:
                    xs = jnp.concatenate([xstage[sub0 + t].reshape(EXPERT_SUB, xbuf.shape[1])
                                          for t in range(nrows // EXPERT_SUB)], axis=0)
                    xbuf[pl.ds(row0, nrows), :] = xs
                else:
                    xs = xbuf[pl.ds(row0, nrows), :]
                g = _dot(xs, wgb_ref[...]) + bg_ref[0]
                u = _dot(xs, wub_ref[...]) + bu_ref[0]
                g = jnp.minimum(g, SWIGLU_LIMIT)
                u = jnp.clip(u, -SWIGLU_LIMIT, SWIGLU_LIMIT)
                hbuf[hs, pl.ds(row0, nrows), :] = ((u + 1.0) * (g * jax.nn.sigmoid(SWIGLU_ALPHA * g))).astype(BF16)

            return chunk

        @pl.when(j == 0)
        def _():
            _row_chunks(nsub, make_chunk(True))

        @pl.when(j > 0)
        def _():
            _row_chunks(nsub, make_chunk(False))

    @pl.when(jnp.logical_and(j == 0, s + 1 < n_s))
    def _():
        nxt = jnp.minimum(s + 1, n_s - 1)
        for_x_subs(nxt, lambda first, count: x_copy(nxt, first, count).start())

    @pl.when(zf_ref[s] == 1)
    def _():
        hbuf[hs] = jnp.zeros(hbuf.shape[1:], hbuf.dtype)

    for_h_subs(s, lambda first, count: h_copy(s, j, first, count, hs).start())

    @pl.when(step == n_s * nj - 1)
    def _():
        sp = lax.div(step - 1, nj)
        jp = step - 1 - sp * nj
        for_h_subs(sp, lambda first, count: h_copy(sp, jp, first, count, 1 - hs).wait())
        for_h_subs(s, lambda first, count: h_copy(s, j, first, count, hs).wait())


def _ffn_down_kernel(st_ref, ns_ref, zf_ref, se_ref, cm_ref, h_hbm, wd_ref, bd_ref, y_hbm,
                     hb, yrow, ytile, wdb_ref, sem_h, sem_y):
    s = pl.program_id(0)
    c = pl.program_id(1)
    n_s = pl.num_programs(0)
    nc = pl.num_programs(1)
    step = s * nc + c
    njh = hb.shape[1]
    tn = wdb_ref.shape[1]
    nsub = ns_ref[s]
    real = jnp.logical_and(nsub > 0, zf_ref[s] == 0)

    def h_copy(sb, jj, first, count, slot):
        row = pl.multiple_of((st_ref[sb] + first) * EXPERT_SUB, EXPERT_SUB)
        return pltpu.make_async_copy(
            h_hbm.at[jj, pl.ds(row, count * EXPERT_SUB)],
            hb.at[slot, jj, pl.ds(pl.multiple_of(first * EXPERT_SUB, EXPERT_SUB), count * EXPERT_SUB)],
            sem_h.at[slot])

    def y_copy(sb, first, count):
        return pltpu.make_async_copy(ytile.at[pl.ds(first, count)],
                                     y_hbm.at[pl.ds(st_ref[sb] + first, count)], sem_y)

    def for_h_subs(sb, fn):
        def all_tiles(first, count):
            for jj in range(njh):
                fn(jj, first, count)

        _sub_pieces(ns_ref[sb] * (1 - zf_ref[sb]), all_tiles)

    def for_y_subs(sb, fn):
        _sub_pieces(ns_ref[sb], fn)

    @pl.when(step == 0)
    def _():
        for_h_subs(0, lambda jj, first, count: h_copy(0, jj, first, count, 0).start())

    @pl.when(jnp.logical_and(c == 0, s + 1 < n_s))
    def _():
        nxt = jnp.minimum(s + 1, n_s - 1)
        for_h_subs(nxt, lambda jj, first, count: h_copy(nxt, jj, first, count, (s + 1) % 2).start())

    @pl.when(c == 0)
    def _():
        for_h_subs(s, lambda jj, first, count: h_copy(s, jj, first, count, s % 2).wait())

    @pl.when(jnp.logical_and(c == nc - 1, s > 0))
    def _():
        sp = jnp.maximum(s - 1, 0)
        for_y_subs(sp, lambda first, count: y_copy(sp, first, count).wait())

    @pl.when(real)
    def _():
        wdb_ref[...] = wd_ref[0].astype(BF16)
        h_slot = s % 2
        n_tiles = yrow.shape[1] // tn
        for cc in range(n_tiles):
            @pl.when(c == cc)
            def _():
                def chunk(sub0, row0, nrows):
                    hid = jnp.concatenate([hb[h_slot, jj, pl.ds(row0, nrows), :] for jj in range(njh)], axis=1)
                    y = (_dot(hid, wdb_ref[...]) + bd_ref[0]).astype(yrow.dtype)
                    if cc < n_tiles - 1:
                        yrow[pl.ds(row0, nrows), cc * tn:(cc + 1) * tn] = y
                    else:
                        full = jnp.concatenate([yrow[pl.ds(row0, nrows), 0:cc * tn], y], axis=1)
                        for t in range(nrows // EXPERT_SUB):
                            ytile[sub0 + t] = full[t * EXPERT_SUB:(t + 1) * EXPERT_SUB].reshape(ytile.shape[1:])

                _row_chunks(nsub, chunk)

    @pl.when(c == nc - 1)
    def _():
        @pl.when(zf_ref[s] == 1)
        def _():
            ytile[...] = jnp.zeros(ytile.shape, ytile.dtype)

        for_y_subs(s, lambda first, count: y_copy(s, first, count).start())

        @pl.when(s == n_s - 1)
        def _():
            for_y_subs(s, lambda first, count: y_copy(s, first, count).wait())


def _experts(x_rows, w_up, b_up, w_down, b_down, sb_start, sb_nsub, sb_zero, sb_expert):
    n_rows, x_sub, x_lanes = x_rows.shape
    d = x_sub * x_lanes
    ne, _, ff2 = w_up.shape
    ff = ff2 // 2
    tf = EXPERT_FF_TILE
    tn = EXPERT_OUT_TILE
    nj = ff // tf
    nc = d // tn
    n_sb = jnp.sum((sb_nsub > 0).astype(jnp.int32))
    rows = EXPERT_SUBS * EXPERT_SUB
    n_sub = n_rows // EXPERT_SUB
    sub_shape = (n_sub, EXPERT_SUB, x_sub, x_lanes)
    x_rows = x_rows.reshape(sub_shape)
    sb_start = sb_start // EXPERT_SUB
    is_real = jnp.logical_and(sb_nsub > 0, sb_zero == 0)[:, None]
    jm = jnp.where(is_real, jnp.arange(nj, dtype=jnp.int32)[None, :], nj - 1).astype(jnp.int32)
    cm = jnp.where(is_real, jnp.arange(nc, dtype=jnp.int32)[None, :], nc - 1).astype(jnp.int32)
    any_spec = pl.BlockSpec(memory_space=pl.ANY)

    up_spec = pltpu.PrefetchScalarGridSpec(
        num_scalar_prefetch=5,
        grid=(n_sb, nj),
        in_specs=[
            any_spec,
            pl.BlockSpec((1, d, tf), lambda s, j, st, ns, zf, se, jm: (se[s], 0, jm[s, j])),
            pl.BlockSpec((1, d, tf), lambda s, j, st, ns, zf, se, jm: (se[s], 0, nj + jm[s, j])),
            pl.BlockSpec((1, 1, tf), lambda s, j, st, ns, zf, se, jm: (se[s], 0, jm[s, j])),
            pl.BlockSpec((1, 1, tf), lambda s, j, st, ns, zf, se, jm: (se[s], 0, nj + jm[s, j])),
        ],
        out_specs=any_spec,
        scratch_shapes=[
            pltpu.VMEM((EXPERT_SUBS, EXPERT_SUB, x_sub, x_lanes), BF16),
            pltpu.VMEM((rows, d), BF16),
            pltpu.VMEM((2, rows, tf), BF16),
            pltpu.VMEM((d, tf), BF16),
            pltpu.VMEM((d, tf), BF16),
            pltpu.SemaphoreType.DMA(()),
            pltpu.SemaphoreType.DMA((2,)),
        ],
    )
    hidden = pl.pallas_call(
        _ffn_up_kernel,
        grid_spec=up_spec,
        out_shape=jax.ShapeDtypeStruct((nj, n_rows, tf), BF16),
        compiler_params=_params("arbitrary", "arbitrary"),
        name="ffn_up",
    )(sb_start, sb_nsub, sb_zero, sb_expert, jm, x_rows, w_up, w_up,
      b_up.reshape(ne, 1, ff2), b_up.reshape(ne, 1, ff2))

    down_spec = pltpu.PrefetchScalarGridSpec(
        num_scalar_prefetch=5,
        grid=(n_sb, nc),
        in_specs=[
            any_spec,
            pl.BlockSpec((1, ff, tn), lambda s, c, st, ns, zf, se, cm: (se[s], 0, cm[s, c])),
            pl.BlockSpec((1, 1, tn), lambda s, c, st, ns, zf, se, cm: (se[s], 0, cm[s, c])),
        ],
        out_specs=any_spec,
        scratch_shapes=[
            pltpu.VMEM((2, nj, rows, tf), BF16),
            pltpu.VMEM((rows, d), BF16),
            pltpu.VMEM((EXPERT_SUBS, EXPERT_SUB, x_sub, x_lanes), BF16),
            pltpu.VMEM((ff, tn), BF16),
            pltpu.SemaphoreType.DMA((2,)),
            pltpu.SemaphoreType.DMA(()),
        ],
    )
    y_rows = pl.pallas_call(
        _ffn_down_kernel,
        grid_spec=down_spec,
        out_shape=jax.ShapeDtypeStruct(sub_shape, BF16),
        compiler_params=_params("arbitrary", "arbitrary"),
        name="ffn_down",
    )(sb_start, sb_nsub, sb_zero, sb_expert, cm, hidden, w_down, b_down.reshape(ne, 1, d))
    return y_rows.reshape(n_rows, x_sub, x_lanes)


def _combine_kernel(pos_ref, pos_next_ref, gate_ref, h1_ref, fw_ref, y_hbm, o_ref, buf_ref, sem):
    i = pl.program_id(0)
    tt = h1_ref.shape[0]

    def start_rows(pos, slot):
        def issue(r, carry):
            for k in range(TOP_K):
                p = pos[0, 0, k * tt + r]
                pltpu.make_async_copy(y_hbm.at[pl.ds(p, 1)], buf_ref.at[slot, k, pl.ds(r, 1)],
                                      sem.at[slot]).start(priority=k % 2)
            return carry

        lax.fori_loop(0, tt, issue, 0, unroll=4)

    @pl.when(i == 0)
    def _():
        start_rows(pos_ref, 0)

    @pl.when(i + 1 < pl.num_programs(0))
    def _():
        start_rows(pos_next_ref, (i + 1) % 2)

    slot = i % 2
    for k in range(TOP_K):
        pltpu.make_async_copy(y_hbm.at[pl.ds(0, tt)], buf_ref.at[slot, k], sem.at[slot]).wait()

    gate = gate_ref[...]
    gpad = jnp.concatenate([gate, jnp.zeros((tt - TOP_K, tt), F32)], axis=0)
    gcol = gpad.T
    acc = h1_ref[...]
    for k in range(TOP_K):
        acc = acc + gcol[:, k:k + 1] * buf_ref[slot, k].reshape(acc.shape).astype(F32)
    ms = jnp.mean(acc * acc, axis=-1, keepdims=True)
    o_ref[...] = acc * lax.rsqrt(ms + EPS) * fw_ref[...]


def _pos_tiles(pos, m):
    tt = COMBINE_ROWS
    nt = m // tt
    return pos.reshape(TOP_K, nt, tt).transpose(1, 0, 2).reshape(nt, 1, TOP_K * tt)


def _combine(y_rows, pos_t, gate, h1, final_w):
    m, d = h1.shape
    tt = COMBINE_ROWS
    nt = m // tt
    return pl.pallas_call(
        _combine_kernel,
        grid=(nt,),
        in_specs=[
            pl.BlockSpec((1, 1, TOP_K * tt), lambda i: (i, 0, 0), memory_space=pltpu.SMEM),
            pl.BlockSpec((1, 1, TOP_K * tt), lambda i: (jnp.minimum(i + 1, nt - 1), 0, 0), memory_space=pltpu.SMEM),
            pl.BlockSpec((TOP_K, tt), lambda i: (0, i)),
            pl.BlockSpec((tt, d), lambda i: (i, 0)),
            pl.BlockSpec((1, d), lambda i: (0, 0)),
            pl.BlockSpec(memory_space=pl.ANY),
        ],
        out_specs=pl.BlockSpec((tt, d), lambda i: (i, 0)),
        out_shape=jax.ShapeDtypeStruct((m, d), F32),
        scratch_shapes=[pltpu.VMEM((2, TOP_K, tt) + y_rows.shape[1:], y_rows.dtype), pltpu.SemaphoreType.DMA((2,))],
        compiler_params=_params("arbitrary"),
        name="combine",
    )(pos_t, pos_t, gate, h1, final_w.reshape(1, d), y_rows)


def _routing_tables(idx, m):
    i32 = jnp.int32
    n_assign = TOP_K * m
    flat_e = idx.reshape(n_assign)
    onehot = flat_e[:, None] == jnp.arange(N_EXPERTS, dtype=i32)[None, :]
    csum = jnp.cumsum(onehot.astype(i32), axis=0)
    rank = jnp.sum(jnp.where(onehot, csum - 1, 0), axis=1)
    counts = csum[-1]
    padded = (counts + EXPERT_SUB - 1) // EXPERT_SUB * EXPERT_SUB
    pad_end = jnp.cumsum(padded)
    pad_start = pad_end - padded
    pos = (jnp.sum(jnp.where(onehot, pad_start[None, :], 0), axis=1) + rank).astype(i32)
    n_sub = -(-(n_assign + N_EXPERTS * (EXPERT_SUB - 1)) // EXPERT_SUB)
    n_rows = n_sub * EXPERT_SUB
    used_sub = pad_end[-1] // EXPERT_SUB

    rows = EXPERT_SUBS * EXPERT_SUB
    n_sb = -(-n_sub // EXPERT_SUBS) + N_EXPERTS + 1
    nsb_e = (padded + rows - 1) // rows
    sb_cum = jnp.cumsum(nsb_e)
    total_real = sb_cum[-1]
    s = jnp.arange(n_sb, dtype=i32)
    e_s = jnp.minimum(jnp.sum((sb_cum[None, :] <= s[:, None]).astype(i32), axis=1), N_EXPERTS - 1)
    local = s - (sb_cum[e_s] - nsb_e[e_s])
    real = s < total_real
    start_real = pad_start[e_s] + local * rows
    nsub_real = jnp.clip((padded[e_s] - local * rows) // EXPERT_SUB, 0, EXPERT_SUBS)
    fill_idx = s - total_real
    start_fill = pad_end[-1] + fill_idx * rows
    nsub_fill = jnp.clip(n_sub - used_sub - fill_idx * EXPERT_SUBS, 0, EXPERT_SUBS)
    is_fill = jnp.logical_and(jnp.logical_not(real), nsub_fill > 0)
    sb_start = jnp.where(real, start_real, jnp.where(is_fill, start_fill, 0)).astype(i32)
    sb_nsub = jnp.where(real, nsub_real, jnp.where(is_fill, nsub_fill, 0)).astype(i32)
    sb_zero = is_fill.astype(i32)
    last_e = e_s[jnp.maximum(total_real - 1, 0)]
    sb_expert = jnp.where(real, e_s, last_e).astype(i32)
    return pos, counts.astype(i32), pad_start.astype(i32), used_sub.astype(i32), n_rows, sb_start, sb_nsub, sb_zero, sb_expert


def _main_tiles(d):
    w = HEADS * HEAD_DIM
    sc = d // 2
    sizes = (w, w, w, w, sc, sc, sc, d, d)
    acts = (1, None, 0, 1, 0, 0, 0, 2, 2)
    order = (0, 2, 3, 4, 5, 6, 7, 8)
    starts = np.concatenate([[0], np.cumsum(sizes)])
    cols, codes = [], []
    for seg in order:
        assert sizes[seg] % COL_TILE == 0 and starts[seg] % COL_TILE == 0
        for t in range(sizes[seg] // COL_TILE):
            cols.append(int(starts[seg]) // COL_TILE + t)
            codes.append(acts[seg])
    return cols, codes, int(starts[1]) // COL_TILE


def _layer(x2d, meta, bsz, seq, norm_mix_w, w_in, lb_logits, g_norm_w, w_hgrn_out, conv_w, w_conv_out, w_o,
           norm_ffn_w, w_router, b_router, w_up, b_up, w_down, b_down, final_norm_w):
    m, d = x2d.shape
    cols, codes, fcol = _main_tiles(d)
    consts = _hgrn_constants(CHUNK)

    xn, lf, kk, xn_meta, lf_meta, kk_meta = _fgate(x2d, meta, norm_mix_w, w_in, lb_logits, fcol, NORM_ROWS)
    proj, proj_meta = _inproj(xn, xn_meta, w_in, cols, codes, INPROJ_ROWS)

    pad = CHUNK - N_META
    front = lambda a: jnp.pad(a, ((pad, 0), (0, 0)))
    s_zero = jnp.zeros((HEADS, HEAD_DIM, HEAD_DIM), F32)
    _, s_meta = _hgrn(front(proj_meta), front(lf_meta), front(kk_meta), g_norm_w, s_zero, consts, 1, CHUNK, CHUNK)
    og, _ = _hgrn(proj, lf, kk, g_norm_w, s_meta[0], consts, bsz, seq, HGRN_ROWS)
    h1, xn_ffn, idx, gate = _mixer_out(og, proj, proj_meta, x2d, w_hgrn_out.astype(BF16), w_conv_out.astype(BF16),
                                   w_o.astype(BF16), conv_w, norm_ffn_w, w_router, b_router, seq, MIXER_ROWS)

    pos, counts, pad_start, used_sub, n_rows, sb_start, sb_nsub, sb_zero, sb_expert = _routing_tables(idx, m)
    pos_t = _pos_tiles(pos, m)
    x_rows = _dispatch(xn_ffn, pos_t, counts, pad_start, used_sub, n_rows)
    y_rows = _experts(x_rows, w_up, b_up, w_down, b_down, sb_start, sb_nsub, sb_zero, sb_expert)
    return _combine(y_rows, pos_t, gate, h1, final_norm_w)


def kernel(x, meta_tokens, norm_mix_w, w_in, lb_logits, g_norm_w, w_hgrn_out, conv_w, w_conv_out, w_o, norm_ffn_w,
           w_router, b_router, w_up, b_up, w_down, b_down, final_norm_w):
    bsz, seq, d = x.shape
    assert norm_mix_w.shape[0] == 1, "single-layer block"
    out = _layer(x.reshape(bsz * seq, d), meta_tokens.astype(x.dtype), bsz, seq, norm_mix_w[0], w_in[0], lb_logits,
                 g_norm_w[0], w_hgrn_out[0], conv_w[0], w_conv_out[0], w_o[0], norm_ffn_w[0], w_router[0],
                 b_router[0], w_up[0], b_up[0], w_down[0], b_down[0], final_norm_w)
    return out.reshape(bsz, seq, d)
```

```python
import functools

import numpy as np
import jax
import jax.numpy as jnp
from jax import lax
from jax.experimental import pallas as pl
from jax.experimental.pallas import tpu as pltpu

F32 = jnp.float32
BF16 = jnp.bfloat16

LANES = 128
N_META = 16
HEADS = 8
HEAD_DIM = 128
N_EXPERTS = 32
TOP_K = 4
SWIGLU_LIMIT = 7.0
SWIGLU_ALPHA = 1.702
EPS = 1e-6

CHUNK = 256
HGRN_BASE = 64
HGRN_ROWS = 512
NORM_ROWS = 512
INPROJ_ROWS = 1024
MIXER_ROWS = 256
COL_TILE = 1024
EXPERT_SUB = 128
EXPERT_SUBS = 18
EXPERT_CHUNK_SUBS = 8
EXPERT_FF_TILE = 512
EXPERT_OUT_TILE = 512
COMBINE_ROWS = 256

V7X_VMEM_BYTES = 64 * 1024 * 1024
V7X_VMEM_LIMIT = V7X_VMEM_BYTES - 8 * 1024 * 1024


def _dot(a, b):
    return jnp.dot(a, b, preferred_element_type=F32)


def _dot_bt(a, b):
    return lax.dot_general(a, b, (((1,), (1,)), ((), ())), preferred_element_type=F32)


def _dot_at(a, b):
    return lax.dot_general(a, b, (((0,), (0,)), ((), ())), preferred_element_type=F32)


def _split3(x):
    hi = x.astype(BF16)
    r1 = x - hi.astype(F32)
    mid = r1.astype(BF16)
    lo = (r1 - mid.astype(F32)).astype(BF16)
    return hi, mid, lo


def _params(*sem):
    return pltpu.CompilerParams(dimension_semantics=sem, vmem_limit_bytes=V7X_VMEM_LIMIT)


def _activate(z, code):
    s = jax.nn.sigmoid(z)
    return jnp.where(code == 0, z, jnp.where(code == 1, z * s, s))


def _inproj_kernel(col_ref, code_ref, x_ref, xm_ref, w_ref, o_ref, om_ref, wb_ref):
    code = code_ref[pl.program_id(0)]

    @pl.when(pl.program_id(1) == 0)
    def _():
        wb_ref[...] = w_ref[...].astype(BF16)
        om_ref[...] = _activate(_dot(xm_ref[...], wb_ref[...]), code).astype(om_ref.dtype)

    o_ref[...] = _activate(_dot(x_ref[...], wb_ref[...]), code).astype(o_ref.dtype)


def _inproj(xn, xn_meta, w_in, cols, codes, tm):
    m, d = xn.shape
    mm = xn_meta.shape[0]
    nt = len(cols)
    grid_spec = pltpu.PrefetchScalarGridSpec(
        num_scalar_prefetch=2,
        grid=(nt, m // tm),
        in_specs=[
            pl.BlockSpec((tm, d), lambda n, i, col, code: (i, 0)),
            pl.BlockSpec((mm, d), lambda n, i, col, code: (0, 0)),
            pl.BlockSpec((d, COL_TILE), lambda n, i, col, code: (0, col[n])),
        ],
        out_specs=[
            pl.BlockSpec((tm, COL_TILE), lambda n, i, col, code: (i, n)),
            pl.BlockSpec((mm, COL_TILE), lambda n, i, col, code: (0, n)),
        ],
        scratch_shapes=[pltpu.VMEM((d, COL_TILE), BF16)],
    )
    return pl.pallas_call(
        _inproj_kernel,
        grid_spec=grid_spec,
        out_shape=[jax.ShapeDtypeStruct((m, nt * COL_TILE), BF16), jax.ShapeDtypeStruct((mm, nt * COL_TILE), BF16)],
        compiler_params=_params("arbitrary", "arbitrary"),
        name="inproj",
    )(jnp.asarray(cols, jnp.int32), jnp.asarray(codes, jnp.int32), xn, xn_meta, w_in)


def _fgate_kernel(x_ref, xm_ref, nw_ref, w_ref, lbl_ref, xn_ref, lf_ref, kk_ref, xnm_ref, lfm_ref, kkm_ref, wb_ref):
    lbl = lbl_ref[...]
    e = jnp.exp(lbl - jnp.max(lbl, axis=0, keepdims=True))
    lb = e[0:1] / jnp.sum(e, axis=0, keepdims=True)

    def gate(x, xn_out, lf_out, kk_out):
        ms = jnp.mean(x * x, axis=-1, keepdims=True)
        xn = (x * lax.rsqrt(ms + EPS) * nw_ref[...]).astype(BF16)
        xn_out[...] = xn
        z = _dot(xn, wb_ref[...])
        lf_out[...] = jnp.log(lb + (1.0 - lb) * jax.nn.sigmoid(z))
        kk_out[...] = (1.0 - lb) * jax.nn.sigmoid(-z)

    @pl.when(pl.program_id(0) == 0)
    def _():
        wb_ref[...] = w_ref[...].astype(BF16)
        gate(xm_ref[...], xnm_ref, lfm_ref, kkm_ref)

    gate(x_ref[...], xn_ref, lf_ref, kk_ref)


def _fgate(x, x_meta, norm_w, w_in, lb_logits, col, tm):
    m, d = x.shape
    mm = x_meta.shape[0]
    r = lb_logits.shape[0]
    out = jax.ShapeDtypeStruct((m, COL_TILE), F32)
    out_meta = jax.ShapeDtypeStruct((mm, COL_TILE), F32)
    return pl.pallas_call(
        _fgate_kernel,
        grid=(m // tm,),
        in_specs=[
            pl.BlockSpec((tm, d), lambda i: (i, 0)),
            pl.BlockSpec((mm, d), lambda i: (0, 0)),
            pl.BlockSpec((1, d), lambda i: (0, 0)),
            pl.BlockSpec((d, COL_TILE), lambda i: (0, col)),
            pl.BlockSpec((r, COL_TILE), lambda i: (0, 0)),
        ],
        out_specs=[pl.BlockSpec((tm, d), lambda i: (i, 0))] + [pl.BlockSpec((tm, COL_TILE), lambda i: (i, 0))] * 2
        + [pl.BlockSpec((mm, d), lambda i: (0, 0))] + [pl.BlockSpec((mm, COL_TILE), lambda i: (0, 0))] * 2,
        out_shape=[jax.ShapeDtypeStruct((m, d), BF16), out, out,
                   jax.ShapeDtypeStruct((mm, d), BF16), out_meta, out_meta],
        scratch_shapes=[pltpu.VMEM((d, COL_TILE), BF16)],
        compiler_params=_params("arbitrary"),
        name="norm_fgate",
    )(x, x_meta, norm_w.reshape(1, d), w_in, lb_logits)


def _hgrn_constants(c):
    base = HGRN_BASE
    nlow = int(np.log2(base))
    nl = int(np.log2(c))
    assert (1 << nl) == c and (1 << nlow) == base and c >= base
    rr = np.arange(base)[:, None]
    uu = np.arange(base)[None, :]
    mats = [uu <= rr, uu > rr]
    sels = []
    for lvl in range(nlow):
        b = 1 << lvl
        start = (rr // (2 * b)) * (2 * b)
        mid = start + b - 1
        second = (rr - start) >= b
        mats.append(np.where(second, (uu > mid) & (uu <= rr), (uu > rr) & (uu <= mid)))
        sels.append(np.broadcast_to(second, (base, HEADS * HEAD_DIM)))
    tt = np.arange(c)[:, None]
    ss = np.arange(c)[None, :]
    masks = []
    for lvl in range(nl):
        b = 1 << lvl
        masks.append(((tt // (2 * b)) == (ss // (2 * b))) & ((tt % (2 * b)) >= b) & ((ss % (2 * b)) < b))
    masks.append(np.eye(c, dtype=bool))
    m1 = np.concatenate(mats, 0).astype(np.float32)
    mall = jnp.asarray(np.concatenate([m1, m1, m1], axis=1), BF16)
    return mall, jnp.asarray(np.stack(masks).astype(np.float32)), jnp.asarray(np.stack(sels).astype(np.float32))


def _hgrn_kernel(q_ref, v_ref, g_ref, lf_ref, kk_ref, gw_ref, s0_ref, mall_ref, mask_ref, sel_ref,
                 o_ref, sfin_ref, st_ref, *, chunk, n_chunks):
    base = HGRN_BASE
    nb = chunk // base
    nlow = sel_ref.shape[0]
    nl = mask_ref.shape[0] - 1
    step = pl.program_id(1)

    @pl.when(step == 0)
    def _():
        st_ref[...] = s0_ref[...]

    def chunk_body(ci, carry):
        r0 = pl.multiple_of(ci * chunk, chunk)
        rows = pl.ds(r0, chunk)
        qb = q_ref[rows, :]
        q = qb.astype(F32)
        k = kk_ref[rows, :]
        blk = lambda a, i: a[i * base:(i + 1) * base]

        pre, suf, e_low = [], [], []
        for i in range(nb):
            hi, mid, lo = _split3(lf_ref[pl.ds(r0 + i * base, base), :])
            args = _dot(mall_ref[...], jnp.concatenate([hi, mid, lo], axis=0))
            pre.append(args[0:base])
            suf.append(args[base:2 * base])
            e_low.append(jnp.exp(args[2 * base:]))
        tot = [p[base - 1:base] for p in pre]

        def span(lo_blk, hi_blk):
            acc = None
            for j in range(lo_blk, hi_blk):
                acc = tot[j] if acc is None else acc + tot[j]
            return acc

        def shifted(a, off):
            return a if off is None else a + off

        xs = []
        for lvl in range(nlow):
            parts = [jnp.where(sel_ref[lvl] > 0.5, blk(q, i), blk(k, i)) * blk(e_low[i], lvl) for i in range(nb)]
            xs.append(jnp.concatenate(parts, axis=0).astype(BF16))
        for lvl in range(nlow, nl):
            half = (1 << lvl) // base
            parts = []
            for i in range(nb):
                g = i % (2 * half)
                if g >= half:
                    parts.append(blk(q, i) * jnp.exp(shifted(pre[i], span(i - (g - half), i))))
                else:
                    parts.append(blk(k, i) * jnp.exp(shifted(suf[i], span(i + 1, i - g + half))))
            xs.append(jnp.concatenate(parts, axis=0).astype(BF16))
        q_in = jnp.concatenate([blk(q, i) * jnp.exp(shifted(pre[i], span(0, i))) for i in range(nb)],
                               axis=0).astype(BF16)
        k_out = jnp.concatenate([blk(k, i) * jnp.exp(shifted(suf[i], span(i + 1, nb))) for i in range(nb)],
                                axis=0).astype(BF16)
        dec = jnp.exp(span(0, nb))
        kb = k.astype(BF16)

        for h in range(HEADS):
            cs = slice(h * HEAD_DIM, (h + 1) * HEAD_DIM)
            scores = mask_ref[nl] * _dot_bt(qb[:, cs], kb[:, cs])
            for lvl in range(nl):
                x = xs[lvl][:, cs]
                scores = scores + mask_ref[lvl] * _dot_bt(x, x)
            v = v_ref[rows, cs]
            st = st_ref[h]
            o = _dot(scores.astype(BF16), v) + _dot_bt(q_in[:, cs], st.astype(BF16))
            st_ref[h] = st * dec[:, cs] + _dot_at(v, k_out[:, cs])
            ms = jnp.mean(o * o, axis=-1, keepdims=True)
            on = o * lax.rsqrt(ms + EPS) * gw_ref[...]
            o_ref[rows, cs] = (on * g_ref[rows, cs].astype(F32)).astype(o_ref.dtype)
        return carry

    lax.fori_loop(0, n_chunks, chunk_body, 0)

    @pl.when(step == pl.num_programs(1) - 1)
    def _():
        sfin_ref[0] = st_ref[...]


def _hgrn(proj, lf, kk, g_norm_w, s0, consts, bsz, seq, rows):
    mall, masks, sels = consts
    steps = seq // rows
    w = HEADS * HEAD_DIM
    assert w == COL_TILE
    row_map = lambda col: (lambda b, s: (b * steps + s, col))
    const2 = lambda b, s: (0, 0)
    const3 = lambda b, s: (0, 0, 0)
    kern = functools.partial(_hgrn_kernel, chunk=CHUNK, n_chunks=rows // CHUNK)
    return pl.pallas_call(
        kern,
        grid=(bsz, steps),
        in_specs=[
            pl.BlockSpec((rows, w), row_map(0)),
            pl.BlockSpec((rows, w), row_map(1)),
            pl.BlockSpec((rows, w), row_map(2)),
            pl.BlockSpec((rows, w), row_map(0)),
            pl.BlockSpec((rows, w), row_map(0)),
            pl.BlockSpec((1, HEAD_DIM), const2),
            pl.BlockSpec((HEADS, HEAD_DIM, HEAD_DIM), const3),
            pl.BlockSpec(mall.shape, const2),
            pl.BlockSpec(masks.shape, const3),
            pl.BlockSpec(sels.shape, const3),
        ],
        out_specs=[
            pl.BlockSpec((rows, w), row_map(0)),
            pl.BlockSpec((1, HEADS, HEAD_DIM, HEAD_DIM), lambda b, s: (b, 0, 0, 0)),
        ],
        out_shape=[
            jax.ShapeDtypeStruct((bsz * seq, w), BF16),
            jax.ShapeDtypeStruct((bsz, HEADS, HEAD_DIM, HEAD_DIM), F32),
        ],
        scratch_shapes=[pltpu.VMEM((HEADS, HEAD_DIM, HEAD_DIM), F32)],
        compiler_params=_params("arbitrary", "arbitrary"),
        name="hgrn2",
    )(proj, proj, proj, lf, kk, g_norm_w.reshape(1, HEAD_DIM), s0, mall, masks, sels)


def _mixer_out_kernel(og_ref, scv_ref, scb_ref, scc_ref, ga_ref, gb_ref, x_ref,
                      pv_ref, pc_ref, mv_ref, mc_ref,
                      wa_ref, wb_ref, wo_ref, cw_ref, nw_ref, wr_ref, br_ref,
                      h1_ref, xn_ref, idx_ref, gate_ref, hbuf, *, tiles_per_seq, n_tiles):
    i = pl.program_id(0)
    tm = x_ref.shape[0]
    tile = jnp.minimum(i, n_tiles - 1)
    first = (tile % tiles_per_seq) == 0

    @pl.when(i == 0)
    def _():
        hbuf[1] = jnp.zeros(hbuf.shape[1:], hbuf.dtype)

    hp = hbuf[(i + 1) % 2]
    ms = jnp.mean(hp * hp, axis=-1, keepdims=True)
    xn = hp * lax.rsqrt(ms + EPS) * nw_ref[...]
    xn_ref[...] = xn.astype(BF16).reshape(xn_ref.shape)

    xh = xn.astype(BF16)
    xl = (xn - xh.astype(F32)).astype(BF16)
    wr = wr_ref[...]
    wh = wr.astype(BF16)
    wl = (wr - wh.astype(F32)).astype(BF16)
    logits = _dot_bt(wh, xh) + _dot_bt(wh, xl) + _dot_bt(wl, xh) + br_ref[...]
    ne = logits.shape[0]
    ie = lax.broadcasted_iota(jnp.int32, logits.shape, 0)
    tops, idxs = [], []
    for _ in range(TOP_K):
        mx = jnp.max(logits, axis=0, keepdims=True)
        ix = jnp.min(jnp.where(logits == mx, ie, ne), axis=0, keepdims=True)
        tops.append(mx)
        idxs.append(ix)
        logits = jnp.where(ie == ix, -jnp.inf, logits)
    es = [jnp.exp(t - tops[0]) for t in tops]
    den = es[0]
    for e in es[1:]:
        den = den + e
    gate_ref[...] = jnp.concatenate([e / den for e in es], axis=0)
    idx_ref[...] = jnp.concatenate(idxs, axis=0)

    u = scc_ref[...].astype(F32) * scv_ref[...].astype(F32)
    halo_prev = pc_ref[...].astype(F32) * pv_ref[...].astype(F32)
    halo_meta = mc_ref[...].astype(F32) * mv_ref[...].astype(F32)
    halo = jnp.where(first, halo_meta, halo_prev)
    hr = halo.shape[0]
    r = lax.broadcasted_iota(jnp.int32, (tm, 1), 0)
    u1 = jnp.where(r == 0, halo[hr - 1:hr], pltpu.roll(u, 1, 0))
    u2 = jnp.where(r == 0, halo[hr - 2:hr - 1], jnp.where(r == 1, halo[hr - 1:hr], pltpu.roll(u, 2, 0)))
    conv = cw_ref[2:3] * u + cw_ref[1:2] * u1 + cw_ref[0:1] * u2
    yb_in = (scb_ref[...].astype(F32) * conv).astype(BF16)

    y_a = _dot(og_ref[...], wa_ref[...])
    y_b = _dot(yb_in, wb_ref[...])
    merged = (ga_ref[...].astype(F32) * y_a + gb_ref[...].astype(F32) * y_b).astype(BF16)
    h1 = x_ref[...] + _dot(merged, wo_ref[...])
    h1_ref[...] = h1
    hbuf[i % 2] = h1


def _mixer_out(og, proj, proj_meta, x2d, wa, wb, wo, conv_w, norm_w, w_router, b_router, seq, tm):
    m, d = x2d.shape
    w = COL_TILE
    halo = proj_meta.shape[0]
    assert tm % halo == 0 and seq % tm == 0 and d == 2 * w
    ne = w_router.shape[1]
    per_halo = tm // halo
    n_tiles = m // tm
    cur = lambda i: jnp.minimum(i, n_tiles - 1)
    done = lambda i: jnp.maximum(i - 1, 0)
    row = lambda col: (lambda i: (cur(i), col))
    prev = lambda col: (lambda i: (jnp.maximum(cur(i) * per_halo - 1, 0), col))
    const = lambda i: (0, 0)
    whole = lambda a: pl.BlockSpec(a.shape, const)
    wr_t = w_router.T
    kern = functools.partial(_mixer_out_kernel, tiles_per_seq=seq // tm, n_tiles=n_tiles)
    return pl.pallas_call(
        kern,
        grid=(n_tiles + 1,),
        in_specs=[
            pl.BlockSpec((tm, w), row(0)),
            pl.BlockSpec((tm, w), row(3)),
            pl.BlockSpec((tm, w), row(4)),
            pl.BlockSpec((tm, w), row(5)),
            pl.BlockSpec((tm, d), row(3)),
            pl.BlockSpec((tm, d), row(4)),
            pl.BlockSpec((tm, d), row(0)),
            pl.BlockSpec((halo, w), prev(3)),
            pl.BlockSpec((halo, w), prev(5)),
            pl.BlockSpec((halo, w), lambda i: (0, 3)),
            pl.BlockSpec((halo, w), lambda i: (0, 5)),
            whole(wa), whole(wb), whole(wo),
            pl.BlockSpec(conv_w.shape, const),
            pl.BlockSpec((1, d), const),
            pl.BlockSpec((ne, d), const),
            pl.BlockSpec((ne, 1), const),
        ],
        out_specs=[
            pl.BlockSpec((tm, d), row(0)),
            pl.BlockSpec((tm, d // LANES, LANES), lambda i: (done(i), 0, 0)),
            pl.BlockSpec((TOP_K, tm), lambda i: (0, done(i))),
            pl.BlockSpec((TOP_K, tm), lambda i: (0, done(i))),
        ],
        out_shape=[
            jax.ShapeDtypeStruct((m, d), F32),
            jax.ShapeDtypeStruct((m, d // LANES, LANES), BF16),
            jax.ShapeDtypeStruct((TOP_K, m), jnp.int32),
            jax.ShapeDtypeStruct((TOP_K, m), F32),
        ],
        scratch_shapes=[pltpu.VMEM((2, tm, d), F32)],
        compiler_params=_params("arbitrary"),
        name="mixer_out",
    )(og, proj, proj, proj, proj, proj, x2d, proj, proj, proj_meta, proj_meta,
      wa, wb, wo, conv_w, norm_w.reshape(1, d), wr_t, b_router.reshape(ne, 1))


def _dispatch_kernel(cnt_ref, pst_ref, used_ref, pos_ref, x_ref, o_hbm, xbuf, zbuf, sem, zsem):
    i = pl.program_id(0)
    n_steps = pl.num_programs(0)
    tt = x_ref.shape[0]
    n_sub = o_hbm.shape[0] // EXPERT_SUB
    bits = [1 << b for b in reversed(range(EXPERT_SUB.bit_length() - 1))]
    slot = i % 2

    xbuf[slot] = x_ref[...]

    def issue(r, carry):
        for k in range(TOP_K):
            p = pos_ref[0, 0, k * tt + r]
            pltpu.make_async_copy(xbuf.at[slot, pl.ds(r, 1)], o_hbm.at[pl.ds(p, 1)],
                                  sem.at[slot]).start(priority=k % 2)
        return carry

    lax.fori_loop(0, tt, issue, 0, unroll=4)

    def tokens_done(s):
        return [pltpu.make_async_copy(xbuf.at[s], o_hbm.at[pl.ds(0, tt)], sem.at[s]) for _ in range(TOP_K)]

    def zero_copies(fn):
        for e in range(cnt_ref.shape[0]):
            npad = (-cnt_ref[e]) & (EXPERT_SUB - 1)
            base = pst_ref[e] + cnt_ref[e]
            for bit in bits:
                @pl.when((npad & bit) != 0)
                def _():
                    row = base + (npad & ~(2 * bit - 1))
                    fn(pltpu.make_async_copy(zbuf.at[pl.ds(0, bit)], o_hbm.at[pl.ds(row, bit)], zsem))
        for j in range(cnt_ref.shape[0]):
            blk = used_ref[0] + j

            @pl.when(blk < n_sub)
            def _():
                row = pl.multiple_of(blk * EXPERT_SUB, EXPERT_SUB)
                fn(pltpu.make_async_copy(zbuf, o_hbm.at[pl.ds(row, EXPERT_SUB)], zsem))

    @pl.when(i == 0)
    def _():
        zbuf[...] = jnp.zeros_like(zbuf)
        zero_copies(lambda c: c.start())

    @pl.when(i > 0)
    def _():
        for c in tokens_done(1 - slot):
            c.wait()

    @pl.when(i == n_steps - 1)
    def _():
        for c in tokens_done(slot):
            c.wait()
        zero_copies(lambda c: c.wait())


def _dispatch(xn3, pos_t, counts, pad_start, used_sub, n_rows):
    m, sub, lanes = xn3.shape
    nt = pos_t.shape[0]
    tt = m // nt
    grid_spec = pltpu.PrefetchScalarGridSpec(
        num_scalar_prefetch=3,
        grid=(nt,),
        in_specs=[
            pl.BlockSpec((1, 1, pos_t.shape[2]), lambda i, c, p, u: (i, 0, 0), memory_space=pltpu.SMEM),
            pl.BlockSpec((tt, sub, lanes), lambda i, c, p, u: (i, 0, 0)),
        ],
        out_specs=pl.BlockSpec(memory_space=pl.ANY),
        scratch_shapes=[pltpu.VMEM((2, tt, sub, lanes), xn3.dtype), pltpu.VMEM((EXPERT_SUB, sub, lanes), xn3.dtype),
                        pltpu.SemaphoreType.DMA((2,)), pltpu.SemaphoreType.DMA(())],
    )
    return pl.pallas_call(
        _dispatch_kernel,
        grid_spec=grid_spec,
        out_shape=jax.ShapeDtypeStruct((n_rows, sub, lanes), xn3.dtype),
        compiler_params=_params("arbitrary"),
        name="dispatch",
    )(counts, pad_start, used_sub.reshape(1), pos_t, xn3)


def _row_chunks(nsub, chunk_fn):
    per = EXPERT_CHUNK_SUBS
    big = per * EXPERT_SUB
    n_big = lax.div(nsub, per)

    def body(c, carry):
        chunk_fn(c * per, pl.multiple_of(c * big, big), big)
        return carry

    lax.fori_loop(0, n_big, body, 0)
    rem = nsub - n_big * per
    bit = per // 2
    while bit >= 1:
        done = rem & ~(2 * bit - 1)

        @pl.when((rem & bit) != 0)
        def _():
            sub0 = n_big * per + done
            chunk_fn(sub0, pl.multiple_of(sub0 * EXPERT_SUB, EXPERT_SUB), bit * EXPERT_SUB)

        bit //= 2


def _sub_pieces(n, fn):
    bit = 1 << (EXPERT_SUBS.bit_length() - 1)
    while bit >= 1:
        @pl.when((n & bit) != 0)
        def _():
            fn(n & ~(2 * bit - 1), bit)

        bit //= 2


def _ffn_up_kernel(st_ref, ns_ref, zf_ref, se_ref, jm_ref, x_hbm, wg_ref, wu_ref, b_ref, h_hbm,
                   xstage, xbuf, hbuf, wgb_ref, wub_ref, sem_x, sem_h):
    s = pl.program_id(0)
    j = pl.program_id(1)
    n_s = pl.num_programs(0)
    nj = pl.num_programs(1)
    step = s * nj + j
    nsub = ns_ref[s]
    real = jnp.logical_and(nsub > 0, zf_ref[s] == 0)
    hs = step % 2

    def x_copy(sb, first, count):
        return pltpu.make_async_copy(x_hbm.at[pl.ds(st_ref[sb] + first, count)],
                                     xstage.at[pl.ds(first, count)], sem_x)

    def h_copy(sb, jj, first, count, slot):
        row = pl.multiple_of((st_ref[sb] + first) * EXPERT_SUB, EXPERT_SUB)
        return pltpu.make_async_copy(
            hbuf.at[slot, pl.ds(pl.multiple_of(first * EXPERT_SUB, EXPERT_SUB), count * EXPERT_SUB)],
            h_hbm.at[jj, pl.ds(row, count * EXPERT_SUB)], sem_h.at[slot])

    def for_x_subs(sb, fn):
        _sub_pieces(ns_ref[sb] * (1 - zf_ref[sb]), fn)

    def for_h_subs(sb, fn):
        _sub_pieces(ns_ref[sb], fn)

    @pl.when(step == 0)
    def _():
        for_x_subs(0, lambda first, count: x_copy(0, first, count).start())

    @pl.when(j == 0)
    def _():
        for_x_subs(s, lambda first, count: x_copy(s, first, count).wait())

    @pl.when(step >= 2)
    def _():
        sp = lax.div(step - 2, nj)
        jp = step - 2 - sp * nj
        for_h_subs(sp, lambda first, count: h_copy(sp, jp, first, count, hs).wait())

    @pl.when(real)
    def _():
        wgb_ref[...] = wg_ref[0].astype(BF16)
        wub_ref[...] = wu_ref[0].astype(BF16)
        b_row = se_ref[s] * (2 * nj) + j
        bg = b_ref[pl.ds(b_row, 1), :]
        bu = b_ref[pl.ds(b_row + nj, 1), :]

        def make_chunk(first_tile):
            def chunk(sub0, row0, nrows):
                if first_tile:
                    xs = jnp.concatenate([xstage[sub0 + t].reshape(EXPERT_SUB, xbuf.shape[1])
                                          for t in range(nrows // EXPERT_SUB)], axis=0)
                    xbuf[pl.ds(row0, nrows), :] = xs
                else:
                    xs = xbuf[pl.ds(row0, nrows), :]
                g = _dot(xs, wgb_ref[...]) + bg
                u = _dot(xs, wub_ref[...]) + bu
                g = jnp.minimum(g, SWIGLU_LIMIT)
                u = jnp.clip(u, -SWIGLU_LIMIT, SWIGLU_LIMIT)
                hbuf[hs, pl.ds(row0, nrows), :] = ((u + 1.0) * (g * jax.nn.sigmoid(SWIGLU_ALPHA * g))).astype(BF16)

            return chunk

        @pl.when(j == 0)
        def _():
            _row_chunks(nsub, make_chunk(True))

        @pl.when(j > 0)
        def _():
            _row_chunks(nsub, make_chunk(False))

    @pl.when(jnp.logical_and(j == 0, s + 1 < n_s))
    def _():
        nxt = jnp.minimum(s + 1, n_s - 1)
        for_x_subs(nxt, lambda first, count: x_copy(nxt, first, count).start())

    @pl.when(zf_ref[s] == 1)
    def _():
        hbuf[hs] = jnp.zeros(hbuf.shape[1:], hbuf.dtype)

    for_h_subs(s, lambda first, count: h_copy(s, j, first, count, hs).start())

    @pl.when(step == n_s * nj - 1)
    def _():
        sp = lax.div(step - 1, nj)
        jp = step - 1 - sp * nj
        for_h_subs(sp, lambda first, count: h_copy(sp, jp, first, count, 1 - hs).wait())
        for_h_subs(s, lambda first, count: h_copy(s, j, first, count, hs).wait())


def _ffn_down_kernel(st_ref, ns_ref, zf_ref, se_ref, cm_ref, h_hbm, wd_ref, b_ref, y_hbm,
                     hb, yrow, ytile, wdb_ref, sem_h, sem_y):
    s = pl.program_id(0)
    c = pl.program_id(1)
    n_s = pl.num_programs(0)
    nc = pl.num_programs(1)
    step = s * nc + c
    njh = hb.shape[1]
    tn = wdb_ref.shape[1]
    nsub = ns_ref[s]
    real = jnp.logical_and(nsub > 0, zf_ref[s] == 0)

    def h_copy(sb, jj, first, count, slot):
        row = pl.multiple_of((st_ref[sb] + first) * EXPERT_SUB, EXPERT_SUB)
        return pltpu.make_async_copy(
            h_hbm.at[jj, pl.ds(row, count * EXPERT_SUB)],
            hb.at[slot, jj, pl.ds(pl.multiple_of(first * EXPERT_SUB, EXPERT_SUB), count * EXPERT_SUB)],
            sem_h.at[slot])

    def y_copy(sb, first, count):
        return pltpu.make_async_copy(ytile.at[pl.ds(first, count)],
                                     y_hbm.at[pl.ds(st_ref[sb] + first, count)], sem_y)

    def for_h_subs(sb, fn):
        def all_tiles(first, count):
            for jj in range(njh):
                fn(jj, first, count)

        _sub_pieces(ns_ref[sb] * (1 - zf_ref[sb]), all_tiles)

    def for_y_subs(sb, fn):
        _sub_pieces(ns_ref[sb], fn)

    @pl.when(step == 0)
    def _():
        for_h_subs(0, lambda jj, first, count: h_copy(0, jj, first, count, 0).start())

    @pl.when(jnp.logical_and(c == 0, s + 1 < n_s))
    def _():
        nxt = jnp.minimum(s + 1, n_s - 1)
        for_h_subs(nxt, lambda jj, first, count: h_copy(nxt, jj, first, count, (s + 1) % 2).start())

    @pl.when(c == 0)
    def _():
        for_h_subs(s, lambda jj, first, count: h_copy(s, jj, first, count, s % 2).wait())

    @pl.when(jnp.logical_and(c == nc - 1, s > 0))
    def _():
        sp = jnp.maximum(s - 1, 0)
        for_y_subs(sp, lambda first, count: y_copy(sp, first, count).wait())

    @pl.when(real)
    def _():
        wdb_ref[...] = wd_ref[0].astype(BF16)
        bd = b_ref[pl.ds(se_ref[s] * nc + c, 1), :]
        h_slot = s % 2
        n_tiles = yrow.shape[1] // tn
        for cc in range(n_tiles):
            @pl.when(c == cc)
            def _():
                def chunk(sub0, row0, nrows):
                    hid = jnp.concatenate([hb[h_slot, jj, pl.ds(row0, nrows), :] for jj in range(njh)], axis=1)
                    y = (_dot(hid, wdb_ref[...]) + bd).astype(yrow.dtype)
                    if cc < n_tiles - 1:
                        yrow[pl.ds(row0, nrows), cc * tn:(cc + 1) * tn] = y
                    else:
                        full = jnp.concatenate([yrow[pl.ds(row0, nrows), 0:cc * tn], y], axis=1)
                        for t in range(nrows // EXPERT_SUB):
                            ytile[sub0 + t] = full[t * EXPERT_SUB:(t + 1) * EXPERT_SUB].reshape(ytile.shape[1:])

                _row_chunks(nsub, chunk)

    @pl.when(c == nc - 1)
    def _():
        @pl.when(zf_ref[s] == 1)
        def _():
            ytile[...] = jnp.zeros(ytile.shape, ytile.dtype)

        for_y_subs(s, lambda first, count: y_copy(s, first, count).start())

        @pl.when(s == n_s - 1)
        def _():
            for_y_subs(s, lambda first, count: y_copy(s, first, count).wait())


def _experts(x_rows, w_up, b_up, w_down, b_down, sb_start, sb_nsub, sb_zero, sb_expert):
    n_rows, x_sub, x_lanes = x_rows.shape
    d = x_sub * x_lanes
    ne, _, ff2 = w_up.shape
    ff = ff2 // 2
    tf = EXPERT_FF_TILE
    tn = EXPERT_OUT_TILE
    nj = ff // tf
    nc = d // tn
    n_sb = jnp.sum((sb_nsub > 0).astype(jnp.int32))
    rows = EXPERT_SUBS * EXPERT_SUB
    n_sub = n_rows // EXPERT_SUB
    sub_shape = (n_sub, EXPERT_SUB, x_sub, x_lanes)
    x_rows = x_rows.reshape(sub_shape)
    sb_start = sb_start // EXPERT_SUB
    is_real = jnp.logical_and(sb_nsub > 0, sb_zero == 0)[:, None]
    jm = jnp.where(is_real, jnp.arange(nj, dtype=jnp.int32)[None, :], nj - 1).astype(jnp.int32)
    cm = jnp.where(is_real, jnp.arange(nc, dtype=jnp.int32)[None, :], nc - 1).astype(jnp.int32)
    any_spec = pl.BlockSpec(memory_space=pl.ANY)

    up_spec = pltpu.PrefetchScalarGridSpec(
        num_scalar_prefetch=5,
        grid=(n_sb, nj),
        in_specs=[
            any_spec,
            pl.BlockSpec((1, d, tf), lambda s, j, st, ns, zf, se, jm: (se[s], 0, jm[s, j])),
            pl.BlockSpec((1, d, tf), lambda s, j, st, ns, zf, se, jm: (se[s], 0, nj + jm[s, j])),
            pl.BlockSpec((ne * 2 * nj, tf), lambda s, j, st, ns, zf, se, jm: (0, 0)),
        ],
        out_specs=any_spec,
        scratch_shapes=[
            pltpu.VMEM((EXPERT_SUBS, EXPERT_SUB, x_sub, x_lanes), BF16),
            pltpu.VMEM((rows, d), BF16),
            pltpu.VMEM((2, rows, tf), BF16),
            pltpu.VMEM((d, tf), BF16),
            pltpu.VMEM((d, tf), BF16),
            pltpu.SemaphoreType.DMA(()),
            pltpu.SemaphoreType.DMA((2,)),
        ],
    )
    hidden = pl.pallas_call(
        _ffn_up_kernel,
        grid_spec=up_spec,
        out_shape=jax.ShapeDtypeStruct((nj, n_rows, tf), BF16),
        compiler_params=_params("arbitrary", "arbitrary"),
        name="ffn_up",
    )(sb_start, sb_nsub, sb_zero, sb_expert, jm, x_rows, w_up, w_up,
      b_up.reshape(ne * 2 * nj, tf))

    down_spec = pltpu.PrefetchScalarGridSpec(
        num_scalar_prefetch=5,
        grid=(n_sb, nc),
        in_specs=[
            any_spec,
            pl.BlockSpec((1, ff, tn), lambda s, c, st, ns, zf, se, cm: (se[s], 0, cm[s, c])),
            pl.BlockSpec((ne * nc, tn), lambda s, c, st, ns, zf, se, cm: (0, 0)),
        ],
        out_specs=any_spec,
        scratch_shapes=[
            pltpu.VMEM((2, nj, rows, tf), BF16),
            pltpu.VMEM((rows, d), BF16),
            pltpu.VMEM((EXPERT_SUBS, EXPERT_SUB, x_sub, x_lanes), BF16),
            pltpu.VMEM((ff, tn), BF16),
            pltpu.SemaphoreType.DMA((2,)),
            pltpu.SemaphoreType.DMA(()),
        ],
    )
    y_rows = pl.pallas_call(
        _ffn_down_kernel,
        grid_spec=down_spec,
        out_shape=jax.ShapeDtypeStruct(sub_shape, BF16),
        compiler_params=_params("arbitrary", "arbitrary"),
        name="ffn_down",
    )(sb_start, sb_nsub, sb_zero, sb_expert, cm, hidden, w_down, b_down.reshape(ne * nc, tn))
    return y_rows.reshape(n_rows, x_sub, x_lanes)


def _combine_kernel(pos_ref, pos_next_ref, gate_ref, h1_ref, fw_ref, y_hbm, o_ref, buf_ref, sem):
    i = pl.program_id(0)
    tt = h1_ref.shape[0]

    def start_rows(pos, slot):
        def issue(r, carry):
            for k in range(TOP_K):
                p = pos[0, 0, k * tt + r]
                pltpu.make_async_copy(y_hbm.at[pl.ds(p, 1)], buf_ref.at[slot, k, pl.ds(r, 1)],
                                      sem.at[slot]).start(priority=k % 2)
            return carry

        lax.fori_loop(0, tt, issue, 0, unroll=4)

    @pl.when(i == 0)
    def _():
        start_rows(pos_ref, 0)

    @pl.when(i + 1 < pl.num_programs(0))
    def _():
        start_rows(pos_next_ref, (i + 1) % 2)

    slot = i % 2
    for k in range(TOP_K):
        pltpu.make_async_copy(y_hbm.at[pl.ds(0, tt)], buf_ref.at[slot, k], sem.at[slot]).wait()

    gate = gate_ref[...]
    gpad = jnp.concatenate([gate, jnp.zeros((tt - TOP_K, tt), F32)], axis=0)
    gcol = gpad.T
    acc = h1_ref[...]
    for k in range(TOP_K):
        acc = acc + gcol[:, k:k + 1] * buf_ref[slot, k].reshape(acc.shape).astype(F32)
    ms = jnp.mean(acc * acc, axis=-1, keepdims=True)
    o_ref[...] = acc * lax.rsqrt(ms + EPS) * fw_ref[...]


def _pos_tiles(pos, m):
    tt = COMBINE_ROWS
    nt = m // tt
    return pos.reshape(TOP_K, nt, tt).transpose(1, 0, 2).reshape(nt, 1, TOP_K * tt)


def _combine(y_rows, pos_t, gate, h1, final_w):
    m, d = h1.shape
    tt = COMBINE_ROWS
    nt = m // tt
    return pl.pallas_call(
        _combine_kernel,
        grid=(nt,),
        in_specs=[
            pl.BlockSpec((1, 1, TOP_K * tt), lambda i: (i, 0, 0), memory_space=pltpu.SMEM),
            pl.BlockSpec((1, 1, TOP_K * tt), lambda i: (jnp.minimum(i + 1, nt - 1), 0, 0), memory_space=pltpu.SMEM),
            pl.BlockSpec((TOP_K, tt), lambda i: (0, i)),
            pl.BlockSpec((tt, d), lambda i: (i, 0)),
            pl.BlockSpec((1, d), lambda i: (0, 0)),
            pl.BlockSpec(memory_space=pl.ANY),
        ],
        out_specs=pl.BlockSpec((tt, d), lambda i: (i, 0)),
        out_shape=jax.ShapeDtypeStruct((m, d), F32),
        scratch_shapes=[pltpu.VMEM((2, TOP_K, tt) + y_rows.shape[1:], y_rows.dtype), pltpu.SemaphoreType.DMA((2,))],
        compiler_params=_params("arbitrary"),
        name="combine",
    )(pos_t, pos_t, gate, h1, final_w.reshape(1, d), y_rows)


def _routing_tables(idx, m):
    i32 = jnp.int32
    n_assign = TOP_K * m
    flat_e = idx.reshape(n_assign)
    onehot = flat_e[:, None] == jnp.arange(N_EXPERTS, dtype=i32)[None, :]
    csum = jnp.cumsum(onehot.astype(i32), axis=0)
    rank = jnp.sum(jnp.where(onehot, csum - 1, 0), axis=1)
    counts = csum[-1]
    padded = (counts + EXPERT_SUB - 1) // EXPERT_SUB * EXPERT_SUB
    pad_end = jnp.cumsum(padded)
    pad_start = pad_end - padded
    pos = (jnp.sum(jnp.where(onehot, pad_start[None, :], 0), axis=1) + rank).astype(i32)
    n_sub = -(-(n_assign + N_EXPERTS * (EXPERT_SUB - 1)) // EXPERT_SUB)
    n_rows = n_sub * EXPERT_SUB
    used_sub = pad_end[-1] // EXPERT_SUB

    rows = EXPERT_SUBS * EXPERT_SUB
    n_sb = -(-n_sub // EXPERT_SUBS) + N_EXPERTS + 1
    nsb_e = (padded + rows - 1) // rows
    sb_cum = jnp.cumsum(nsb_e)
    total_real = sb_cum[-1]
    s = jnp.arange(n_sb, dtype=i32)
    e_s = jnp.minimum(jnp.sum((sb_cum[None, :] <= s[:, None]).astype(i32), axis=1), N_EXPERTS - 1)
    local = s - (sb_cum[e_s] - nsb_e[e_s])
    real = s < total_real
    start_real = pad_start[e_s] + local * rows
    nsub_real = jnp.clip((padded[e_s] - local * rows) // EXPERT_SUB, 0, EXPERT_SUBS)
    fill_idx = s - total_real
    start_fill = pad_end[-1] + fill_idx * rows
    nsub_fill = jnp.clip(n_sub - used_sub - fill_idx * EXPERT_SUBS, 0, EXPERT_SUBS)
    is_fill = jnp.logical_and(jnp.logical_not(real), nsub_fill > 0)
    sb_start = jnp.where(real, start_real, jnp.where(is_fill, start_fill, 0)).astype(i32)
    sb_nsub = jnp.where(real, nsub_real, jnp.where(is_fill, nsub_fill, 0)).astype(i32)
    sb_zero = is_fill.astype(i32)
    last_e = e_s[jnp.maximum(total_real - 1, 0)]
    sb_expert = jnp.where(real, e_s, last_e).astype(i32)
    return pos, counts.astype(i32), pad_start.astype(i32), used_sub.astype(i32), n_rows, sb_start, sb_nsub, sb_zero, sb_expert


def _main_tiles(d):
    w = HEADS * HEAD_DIM
    sc = d // 2
    sizes = (w, w, w, w, sc, sc, sc, d, d)
    acts = (1, None, 0, 1, 0, 0, 0, 2, 2)
    order = (0, 2, 3, 4, 5, 6, 7, 8)
    starts = np.concatenate([[0], np.cumsum(sizes)])
    cols, codes = [], []
    for seg in order:
        assert sizes[seg] % COL_TILE == 0 and starts[seg] % COL_TILE == 0
        for t in range(sizes[seg] // COL_TILE):
            cols.append(int(starts[seg]) // COL_TILE + t)
            codes.append(acts[seg])
    return cols, codes, int(starts[1]) // COL_TILE


def _layer(x2d, meta, bsz, seq, norm_mix_w, w_in, lb_logits, g_norm_w, w_hgrn_out, conv_w, w_conv_out, w_o,
           norm_ffn_w, w_router, b_router, w_up, b_up, w_down, b_down, final_norm_w):
    m, d = x2d.shape
    cols, codes, fcol = _main_tiles(d)
    consts = _hgrn_constants(CHUNK)

    xn, lf, kk, xn_meta, lf_meta, kk_meta = _fgate(x2d, meta, norm_mix_w, w_in, lb_logits, fcol, NORM_ROWS)
    proj, proj_meta = _inproj(xn, xn_meta, w_in, cols, codes, INPROJ_ROWS)

    pad = CHUNK - N_META
    front = lambda a: jnp.pad(a, ((pad, 0), (0, 0)))
    s_zero = jnp.zeros((HEADS, HEAD_DIM, HEAD_DIM), F32)
    _, s_meta = _hgrn(front(proj_meta), front(lf_meta), front(kk_meta), g_norm_w, s_zero, consts, 1, CHUNK, CHUNK)
    og, _ = _hgrn(proj, lf, kk, g_norm_w, s_meta[0], consts, bsz, seq, HGRN_ROWS)
    h1, xn_ffn, idx, gate = _mixer_out(og, proj, proj_meta, x2d, w_hgrn_out.astype(BF16), w_conv_out.astype(BF16),
                                   w_o.astype(BF16), conv_w, norm_ffn_w, w_router, b_router, seq, MIXER_ROWS)

    pos, counts, pad_start, used_sub, n_rows, sb_start, sb_nsub, sb_zero, sb_expert = _routing_tables(idx, m)
    pos_t = _pos_tiles(pos, m)
    x_rows = _dispatch(xn_ffn, pos_t, counts, pad_start, used_sub, n_rows)
    y_rows = _experts(x_rows, w_up, b_up, w_down, b_down, sb_start, sb_nsub, sb_zero, sb_expert)
    return _combine(y_rows, pos_t, gate, h1, final_norm_w)


def kernel(x, meta_tokens, norm_mix_w, w_in, lb_logits, g_norm_w, w_hgrn_out, conv_w, w_conv_out, w_o, norm_ffn_w,
           w_router, b_router, w_up, b_up, w_down, b_down, final_norm_w):
    bsz, seq, d = x.shape
    assert norm_mix_w.shape[0] == 1, "single-layer block"
    out = _layer(x.reshape(bsz * seq, d), meta_tokens.astype(x.dtype), bsz, seq, norm_mix_w[0], w_in[0], lb_logits,
                 g_norm_w[0], w_hgrn_out[0], conv_w[0], w_conv_out[0], w_o[0], norm_ffn_w[0], w_router[0],
                 b_router[0], w_up[0], b_up[0], w_down[0], b_down[0], final_norm_w)
    return out.reshape(bsz, seq, d)
```

```python
import functools

import numpy as np
import jax
import jax.numpy as jnp
from jax import lax
from jax.experimental import pallas as pl
from jax.experimental.pallas import tpu as pltpu

F32 = jnp.float32
BF16 = jnp.bfloat16

LANES = 128
N_META = 16
HEADS = 8
HEAD_DIM = 128
N_EXPERTS = 32
TOP_K = 4
SWIGLU_LIMIT = 7.0
SWIGLU_ALPHA = 1.702
EPS = 1e-6

CHUNK = 256
HGRN_BASE = 64
HGRN_ROWS = 512
NORM_ROWS = 512
INPROJ_ROWS = 1024
MIXER_ROWS = 256
COL_TILE = 1024
EXPERT_SUB = 128
EXPERT_SUBS = 18
EXPERT_CHUNK_SUBS = 8
EXPERT_FF_TILE = 512
EXPERT_OUT_TILE = 512
COMBINE_ROWS = 256
RANK_ROWS = 1024
RANK_LANES = 256

V7X_VMEM_BYTES = 64 * 1024 * 1024
V7X_VMEM_LIMIT = V7X_VMEM_BYTES - 8 * 1024 * 1024


def _dot(a, b):
    return jnp.dot(a, b, preferred_element_type=F32)


def _dot_bt(a, b):
    return lax.dot_general(a, b, (((1,), (1,)), ((), ())), preferred_element_type=F32)


def _dot_at(a, b):
    return lax.dot_general(a, b, (((0,), (0,)), ((), ())), preferred_element_type=F32)


def _split3(x):
    hi = x.astype(BF16)
    r1 = x - hi.astype(F32)
    mid = r1.astype(BF16)
    lo = (r1 - mid.astype(F32)).astype(BF16)
    return hi, mid, lo


def _params(*sem):
    return pltpu.CompilerParams(dimension_semantics=sem, vmem_limit_bytes=V7X_VMEM_LIMIT)


def _activate(z, code):
    s = jax.nn.sigmoid(z)
    return jnp.where(code == 0, z, jnp.where(code == 1, z * s, s))


def _inproj_kernel(col_ref, code_ref, x_ref, xm_ref, w_ref, o_ref, om_ref, wb_ref):
    code = code_ref[pl.program_id(0)]

    @pl.when(pl.program_id(1) == 0)
    def _():
        wb_ref[...] = w_ref[...].astype(BF16)
        om_ref[...] = _activate(_dot(xm_ref[...], wb_ref[...]), code).astype(om_ref.dtype)

    o_ref[...] = _activate(_dot(x_ref[...], wb_ref[...]), code).astype(o_ref.dtype)


def _inproj(xn, xn_meta, w_in, cols, codes, tm):
    m, d = xn.shape
    mm = xn_meta.shape[0]
    nt = len(cols)
    grid_spec = pltpu.PrefetchScalarGridSpec(
        num_scalar_prefetch=2,
        grid=(nt, m // tm),
        in_specs=[
            pl.BlockSpec((tm, d), lambda n, i, col, code: (i, 0)),
            pl.BlockSpec((mm, d), lambda n, i, col, code: (0, 0)),
            pl.BlockSpec((d, COL_TILE), lambda n, i, col, code: (0, col[n])),
        ],
        out_specs=[
            pl.BlockSpec((tm, COL_TILE), lambda n, i, col, code: (i, n)),
            pl.BlockSpec((mm, COL_TILE), lambda n, i, col, code: (0, n)),
        ],
        scratch_shapes=[pltpu.VMEM((d, COL_TILE), BF16)],
    )
    return pl.pallas_call(
        _inproj_kernel,
        grid_spec=grid_spec,
        out_shape=[jax.ShapeDtypeStruct((m, nt * COL_TILE), BF16), jax.ShapeDtypeStruct((mm, nt * COL_TILE), BF16)],
        compiler_params=_params("arbitrary", "arbitrary"),
        name="inproj",
    )(jnp.asarray(cols, jnp.int32), jnp.asarray(codes, jnp.int32), xn, xn_meta, w_in)


def _fgate_kernel(x_ref, xm_ref, nw_ref, w_ref, lbl_ref, xn_ref, lf_ref, kk_ref, xnm_ref, lfm_ref, kkm_ref, wb_ref):
    lbl = lbl_ref[...]
    e = jnp.exp(lbl - jnp.max(lbl, axis=0, keepdims=True))
    lb = e[0:1] / jnp.sum(e, axis=0, keepdims=True)

    def gate(x, xn_out, lf_out, kk_out):
        ms = jnp.mean(x * x, axis=-1, keepdims=True)
        xn = (x * lax.rsqrt(ms + EPS) * nw_ref[...]).astype(BF16)
        xn_out[...] = xn
        z = _dot(xn, wb_ref[...])
        lf_out[...] = jnp.log(lb + (1.0 - lb) * jax.nn.sigmoid(z))
        kk_out[...] = (1.0 - lb) * jax.nn.sigmoid(-z)

    @pl.when(pl.program_id(0) == 0)
    def _():
        wb_ref[...] = w_ref[...].astype(BF16)
        gate(xm_ref[...], xnm_ref, lfm_ref, kkm_ref)

    gate(x_ref[...], xn_ref, lf_ref, kk_ref)


def _fgate(x, x_meta, norm_w, w_in, lb_logits, col, tm):
    m, d = x.shape
    mm = x_meta.shape[0]
    r = lb_logits.shape[0]
    out = jax.ShapeDtypeStruct((m, COL_TILE), F32)
    out_meta = jax.ShapeDtypeStruct((mm, COL_TILE), F32)
    return pl.pallas_call(
        _fgate_kernel,
        grid=(m // tm,),
        in_specs=[
            pl.BlockSpec((tm, d), lambda i: (i, 0)),
            pl.BlockSpec((mm, d), lambda i: (0, 0)),
            pl.BlockSpec((1, d), lambda i: (0, 0)),
            pl.BlockSpec((d, COL_TILE), lambda i: (0, col)),
            pl.BlockSpec((r, COL_TILE), lambda i: (0, 0)),
        ],
        out_specs=[pl.BlockSpec((tm, d), lambda i: (i, 0))] + [pl.BlockSpec((tm, COL_TILE), lambda i: (i, 0))] * 2
        + [pl.BlockSpec((mm, d), lambda i: (0, 0))] + [pl.BlockSpec((mm, COL_TILE), lambda i: (0, 0))] * 2,
        out_shape=[jax.ShapeDtypeStruct((m, d), BF16), out, out,
                   jax.ShapeDtypeStruct((mm, d), BF16), out_meta, out_meta],
        scratch_shapes=[pltpu.VMEM((d, COL_TILE), BF16)],
        compiler_params=_params("arbitrary"),
        name="norm_fgate",
    )(x, x_meta, norm_w.reshape(1, d), w_in, lb_logits)


def _hgrn_constants(c):
    base = HGRN_BASE
    nlow = int(np.log2(base))
    nl = int(np.log2(c))
    assert (1 << nl) == c and (1 << nlow) == base and c >= base
    rr = np.arange(base)[:, None]
    uu = np.arange(base)[None, :]
    mats = [uu <= rr, uu > rr]
    sels = []
    for lvl in range(nlow):
        b = 1 << lvl
        start = (rr // (2 * b)) * (2 * b)
        mid = start + b - 1
        second = (rr - start) >= b
        mats.append(np.where(second, (uu > mid) & (uu <= rr), (uu > rr) & (uu <= mid)))
        sels.append(np.broadcast_to(second, (base, HEADS * HEAD_DIM)))
    tt = np.arange(c)[:, None]
    ss = np.arange(c)[None, :]
    masks = []
    for lvl in range(nl):
        b = 1 << lvl
        masks.append(((tt // (2 * b)) == (ss // (2 * b))) & ((tt % (2 * b)) >= b) & ((ss % (2 * b)) < b))
    masks.append(np.eye(c, dtype=bool))
    m1 = np.concatenate(mats, 0).astype(np.float32)
    mall = jnp.asarray(np.concatenate([m1, m1, m1], axis=1), BF16)
    return mall, jnp.asarray(np.stack(masks).astype(np.float32)), jnp.asarray(np.stack(sels).astype(np.float32))


def _hgrn_kernel(q_ref, v_ref, g_ref, lf_ref, kk_ref, gw_ref, s0_ref, mall_ref, mask_ref, sel_ref,
                 o_ref, sfin_ref, st_ref, *, chunk, n_chunks):
    base = HGRN_BASE
    nb = chunk // base
    nlow = sel_ref.shape[0]
    nl = mask_ref.shape[0] - 1
    step = pl.program_id(1)

    @pl.when(step == 0)
    def _():
        st_ref[...] = s0_ref[...]

    def chunk_body(ci, carry):
        r0 = pl.multiple_of(ci * chunk, chunk)
        rows = pl.ds(r0, chunk)
        qb = q_ref[rows, :]
        q = qb.astype(F32)
        k = kk_ref[rows, :]
        blk = lambda a, i: a[i * base:(i + 1) * base]

        pre, suf, e_low = [], [], []
        for i in range(nb):
            hi, mid, lo = _split3(lf_ref[pl.ds(r0 + i * base, base), :])
            args = _dot(mall_ref[...], jnp.concatenate([hi, mid, lo], axis=0))
            pre.append(args[0:base])
            suf.append(args[base:2 * base])
            e_low.append(jnp.exp(args[2 * base:]))
        tot = [p[base - 1:base] for p in pre]

        def span(lo_blk, hi_blk):
            acc = None
            for j in range(lo_blk, hi_blk):
                acc = tot[j] if acc is None else acc + tot[j]
            return acc

        def shifted(a, off):
            return a if off is None else a + off

        xs = []
        for lvl in range(nlow):
            parts = [jnp.where(sel_ref[lvl] > 0.5, blk(q, i), blk(k, i)) * blk(e_low[i], lvl) for i in range(nb)]
            xs.append(jnp.concatenate(parts, axis=0).astype(BF16))
        for lvl in range(nlow, nl):
            half = (1 << lvl) // base
            parts = []
            for i in range(nb):
                g = i % (2 * half)
                if g >= half:
                    parts.append(blk(q, i) * jnp.exp(shifted(pre[i], span(i - (g - half), i))))
                else:
                    parts.append(blk(k, i) * jnp.exp(shifted(suf[i], span(i + 1, i - g + half))))
            xs.append(jnp.concatenate(parts, axis=0).astype(BF16))
        q_in = jnp.concatenate([blk(q, i) * jnp.exp(shifted(pre[i], span(0, i))) for i in range(nb)],
                               axis=0).astype(BF16)
        k_out = jnp.concatenate([blk(k, i) * jnp.exp(shifted(suf[i], span(i + 1, nb))) for i in range(nb)],
                                axis=0).astype(BF16)
        dec = jnp.exp(span(0, nb))
        kb = k.astype(BF16)

        for h in range(HEADS):
            cs = slice(h * HEAD_DIM, (h + 1) * HEAD_DIM)
            scores = mask_ref[nl] * _dot_bt(qb[:, cs], kb[:, cs])
            for lvl in range(nl):
                x = xs[lvl][:, cs]
                scores = scores + mask_ref[lvl] * _dot_bt(x, x)
            v = v_ref[rows, cs]
            st = st_ref[h]
            o = _dot(scores.astype(BF16), v) + _dot_bt(q_in[:, cs], st.astype(BF16))
            st_ref[h] = st * dec[:, cs] + _dot_at(v, k_out[:, cs])
            ms = jnp.mean(o * o, axis=-1, keepdims=True)
            on = o * lax.rsqrt(ms + EPS) * gw_ref[...]
            o_ref[rows, cs] = (on * g_ref[rows, cs].astype(F32)).astype(o_ref.dtype)
        return carry

    lax.fori_loop(0, n_chunks, chunk_body, 0)

    @pl.when(step == pl.num_programs(1) - 1)
    def _():
        sfin_ref[0] = st_ref[...]


def _hgrn(proj, lf, kk, g_norm_w, s0, consts, bsz, seq, rows):
    mall, masks, sels = consts
    steps = seq // rows
    w = HEADS * HEAD_DIM
    assert w == COL_TILE
    row_map = lambda col: (lambda b, s: (b * steps + s, col))
    const2 = lambda b, s: (0, 0)
    const3 = lambda b, s: (0, 0, 0)
    kern = functools.partial(_hgrn_kernel, chunk=CHUNK, n_chunks=rows // CHUNK)
    return pl.pallas_call(
        kern,
        grid=(bsz, steps),
        in_specs=[
            pl.BlockSpec((rows, w), row_map(0)),
            pl.BlockSpec((rows, w), row_map(1)),
            pl.BlockSpec((rows, w), row_map(2)),
            pl.BlockSpec((rows, w), row_map(0)),
            pl.BlockSpec((rows, w), row_map(0)),
            pl.BlockSpec((1, HEAD_DIM), const2),
            pl.BlockSpec((HEADS, HEAD_DIM, HEAD_DIM), const3),
            pl.BlockSpec(mall.shape, const2),
            pl.BlockSpec(masks.shape, const3),
            pl.BlockSpec(sels.shape, const3),
        ],
        out_specs=[
            pl.BlockSpec((rows, w), row_map(0)),
            pl.BlockSpec((1, HEADS, HEAD_DIM, HEAD_DIM), lambda b, s: (b, 0, 0, 0)),
        ],
        out_shape=[
            jax.ShapeDtypeStruct((bsz * seq, w), BF16),
            jax.ShapeDtypeStruct((bsz, HEADS, HEAD_DIM, HEAD_DIM), F32),
        ],
        scratch_shapes=[pltpu.VMEM((HEADS, HEAD_DIM, HEAD_DIM), F32)],
        compiler_params=_params("arbitrary", "arbitrary"),
        name="hgrn2",
    )(proj, proj, proj, lf, kk, g_norm_w.reshape(1, HEAD_DIM), s0, mall, masks, sels)


def _mixer_out_kernel(og_ref, scv_ref, scb_ref, scc_ref, ga_ref, gb_ref, x_ref,
                      pv_ref, pc_ref, mv_ref, mc_ref,
                      wa_ref, wb_ref, wo_ref, cw_ref, nw_ref, wr_ref, br_ref,
                      h1_ref, xn_ref, idx_ref, gate_ref, hbuf, *, tiles_per_seq, n_tiles):
    i = pl.program_id(0)
    tm = x_ref.shape[0]
    tile = jnp.minimum(i, n_tiles - 1)
    first = (tile % tiles_per_seq) == 0

    @pl.when(i == 0)
    def _():
        hbuf[1] = jnp.zeros(hbuf.shape[1:], hbuf.dtype)

    hp = hbuf[(i + 1) % 2]
    ms = jnp.mean(hp * hp, axis=-1, keepdims=True)
    xn = hp * lax.rsqrt(ms + EPS) * nw_ref[...]
    xn_ref[...] = xn.astype(BF16).reshape(xn_ref.shape)

    xh = xn.astype(BF16)
    xl = (xn - xh.astype(F32)).astype(BF16)
    wr = wr_ref[...]
    wh = wr.astype(BF16)
    wl = (wr - wh.astype(F32)).astype(BF16)
    logits = _dot_bt(wh, xh) + _dot_bt(wh, xl) + _dot_bt(wl, xh) + br_ref[...]
    ne = logits.shape[0]
    ie = lax.broadcasted_iota(jnp.int32, logits.shape, 0)
    tops, idxs = [], []
    for _ in range(TOP_K):
        mx = jnp.max(logits, axis=0, keepdims=True)
        ix = jnp.min(jnp.where(logits == mx, ie, ne), axis=0, keepdims=True)
        tops.append(mx)
        idxs.append(ix)
        logits = jnp.where(ie == ix, -jnp.inf, logits)
    es = [jnp.exp(t - tops[0]) for t in tops]
    den = es[0]
    for e in es[1:]:
        den = den + e
    gate_ref[...] = jnp.concatenate([e / den for e in es], axis=0)
    idx_ref[...] = jnp.concatenate(idxs, axis=0)

    u = scc_ref[...].astype(F32) * scv_ref[...].astype(F32)
    halo_prev = pc_ref[...].astype(F32) * pv_ref[...].astype(F32)
    halo_meta = mc_ref[...].astype(F32) * mv_ref[...].astype(F32)
    halo = jnp.where(first, halo_meta, halo_prev)
    hr = halo.shape[0]
    r = lax.broadcasted_iota(jnp.int32, (tm, 1), 0)
    u1 = jnp.where(r == 0, halo[hr - 1:hr], pltpu.roll(u, 1, 0))
    u2 = jnp.where(r == 0, halo[hr - 2:hr - 1], jnp.where(r == 1, halo[hr - 1:hr], pltpu.roll(u, 2, 0)))
    conv = cw_ref[2:3] * u + cw_ref[1:2] * u1 + cw_ref[0:1] * u2
    yb_in = (scb_ref[...].astype(F32) * conv).astype(BF16)

    y_a = _dot(og_ref[...], wa_ref[...])
    y_b = _dot(yb_in, wb_ref[...])
    merged = (ga_ref[...].astype(F32) * y_a + gb_ref[...].astype(F32) * y_b).astype(BF16)
    h1 = x_ref[...] + _dot(merged, wo_ref[...])
    h1_ref[...] = h1
    hbuf[i % 2] = h1


def _mixer_out(og, proj, proj_meta, x2d, wa, wb, wo, conv_w, norm_w, w_router, b_router, seq, tm):
    m, d = x2d.shape
    w = COL_TILE
    halo = proj_meta.shape[0]
    assert tm % halo == 0 and seq % tm == 0 and d == 2 * w
    ne = w_router.shape[1]
    per_halo = tm // halo
    n_tiles = m // tm
    cur = lambda i: jnp.minimum(i, n_tiles - 1)
    done = lambda i: jnp.maximum(i - 1, 0)
    row = lambda col: (lambda i: (cur(i), col))
    prev = lambda col: (lambda i: (jnp.maximum(cur(i) * per_halo - 1, 0), col))
    const = lambda i: (0, 0)
    whole = lambda a: pl.BlockSpec(a.shape, const)
    wr_t = w_router.T
    kern = functools.partial(_mixer_out_kernel, tiles_per_seq=seq // tm, n_tiles=n_tiles)
    return pl.pallas_call(
        kern,
        grid=(n_tiles + 1,),
        in_specs=[
            pl.BlockSpec((tm, w), row(0)),
            pl.BlockSpec((tm, w), row(3)),
            pl.BlockSpec((tm, w), row(4)),
            pl.BlockSpec((tm, w), row(5)),
            pl.BlockSpec((tm, d), row(3)),
            pl.BlockSpec((tm, d), row(4)),
            pl.BlockSpec((tm, d), row(0)),
            pl.BlockSpec((halo, w), prev(3)),
            pl.BlockSpec((halo, w), prev(5)),
            pl.BlockSpec((halo, w), lambda i: (0, 3)),
            pl.BlockSpec((halo, w), lambda i: (0, 5)),
            whole(wa), whole(wb), whole(wo),
            pl.BlockSpec(conv_w.shape, const),
            pl.BlockSpec((1, d), const),
            pl.BlockSpec((ne, d), const),
            pl.BlockSpec((ne, 1), const),
        ],
        out_specs=[
            pl.BlockSpec((tm, d), row(0)),
            pl.BlockSpec((tm, d // LANES, LANES), lambda i: (done(i), 0, 0)),
            pl.BlockSpec((TOP_K, tm), lambda i: (0, done(i))),
            pl.BlockSpec((TOP_K, tm), lambda i: (0, done(i))),
        ],
        out_shape=[
            jax.ShapeDtypeStruct((m, d), F32),
            jax.ShapeDtypeStruct((m, d // LANES, LANES), BF16),
            jax.ShapeDtypeStruct((TOP_K, m), jnp.int32),
            jax.ShapeDtypeStruct((TOP_K, m), F32),
        ],
        scratch_shapes=[pltpu.VMEM((2, tm, d), F32)],
        compiler_params=_params("arbitrary"),
        name="mixer_out",
    )(og, proj, proj, proj, proj, proj, x2d, proj, proj, proj_meta, proj_meta,
      wa, wb, wo, conv_w, norm_w.reshape(1, d), wr_t, b_router.reshape(ne, 1))


def _dispatch_kernel(cnt_ref, pst_ref, used_ref, pos_ref, x_ref, o_hbm, xbuf, zbuf, sem, zsem):
    i = pl.program_id(0)
    n_steps = pl.num_programs(0)
    tt = x_ref.shape[0]
    n_sub = o_hbm.shape[0] // EXPERT_SUB
    bits = [1 << b for b in reversed(range(EXPERT_SUB.bit_length() - 1))]
    slot = i % 2

    xbuf[slot] = x_ref[...]

    def issue(r, carry):
        for k in range(TOP_K):
            p = pos_ref[0, 0, k * tt + r]
            pltpu.make_async_copy(xbuf.at[slot, pl.ds(r, 1)], o_hbm.at[pl.ds(p, 1)],
                                  sem.at[slot]).start(priority=k % 2)
        return carry

    lax.fori_loop(0, tt, issue, 0, unroll=4)

    def tokens_done(s):
        return [pltpu.make_async_copy(xbuf.at[s], o_hbm.at[pl.ds(0, tt)], sem.at[s]) for _ in range(TOP_K)]

    def zero_copies(fn):
        for e in range(cnt_ref.shape[0]):
            npad = (-cnt_ref[e]) & (EXPERT_SUB - 1)
            base = pst_ref[e] + cnt_ref[e]
            for bit in bits:
                @pl.when((npad & bit) != 0)
                def _():
                    row = base + (npad & ~(2 * bit - 1))
                    fn(pltpu.make_async_copy(zbuf.at[pl.ds(0, bit)], o_hbm.at[pl.ds(row, bit)], zsem))
        for j in range(cnt_ref.shape[0]):
            blk = used_ref[0] + j

            @pl.when(blk < n_sub)
            def _():
                row = pl.multiple_of(blk * EXPERT_SUB, EXPERT_SUB)
                fn(pltpu.make_async_copy(zbuf, o_hbm.at[pl.ds(row, EXPERT_SUB)], zsem))

    @pl.when(i == 0)
    def _():
        zbuf[...] = jnp.zeros_like(zbuf)
        zero_copies(lambda c: c.start())

    @pl.when(i > 0)
    def _():
        for c in tokens_done(1 - slot):
            c.wait()

    @pl.when(i == n_steps - 1)
    def _():
        for c in tokens_done(slot):
            c.wait()
        zero_copies(lambda c: c.wait())


def _dispatch(xn3, pos_t, counts, pad_start, used_sub, n_rows):
    m, sub, lanes = xn3.shape
    nt = pos_t.shape[0]
    tt = m // nt
    grid_spec = pltpu.PrefetchScalarGridSpec(
        num_scalar_prefetch=3,
        grid=(nt,),
        in_specs=[
            pl.BlockSpec((1, 1, pos_t.shape[2]), lambda i, c, p, u: (i, 0, 0), memory_space=pltpu.SMEM),
            pl.BlockSpec((tt, sub, lanes), lambda i, c, p, u: (i, 0, 0)),
        ],
        out_specs=pl.BlockSpec(memory_space=pl.ANY),
        scratch_shapes=[pltpu.VMEM((2, tt, sub, lanes), xn3.dtype), pltpu.VMEM((EXPERT_SUB, sub, lanes), xn3.dtype),
                        pltpu.SemaphoreType.DMA((2,)), pltpu.SemaphoreType.DMA(())],
    )
    return pl.pallas_call(
        _dispatch_kernel,
        grid_spec=grid_spec,
        out_shape=jax.ShapeDtypeStruct((n_rows, sub, lanes), xn3.dtype),
        compiler_params=_params("arbitrary"),
        name="dispatch",
    )(counts, pad_start, used_sub.reshape(1), pos_t, xn3)


def _row_chunks(nsub, chunk_fn):
    per = EXPERT_CHUNK_SUBS
    big = per * EXPERT_SUB
    n_big = lax.div(nsub, per)

    def body(c, carry):
        chunk_fn(c * per, pl.multiple_of(c * big, big), big)
        return carry

    lax.fori_loop(0, n_big, body, 0)
    rem = nsub - n_big * per
    bit = per // 2
    while bit >= 1:
        done = rem & ~(2 * bit - 1)

        @pl.when((rem & bit) != 0)
        def _():
            sub0 = n_big * per + done
            chunk_fn(sub0, pl.multiple_of(sub0 * EXPERT_SUB, EXPERT_SUB), bit * EXPERT_SUB)

        bit //= 2


def _sub_pieces(n, fn):
    bit = 1 << (EXPERT_SUBS.bit_length() - 1)
    while bit >= 1:
        @pl.when((n & bit) != 0)
        def _():
            fn(n & ~(2 * bit - 1), bit)

        bit //= 2


def _ffn_up_kernel(st_ref, ns_ref, zf_ref, se_ref, jm_ref, x_hbm, wg_ref, wu_ref, bg_ref, bu_ref, h_hbm,
                   xstage, xbuf, hbuf, wgb_ref, wub_ref, sem_x, sem_h):
    s = pl.program_id(0)
    j = pl.program_id(1)
    n_s = pl.num_programs(0)
    nj = pl.num_programs(1)
    step = s * nj + j
    nsub = ns_ref[s]
    real = jnp.logical_and(nsub > 0, zf_ref[s] == 0)
    hs = step % 2

    def x_copy(sb, first, count):
        return pltpu.make_async_copy(x_hbm.at[pl.ds(st_ref[sb] + first, count)],
                                     xstage.at[pl.ds(first, count)], sem_x)

    def h_copy(sb, jj, first, count, slot):
        row = pl.multiple_of((st_ref[sb] + first) * EXPERT_SUB, EXPERT_SUB)
        return pltpu.make_async_copy(
            hbuf.at[slot, pl.ds(pl.multiple_of(first * EXPERT_SUB, EXPERT_SUB), count * EXPERT_SUB)],
            h_hbm.at[jj, pl.ds(row, count * EXPERT_SUB)], sem_h.at[slot])

    def for_x_subs(sb, fn):
        _sub_pieces(ns_ref[sb] * (1 - zf_ref[sb]), fn)

    def for_h_subs(sb, fn):
        _sub_pieces(ns_ref[sb], fn)

    @pl.when(step == 0)
    def _():
        for_x_subs(0, lambda first, count: x_copy(0, first, count).start())

    @pl.when(j == 0)
    def _():
        for_x_subs(s, lambda first, count: x_copy(s, first, count).wait())

    @pl.when(step >= 2)
    def _():
        sp = lax.div(step - 2, nj)
        jp = step - 2 - sp * nj
        for_h_subs(sp, lambda first, count: h_copy(sp, jp, first, count, hs).wait())

    @pl.when(real)
    def _():
        wgb_ref[...] = wg_ref[0].astype(BF16)
        wub_ref[...] = wu_ref[0].astype(BF16)

        def make_chunk(first_tile):
            def chunk(sub0, row0, nrows):
                if first_tile:
                    xs = jnp.concatenate([xstage[sub0 + t].reshape(EXPERT_SUB, xbuf.shape[1])
                                          for t in range(nrows // EXPERT_SUB)], axis=0)
                    xbuf[pl.ds(row0, nrows), :] = xs
                else:
                    xs = xbuf[pl.ds(row0, nrows), :]
                g = _dot(xs, wgb_ref[...]) + bg_ref[0]
                u = _dot(xs, wub_ref[...]) + bu_ref[0]
                g = jnp.minimum(g, SWIGLU_LIMIT)
                u = jnp.clip(u, -SWIGLU_LIMIT, SWIGLU_LIMIT)
                hbuf[hs, pl.ds(row0, nrows), :] = ((u + 1.0) * (g * jax.nn.sigmoid(SWIGLU_ALPHA * g))).astype(BF16)

            return chunk

        @pl.when(j == 0)
        def _():
            _row_chunks(nsub, make_chunk(True))

        @pl.when(j > 0)
        def _():
            _row_chunks(nsub, make_chunk(False))

    @pl.when(jnp.logical_and(j == 0, s + 1 < n_s))
    def _():
        nxt = jnp.minimum(s + 1, n_s - 1)
        for_x_subs(nxt, lambda first, count: x_copy(nxt, first, count).start())

    @pl.when(zf_ref[s] == 1)
    def _():
        hbuf[hs] = jnp.zeros(hbuf.shape[1:], hbuf.dtype)

    for_h_subs(s, lambda first, count: h_copy(s, j, first, count, hs).start())

    @pl.when(step == n_s * nj - 1)
    def _():
        sp = lax.div(step - 1, nj)
        jp = step - 1 - sp * nj
        for_h_subs(sp, lambda first, count: h_copy(sp, jp, first, count, 1 - hs).wait())
        for_h_subs(s, lambda first, count: h_copy(s, j, first, count, hs).wait())


def _ffn_down_kernel(st_ref, ns_ref, zf_ref, se_ref, cm_ref, h_hbm, wd_ref, bd_ref, y_hbm,
                     hb, yrow, ytile, wdb_ref, sem_h, sem_y):
    s = pl.program_id(0)
    c = pl.program_id(1)
    n_s = pl.num_programs(0)
    nc = pl.num_programs(1)
    step = s * nc + c
    njh = hb.shape[1]
    tn = wdb_ref.shape[1]
    nsub = ns_ref[s]
    real = jnp.logical_and(nsub > 0, zf_ref[s] == 0)

    def h_copy(sb, jj, first, count, slot):
        row = pl.multiple_of((st_ref[sb] + first) * EXPERT_SUB, EXPERT_SUB)
        return pltpu.make_async_copy(
            h_hbm.at[jj, pl.ds(row, count * EXPERT_SUB)],
            hb.at[slot, jj, pl.ds(pl.multiple_of(first * EXPERT_SUB, EXPERT_SUB), count * EXPERT_SUB)],
            sem_h.at[slot])

    def y_copy(sb, first, count):
        return pltpu.make_async_copy(ytile.at[pl.ds(first, count)],
                                     y_hbm.at[pl.ds(st_ref[sb] + first, count)], sem_y)

    def for_h_subs(sb, fn):
        def all_tiles(first, count):
            for jj in range(njh):
                fn(jj, first, count)

        _sub_pieces(ns_ref[sb] * (1 - zf_ref[sb]), all_tiles)

    def for_y_subs(sb, fn):
        _sub_pieces(ns_ref[sb], fn)

    @pl.when(step == 0)
    def _():
        for_h_subs(0, lambda jj, first, count: h_copy(0, jj, first, count, 0).start())

    @pl.when(jnp.logical_and(c == 0, s + 1 < n_s))
    def _():
        nxt = jnp.minimum(s + 1, n_s - 1)
        for_h_subs(nxt, lambda jj, first, count: h_copy(nxt, jj, first, count, (s + 1) % 2).start())

    @pl.when(c == 0)
    def _():
        for_h_subs(s, lambda jj, first, count: h_copy(s, jj, first, count, s % 2).wait())

    @pl.when(jnp.logical_and(c == nc - 1, s > 0))
    def _():
        sp = jnp.maximum(s - 1, 0)
        for_y_subs(sp, lambda first, count: y_copy(sp, first, count).wait())

    @pl.when(real)
    def _():
        wdb_ref[...] = wd_ref[0].astype(BF16)
        h_slot = s % 2
        n_tiles = yrow.shape[1] // tn
        for cc in range(n_tiles):
            @pl.when(c == cc)
            def _():
                def chunk(sub0, row0, nrows):
                    hid = jnp.concatenate([hb[h_slot, jj, pl.ds(row0, nrows), :] for jj in range(njh)], axis=1)
                    y = (_dot(hid, wdb_ref[...]) + bd_ref[0]).astype(yrow.dtype)
                    if cc < n_tiles - 1:
                        yrow[pl.ds(row0, nrows), cc * tn:(cc + 1) * tn] = y
                    else:
                        full = jnp.concatenate([yrow[pl.ds(row0, nrows), 0:cc * tn], y], axis=1)
                        for t in range(nrows // EXPERT_SUB):
                            ytile[sub0 + t] = full[t * EXPERT_SUB:(t + 1) * EXPERT_SUB].reshape(ytile.shape[1:])

                _row_chunks(nsub, chunk)

    @pl.when(c == nc - 1)
    def _():
        @pl.when(zf_ref[s] == 1)
        def _():
            ytile[...] = jnp.zeros(ytile.shape, ytile.dtype)

        for_y_subs(s, lambda first, count: y_copy(s, first, count).start())

        @pl.when(s == n_s - 1)
        def _():
            for_y_subs(s, lambda first, count: y_copy(s, first, count).wait())


def _experts(x_rows, w_up, b_up, w_down, b_down, sb_start, sb_nsub, sb_zero, sb_expert):
    n_rows, x_sub, x_lanes = x_rows.shape
    d = x_sub * x_lanes
    ne, _, ff2 = w_up.shape
    ff = ff2 // 2
    tf = EXPERT_FF_TILE
    tn = EXPERT_OUT_TILE
    nj = ff // tf
    nc = d // tn
    n_sb = jnp.sum((sb_nsub > 0).astype(jnp.int32))
    rows = EXPERT_SUBS * EXPERT_SUB
    n_sub = n_rows // EXPERT_SUB
    sub_shape = (n_sub, EXPERT_SUB, x_sub, x_lanes)
    x_rows = x_rows.reshape(sub_shape)
    sb_start = sb_start // EXPERT_SUB
    is_real = jnp.logical_and(sb_nsub > 0, sb_zero == 0)[:, None]
    jm = jnp.where(is_real, jnp.arange(nj, dtype=jnp.int32)[None, :], nj - 1).astype(jnp.int32)
    cm = jnp.where(is_real, jnp.arange(nc, dtype=jnp.int32)[None, :], nc - 1).astype(jnp.int32)
    any_spec = pl.BlockSpec(memory_space=pl.ANY)

    up_spec = pltpu.PrefetchScalarGridSpec(
        num_scalar_prefetch=5,
        grid=(n_sb, nj),
        in_specs=[
            any_spec,
            pl.BlockSpec((1, d, tf), lambda s, j, st, ns, zf, se, jm: (se[s], 0, jm[s, j])),
            pl.BlockSpec((1, d, tf), lambda s, j, st, ns, zf, se, jm: (se[s], 0, nj + jm[s, j])),
            pl.BlockSpec((1, 1, tf), lambda s, j, st, ns, zf, se, jm: (se[s], 0, jm[s, j])),
            pl.BlockSpec((1, 1, tf), lambda s, j, st, ns, zf, se, jm: (se[s], 0, nj + jm[s, j])),
        ],
        out_specs=any_spec,
        scratch_shapes=[
            pltpu.VMEM((EXPERT_SUBS, EXPERT_SUB, x_sub, x_lanes), BF16),
            pltpu.VMEM((rows, d), BF16),
            pltpu.VMEM((2, rows, tf), BF16),
            pltpu.VMEM((d, tf), BF16),
            pltpu.VMEM((d, tf), BF16),
            pltpu.SemaphoreType.DMA(()),
            pltpu.SemaphoreType.DMA((2,)),
        ],
    )
    hidden = pl.pallas_call(
        _ffn_up_kernel,
        grid_spec=up_spec,
        out_shape=jax.ShapeDtypeStruct((nj, n_rows, tf), BF16),
        compiler_params=_params("arbitrary", "arbitrary"),
        name="ffn_up",
    )(sb_start, sb_nsub, sb_zero, sb_expert, jm, x_rows, w_up, w_up,
      b_up.reshape(ne, 1, ff2), b_up.reshape(ne, 1, ff2))

    down_spec = pltpu.PrefetchScalarGridSpec(
        num_scalar_prefetch=5,
        grid=(n_sb, nc),
        in_specs=[
            any_spec,
            pl.BlockSpec((1, ff, tn), lambda s, c, st, ns, zf, se, cm: (se[s], 0, cm[s, c])),
            pl.BlockSpec((1, 1, tn), lambda s, c, st, ns, zf, se, cm: (se[s], 0, cm[s, c])),
        ],
        out_specs=any_spec,
        scratch_shapes=[
            pltpu.VMEM((2, nj, rows, tf), BF16),
            pltpu.VMEM((rows, d), BF16),
            pltpu.VMEM((EXPERT_SUBS, EXPERT_SUB, x_sub, x_lanes), BF16),
            pltpu.VMEM((ff, tn), BF16),
            pltpu.SemaphoreType.DMA((2,)),
            pltpu.SemaphoreType.DMA(()),
        ],
    )
    y_rows = pl.pallas_call(
        _ffn_down_kernel,
        grid_spec=down_spec,
        out_shape=jax.ShapeDtypeStruct(sub_shape, BF16),
        compiler_params=_params("arbitrary", "arbitrary"),
        name="ffn_down",
    )(sb_start, sb_nsub, sb_zero, sb_expert, cm, hidden, w_down, b_down.reshape(ne, 1, d))
    return y_rows.reshape(n_rows, x_sub, x_lanes)


def _combine_kernel(pos_ref, pos_next_ref, gate_ref, h1_ref, fw_ref, y_hbm, o_ref, buf_ref, sem):
    i = pl.program_id(0)
    tt = h1_ref.shape[0]

    def start_rows(pos, slot):
        def issue(r, carry):
            for k in range(TOP_K):
                p = pos[0, 0, k * tt + r]
                pltpu.make_async_copy(y_hbm.at[pl.ds(p, 1)], buf_ref.at[slot, k, pl.ds(r, 1)],
                                      sem.at[slot]).start(priority=k % 2)
            return carry

        lax.fori_loop(0, tt, issue, 0, unroll=4)

    @pl.when(i == 0)
    def _():
        start_rows(pos_ref, 0)

    @pl.when(i + 1 < pl.num_programs(0))
    def _():
        start_rows(pos_next_ref, (i + 1) % 2)

    slot = i % 2
    for k in range(TOP_K):
        pltpu.make_async_copy(y_hbm.at[pl.ds(0, tt)], buf_ref.at[slot, k], sem.at[slot]).wait()

    gate = gate_ref[...]
    gpad = jnp.concatenate([gate, jnp.zeros((tt - TOP_K, tt), F32)], axis=0)
    gcol = gpad.T
    acc = h1_ref[...]
    for k in range(TOP_K):
        acc = acc + gcol[:, k:k + 1] * buf_ref[slot, k].reshape(acc.shape).astype(F32)
    ms = jnp.mean(acc * acc, axis=-1, keepdims=True)
    o_ref[...] = acc * lax.rsqrt(ms + EPS) * fw_ref[...]


def _pos_tiles(pos, m):
    tt = COMBINE_ROWS
    nt = m // tt
    return pos.reshape(TOP_K, nt, tt).transpose(1, 0, 2).reshape(nt, 1, TOP_K * tt)


def _combine(y_rows, pos_t, gate, h1, final_w):
    m, d = h1.shape
    tt = COMBINE_ROWS
    nt = m // tt
    return pl.pallas_call(
        _combine_kernel,
        grid=(nt,),
        in_specs=[
            pl.BlockSpec((1, 1, TOP_K * tt), lambda i: (i, 0, 0), memory_space=pltpu.SMEM),
            pl.BlockSpec((1, 1, TOP_K * tt), lambda i: (jnp.minimum(i + 1, nt - 1), 0, 0), memory_space=pltpu.SMEM),
            pl.BlockSpec((TOP_K, tt), lambda i: (0, i)),
            pl.BlockSpec((tt, d), lambda i: (i, 0)),
            pl.BlockSpec((1, d), lambda i: (0, 0)),
            pl.BlockSpec(memory_space=pl.ANY),
        ],
        out_specs=pl.BlockSpec((tt, d), lambda i: (i, 0)),
        out_shape=jax.ShapeDtypeStruct((m, d), F32),
        scratch_shapes=[pltpu.VMEM((2, TOP_K, tt) + y_rows.shape[1:], y_rows.dtype), pltpu.SemaphoreType.DMA((2,))],
        compiler_params=_params("arbitrary"),
        name="combine",
    )(pos_t, pos_t, gate, h1, final_w.reshape(1, d), y_rows)


def _rank_kernel(idx_ref, tri_ref, rank_ref, cnt_ref, carry_ref):
    i = pl.program_id(0)
    tt = idx_ref.shape[1]
    w = tri_ref.shape[0]

    @pl.when(i == 0)
    def _():
        carry_ref[...] = jnp.zeros_like(carry_ref)

    carry = carry_ref[:, 0:1]
    expert = lax.broadcasted_iota(jnp.int32, (N_EXPERTS, w), 0)
    for k in range(TOP_K):
        for c in range(tt // w):
            cols = slice(c * w, (c + 1) * w)
            hit = expert == idx_ref[k:k + 1, cols]
            prefix = _dot(hit.astype(BF16), tri_ref[...])
            before = jnp.where(hit, carry + prefix - 1.0, 0.0)
            rank_ref[k:k + 1, cols] = jnp.sum(before, axis=0, keepdims=True).astype(jnp.int32)
            carry = carry + prefix[:, w - 1:w]
    carry_ref[...] = jnp.broadcast_to(carry, carry_ref.shape)
    cnt_ref[...] = jnp.broadcast_to(carry, cnt_ref.shape)


def _ranks(idx):
    m = idx.shape[1]
    tt = RANK_ROWS
    w = RANK_LANES
    tri = jnp.asarray(np.triu(np.ones((w, w), np.float32)), BF16)
    rank, cnt = pl.pallas_call(
        _rank_kernel,
        grid=(m // tt,),
        in_specs=[pl.BlockSpec((TOP_K, tt), lambda i: (0, i)), pl.BlockSpec((w, w), lambda i: (0, 0))],
        out_specs=[pl.BlockSpec((TOP_K, tt), lambda i: (0, i)), pl.BlockSpec((N_EXPERTS, LANES), lambda i: (0, 0))],
        out_shape=[jax.ShapeDtypeStruct((TOP_K, m), jnp.int32), jax.ShapeDtypeStruct((N_EXPERTS, LANES), F32)],
        scratch_shapes=[pltpu.VMEM((N_EXPERTS, LANES), F32)],
        compiler_params=_params("arbitrary"),
        name="ranks",
    )(idx, tri)
    return rank, cnt[:, 0].astype(jnp.int32)


def _routing_tables(idx, m):
    i32 = jnp.int32
    n_assign = TOP_K * m
    rank, counts = _ranks(idx)
    padded = (counts + EXPERT_SUB - 1) // EXPERT_SUB * EXPERT_SUB
    pad_end = jnp.cumsum(padded)
    pad_start = pad_end - padded
    pos = (pad_start[idx] + rank).astype(i32).reshape(n_assign)
    n_sub = -(-(n_assign + N_EXPERTS * (EXPERT_SUB - 1)) // EXPERT_SUB)
    n_rows = n_sub * EXPERT_SUB
    used_sub = pad_end[-1] // EXPERT_SUB

    rows = EXPERT_SUBS * EXPERT_SUB
    n_sb = -(-n_sub // EXPERT_SUBS) + N_EXPERTS + 1
    nsb_e = (padded + rows - 1) // rows
    sb_cum = jnp.cumsum(nsb_e)
    total_real = sb_cum[-1]
    s = jnp.arange(n_sb, dtype=i32)
    e_s = jnp.minimum(jnp.sum((sb_cum[None, :] <= s[:, None]).astype(i32), axis=1), N_EXPERTS - 1)
    local = s - (sb_cum[e_s] - nsb_e[e_s])
    real = s < total_real
    start_real = pad_start[e_s] + local * rows
    nsub_real = jnp.clip((padded[e_s] - local * rows) // EXPERT_SUB, 0, EXPERT_SUBS)
    fill_idx = s - total_real
    start_fill = pad_end[-1] + fill_idx * rows
    nsub_fill = jnp.clip(n_sub - used_sub - fill_idx * EXPERT_SUBS, 0, EXPERT_SUBS)
    is_fill = jnp.logical_and(jnp.logical_not(real), nsub_fill > 0)
    sb_start = jnp.where(real, start_real, jnp.where(is_fill, start_fill, 0)).astype(i32)
    sb_nsub = jnp.where(real, nsub_real, jnp.where(is_fill, nsub_fill, 0)).astype(i32)
    sb_zero = is_fill.astype(i32)
    last_e = e_s[jnp.maximum(total_real - 1, 0)]
    sb_expert = jnp.where(real, e_s, last_e).astype(i32)
    return pos, counts.astype(i32), pad_start.astype(i32), used_sub.astype(i32), n_rows, sb_start, sb_nsub, sb_zero, sb_expert


def _main_tiles(d):
    w = HEADS * HEAD_DIM
    sc = d // 2
    sizes = (w, w, w, w, sc, sc, sc, d, d)
    acts = (1, None, 0, 1, 0, 0, 0, 2, 2)
    order = (0, 2, 3, 4, 5, 6, 7, 8)
    starts = np.concatenate([[0], np.cumsum(sizes)])
    cols, codes = [], []
    for seg in order:
        assert sizes[seg] % COL_TILE == 0 and starts[seg] % COL_TILE == 0
        for t in range(sizes[seg] // COL_TILE):
            cols.append(int(starts[seg]) // COL_TILE + t)
            codes.append(acts[seg])
    return cols, codes, int(starts[1]) // COL_TILE


def _layer(x2d, meta, bsz, seq, norm_mix_w, w_in, lb_logits, g_norm_w, w_hgrn_out, conv_w, w_conv_out, w_o,
           norm_ffn_w, w_router, b_router, w_up, b_up, w_down, b_down, final_norm_w):
    m, d = x2d.shape
    cols, codes, fcol = _main_tiles(d)
    consts = _hgrn_constants(CHUNK)

    xn, lf, kk, xn_meta, lf_meta, kk_meta = _fgate(x2d, meta, norm_mix_w, w_in, lb_logits, fcol, NORM_ROWS)
    proj, proj_meta = _inproj(xn, xn_meta, w_in, cols, codes, INPROJ_ROWS)

    pad = CHUNK - N_META
    front = lambda a: jnp.pad(a, ((pad, 0), (0, 0)))
    s_zero = jnp.zeros((HEADS, HEAD_DIM, HEAD_DIM), F32)
    _, s_meta = _hgrn(front(proj_meta), front(lf_meta), front(kk_meta), g_norm_w, s_zero, consts, 1, CHUNK, CHUNK)
    og, _ = _hgrn(proj, lf, kk, g_norm_w, s_meta[0], consts, bsz, seq, HGRN_ROWS)
    h1, xn_ffn, idx, gate = _mixer_out(og, proj, proj_meta, x2d, w_hgrn_out.astype(BF16), w_conv_out.astype(BF16),
                                   w_o.astype(BF16), conv_w, norm_ffn_w, w_router, b_router, seq, MIXER_ROWS)

    pos, counts, pad_start, used_sub, n_rows, sb_start, sb_nsub, sb_zero, sb_expert = _routing_tables(idx, m)
    pos_t = _pos_tiles(pos, m)
    x_rows = _dispatch(xn_ffn, pos_t, counts, pad_start, used_sub, n_rows)
    y_rows = _experts(x_rows, w_up, b_up, w_down, b_down, sb_start, sb_nsub, sb_zero, sb_expert)
    return _combine(y_rows, pos_t, gate, h1, final_norm_w)


def kernel(x, meta_tokens, norm_mix_w, w_in, lb_logits, g_norm_w, w_hgrn_out, conv_w, w_conv_out, w_o, norm_ffn_w,
           w_router, b_router, w_up, b_up, w_down, b_down, final_norm_w):
    bsz, seq, d = x.shape
    assert norm_mix_w.shape[0] == 1, "single-layer block"
    out = _layer(x.reshape(bsz * seq, d), meta_tokens.astype(x.dtype), bsz, seq, norm_mix_w[0], w_in[0], lb_logits,
                 g_norm_w[0], w_hgrn_out[0], conv_w[0], w_conv_out[0], w_o[0], norm_ffn_w[0], w_router[0],
                 b_router[0], w_up[0], b_up[0], w_down[0], b_down[0], final_norm_w)
    return out.reshape(bsz, seq, d)
```

```python
import functools

import numpy as np
import jax
import jax.numpy as jnp
from jax import lax
from jax.experimental import pallas as pl
from jax.experimental.pallas import tpu as pltpu

F32 = jnp.float32
BF16 = jnp.bfloat16

LANES = 128
N_META = 16
HEADS = 8
HEAD_DIM = 128
N_EXPERTS = 32
TOP_K = 4
SWIGLU_LIMIT = 7.0
SWIGLU_ALPHA = 1.702
EPS = 1e-6

CHUNK = 256
HGRN_BASE = 64
HGRN_ROWS = 512
NORM_ROWS = 512
INPROJ_ROWS = 1024
MIXER_ROWS = 256
COL_TILE = 1024
EXPERT_SUB = 128
EXPERT_SUBS = 18
EXPERT_CHUNK_SUBS = 8
EXPERT_FF_TILE = 512
EXPERT_OUT_TILE = 512
COMBINE_ROWS = 256
RANK_ROWS = 1024
RANK_LANES = 256

V7X_VMEM_BYTES = 64 * 1024 * 1024
V7X_VMEM_LIMIT = V7X_VMEM_BYTES - 8 * 1024 * 1024


def _dot(a, b):
    return jnp.dot(a, b, preferred_element_type=F32)


def _dot_bt(a, b):
    return lax.dot_general(a, b, (((1,), (1,)), ((), ())), preferred_element_type=F32)


def _dot_at(a, b):
    return lax.dot_general(a, b, (((0,), (0,)), ((), ())), preferred_element_type=F32)


def _split3(x):
    hi = x.astype(BF16)
    r1 = x - hi.astype(F32)
    mid = r1.astype(BF16)
    lo = (r1 - mid.astype(F32)).astype(BF16)
    return hi, mid, lo


def _params(*sem):
    return pltpu.CompilerParams(dimension_semantics=sem, vmem_limit_bytes=V7X_VMEM_LIMIT)


def _activate(z, code):
    s = jax.nn.sigmoid(z)
    return jnp.where(code == 0, z, jnp.where(code == 1, z * s, s))


def _inproj_kernel(col_ref, code_ref, x_ref, xm_ref, w_ref, o_ref, om_ref, wb_ref):
    code = code_ref[pl.program_id(0)]

    @pl.when(pl.program_id(1) == 0)
    def _():
        wb_ref[...] = w_ref[...].astype(BF16)
        om_ref[...] = _activate(_dot(xm_ref[...], wb_ref[...]), code).astype(om_ref.dtype)

    o_ref[...] = _activate(_dot(x_ref[...], wb_ref[...]), code).astype(o_ref.dtype)


def _inproj(xn, xn_meta, w_in, cols, codes, tm):
    m, d = xn.shape
    mm = xn_meta.shape[0]
    nt = len(cols)
    grid_spec = pltpu.PrefetchScalarGridSpec(
        num_scalar_prefetch=2,
        grid=(nt, m // tm),
        in_specs=[
            pl.BlockSpec((tm, d), lambda n, i, col, code: (i, 0)),
            pl.BlockSpec((mm, d), lambda n, i, col, code: (0, 0)),
            pl.BlockSpec((d, COL_TILE), lambda n, i, col, code: (0, col[n])),
        ],
        out_specs=[
            pl.BlockSpec((tm, COL_TILE), lambda n, i, col, code: (i, n)),
            pl.BlockSpec((mm, COL_TILE), lambda n, i, col, code: (0, n)),
        ],
        scratch_shapes=[pltpu.VMEM((d, COL_TILE), BF16)],
    )
    return pl.pallas_call(
        _inproj_kernel,
        grid_spec=grid_spec,
        out_shape=[jax.ShapeDtypeStruct((m, nt * COL_TILE), BF16), jax.ShapeDtypeStruct((mm, nt * COL_TILE), BF16)],
        compiler_params=_params("arbitrary", "arbitrary"),
        name="inproj",
    )(jnp.asarray(cols, jnp.int32), jnp.asarray(codes, jnp.int32), xn, xn_meta, w_in)


def _fgate_kernel(x_ref, xm_ref, nw_ref, w_ref, lbl_ref, xn_ref, lf_ref, kk_ref, xnm_ref, lfm_ref, kkm_ref, wb_ref):
    lbl = lbl_ref[...]
    e = jnp.exp(lbl - jnp.max(lbl, axis=0, keepdims=True))
    lb = e[0:1] / jnp.sum(e, axis=0, keepdims=True)

    def gate(x, xn_out, lf_out, kk_out):
        ms = jnp.mean(x * x, axis=-1, keepdims=True)
        xn = (x * lax.rsqrt(ms + EPS) * nw_ref[...]).astype(BF16)
        xn_out[...] = xn
        z = _dot(xn, wb_ref[...])
        lf_out[...] = jnp.log(lb + (1.0 - lb) * jax.nn.sigmoid(z))
        kk_out[...] = (1.0 - lb) * jax.nn.sigmoid(-z)

    @pl.when(pl.program_id(0) == 0)
    def _():
        wb_ref[...] = w_ref[...].astype(BF16)
        gate(xm_ref[...], xnm_ref, lfm_ref, kkm_ref)

    gate(x_ref[...], xn_ref, lf_ref, kk_ref)


def _fgate(x, x_meta, norm_w, w_in, lb_logits, col, tm):
    m, d = x.shape
    mm = x_meta.shape[0]
    r = lb_logits.shape[0]
    out = jax.ShapeDtypeStruct((m, COL_TILE), F32)
    out_meta = jax.ShapeDtypeStruct((mm, COL_TILE), F32)
    return pl.pallas_call(
        _fgate_kernel,
        grid=(m // tm,),
        in_specs=[
            pl.BlockSpec((tm, d), lambda i: (i, 0)),
            pl.BlockSpec((mm, d), lambda i: (0, 0)),
            pl.BlockSpec((1, d), lambda i: (0, 0)),
            pl.BlockSpec((d, COL_TILE), lambda i: (0, col)),
            pl.BlockSpec((r, COL_TILE), lambda i: (0, 0)),
        ],
        out_specs=[pl.BlockSpec((tm, d), lambda i: (i, 0))] + [pl.BlockSpec((tm, COL_TILE), lambda i: (i, 0))] * 2
        + [pl.BlockSpec((mm, d), lambda i: (0, 0))] + [pl.BlockSpec((mm, COL_TILE), lambda i: (0, 0))] * 2,
        out_shape=[jax.ShapeDtypeStruct((m, d), BF16), out, out,
                   jax.ShapeDtypeStruct((mm, d), BF16), out_meta, out_meta],
        scratch_shapes=[pltpu.VMEM((d, COL_TILE), BF16)],
        compiler_params=_params("arbitrary"),
        name="norm_fgate",
    )(x, x_meta, norm_w.reshape(1, d), w_in, lb_logits)


def _hgrn_constants(c):
    base = HGRN_BASE
    nlow = int(np.log2(base))
    nl = int(np.log2(c))
    assert (1 << nl) == c and (1 << nlow) == base and c >= base
    rr = np.arange(base)[:, None]
    uu = np.arange(base)[None, :]
    mats = [uu <= rr, uu > rr]
    sels = []
    for lvl in range(nlow):
        b = 1 << lvl
        start = (rr // (2 * b)) * (2 * b)
        mid = start + b - 1
        second = (rr - start) >= b
        mats.append(np.where(second, (uu > mid) & (uu <= rr), (uu > rr) & (uu <= mid)))
        sels.append(np.broadcast_to(second, (base, HEADS * HEAD_DIM)))
    tt = np.arange(c)[:, None]
    ss = np.arange(c)[None, :]
    masks = []
    for lvl in range(nl):
        b = 1 << lvl
        masks.append(((tt // (2 * b)) == (ss // (2 * b))) & ((tt % (2 * b)) >= b) & ((ss % (2 * b)) < b))
    masks.append(np.eye(c, dtype=bool))
    m1 = np.concatenate(mats, 0).astype(np.float32)
    mall = jnp.asarray(np.concatenate([m1, m1, m1], axis=1), BF16)
    return mall, jnp.asarray(np.stack(masks).astype(np.float32)), jnp.asarray(np.stack(sels).astype(np.float32))


def _hgrn_kernel(q_ref, v_ref, g_ref, lf_ref, kk_ref, gw_ref, s0_ref, mall_ref, mask_ref, sel_ref,
                 o_ref, sfin_ref, st_ref, *, chunk, n_chunks):
    base = HGRN_BASE
    nb = chunk // base
    nlow = sel_ref.shape[0]
    nl = mask_ref.shape[0] - 1
    step = pl.program_id(1)

    @pl.when(step == 0)
    def _():
        st_ref[...] = s0_ref[...]

    def chunk_body(ci, carry):
        r0 = pl.multiple_of(ci * chunk, chunk)
        rows = pl.ds(r0, chunk)
        qb = q_ref[rows, :]
        q = qb.astype(F32)
        k = kk_ref[rows, :]
        blk = lambda a, i: a[i * base:(i + 1) * base]

        pre, suf, e_low = [], [], []
        for i in range(nb):
            hi, mid, lo = _split3(lf_ref[pl.ds(r0 + i * base, base), :])
            args = _dot(mall_ref[...], jnp.concatenate([hi, mid, lo], axis=0))
            pre.append(args[0:base])
            suf.append(args[base:2 * base])
            e_low.append(jnp.exp(args[2 * base:]))
        tot = [p[base - 1:base] for p in pre]

        def span(lo_blk, hi_blk):
            acc = None
            for j in range(lo_blk, hi_blk):
                acc = tot[j] if acc is None else acc + tot[j]
            return acc

        def shifted(a, off):
            return a if off is None else a + off

        xs = []
        for lvl in range(nlow):
            parts = [jnp.where(sel_ref[lvl] > 0.5, blk(q, i), blk(k, i)) * blk(e_low[i], lvl) for i in range(nb)]
            xs.append(jnp.concatenate(parts, axis=0).astype(BF16))
        for lvl in range(nlow, nl):
            half = (1 << lvl) // base
            parts = []
            for i in range(nb):
                g = i % (2 * half)
                if g >= half:
                    parts.append(blk(q, i) * jnp.exp(shifted(pre[i], span(i - (g - half), i))))
                else:
                    parts.append(blk(k, i) * jnp.exp(shifted(suf[i], span(i + 1, i - g + half))))
            xs.append(jnp.concatenate(parts, axis=0).astype(BF16))
        q_in = jnp.concatenate([blk(q, i) * jnp.exp(shifted(pre[i], span(0, i))) for i in range(nb)],
                               axis=0).astype(BF16)
        k_out = jnp.concatenate([blk(k, i) * jnp.exp(shifted(suf[i], span(i + 1, nb))) for i in range(nb)],
                                axis=0).astype(BF16)
        dec = jnp.exp(span(0, nb))
        kb = k.astype(BF16)

        for h in range(HEADS):
            cs = slice(h * HEAD_DIM, (h + 1) * HEAD_DIM)
            scores = mask_ref[nl] * _dot_bt(qb[:, cs], kb[:, cs])
            for lvl in range(nl):
                x = xs[lvl][:, cs]
                scores = scores + mask_ref[lvl] * _dot_bt(x, x)
            v = v_ref[rows, cs]
            st = st_ref[h]
            o = _dot(scores.astype(BF16), v) + _dot_bt(q_in[:, cs], st.astype(BF16))
            st_ref[h] = st * dec[:, cs] + _dot_at(v, k_out[:, cs])
            ms = jnp.mean(o * o, axis=-1, keepdims=True)
            on = o * lax.rsqrt(ms + EPS) * gw_ref[...]
            o_ref[rows, cs] = (on * g_ref[rows, cs].astype(F32)).astype(o_ref.dtype)
        return carry

    lax.fori_loop(0, n_chunks, chunk_body, 0)

    @pl.when(step == pl.num_programs(1) - 1)
    def _():
        sfin_ref[0] = st_ref[...]


def _hgrn(proj, lf, kk, g_norm_w, s0, consts, bsz, seq, rows):
    mall, masks, sels = consts
    steps = seq // rows
    w = HEADS * HEAD_DIM
    assert w == COL_TILE
    row_map = lambda col: (lambda b, s: (b * steps + s, col))
    const2 = lambda b, s: (0, 0)
    const3 = lambda b, s: (0, 0, 0)
    kern = functools.partial(_hgrn_kernel, chunk=CHUNK, n_chunks=rows // CHUNK)
    return pl.pallas_call(
        kern,
        grid=(bsz, steps),
        in_specs=[
            pl.BlockSpec((rows, w), row_map(0)),
            pl.BlockSpec((rows, w), row_map(1)),
            pl.BlockSpec((rows, w), row_map(2)),
            pl.BlockSpec((rows, w), row_map(0)),
            pl.BlockSpec((rows, w), row_map(0)),
            pl.BlockSpec((1, HEAD_DIM), const2),
            pl.BlockSpec((HEADS, HEAD_DIM, HEAD_DIM), const3),
            pl.BlockSpec(mall.shape, const2),
            pl.BlockSpec(masks.shape, const3),
            pl.BlockSpec(sels.shape, const3),
        ],
        out_specs=[
            pl.BlockSpec((rows, w), row_map(0)),
            pl.BlockSpec((1, HEADS, HEAD_DIM, HEAD_DIM), lambda b, s: (b, 0, 0, 0)),
        ],
        out_shape=[
            jax.ShapeDtypeStruct((bsz * seq, w), BF16),
            jax.ShapeDtypeStruct((bsz, HEADS, HEAD_DIM, HEAD_DIM), F32),
        ],
        scratch_shapes=[pltpu.VMEM((HEADS, HEAD_DIM, HEAD_DIM), F32)],
        compiler_params=_params("arbitrary", "arbitrary"),
        name="hgrn2",
    )(proj, proj, proj, lf, kk, g_norm_w.reshape(1, HEAD_DIM), s0, mall, masks, sels)


def _mixer_out_kernel(og_ref, scv_ref, scb_ref, scc_ref, ga_ref, gb_ref, x_ref,
                      pv_ref, pc_ref, mv_ref, mc_ref,
                      wa_ref, wb_ref, wo_ref, cw_ref, nw_ref, wr_ref, br_ref,
                      h1_ref, xn_ref, idx_ref, gate_ref, hbuf, *, tiles_per_seq, n_tiles):
    i = pl.program_id(0)
    tm = x_ref.shape[0]
    tile = jnp.minimum(i, n_tiles - 1)
    first = (tile % tiles_per_seq) == 0

    @pl.when(i == 0)
    def _():
        hbuf[1] = jnp.zeros(hbuf.shape[1:], hbuf.dtype)

    hp = hbuf[(i + 1) % 2]
    ms = jnp.mean(hp * hp, axis=-1, keepdims=True)
    xn = hp * lax.rsqrt(ms + EPS) * nw_ref[...]
    xn_ref[...] = xn.astype(BF16).reshape(xn_ref.shape)

    xh = xn.astype(BF16)
    xl = (xn - xh.astype(F32)).astype(BF16)
    wr = wr_ref[...]
    wh = wr.astype(BF16)
    wl = (wr - wh.astype(F32)).astype(BF16)
    logits = _dot_bt(wh, xh) + _dot_bt(wh, xl) + _dot_bt(wl, xh) + br_ref[...]
    ne = logits.shape[0]
    ie = lax.broadcasted_iota(jnp.int32, logits.shape, 0)
    tops, idxs = [], []
    for _ in range(TOP_K):
        mx = jnp.max(logits, axis=0, keepdims=True)
        ix = jnp.min(jnp.where(logits == mx, ie, ne), axis=0, keepdims=True)
        tops.append(mx)
        idxs.append(ix)
        logits = jnp.where(ie == ix, -jnp.inf, logits)
    es = [jnp.exp(t - tops[0]) for t in tops]
    den = es[0]
    for e in es[1:]:
        den = den + e
    gate_ref[...] = jnp.concatenate([e / den for e in es], axis=0)
    idx_ref[...] = jnp.concatenate(idxs, axis=0)

    u = scc_ref[...].astype(F32) * scv_ref[...].astype(F32)
    halo_prev = pc_ref[...].astype(F32) * pv_ref[...].astype(F32)
    halo_meta = mc_ref[...].astype(F32) * mv_ref[...].astype(F32)
    halo = jnp.where(first, halo_meta, halo_prev)
    hr = halo.shape[0]
    r = lax.broadcasted_iota(jnp.int32, (tm, 1), 0)
    u1 = jnp.where(r == 0, halo[hr - 1:hr], pltpu.roll(u, 1, 0))
    u2 = jnp.where(r == 0, halo[hr - 2:hr - 1], jnp.where(r == 1, halo[hr - 1:hr], pltpu.roll(u, 2, 0)))
    conv = cw_ref[2:3] * u + cw_ref[1:2] * u1 + cw_ref[0:1] * u2
    yb_in = (scb_ref[...].astype(F32) * conv).astype(BF16)

    y_a = _dot(og_ref[...], wa_ref[...])
    y_b = _dot(yb_in, wb_ref[...])
    merged = (ga_ref[...].astype(F32) * y_a + gb_ref[...].astype(F32) * y_b).astype(BF16)
    h1 = x_ref[...] + _dot(merged, wo_ref[...])
    h1_ref[...] = h1
    hbuf[i % 2] = h1


def _mixer_out(og, proj, proj_meta, x2d, wa, wb, wo, conv_w, norm_w, w_router, b_router, seq, tm):
    m, d = x2d.shape
    w = COL_TILE
    halo = proj_meta.shape[0]
    assert tm % halo == 0 and seq % tm == 0 and d == 2 * w
    ne = w_router.shape[1]
    per_halo = tm // halo
    n_tiles = m // tm
    cur = lambda i: jnp.minimum(i, n_tiles - 1)
    done = lambda i: jnp.maximum(i - 1, 0)
    row = lambda col: (lambda i: (cur(i), col))
    prev = lambda col: (lambda i: (jnp.maximum(cur(i) * per_halo - 1, 0), col))
    const = lambda i: (0, 0)
    whole = lambda a: pl.BlockSpec(a.shape, const)
    wr_t = w_router.T
    kern = functools.partial(_mixer_out_kernel, tiles_per_seq=seq // tm, n_tiles=n_tiles)
    return pl.pallas_call(
        kern,
        grid=(n_tiles + 1,),
        in_specs=[
            pl.BlockSpec((tm, w), row(0)),
            pl.BlockSpec((tm, w), row(3)),
            pl.BlockSpec((tm, w), row(4)),
            pl.BlockSpec((tm, w), row(5)),
            pl.BlockSpec((tm, d), row(3)),
            pl.BlockSpec((tm, d), row(4)),
            pl.BlockSpec((tm, d), row(0)),
            pl.BlockSpec((halo, w), prev(3)),
            pl.BlockSpec((halo, w), prev(5)),
            pl.BlockSpec((halo, w), lambda i: (0, 3)),
            pl.BlockSpec((halo, w), lambda i: (0, 5)),
            whole(wa), whole(wb), whole(wo),
            pl.BlockSpec(conv_w.shape, const),
            pl.BlockSpec((1, d), const),
            pl.BlockSpec((ne, d), const),
            pl.BlockSpec((ne, 1), const),
        ],
        out_specs=[
            pl.BlockSpec((tm, d), row(0)),
            pl.BlockSpec((tm, d // LANES, LANES), lambda i: (done(i), 0, 0)),
            pl.BlockSpec((TOP_K, tm), lambda i: (0, done(i))),
            pl.BlockSpec((TOP_K, tm), lambda i: (0, done(i))),
        ],
        out_shape=[
            jax.ShapeDtypeStruct((m, d), F32),
            jax.ShapeDtypeStruct((m, d // LANES, LANES), BF16),
            jax.ShapeDtypeStruct((TOP_K, m), jnp.int32),
            jax.ShapeDtypeStruct((TOP_K, m), F32),
        ],
        scratch_shapes=[pltpu.VMEM((2, tm, d), F32)],
        compiler_params=_params("arbitrary"),
        name="mixer_out",
    )(og, proj, proj, proj, proj, proj, x2d, proj, proj, proj_meta, proj_meta,
      wa, wb, wo, conv_w, norm_w.reshape(1, d), wr_t, b_router.reshape(ne, 1))


def _dispatch_kernel(cnt_ref, pst_ref, used_ref, pos_ref, x_ref, o_hbm, xbuf, zbuf, sem, zsem):
    i = pl.program_id(0)
    n_steps = pl.num_programs(0)
    tt = x_ref.shape[0]
    n_sub = o_hbm.shape[0] // EXPERT_SUB
    bits = [1 << b for b in reversed(range(EXPERT_SUB.bit_length() - 1))]
    slot = i % 2

    xbuf[slot] = x_ref[...]

    def issue(r, carry):
        for k in range(TOP_K):
            p = pos_ref[0, 0, k * tt + r]
            pltpu.make_async_copy(xbuf.at[slot, pl.ds(r, 1)], o_hbm.at[pl.ds(p, 1)],
                                  sem.at[slot]).start(priority=k % 2)
        return carry

    lax.fori_loop(0, tt, issue, 0, unroll=4)

    def tokens_done(s):
        return [pltpu.make_async_copy(xbuf.at[s], o_hbm.at[pl.ds(0, tt)], sem.at[s]) for _ in range(TOP_K)]

    def zero_copies(fn):
        for e in range(cnt_ref.shape[0]):
            npad = (-cnt_ref[e]) & (EXPERT_SUB - 1)
            base = pst_ref[e] + cnt_ref[e]
            for bit in bits:
                @pl.when((npad & bit) != 0)
                def _():
                    row = base + (npad & ~(2 * bit - 1))
                    fn(pltpu.make_async_copy(zbuf.at[pl.ds(0, bit)], o_hbm.at[pl.ds(row, bit)], zsem))
        for j in range(cnt_ref.shape[0]):
            blk = used_ref[0] + j

            @pl.when(blk < n_sub)
            def _():
                row = pl.multiple_of(blk * EXPERT_SUB, EXPERT_SUB)
                fn(pltpu.make_async_copy(zbuf, o_hbm.at[pl.ds(row, EXPERT_SUB)], zsem))

    @pl.when(i == 0)
    def _():
        zbuf[...] = jnp.zeros_like(zbuf)
        zero_copies(lambda c: c.start())

    @pl.when(i > 0)
    def _():
        for c in tokens_done(1 - slot):
            c.wait()

    @pl.when(i == n_steps - 1)
    def _():
        for c in tokens_done(slot):
            c.wait()
        zero_copies(lambda c: c.wait())


def _dispatch(xn3, pos_t, counts, pad_start, used_sub, n_rows):
    m, sub, lanes = xn3.shape
    nt = pos_t.shape[0]
    tt = m // nt
    grid_spec = pltpu.PrefetchScalarGridSpec(
        num_scalar_prefetch=3,
        grid=(nt,),
        in_specs=[
            pl.BlockSpec((1, 1, pos_t.shape[2]), lambda i, c, p, u: (i, 0, 0), memory_space=pltpu.SMEM),
            pl.BlockSpec((tt, sub, lanes), lambda i, c, p, u: (i, 0, 0)),
        ],
        out_specs=pl.BlockSpec(memory_space=pl.ANY),
        scratch_shapes=[pltpu.VMEM((2, tt, sub, lanes), xn3.dtype), pltpu.VMEM((EXPERT_SUB, sub, lanes), xn3.dtype),
                        pltpu.SemaphoreType.DMA((2,)), pltpu.SemaphoreType.DMA(())],
    )
    return pl.pallas_call(
        _dispatch_kernel,
        grid_spec=grid_spec,
        out_shape=jax.ShapeDtypeStruct((n_rows, sub, lanes), xn3.dtype),
        compiler_params=_params("arbitrary"),
        name="dispatch",
    )(counts, pad_start, used_sub.reshape(1), pos_t, xn3)


def _row_chunks(nsub, chunk_fn):
    per = EXPERT_CHUNK_SUBS
    big = per * EXPERT_SUB
    n_big = lax.div(nsub, per)

    def body(c, carry):
        chunk_fn(c * per, pl.multiple_of(c * big, big), big)
        return carry

    lax.fori_loop(0, n_big, body, 0)
    rem = nsub - n_big * per
    bit = per // 2
    while bit >= 1:
        done = rem & ~(2 * bit - 1)

        @pl.when((rem & bit) != 0)
        def _():
            sub0 = n_big * per + done
            chunk_fn(sub0, pl.multiple_of(sub0 * EXPERT_SUB, EXPERT_SUB), bit * EXPERT_SUB)

        bit //= 2


def _sub_pieces(n, fn):
    bit = 1 << (EXPERT_SUBS.bit_length() - 1)
    while bit >= 1:
        @pl.when((n & bit) != 0)
        def _():
            fn(n & ~(2 * bit - 1), bit)

        bit //= 2


def _ffn_up_kernel(st_ref, ns_ref, zf_ref, se_ref, jm_ref, x_hbm, wg_ref, wu_ref, bg_ref, bu_ref, h_hbm,
                   xstage, xbuf, hbuf, wgb_ref, wub_ref, sem_x, sem_h):
    s = pl.program_id(0)
    j = pl.program_id(1)
    n_s = pl.num_programs(0)
    nj = pl.num_programs(1)
    step = s * nj + j
    nsub = ns_ref[s]
    real = jnp.logical_and(nsub > 0, zf_ref[s] == 0)
    hs = step % 2

    def x_copy(sb, first, count):
        return pltpu.make_async_copy(x_hbm.at[pl.ds(st_ref[sb] + first, count)],
                                     xstage.at[pl.ds(first, count)], sem_x)

    def h_copy(sb, jj, first, count, slot):
        row = pl.multiple_of((st_ref[sb] + first) * EXPERT_SUB, EXPERT_SUB)
        return pltpu.make_async_copy(
            hbuf.at[slot, pl.ds(pl.multiple_of(first * EXPERT_SUB, EXPERT_SUB), count * EXPERT_SUB)],
            h_hbm.at[jj, pl.ds(row, count * EXPERT_SUB)], sem_h.at[slot])

    def for_x_subs(sb, fn):
        _sub_pieces(ns_ref[sb] * (1 - zf_ref[sb]), fn)

    def for_h_subs(sb, fn):
        _sub_pieces(ns_ref[sb], fn)

    @pl.when(step == 0)
    def _():
        for_x_subs(0, lambda first, count: x_copy(0, first, count).start())

    @pl.when(j == 0)
    def _():
        for_x_subs(s, lambda first, count: x_copy(s, first, count).wait())

    @pl.when(step >= 2)
    def _():
        sp = lax.div(step - 2, nj)
        jp = step - 2 - sp * nj
        for_h_subs(sp, lambda first, count: h_copy(sp, jp, first, count, hs).wait())

    @pl.when(real)
    def _():
        wgb_ref[...] = wg_ref[0].astype(BF16)
        wub_ref[...] = wu_ref[0].astype(BF16)

        def make_chunk(first_tile):
            def chunk(sub0, row0, nrows):
                if first_tile:
                    xs = jnp.concatenate([xstage[sub0 + t].reshape(EXPERT_SUB, xbuf.shape[1])
                                          for t in range(nrows // EXPERT_SUB)], axis=0)
                    xbuf[pl.ds(row0, nrows), :] = xs
                else:
                    xs = xbuf[pl.ds(row0, nrows), :]
                g = _dot(xs, wgb_ref[...]) + bg_ref[0]
                u = _dot(xs, wub_ref[...]) + bu_ref[0]
                g = jnp.minimum(g, SWIGLU_LIMIT)
                u = jnp.clip(u, -SWIGLU_LIMIT, SWIGLU_LIMIT)
                hbuf[hs, pl.ds(row0, nrows), :] = ((u + 1.0) * (g * jax.nn.sigmoid(SWIGLU_ALPHA * g))).astype(BF16)

            return chunk

        @pl.when(j == 0)
        def _():
            _row_chunks(nsub, make_chunk(True))

        @pl.when(j > 0)
        def _():
            _row_chunks(nsub, make_chunk(False))

    @pl.when(jnp.logical_and(j == 0, s + 1 < n_s))
    def _():
        nxt = jnp.minimum(s + 1, n_s - 1)
        for_x_subs(nxt, lambda first, count: x_copy(nxt, first, count).start())

    @pl.when(zf_ref[s] == 1)
    def _():
        hbuf[hs] = jnp.zeros(hbuf.shape[1:], hbuf.dtype)

    for_h_subs(s, lambda first, count: h_copy(s, j, first, count, hs).start())

    @pl.when(step == n_s * nj - 1)
    def _():
        sp = lax.div(step - 1, nj)
        jp = step - 1 - sp * nj
        for_h_subs(sp, lambda first, count: h_copy(sp, jp, first, count, 1 - hs).wait())
        for_h_subs(s, lambda first, count: h_copy(s, j, first, count, hs).wait())


def _ffn_down_kernel(st_ref, ns_ref, zf_ref, se_ref, cm_ref, h_hbm, wd_ref, bd_ref, y_hbm,
                     hb, yrow, ytile, wdb_ref, sem_h, sem_y):
    s = pl.program_id(0)
    c = pl.program_id(1)
    n_s = pl.num_programs(0)
    nc = pl.num_programs(1)
    step = s * nc + c
    njh = hb.shape[1]
    tn = wdb_ref.shape[1]
    nsub = ns_ref[s]
    real = jnp.logical_and(nsub > 0, zf_ref[s] == 0)

    def h_copy(sb, jj, first, count, slot):
        row = pl.multiple_of((st_ref[sb] + first) * EXPERT_SUB, EXPERT_SUB)
        return pltpu.make_async_copy(
            h_hbm.at[jj, pl.ds(row, count * EXPERT_SUB)],
            hb.at[slot, jj, pl.ds(pl.multiple_of(first * EXPERT_SUB, EXPERT_SUB), count * EXPERT_SUB)],
            sem_h.at[slot])

    def y_copy(sb, first, count):
        return pltpu.make_async_copy(ytile.at[pl.ds(first, count)],
                                     y_hbm.at[pl.ds(st_ref[sb] + first, count)], sem_y)

    def for_h_subs(sb, fn):
        def all_tiles(first, count):
            for jj in range(njh):
                fn(jj, first, count)

        _sub_pieces(ns_ref[sb] * (1 - zf_ref[sb]), all_tiles)

    def for_y_subs(sb, fn):
        _sub_pieces(ns_ref[sb], fn)

    @pl.when(step == 0)
    def _():
        for_h_subs(0, lambda jj, first, count: h_copy(0, jj, first, count, 0).start())

    @pl.when(jnp.logical_and(c == 0, s + 1 < n_s))
    def _():
        nxt = jnp.minimum(s + 1, n_s - 1)
        for_h_subs(nxt, lambda jj, first, count: h_copy(nxt, jj, first, count, (s + 1) % 2).start())

    @pl.when(c == 0)
    def _():
        for_h_subs(s, lambda jj, first, count: h_copy(s, jj, first, count, s % 2).wait())

    @pl.when(jnp.logical_and(c == nc - 1, s > 0))
    def _():
        sp = jnp.maximum(s - 1, 0)
        for_y_subs(sp, lambda first, count: y_copy(sp, first, count).wait())

    @pl.when(real)
    def _():
        wdb_ref[...] = wd_ref[0].astype(BF16)
        h_slot = s % 2
        n_tiles = yrow.shape[1] // tn
        for cc in range(n_tiles):
            @pl.when(c == cc)
            def _():
                def chunk(sub0, row0, nrows):
                    hid = jnp.concatenate([hb[h_slot, jj, pl.ds(row0, nrows), :] for jj in range(njh)], axis=1)
                    y = (_dot(hid, wdb_ref[...]) + bd_ref[0]).astype(yrow.dtype)
                    if cc < n_tiles - 1:
                        yrow[pl.ds(row0, nrows), cc * tn:(cc + 1) * tn] = y
                    else:
                        full = jnp.concatenate([yrow[pl.ds(row0, nrows), 0:cc * tn], y], axis=1)
                        for t in range(nrows // EXPERT_SUB):
                            ytile[sub0 + t] = full[t * EXPERT_SUB:(t + 1) * EXPERT_SUB].reshape(ytile.shape[1:])

                _row_chunks(nsub, chunk)

    @pl.when(c == nc - 1)
    def _():
        @pl.when(zf_ref[s] == 1)
        def _():
            ytile[...] = jnp.zeros(ytile.shape, ytile.dtype)

        for_y_subs(s, lambda first, count: y_copy(s, first, count).start())

        @pl.when(s == n_s - 1)
        def _():
            for_y_subs(s, lambda first, count: y_copy(s, first, count).wait())


def _experts(x_rows, w_up, b_up, w_down, b_down, sb_start, sb_nsub, sb_zero, sb_expert):
    n_rows, x_sub, x_lanes = x_rows.shape
    d = x_sub * x_lanes
    ne, _, ff2 = w_up.shape
    ff = ff2 // 2
    tf = EXPERT_FF_TILE
    tn = EXPERT_OUT_TILE
    nj = ff // tf
    nc = d // tn
    n_sb = jnp.sum((sb_nsub > 0).astype(jnp.int32))
    rows = EXPERT_SUBS * EXPERT_SUB
    n_sub = n_rows // EXPERT_SUB
    sub_shape = (n_sub, EXPERT_SUB, x_sub, x_lanes)
    x_rows = x_rows.reshape(sub_shape)
    sb_start = sb_start // EXPERT_SUB
    is_real = jnp.logical_and(sb_nsub > 0, sb_zero == 0)[:, None]
    jm = jnp.where(is_real, jnp.arange(nj, dtype=jnp.int32)[None, :], nj - 1).astype(jnp.int32)
    cm = jnp.where(is_real, jnp.arange(nc, dtype=jnp.int32)[None, :], nc - 1).astype(jnp.int32)
    any_spec = pl.BlockSpec(memory_space=pl.ANY)

    up_spec = pltpu.PrefetchScalarGridSpec(
        num_scalar_prefetch=5,
        grid=(n_sb, nj),
        in_specs=[
            any_spec,
            pl.BlockSpec((1, d, tf), lambda s, j, st, ns, zf, se, jm: (se[s], 0, jm[s, j])),
            pl.BlockSpec((1, d, tf), lambda s, j, st, ns, zf, se, jm: (se[s], 0, nj + jm[s, j])),
            pl.BlockSpec((1, 1, tf), lambda s, j, st, ns, zf, se, jm: (se[s], 0, jm[s, j])),
            pl.BlockSpec((1, 1, tf), lambda s, j, st, ns, zf, se, jm: (se[s], 0, nj + jm[s, j])),
        ],
        out_specs=any_spec,
        scratch_shapes=[
            pltpu.VMEM((EXPERT_SUBS, EXPERT_SUB, x_sub, x_lanes), BF16),
            pltpu.VMEM((rows, d), BF16),
            pltpu.VMEM((2, rows, tf), BF16),
            pltpu.VMEM((d, tf), BF16),
            pltpu.VMEM((d, tf), BF16),
            pltpu.SemaphoreType.DMA(()),
            pltpu.SemaphoreType.DMA((2,)),
        ],
    )
    hidden = pl.pallas_call(
        _ffn_up_kernel,
        grid_spec=up_spec,
        out_shape=jax.ShapeDtypeStruct((nj, n_rows, tf), BF16),
        compiler_params=_params("arbitrary", "arbitrary"),
        name="ffn_up",
    )(sb_start, sb_nsub, sb_zero, sb_expert, jm, x_rows, w_up, w_up,
      b_up.reshape(ne, 1, ff2), b_up.reshape(ne, 1, ff2))

    down_spec = pltpu.PrefetchScalarGridSpec(
        num_scalar_prefetch=5,
        grid=(n_sb, nc),
        in_specs=[
            any_spec,
            pl.BlockSpec((1, ff, tn), lambda s, c, st, ns, zf, se, cm: (se[s], 0, cm[s, c])),
            pl.BlockSpec((1, 1, tn), lambda s, c, st, ns, zf, se, cm: (se[s], 0, cm[s, c])),
        ],
        out_specs=any_spec,
        scratch_shapes=[
            pltpu.VMEM((2, nj, rows, tf), BF16),
            pltpu.VMEM((rows, d), BF16),
            pltpu.VMEM((EXPERT_SUBS, EXPERT_SUB, x_sub, x_lanes), BF16),
            pltpu.VMEM((ff, tn), BF16),
            pltpu.SemaphoreType.DMA((2,)),
            pltpu.SemaphoreType.DMA(()),
        ],
    )
    y_rows = pl.pallas_call(
        _ffn_down_kernel,
        grid_spec=down_spec,
        out_shape=jax.ShapeDtypeStruct(sub_shape, BF16),
        compiler_params=_params("arbitrary", "arbitrary"),
        name="ffn_down",
    )(sb_start, sb_nsub, sb_zero, sb_expert, cm, hidden, w_down, b_down.reshape(ne, 1, d))
    return y_rows.reshape(n_rows, x_sub, x_lanes)


def _combine_kernel(pos_ref, pos_next_ref, gate_ref, h1_ref, fw_ref, y_hbm, o_ref, buf_ref, sem):
    i = pl.program_id(0)
    tt = h1_ref.shape[0]

    def start_rows(pos, slot):
        def issue(r, carry):
            for k in range(TOP_K):
                p = pos[0, 0, k * tt + r]
                pltpu.make_async_copy(y_hbm.at[pl.ds(p, 1)], buf_ref.at[slot, k, pl.ds(r, 1)],
                                      sem.at[slot]).start(priority=k % 2)
            return carry

        lax.fori_loop(0, tt, issue, 0, unroll=4)

    @pl.when(i == 0)
    def _():
        start_rows(pos_ref, 0)

    @pl.when(i + 1 < pl.num_programs(0))
    def _():
        start_rows(pos_next_ref, (i + 1) % 2)

    slot = i % 2
    for k in range(TOP_K):
        pltpu.make_async_copy(y_hbm.at[pl.ds(0, tt)], buf_ref.at[slot, k], sem.at[slot]).wait()

    gate = gate_ref[...]
    gpad = jnp.concatenate([gate, jnp.zeros((tt - TOP_K, tt), F32)], axis=0)
    gcol = gpad.T
    acc = h1_ref[...]
    for k in range(TOP_K):
        acc = acc + gcol[:, k:k + 1] * buf_ref[slot, k].reshape(acc.shape).astype(F32)
    ms = jnp.mean(acc * acc, axis=-1, keepdims=True)
    o_ref[...] = acc * lax.rsqrt(ms + EPS) * fw_ref[...]


def _pos_tiles(pos, m):
    tt = COMBINE_ROWS
    nt = m // tt
    return pos.reshape(TOP_K, nt, tt).transpose(1, 0, 2).reshape(nt, 1, TOP_K * tt)


def _combine(y_rows, pos_t, gate, h1, final_w):
    m, d = h1.shape
    tt = COMBINE_ROWS
    nt = m // tt
    return pl.pallas_call(
        _combine_kernel,
        grid=(nt,),
        in_specs=[
            pl.BlockSpec((1, 1, TOP_K * tt), lambda i: (i, 0, 0), memory_space=pltpu.SMEM),
            pl.BlockSpec((1, 1, TOP_K * tt), lambda i: (jnp.minimum(i + 1, nt - 1), 0, 0), memory_space=pltpu.SMEM),
            pl.BlockSpec((TOP_K, tt), lambda i: (0, i)),
            pl.BlockSpec((tt, d), lambda i: (i, 0)),
            pl.BlockSpec((1, d), lambda i: (0, 0)),
            pl.BlockSpec(memory_space=pl.ANY),
        ],
        out_specs=pl.BlockSpec((tt, d), lambda i: (i, 0)),
        out_shape=jax.ShapeDtypeStruct((m, d), F32),
        scratch_shapes=[pltpu.VMEM((2, TOP_K, tt) + y_rows.shape[1:], y_rows.dtype), pltpu.SemaphoreType.DMA((2,))],
        compiler_params=_params("arbitrary"),
        name="combine",
    )(pos_t, pos_t, gate, h1, final_w.reshape(1, d), y_rows)


def _rank_kernel(idx_ref, tri_ref, rank_ref, cnt_ref, carry_ref):
    i = pl.program_id(0)
    tt = idx_ref.shape[1]
    w = tri_ref.shape[0]

    @pl.when(i == 0)
    def _():
        carry_ref[...] = jnp.zeros_like(carry_ref)

    carry = carry_ref[:, 0:1]
    expert = lax.broadcasted_iota(jnp.int32, (N_EXPERTS, w), 0)
    for k in range(TOP_K):
        for c in range(tt // w):
            cols = slice(c * w, (c + 1) * w)
            hit = expert == idx_ref[k:k + 1, cols]
            prefix = _dot(hit.astype(BF16), tri_ref[...])
            before = jnp.where(hit, carry + prefix - 1.0, 0.0)
            rank_ref[k:k + 1, cols] = jnp.sum(before, axis=0, keepdims=True).astype(jnp.int32)
            carry = carry + prefix[:, w - 1:w]
    carry_ref[...] = jnp.broadcast_to(carry, carry_ref.shape)
    cnt_ref[...] = jnp.broadcast_to(carry, cnt_ref.shape)


def _ranks(idx):
    m = idx.shape[1]
    tt = RANK_ROWS
    w = RANK_LANES
    tri = jnp.asarray(np.triu(np.ones((w, w), np.float32)), BF16)
    rank, cnt = pl.pallas_call(
        _rank_kernel,
        grid=(m // tt,),
        in_specs=[pl.BlockSpec((TOP_K, tt), lambda i: (0, i)), pl.BlockSpec((w, w), lambda i: (0, 0))],
        out_specs=[pl.BlockSpec((TOP_K, tt), lambda i: (0, i)), pl.BlockSpec((N_EXPERTS, LANES), lambda i: (0, 0))],
        out_shape=[jax.ShapeDtypeStruct((TOP_K, m), jnp.int32), jax.ShapeDtypeStruct((N_EXPERTS, LANES), F32)],
        scratch_shapes=[pltpu.VMEM((N_EXPERTS, LANES), F32)],
        compiler_params=_params("arbitrary"),
        name="ranks",
    )(idx, tri)
    return rank, cnt[:, 0].astype(jnp.int32)


def _position_kernel(idx_ref, rank_ref, start_ref, pos_ref):
    start = start_ref[:, 0:1]
    expert = lax.broadcasted_iota(jnp.int32, (N_EXPERTS, idx_ref.shape[1]), 0)
    for k in range(TOP_K):
        hit = expert == idx_ref[k:k + 1, :]
        base = jnp.sum(jnp.where(hit, start, 0.0), axis=0, keepdims=True)
        pos_ref[k:k + 1, :] = base.astype(jnp.int32) + rank_ref[k:k + 1, :]


def _positions(idx, rank, group_start):
    m = idx.shape[1]
    tt = RANK_ROWS
    table = jnp.broadcast_to(group_start.astype(F32)[:, None], (N_EXPERTS, LANES))
    tile = pl.BlockSpec((TOP_K, tt), lambda i: (0, i))
    return pl.pallas_call(
        _position_kernel,
        grid=(m // tt,),
        in_specs=[tile, tile, pl.BlockSpec((N_EXPERTS, LANES), lambda i: (0, 0))],
        out_specs=tile,
        out_shape=jax.ShapeDtypeStruct((TOP_K, m), jnp.int32),
        compiler_params=_params("arbitrary"),
        name="positions",
    )(idx, rank, table)


def _routing_tables(idx, m):
    i32 = jnp.int32
    n_assign = TOP_K * m
    rank, counts = _ranks(idx)
    padded = (counts + EXPERT_SUB - 1) // EXPERT_SUB * EXPERT_SUB
    pad_end = jnp.cumsum(padded)
    pad_start = pad_end - padded
    pos = _positions(idx, rank, pad_start).reshape(n_assign)
    n_sub = -(-(n_assign + N_EXPERTS * (EXPERT_SUB - 1)) // EXPERT_SUB)
    n_rows = n_sub * EXPERT_SUB
    used_sub = pad_end[-1] // EXPERT_SUB

    rows = EXPERT_SUBS * EXPERT_SUB
    n_sb = -(-n_sub // EXPERT_SUBS) + N_EXPERTS + 1
    nsb_e = (padded + rows - 1) // rows
    sb_cum = jnp.cumsum(nsb_e)
    total_real = sb_cum[-1]
    s = jnp.arange(n_sb, dtype=i32)
    e_s = jnp.minimum(jnp.sum((sb_cum[None, :] <= s[:, None]).astype(i32), axis=1), N_EXPERTS - 1)
    local = s - (sb_cum[e_s] - nsb_e[e_s])
    real = s < total_real
    start_real = pad_start[e_s] + local * rows
    nsub_real = jnp.clip((padded[e_s] - local * rows) // EXPERT_SUB, 0, EXPERT_SUBS)
    fill_idx = s - total_real
    start_fill = pad_end[-1] + fill_idx * rows
    nsub_fill = jnp.clip(n_sub - used_sub - fill_idx * EXPERT_SUBS, 0, EXPERT_SUBS)
    is_fill = jnp.logical_and(jnp.logical_not(real), nsub_fill > 0)
    sb_start = jnp.where(real, start_real, jnp.where(is_fill, start_fill, 0)).astype(i32)
    sb_nsub = jnp.where(real, nsub_real, jnp.where(is_fill, nsub_fill, 0)).astype(i32)
    sb_zero = is_fill.astype(i32)
    last_e = e_s[jnp.maximum(total_real - 1, 0)]
    sb_expert = jnp.where(real, e_s, last_e).astype(i32)
    return pos, counts.astype(i32), pad_start.astype(i32), used_sub.astype(i32), n_rows, sb_start, sb_nsub, sb_zero, sb_expert


def _main_tiles(d):
    w = HEADS * HEAD_DIM
    sc = d // 2
    sizes = (w, w, w, w, sc, sc, sc, d, d)
    acts = (1, None, 0, 1, 0, 0, 0, 2, 2)
    order = (0, 2, 3, 4, 5, 6, 7, 8)
    starts = np.concatenate([[0], np.cumsum(sizes)])
    cols, codes = [], []
    for seg in order:
        assert sizes[seg] % COL_TILE == 0 and starts[seg] % COL_TILE == 0
        for t in range(sizes[seg] // COL_TILE):
            cols.append(int(starts[seg]) // COL_TILE + t)
            codes.append(acts[seg])
    return cols, codes, int(starts[1]) // COL_TILE


def _layer(x2d, meta, bsz, seq, norm_mix_w, w_in, lb_logits, g_norm_w, w_hgrn_out, conv_w, w_conv_out, w_o,
           norm_ffn_w, w_router, b_router, w_up, b_up, w_down, b_down, final_norm_w):
    m, d = x2d.shape
    cols, codes, fcol = _main_tiles(d)
    consts = _hgrn_constants(CHUNK)

    xn, lf, kk, xn_meta, lf_meta, kk_meta = _fgate(x2d, meta, norm_mix_w, w_in, lb_logits, fcol, NORM_ROWS)
    proj, proj_meta = _inproj(xn, xn_meta, w_in, cols, codes, INPROJ_ROWS)

    pad = CHUNK - N_META
    front = lambda a: jnp.pad(a, ((pad, 0), (0, 0)))
    s_zero = jnp.zeros((HEADS, HEAD_DIM, HEAD_DIM), F32)
    _, s_meta = _hgrn(front(proj_meta), front(lf_meta), front(kk_meta), g_norm_w, s_zero, consts, 1, CHUNK, CHUNK)
    og, _ = _hgrn(proj, lf, kk, g_norm_w, s_meta[0], consts, bsz, seq, HGRN_ROWS)
    h1, xn_ffn, idx, gate = _mixer_out(og, proj, proj_meta, x2d, w_hgrn_out.astype(BF16), w_conv_out.astype(BF16),
                                   w_o.astype(BF16), conv_w, norm_ffn_w, w_router, b_router, seq, MIXER_ROWS)

    pos, counts, pad_start, used_sub, n_rows, sb_start, sb_nsub, sb_zero, sb_expert = _routing_tables(idx, m)
    pos_t = _pos_tiles(pos, m)
    x_rows = _dispatch(xn_ffn, pos_t, counts, pad_start, used_sub, n_rows)
    y_rows = _experts(x_rows, w_up, b_up, w_down, b_down, sb_start, sb_nsub, sb_zero, sb_expert)
    return _combine(y_rows, pos_t, gate, h1, final_norm_w)


def kernel(x, meta_tokens, norm_mix_w, w_in, lb_logits, g_norm_w, w_hgrn_out, conv_w, w_conv_out, w_o, norm_ffn_w,
           w_router, b_router, w_up, b_up, w_down, b_down, final_norm_w):
    bsz, seq, d = x.shape
    assert norm_mix_w.shape[0] == 1, "single-layer block"
    out = _layer(x.reshape(bsz * seq, d), meta_tokens.astype(x.dtype), bsz, seq, norm_mix_w[0], w_in[0], lb_logits,
                 g_norm_w[0], w_hgrn_out[0], conv_w[0], w_conv_out[0], w_o[0], norm_ffn_w[0], w_router[0],
                 b_router[0], w_up[0], b_up[0], w_down[0], b_down[0], final_norm_w)
    return out.reshape(bsz, seq, d)
```

```python
import functools

import numpy as np
import jax
import jax.numpy as jnp
from jax import lax
from jax.experimental import pallas as pl
from jax.experimental.pallas import tpu as pltpu

F32 = jnp.float32
BF16 = jnp.bfloat16

LANES = 128
N_META = 16
HEADS = 8
HEAD_DIM = 128
N_EXPERTS = 32
TOP_K = 4
SWIGLU_LIMIT = 7.0
SWIGLU_ALPHA = 1.702
EPS = 1e-6

CHUNK = 256
HGRN_BASE = 64
HGRN_ROWS = 512
NORM_ROWS = 512
INPROJ_ROWS = 1024
MIXER_ROWS = 256
COL_TILE = 1024
EXPERT_SUB = 128
EXPERT_SUBS = 18
EXPERT_CHUNK_SUBS = 8
EXPERT_FF_TILE = 512
EXPERT_OUT_TILE = 512
COMBINE_ROWS = 256
RANK_ROWS = 1024
RANK_LANES = 256

V7X_VMEM_BYTES = 64 * 1024 * 1024
V7X_VMEM_LIMIT = V7X_VMEM_BYTES - 8 * 1024 * 1024


def _dot(a, b):
    return jnp.dot(a, b, preferred_element_type=F32)


def _dot_bt(a, b):
    return lax.dot_general(a, b, (((1,), (1,)), ((), ())), preferred_element_type=F32)


def _dot_at(a, b):
    return lax.dot_general(a, b, (((0,), (0,)), ((), ())), preferred_element_type=F32)


def _split3(x):
    hi = x.astype(BF16)
    r1 = x - hi.astype(F32)
    mid = r1.astype(BF16)
    lo = (r1 - mid.astype(F32)).astype(BF16)
    return hi, mid, lo


def _params(*sem):
    return pltpu.CompilerParams(dimension_semantics=sem, vmem_limit_bytes=V7X_VMEM_LIMIT)


def _activate(z, code):
    s = jax.nn.sigmoid(z)
    return jnp.where(code == 0, z, jnp.where(code == 1, z * s, s))


def _inproj_kernel(col_ref, code_ref, x_ref, xm_ref, w_ref, o_ref, om_ref, wb_ref):
    code = code_ref[pl.program_id(0)]

    @pl.when(pl.program_id(1) == 0)
    def _():
        wb_ref[...] = w_ref[...].astype(BF16)
        om_ref[...] = _activate(_dot(xm_ref[...], wb_ref[...]), code).astype(om_ref.dtype)

    o_ref[...] = _activate(_dot(x_ref[...], wb_ref[...]), code).astype(o_ref.dtype)


def _inproj(xn, xn_meta, w_in, cols, codes, tm):
    m, d = xn.shape
    mm = xn_meta.shape[0]
    nt = len(cols)
    grid_spec = pltpu.PrefetchScalarGridSpec(
        num_scalar_prefetch=2,
        grid=(nt, m // tm),
        in_specs=[
            pl.BlockSpec((tm, d), lambda n, i, col, code: (i, 0)),
            pl.BlockSpec((mm, d), lambda n, i, col, code: (0, 0)),
            pl.BlockSpec((d, COL_TILE), lambda n, i, col, code: (0, col[n])),
        ],
        out_specs=[
            pl.BlockSpec((tm, COL_TILE), lambda n, i, col, code: (i, n)),
            pl.BlockSpec((mm, COL_TILE), lambda n, i, col, code: (0, n)),
        ],
        scratch_shapes=[pltpu.VMEM((d, COL_TILE), BF16)],
    )
    return pl.pallas_call(
        _inproj_kernel,
        grid_spec=grid_spec,
        out_shape=[jax.ShapeDtypeStruct((m, nt * COL_TILE), BF16), jax.ShapeDtypeStruct((mm, nt * COL_TILE), BF16)],
        compiler_params=_params("arbitrary", "arbitrary"),
        name="inproj",
    )(jnp.asarray(cols, jnp.int32), jnp.asarray(codes, jnp.int32), xn, xn_meta, w_in)


def _fgate_kernel(x_ref, xm_ref, nw_ref, w_ref, lbl_ref, xn_ref, lf_ref, kk_ref, xnm_ref, lfm_ref, kkm_ref, wb_ref):
    lbl = lbl_ref[...]
    e = jnp.exp(lbl - jnp.max(lbl, axis=0, keepdims=True))
    lb = e[0:1] / jnp.sum(e, axis=0, keepdims=True)

    def gate(x, xn_out, lf_out, kk_out):
        ms = jnp.mean(x * x, axis=-1, keepdims=True)
        xn = (x * lax.rsqrt(ms + EPS) * nw_ref[...]).astype(BF16)
        xn_out[...] = xn
        z = _dot(xn, wb_ref[...])
        lf_out[...] = jnp.log(lb + (1.0 - lb) * jax.nn.sigmoid(z))
        kk_out[...] = (1.0 - lb) * jax.nn.sigmoid(-z)

    @pl.when(pl.program_id(0) == 0)
    def _():
        wb_ref[...] = w_ref[...].astype(BF16)
        gate(xm_ref[...], xnm_ref, lfm_ref, kkm_ref)

    gate(x_ref[...], xn_ref, lf_ref, kk_ref)


def _fgate(x, x_meta, norm_w, w_in, lb_logits, col, tm):
    m, d = x.shape
    mm = x_meta.shape[0]
    r = lb_logits.shape[0]
    out = jax.ShapeDtypeStruct((m, COL_TILE), F32)
    out_meta = jax.ShapeDtypeStruct((mm, COL_TILE), F32)
    return pl.pallas_call(
        _fgate_kernel,
        grid=(m // tm,),
        in_specs=[
            pl.BlockSpec((tm, d), lambda i: (i, 0)),
            pl.BlockSpec((mm, d), lambda i: (0, 0)),
            pl.BlockSpec((1, d), lambda i: (0, 0)),
            pl.BlockSpec((d, COL_TILE), lambda i: (0, col)),
            pl.BlockSpec((r, COL_TILE), lambda i: (0, 0)),
        ],
        out_specs=[pl.BlockSpec((tm, d), lambda i: (i, 0))] + [pl.BlockSpec((tm, COL_TILE), lambda i: (i, 0))] * 2
        + [pl.BlockSpec((mm, d), lambda i: (0, 0))] + [pl.BlockSpec((mm, COL_TILE), lambda i: (0, 0))] * 2,
        out_shape=[jax.ShapeDtypeStruct((m, d), BF16), out, out,
                   jax.ShapeDtypeStruct((mm, d), BF16), out_meta, out_meta],
        scratch_shapes=[pltpu.VMEM((d, COL_TILE), BF16)],
        compiler_params=_params("arbitrary"),
        name="norm_fgate",
    )(x, x_meta, norm_w.reshape(1, d), w_in, lb_logits)


def _hgrn_constants(c):
    base = HGRN_BASE
    nlow = int(np.log2(base))
    nl = int(np.log2(c))
    assert (1 << nl) == c and (1 << nlow) == base and c >= base
    rr = np.arange(base)[:, None]
    uu = np.arange(base)[None, :]
    mats = [uu <= rr, uu > rr]
    sels = []
    for lvl in range(nlow):
        b = 1 << lvl
        start = (rr // (2 * b)) * (2 * b)
        mid = start + b - 1
        second = (rr - start) >= b
        mats.append(np.where(second, (uu > mid) & (uu <= rr), (uu > rr) & (uu <= mid)))
        sels.append(np.broadcast_to(second, (base, HEADS * HEAD_DIM)))
    tt = np.arange(c)[:, None]
    ss = np.arange(c)[None, :]
    masks = []
    for lvl in range(nl):
        b = 1 << lvl
        masks.append(((tt // (2 * b)) == (ss // (2 * b))) & ((tt % (2 * b)) >= b) & ((ss % (2 * b)) < b))
    masks.append(np.eye(c, dtype=bool))
    m1 = np.concatenate(mats, 0).astype(np.float32)
    mall = jnp.asarray(np.concatenate([m1, m1, m1], axis=1), BF16)
    return mall, jnp.asarray(np.stack(masks).astype(np.float32)), jnp.asarray(np.stack(sels).astype(np.float32))


def _hgrn_kernel(q_ref, v_ref, g_ref, lf_ref, kk_ref, gw_ref, s0_ref, mall_ref, mask_ref, sel_ref,
                 o_ref, sfin_ref, st_ref, *, chunk, n_chunks):
    base = HGRN_BASE
    nb = chunk // base
    nlow = sel_ref.shape[0]
    nl = mask_ref.shape[0] - 1
    step = pl.program_id(1)

    @pl.when(step == 0)
    def _():
        st_ref[...] = s0_ref[...]

    def chunk_body(ci, carry):
        r0 = pl.multiple_of(ci * chunk, chunk)
        rows = pl.ds(r0, chunk)
        qb = q_ref[rows, :]
        q = qb.astype(F32)
        k = kk_ref[rows, :]
        blk = lambda a, i: a[i * base:(i + 1) * base]

        pre, suf, e_low = [], [], []
        for i in range(nb):
            hi, mid, lo = _split3(lf_ref[pl.ds(r0 + i * base, base), :])
            args = _dot(mall_ref[...], jnp.concatenate([hi, mid, lo], axis=0))
            pre.append(args[0:base])
            suf.append(args[base:2 * base])
            e_low.append(jnp.exp(args[2 * base:]))
        tot = [p[base - 1:base] for p in pre]

        def span(lo_blk, hi_blk):
            acc = None
            for j in range(lo_blk, hi_blk):
                acc = tot[j] if acc is None else acc + tot[j]
            return acc

        def shifted(a, off):
            return a if off is None else a + off

        xs = []
        for lvl in range(nlow):
            parts = [jnp.where(sel_ref[lvl] > 0.5, blk(q, i), blk(k, i)) * blk(e_low[i], lvl) for i in range(nb)]
            xs.append(jnp.concatenate(parts, axis=0).astype(BF16))
        for lvl in range(nlow, nl):
            half = (1 << lvl) // base
            parts = []
            for i in range(nb):
                g = i % (2 * half)
                if g >= half:
                    parts.append(blk(q, i) * jnp.exp(shifted(pre[i], span(i - (g - half), i))))
                else:
                    parts.append(blk(k, i) * jnp.exp(shifted(suf[i], span(i + 1, i - g + half))))
            xs.append(jnp.concatenate(parts, axis=0).astype(BF16))
        q_in = jnp.concatenate([blk(q, i) * jnp.exp(shifted(pre[i], span(0, i))) for i in range(nb)],
                               axis=0).astype(BF16)
        k_out = jnp.concatenate([blk(k, i) * jnp.exp(shifted(suf[i], span(i + 1, nb))) for i in range(nb)],
                                axis=0).astype(BF16)
        dec = jnp.exp(span(0, nb))
        kb = k.astype(BF16)

        for h in range(HEADS):
            cs = slice(h * HEAD_DIM, (h + 1) * HEAD_DIM)
            scores = mask_ref[nl] * _dot_bt(qb[:, cs], kb[:, cs])
            for lvl in range(nl):
                x = xs[lvl][:, cs]
                scores = scores + mask_ref[lvl] * _dot_bt(x, x)
            v = v_ref[rows, cs]
            st = st_ref[h]
            o = _dot(scores.astype(BF16), v) + _dot_bt(q_in[:, cs], st.astype(BF16))
            st_ref[h] = st * dec[:, cs] + _dot_at(v, k_out[:, cs])
            ms = jnp.mean(o * o, axis=-1, keepdims=True)
            on = o * lax.rsqrt(ms + EPS) * gw_ref[...]
            o_ref[rows, cs] = (on * g_ref[rows, cs].astype(F32)).astype(o_ref.dtype)
        return carry

    lax.fori_loop(0, n_chunks, chunk_body, 0)

    @pl.when(step == pl.num_programs(1) - 1)
    def _():
        sfin_ref[0] = st_ref[...]


def _hgrn(proj, lf, kk, g_norm_w, s0, consts, bsz, seq, rows, chunk):
    mall, masks, sels = consts
    steps = seq // rows
    w = HEADS * HEAD_DIM
    assert w == COL_TILE
    row_map = lambda col: (lambda b, s: (b * steps + s, col))
    const2 = lambda b, s: (0, 0)
    const3 = lambda b, s: (0, 0, 0)
    kern = functools.partial(_hgrn_kernel, chunk=chunk, n_chunks=rows // chunk)
    return pl.pallas_call(
        kern,
        grid=(bsz, steps),
        in_specs=[
            pl.BlockSpec((rows, w), row_map(0)),
            pl.BlockSpec((rows, w), row_map(1)),
            pl.BlockSpec((rows, w), row_map(2)),
            pl.BlockSpec((rows, w), row_map(0)),
            pl.BlockSpec((rows, w), row_map(0)),
            pl.BlockSpec((1, HEAD_DIM), const2),
            pl.BlockSpec((HEADS, HEAD_DIM, HEAD_DIM), const3),
            pl.BlockSpec(mall.shape, const2),
            pl.BlockSpec(masks.shape, const3),
            pl.BlockSpec(sels.shape, const3),
        ],
        out_specs=[
            pl.BlockSpec((rows, w), row_map(0)),
            pl.BlockSpec((1, HEADS, HEAD_DIM, HEAD_DIM), lambda b, s: (b, 0, 0, 0)),
        ],
        out_shape=[
            jax.ShapeDtypeStruct((bsz * seq, w), BF16),
            jax.ShapeDtypeStruct((bsz, HEADS, HEAD_DIM, HEAD_DIM), F32),
        ],
        scratch_shapes=[pltpu.VMEM((HEADS, HEAD_DIM, HEAD_DIM), F32)],
        compiler_params=_params("arbitrary", "arbitrary"),
        name="hgrn2",
    )(proj, proj, proj, lf, kk, g_norm_w.reshape(1, HEAD_DIM), s0, mall, masks, sels)


def _mixer_out_kernel(og_ref, scv_ref, scb_ref, scc_ref, ga_ref, gb_ref, x_ref,
                      pv_ref, pc_ref, mv_ref, mc_ref,
                      wa_ref, wb_ref, wo_ref, cw_ref, nw_ref, wr_ref, br_ref,
                      h1_ref, xn_ref, idx_ref, gate_ref, hbuf, *, tiles_per_seq, n_tiles):
    i = pl.program_id(0)
    tm = x_ref.shape[0]
    tile = jnp.minimum(i, n_tiles - 1)
    first = (tile % tiles_per_seq) == 0

    @pl.when(i == 0)
    def _():
        hbuf[1] = jnp.zeros(hbuf.shape[1:], hbuf.dtype)

    hp = hbuf[(i + 1) % 2]
    ms = jnp.mean(hp * hp, axis=-1, keepdims=True)
    xn = hp * lax.rsqrt(ms + EPS) * nw_ref[...]
    xn_ref[...] = xn.astype(BF16).reshape(xn_ref.shape)

    xh = xn.astype(BF16)
    xl = (xn - xh.astype(F32)).astype(BF16)
    wr = wr_ref[...]
    wh = wr.astype(BF16)
    wl = (wr - wh.astype(F32)).astype(BF16)
    logits = _dot_bt(wh, xh) + _dot_bt(wh, xl) + _dot_bt(wl, xh) + br_ref[...]
    ne = logits.shape[0]
    ie = lax.broadcasted_iota(jnp.int32, logits.shape, 0)
    tops, idxs = [], []
    for _ in range(TOP_K):
        mx = jnp.max(logits, axis=0, keepdims=True)
        ix = jnp.min(jnp.where(logits == mx, ie, ne), axis=0, keepdims=True)
        tops.append(mx)
        idxs.append(ix)
        logits = jnp.where(ie == ix, -jnp.inf, logits)
    es = [jnp.exp(t - tops[0]) for t in tops]
    den = es[0]
    for e in es[1:]:
        den = den + e
    gate_ref[...] = jnp.concatenate([e / den for e in es], axis=0)
    idx_ref[...] = jnp.concatenate(idxs, axis=0)

    u = scc_ref[...].astype(F32) * scv_ref[...].astype(F32)
    halo_prev = pc_ref[...].astype(F32) * pv_ref[...].astype(F32)
    halo_meta = mc_ref[...].astype(F32) * mv_ref[...].astype(F32)
    halo = jnp.where(first, halo_meta, halo_prev)
    hr = halo.shape[0]
    r = lax.broadcasted_iota(jnp.int32, (tm, 1), 0)
    u1 = jnp.where(r == 0, halo[hr - 1:hr], pltpu.roll(u, 1, 0))
    u2 = jnp.where(r == 0, halo[hr - 2:hr - 1], jnp.where(r == 1, halo[hr - 1:hr], pltpu.roll(u, 2, 0)))
    conv = cw_ref[2:3] * u + cw_ref[1:2] * u1 + cw_ref[0:1] * u2
    yb_in = (scb_ref[...].astype(F32) * conv).astype(BF16)

    y_a = _dot(og_ref[...], wa_ref[...])
    y_b = _dot(yb_in, wb_ref[...])
    merged = (ga_ref[...].astype(F32) * y_a + gb_ref[...].astype(F32) * y_b).astype(BF16)
    h1 = x_ref[...] + _dot(merged, wo_ref[...])
    h1_ref[...] = h1
    hbuf[i % 2] = h1


def _mixer_out(og, proj, proj_meta, x2d, wa, wb, wo, conv_w, norm_w, w_router, b_router, seq, tm):
    m, d = x2d.shape
    w = COL_TILE
    halo = proj_meta.shape[0]
    assert tm % halo == 0 and seq % tm == 0 and d == 2 * w
    ne = w_router.shape[1]
    per_halo = tm // halo
    n_tiles = m // tm
    cur = lambda i: jnp.minimum(i, n_tiles - 1)
    done = lambda i: jnp.maximum(i - 1, 0)
    row = lambda col: (lambda i: (cur(i), col))
    prev = lambda col: (lambda i: (jnp.maximum(cur(i) * per_halo - 1, 0), col))
    const = lambda i: (0, 0)
    whole = lambda a: pl.BlockSpec(a.shape, const)
    wr_t = w_router.T
    kern = functools.partial(_mixer_out_kernel, tiles_per_seq=seq // tm, n_tiles=n_tiles)
    return pl.pallas_call(
        kern,
        grid=(n_tiles + 1,),
        in_specs=[
            pl.BlockSpec((tm, w), row(0)),
            pl.BlockSpec((tm, w), row(3)),
            pl.BlockSpec((tm, w), row(4)),
            pl.BlockSpec((tm, w), row(5)),
            pl.BlockSpec((tm, d), row(3)),
            pl.BlockSpec((tm, d), row(4)),
            pl.BlockSpec((tm, d), row(0)),
            pl.BlockSpec((halo, w), prev(3)),
            pl.BlockSpec((halo, w), prev(5)),
            pl.BlockSpec((halo, w), lambda i: (0, 3)),
            pl.BlockSpec((halo, w), lambda i: (0, 5)),
            whole(wa), whole(wb), whole(wo),
            pl.BlockSpec(conv_w.shape, const),
            pl.BlockSpec((1, d), const),
            pl.BlockSpec((ne, d), const),
            pl.BlockSpec((ne, 1), const),
        ],
        out_specs=[
            pl.BlockSpec((tm, d), row(0)),
            pl.BlockSpec((tm, d // LANES, LANES), lambda i: (done(i), 0, 0)),
            pl.BlockSpec((TOP_K, tm), lambda i: (0, done(i))),
            pl.BlockSpec((TOP_K, tm), lambda i: (0, done(i))),
        ],
        out_shape=[
            jax.ShapeDtypeStruct((m, d), F32),
            jax.ShapeDtypeStruct((m, d // LANES, LANES), BF16),
            jax.ShapeDtypeStruct((TOP_K, m), jnp.int32),
            jax.ShapeDtypeStruct((TOP_K, m), F32),
        ],
        scratch_shapes=[pltpu.VMEM((2, tm, d), F32)],
        compiler_params=_params("arbitrary"),
        name="mixer_out",
    )(og, proj, proj, proj, proj, proj, x2d, proj, proj, proj_meta, proj_meta,
      wa, wb, wo, conv_w, norm_w.reshape(1, d), wr_t, b_router.reshape(ne, 1))


def _dispatch_kernel(cnt_ref, pst_ref, used_ref, pos_ref, x_ref, o_hbm, xbuf, zbuf, sem, zsem):
    i = pl.program_id(0)
    n_steps = pl.num_programs(0)
    tt = x_ref.shape[0]
    n_sub = o_hbm.shape[0] // EXPERT_SUB
    bits = [1 << b for b in reversed(range(EXPERT_SUB.bit_length() - 1))]
    slot = i % 2

    xbuf[slot] = x_ref[...]

    def issue(r, carry):
        for k in range(TOP_K):
            p = pos_ref[0, 0, k * tt + r]
            pltpu.make_async_copy(xbuf.at[slot, pl.ds(r, 1)], o_hbm.at[pl.ds(p, 1)],
                                  sem.at[slot]).start(priority=k % 2)
        return carry

    lax.fori_loop(0, tt, issue, 0, unroll=4)

    def tokens_done(s):
        return [pltpu.make_async_copy(xbuf.at[s], o_hbm.at[pl.ds(0, tt)], sem.at[s]) for _ in range(TOP_K)]

    def zero_copies(fn):
        for e in range(cnt_ref.shape[0]):
            npad = (-cnt_ref[e]) & (EXPERT_SUB - 1)
            base = pst_ref[e] + cnt_ref[e]
            for bit in bits:
                @pl.when((npad & bit) != 0)
                def _():
                    row = base + (npad & ~(2 * bit - 1))
                    fn(pltpu.make_async_copy(zbuf.at[pl.ds(0, bit)], o_hbm.at[pl.ds(row, bit)], zsem))
        for j in range(cnt_ref.shape[0]):
            blk = used_ref[0] + j

            @pl.when(blk < n_sub)
            def _():
                row = pl.multiple_of(blk * EXPERT_SUB, EXPERT_SUB)
                fn(pltpu.make_async_copy(zbuf, o_hbm.at[pl.ds(row, EXPERT_SUB)], zsem))

    @pl.when(i == 0)
    def _():
        zbuf[...] = jnp.zeros_like(zbuf)
        zero_copies(lambda c: c.start())

    @pl.when(i > 0)
    def _():
        for c in tokens_done(1 - slot):
            c.wait()

    @pl.when(i == n_steps - 1)
    def _():
        for c in tokens_done(slot):
            c.wait()
        zero_copies(lambda c: c.wait())


def _dispatch(xn3, pos_t, counts, pad_start, used_sub, n_rows):
    m, sub, lanes = xn3.shape
    nt = pos_t.shape[0]
    tt = m // nt
    grid_spec = pltpu.PrefetchScalarGridSpec(
        num_scalar_prefetch=3,
        grid=(nt,),
        in_specs=[
            pl.BlockSpec((1, 1, pos_t.shape[2]), lambda i, c, p, u: (i, 0, 0), memory_space=pltpu.SMEM),
            pl.BlockSpec((tt, sub, lanes), lambda i, c, p, u: (i, 0, 0)),
        ],
        out_specs=pl.BlockSpec(memory_space=pl.ANY),
        scratch_shapes=[pltpu.VMEM((2, tt, sub, lanes), xn3.dtype), pltpu.VMEM((EXPERT_SUB, sub, lanes), xn3.dtype),
                        pltpu.SemaphoreType.DMA((2,)), pltpu.SemaphoreType.DMA(())],
    )
    return pl.pallas_call(
        _dispatch_kernel,
        grid_spec=grid_spec,
        out_shape=jax.ShapeDtypeStruct((n_rows, sub, lanes), xn3.dtype),
        compiler_params=_params("arbitrary"),
        name="dispatch",
    )(counts, pad_start, used_sub.reshape(1), pos_t, xn3)


def _row_chunks(nsub, chunk_fn):
    per = EXPERT_CHUNK_SUBS
    big = per * EXPERT_SUB
    n_big = lax.div(nsub, per)

    def body(c, carry):
        chunk_fn(c * per, pl.multiple_of(c * big, big), big)
        return carry

    lax.fori_loop(0, n_big, body, 0)
    rem = nsub - n_big * per
    bit = per // 2
    while bit >= 1:
        done = rem & ~(2 * bit - 1)

        @pl.when((rem & bit) != 0)
        def _():
            sub0 = n_big * per + done
            chunk_fn(sub0, pl.multiple_of(sub0 * EXPERT_SUB, EXPERT_SUB), bit * EXPERT_SUB)

        bit //= 2


def _sub_pieces(n, fn):
    bit = 1 << (EXPERT_SUBS.bit_length() - 1)
    while bit >= 1:
        @pl.when((n & bit) != 0)
        def _():
            fn(n & ~(2 * bit - 1), bit)

        bit //= 2


def _ffn_up_kernel(st_ref, ns_ref, zf_ref, se_ref, jm_ref, x_hbm, wg_ref, wu_ref, bg_ref, bu_ref, h_hbm,
                   xstage, xbuf, hbuf, wgb_ref, wub_ref, sem_x, sem_h):
    s = pl.program_id(0)
    j = pl.program_id(1)
    n_s = pl.num_programs(0)
    nj = pl.num_programs(1)
    step = s * nj + j
    nsub = ns_ref[s]
    real = jnp.logical_and(nsub > 0, zf_ref[s] == 0)
    hs = step % 2

    def x_copy(sb, first, count):
        return pltpu.make_async_copy(x_hbm.at[pl.ds(st_ref[sb] + first, count)],
                                     xstage.at[pl.ds(first, count)], sem_x)

    def h_copy(sb, jj, first, count, slot):
        row = pl.multiple_of((st_ref[sb] + first) * EXPERT_SUB, EXPERT_SUB)
        return pltpu.make_async_copy(
            hbuf.at[slot, pl.ds(pl.multiple_of(first * EXPERT_SUB, EXPERT_SUB), count * EXPERT_SUB)],
            h_hbm.at[jj, pl.ds(row, count * EXPERT_SUB)], sem_h.at[slot])

    def for_x_subs(sb, fn):
        _sub_pieces(ns_ref[sb] * (1 - zf_ref[sb]), fn)

    def for_h_subs(sb, fn):
        _sub_pieces(ns_ref[sb], fn)

    @pl.when(step == 0)
    def _():
        for_x_subs(0, lambda first, count: x_copy(0, first, count).start())

    @pl.when(j == 0)
    def _():
        for_x_subs(s, lambda first, count: x_copy(s, first, count).wait())

    @pl.when(step >= 2)
    def _():
        sp = lax.div(step - 2, nj)
        jp = step - 2 - sp * nj
        for_h_subs(sp, lambda first, count: h_copy(sp, jp, first, count, hs).wait())

    @pl.when(real)
    def _():
        wgb_ref[...] = wg_ref[0].astype(BF16)
        wub_ref[...] = wu_ref[0].astype(BF16)

        def make_chunk(first_tile):
            def chunk(sub0, row0, nrows):
                if first_tile:
                    xs = jnp.concatenate([xstage[sub0 + t].reshape(EXPERT_SUB, xbuf.shape[1])
                                          for t in range(nrows // EXPERT_SUB)], axis=0)
                    xbuf[pl.ds(row0, nrows), :] = xs
                else:
                    xs = xbuf[pl.ds(row0, nrows), :]
                g = _dot(xs, wgb_ref[...]) + bg_ref[0]
                u = _dot(xs, wub_ref[...]) + bu_ref[0]
                g = jnp.minimum(g, SWIGLU_LIMIT)
                u = jnp.clip(u, -SWIGLU_LIMIT, SWIGLU_LIMIT)
                hbuf[hs, pl.ds(row0, nrows), :] = ((u + 1.0) * (g * jax.nn.sigmoid(SWIGLU_ALPHA * g))).astype(BF16)

            return chunk

        @pl.when(j == 0)
        def _():
            _row_chunks(nsub, make_chunk(True))

        @pl.when(j > 0)
        def _():
            _row_chunks(nsub, make_chunk(False))

    @pl.when(jnp.logical_and(j == 0, s + 1 < n_s))
    def _():
        nxt = jnp.minimum(s + 1, n_s - 1)
        for_x_subs(nxt, lambda first, count: x_copy(nxt, first, count).start())

    @pl.when(zf_ref[s] == 1)
    def _():
        hbuf[hs] = jnp.zeros(hbuf.shape[1:], hbuf.dtype)

    for_h_subs(s, lambda first, count: h_copy(s, j, first, count, hs).start())

    @pl.when(step == n_s * nj - 1)
    def _():
        sp = lax.div(step - 1, nj)
        jp = step - 1 - sp * nj
        for_h_subs(sp, lambda first, count: h_copy(sp, jp, first, count, 1 - hs).wait())
        for_h_subs(s, lambda first, count: h_copy(s, j, first, count, hs).wait())


def _ffn_down_kernel(st_ref, ns_ref, zf_ref, se_ref, cm_ref, h_hbm, wd_ref, bd_ref, y_hbm,
                     hb, yrow, ytile, wdb_ref, sem_h, sem_y):
    s = pl.program_id(0)
    c = pl.program_id(1)
    n_s = pl.num_programs(0)
    nc = pl.num_programs(1)
    step = s * nc + c
    njh = hb.shape[1]
    tn = wdb_ref.shape[1]
    nsub = ns_ref[s]
    real = jnp.logical_and(nsub > 0, zf_ref[s] == 0)

    def h_copy(sb, jj, first, count, slot):
        row = pl.multiple_of((st_ref[sb] + first) * EXPERT_SUB, EXPERT_SUB)
        return pltpu.make_async_copy(
            h_hbm.at[jj, pl.ds(row, count * EXPERT_SUB)],
            hb.at[slot, jj, pl.ds(pl.multiple_of(first * EXPERT_SUB, EXPERT_SUB), count * EXPERT_SUB)],
            sem_h.at[slot])

    def y_copy(sb, first, count):
        return pltpu.make_async_copy(ytile.at[pl.ds(first, count)],
                                     y_hbm.at[pl.ds(st_ref[sb] + first, count)], sem_y)

    def for_h_subs(sb, fn):
        def all_tiles(first, count):
            for jj in range(njh):
                fn(jj, first, count)

        _sub_pieces(ns_ref[sb] * (1 - zf_ref[sb]), all_tiles)

    def for_y_subs(sb, fn):
        _sub_pieces(ns_ref[sb], fn)

    @pl.when(step == 0)
    def _():
        for_h_subs(0, lambda jj, first, count: h_copy(0, jj, first, count, 0).start())

    @pl.when(jnp.logical_and(c == 0, s + 1 < n_s))
    def _():
        nxt = jnp.minimum(s + 1, n_s - 1)
        for_h_subs(nxt, lambda jj, first, count: h_copy(nxt, jj, first, count, (s + 1) % 2).start())

    @pl.when(c == 0)
    def _():
        for_h_subs(s, lambda jj, first, count: h_copy(s, jj, first, count, s % 2).wait())

    @pl.when(jnp.logical_and(c == nc - 1, s > 0))
    def _():
        sp = jnp.maximum(s - 1, 0)
        for_y_subs(sp, lambda first, count: y_copy(sp, first, count).wait())

    @pl.when(real)
    def _():
        wdb_ref[...] = wd_ref[0].astype(BF16)
        h_slot = s % 2
        n_tiles = yrow.shape[1] // tn
        for cc in range(n_tiles):
            @pl.when(c == cc)
            def _():
                def chunk(sub0, row0, nrows):
                    hid = jnp.concatenate([hb[h_slot, jj, pl.ds(row0, nrows), :] for jj in range(njh)], axis=1)
                    y = (_dot(hid, wdb_ref[...]) + bd_ref[0]).astype(yrow.dtype)
                    if cc < n_tiles - 1:
                        yrow[pl.ds(row0, nrows), cc * tn:(cc + 1) * tn] = y
                    else:
                        full = jnp.concatenate([yrow[pl.ds(row0, nrows), 0:cc * tn], y], axis=1)
                        for t in range(nrows // EXPERT_SUB):
                            ytile[sub0 + t] = full[t * EXPERT_SUB:(t + 1) * EXPERT_SUB].reshape(ytile.shape[1:])

                _row_chunks(nsub, chunk)

    @pl.when(c == nc - 1)
    def _():
        @pl.when(zf_ref[s] == 1)
        def _():
            ytile[...] = jnp.zeros(ytile.shape, ytile.dtype)

        for_y_subs(s, lambda first, count: y_copy(s, first, count).start())

        @pl.when(s == n_s - 1)
        def _():
            for_y_subs(s, lambda first, count: y_copy(s, first, count).wait())


def _experts(x_rows, w_up, b_up, w_down, b_down, sb_start, sb_nsub, sb_zero, sb_expert):
    n_rows, x_sub, x_lanes = x_rows.shape
    d = x_sub * x_lanes
    ne, _, ff2 = w_up.shape
    ff = ff2 // 2
    tf = EXPERT_FF_TILE
    tn = EXPERT_OUT_TILE
    nj = ff // tf
    nc = d // tn
    n_sb = jnp.sum((sb_nsub > 0).astype(jnp.int32))
    rows = EXPERT_SUBS * EXPERT_SUB
    n_sub = n_rows // EXPERT_SUB
    sub_shape = (n_sub, EXPERT_SUB, x_sub, x_lanes)
    x_rows = x_rows.reshape(sub_shape)
    sb_start = sb_start // EXPERT_SUB
    is_real = jnp.logical_and(sb_nsub > 0, sb_zero == 0)[:, None]
    jm = jnp.where(is_real, jnp.arange(nj, dtype=jnp.int32)[None, :], nj - 1).astype(jnp.int32)
    cm = jnp.where(is_real, jnp.arange(nc, dtype=jnp.int32)[None, :], nc - 1).astype(jnp.int32)
    any_spec = pl.BlockSpec(memory_space=pl.ANY)

    up_spec = pltpu.PrefetchScalarGridSpec(
        num_scalar_prefetch=5,
        grid=(n_sb, nj),
        in_specs=[
            any_spec,
            pl.BlockSpec((1, d, tf), lambda s, j, st, ns, zf, se, jm: (se[s], 0, jm[s, j])),
            pl.BlockSpec((1, d, tf), lambda s, j, st, ns, zf, se, jm: (se[s], 0, nj + jm[s, j])),
            pl.BlockSpec((1, 1, tf), lambda s, j, st, ns, zf, se, jm: (se[s], 0, jm[s, j])),
            pl.BlockSpec((1, 1, tf), lambda s, j, st, ns, zf, se, jm: (se[s], 0, nj + jm[s, j])),
        ],
        out_specs=any_spec,
        scratch_shapes=[
            pltpu.VMEM((EXPERT_SUBS, EXPERT_SUB, x_sub, x_lanes), BF16),
            pltpu.VMEM((rows, d), BF16),
            pltpu.VMEM((2, rows, tf), BF16),
            pltpu.VMEM((d, tf), BF16),
            pltpu.VMEM((d, tf), BF16),
            pltpu.SemaphoreType.DMA(()),
            pltpu.SemaphoreType.DMA((2,)),
        ],
    )
    hidden = pl.pallas_call(
        _ffn_up_kernel,
        grid_spec=up_spec,
        out_shape=jax.ShapeDtypeStruct((nj, n_rows, tf), BF16),
        compiler_params=_params("arbitrary", "arbitrary"),
        name="ffn_up",
    )(sb_start, sb_nsub, sb_zero, sb_expert, jm, x_rows, w_up, w_up,
      b_up.reshape(ne, 1, ff2), b_up.reshape(ne, 1, ff2))

    down_spec = pltpu.PrefetchScalarGridSpec(
        num_scalar_prefetch=5,
        grid=(n_sb, nc),
        in_specs=[
            any_spec,
            pl.BlockSpec((1, ff, tn), lambda s, c, st, ns, zf, se, cm: (se[s], 0, cm[s, c])),
            pl.BlockSpec((1, 1, tn), lambda s, c, st, ns, zf, se, cm: (se[s], 0, cm[s, c])),
        ],
        out_specs=any_spec,
        scratch_shapes=[
            pltpu.VMEM((2, nj, rows, tf), BF16),
            pltpu.VMEM((rows, d), BF16),
            pltpu.VMEM((EXPERT_SUBS, EXPERT_SUB, x_sub, x_lanes), BF16),
            pltpu.VMEM((ff, tn), BF16),
            pltpu.SemaphoreType.DMA((2,)),
            pltpu.SemaphoreType.DMA(()),
        ],
    )
    y_rows = pl.pallas_call(
        _ffn_down_kernel,
        grid_spec=down_spec,
        out_shape=jax.ShapeDtypeStruct(sub_shape, BF16),
        compiler_params=_params("arbitrary", "arbitrary"),
        name="ffn_down",
    )(sb_start, sb_nsub, sb_zero, sb_expert, cm, hidden, w_down, b_down.reshape(ne, 1, d))
    return y_rows.reshape(n_rows, x_sub, x_lanes)


def _combine_kernel(pos_ref, pos_next_ref, gate_ref, h1_ref, fw_ref, y_hbm, o_ref, buf_ref, sem):
    i = pl.program_id(0)
    tt = h1_ref.shape[0]

    def start_rows(pos, slot):
        def issue(r, carry):
            for k in range(TOP_K):
                p = pos[0, 0, k * tt + r]
                pltpu.make_async_copy(y_hbm.at[pl.ds(p, 1)], buf_ref.at[slot, k, pl.ds(r, 1)],
                                      sem.at[slot]).start(priority=k % 2)
            return carry

        lax.fori_loop(0, tt, issue, 0, unroll=4)

    @pl.when(i == 0)
    def _():
        start_rows(pos_ref, 0)

    @pl.when(i + 1 < pl.num_programs(0))
    def _():
        start_rows(pos_next_ref, (i + 1) % 2)

    slot = i % 2
    for k in range(TOP_K):
        pltpu.make_async_copy(y_hbm.at[pl.ds(0, tt)], buf_ref.at[slot, k], sem.at[slot]).wait()

    gate = gate_ref[...]
    gpad = jnp.concatenate([gate, jnp.zeros((tt - TOP_K, tt), F32)], axis=0)
    gcol = gpad.T
    acc = h1_ref[...]
    for k in range(TOP_K):
        acc = acc + gcol[:, k:k + 1] * buf_ref[slot, k].reshape(acc.shape).astype(F32)
    ms = jnp.mean(acc * acc, axis=-1, keepdims=True)
    o_ref[...] = acc * lax.rsqrt(ms + EPS) * fw_ref[...]


def _pos_tiles(pos, m):
    tt = COMBINE_ROWS
    nt = m // tt
    return pos.reshape(TOP_K, nt, tt).transpose(1, 0, 2).reshape(nt, 1, TOP_K * tt)


def _combine(y_rows, pos_t, gate, h1, final_w):
    m, d = h1.shape
    tt = COMBINE_ROWS
    nt = m // tt
    return pl.pallas_call(
        _combine_kernel,
        grid=(nt,),
        in_specs=[
            pl.BlockSpec((1, 1, TOP_K * tt), lambda i: (i, 0, 0), memory_space=pltpu.SMEM),
            pl.BlockSpec((1, 1, TOP_K * tt), lambda i: (jnp.minimum(i + 1, nt - 1), 0, 0), memory_space=pltpu.SMEM),
            pl.BlockSpec((TOP_K, tt), lambda i: (0, i)),
            pl.BlockSpec((tt, d), lambda i: (i, 0)),
            pl.BlockSpec((1, d), lambda i: (0, 0)),
            pl.BlockSpec(memory_space=pl.ANY),
        ],
        out_specs=pl.BlockSpec((tt, d), lambda i: (i, 0)),
        out_shape=jax.ShapeDtypeStruct((m, d), F32),
        scratch_shapes=[pltpu.VMEM((2, TOP_K, tt) + y_rows.shape[1:], y_rows.dtype), pltpu.SemaphoreType.DMA((2,))],
        compiler_params=_params("arbitrary"),
        name="combine",
    )(pos_t, pos_t, gate, h1, final_w.reshape(1, d), y_rows)


def _rank_kernel(idx_ref, tri_ref, rank_ref, cnt_ref, carry_ref):
    i = pl.program_id(0)
    tt = idx_ref.shape[1]
    w = tri_ref.shape[0]

    @pl.when(i == 0)
    def _():
        carry_ref[...] = jnp.zeros_like(carry_ref)

    carry = carry_ref[:, 0:1]
    expert = lax.broadcasted_iota(jnp.int32, (N_EXPERTS, w), 0)
    for k in range(TOP_K):
        for c in range(tt // w):
            cols = slice(c * w, (c + 1) * w)
            hit = expert == idx_ref[k:k + 1, cols]
            prefix = _dot(hit.astype(BF16), tri_ref[...])
            before = jnp.where(hit, carry + prefix - 1.0, 0.0)
            rank_ref[k:k + 1, cols] = jnp.sum(before, axis=0, keepdims=True).astype(jnp.int32)
            carry = carry + prefix[:, w - 1:w]
    carry_ref[...] = jnp.broadcast_to(carry, carry_ref.shape)
    cnt_ref[...] = jnp.broadcast_to(carry, cnt_ref.shape)


def _ranks(idx):
    m = idx.shape[1]
    tt = RANK_ROWS
    w = RANK_LANES
    tri = jnp.asarray(np.triu(np.ones((w, w), np.float32)), BF16)
    rank, cnt = pl.pallas_call(
        _rank_kernel,
        grid=(m // tt,),
        in_specs=[pl.BlockSpec((TOP_K, tt), lambda i: (0, i)), pl.BlockSpec((w, w), lambda i: (0, 0))],
        out_specs=[pl.BlockSpec((TOP_K, tt), lambda i: (0, i)), pl.BlockSpec((N_EXPERTS, LANES), lambda i: (0, 0))],
        out_shape=[jax.ShapeDtypeStruct((TOP_K, m), jnp.int32), jax.ShapeDtypeStruct((N_EXPERTS, LANES), F32)],
        scratch_shapes=[pltpu.VMEM((N_EXPERTS, LANES), F32)],
        compiler_params=_params("arbitrary"),
        name="ranks",
    )(idx, tri)
    return rank, cnt[:, 0].astype(jnp.int32)


def _position_kernel(idx_ref, rank_ref, start_ref, pos_ref):
    start = start_ref[:, 0:1]
    expert = lax.broadcasted_iota(jnp.int32, (N_EXPERTS, idx_ref.shape[1]), 0)
    for k in range(TOP_K):
        hit = expert == idx_ref[k:k + 1, :]
        base = jnp.sum(jnp.where(hit, start, 0.0), axis=0, keepdims=True)
        pos_ref[k:k + 1, :] = base.astype(jnp.int32) + rank_ref[k:k + 1, :]


def _positions(idx, rank, group_start):
    m = idx.shape[1]
    tt = RANK_ROWS
    table = jnp.broadcast_to(group_start.astype(F32)[:, None], (N_EXPERTS, LANES))
    tile = pl.BlockSpec((TOP_K, tt), lambda i: (0, i))
    return pl.pallas_call(
        _position_kernel,
        grid=(m // tt,),
        in_specs=[tile, tile, pl.BlockSpec((N_EXPERTS, LANES), lambda i: (0, 0))],
        out_specs=tile,
        out_shape=jax.ShapeDtypeStruct((TOP_K, m), jnp.int32),
        compiler_params=_params("arbitrary"),
        name="positions",
    )(idx, rank, table)


def _routing_tables(idx, m):
    i32 = jnp.int32
    n_assign = TOP_K * m
    rank, counts = _ranks(idx)
    padded = (counts + EXPERT_SUB - 1) // EXPERT_SUB * EXPERT_SUB
    pad_end = jnp.cumsum(padded)
    pad_start = pad_end - padded
    pos = _positions(idx, rank, pad_start).reshape(n_assign)
    n_sub = -(-(n_assign + N_EXPERTS * (EXPERT_SUB - 1)) // EXPERT_SUB)
    n_rows = n_sub * EXPERT_SUB
    used_sub = pad_end[-1] // EXPERT_SUB

    rows = EXPERT_SUBS * EXPERT_SUB
    n_sb = -(-n_sub // EXPERT_SUBS) + N_EXPERTS + 1
    nsb_e = (padded + rows - 1) // rows
    sb_cum = jnp.cumsum(nsb_e)
    total_real = sb_cum[-1]
    s = jnp.arange(n_sb, dtype=i32)
    e_s = jnp.minimum(jnp.sum((sb_cum[None, :] <= s[:, None]).astype(i32), axis=1), N_EXPERTS - 1)
    local = s - (sb_cum[e_s] - nsb_e[e_s])
    real = s < total_real
    start_real = pad_start[e_s] + local * rows
    nsub_real = jnp.clip((padded[e_s] - local * rows) // EXPERT_SUB, 0, EXPERT_SUBS)
    fill_idx = s - total_real
    start_fill = pad_end[-1] + fill_idx * rows
    nsub_fill = jnp.clip(n_sub - used_sub - fill_idx * EXPERT_SUBS, 0, EXPERT_SUBS)
    is_fill = jnp.logical_and(jnp.logical_not(real), nsub_fill > 0)
    sb_start = jnp.where(real, start_real, jnp.where(is_fill, start_fill, 0)).astype(i32)
    sb_nsub = jnp.where(real, nsub_real, jnp.where(is_fill, nsub_fill, 0)).astype(i32)
    sb_zero = is_fill.astype(i32)
    last_e = e_s[jnp.maximum(total_real - 1, 0)]
    sb_expert = jnp.where(real, e_s, last_e).astype(i32)
    return pos, counts.astype(i32), pad_start.astype(i32), used_sub.astype(i32), n_rows, sb_start, sb_nsub, sb_zero, sb_expert


def _main_tiles(d):
    w = HEADS * HEAD_DIM
    sc = d // 2
    sizes = (w, w, w, w, sc, sc, sc, d, d)
    acts = (1, None, 0, 1, 0, 0, 0, 2, 2)
    order = (0, 2, 3, 4, 5, 6, 7, 8)
    starts = np.concatenate([[0], np.cumsum(sizes)])
    cols, codes = [], []
    for seg in order:
        assert sizes[seg] % COL_TILE == 0 and starts[seg] % COL_TILE == 0
        for t in range(sizes[seg] // COL_TILE):
            cols.append(int(starts[seg]) // COL_TILE + t)
            codes.append(acts[seg])
    return cols, codes, int(starts[1]) // COL_TILE


def _layer(x2d, meta, bsz, seq, norm_mix_w, w_in, lb_logits, g_norm_w, w_hgrn_out, conv_w, w_conv_out, w_o,
           norm_ffn_w, w_router, b_router, w_up, b_up, w_down, b_down, final_norm_w):
    m, d = x2d.shape
    cols, codes, fcol = _main_tiles(d)
    consts = _hgrn_constants(CHUNK)

    xn, lf, kk, xn_meta, lf_meta, kk_meta = _fgate(x2d, meta, norm_mix_w, w_in, lb_logits, fcol, NORM_ROWS)
    proj, proj_meta = _inproj(xn, xn_meta, w_in, cols, codes, INPROJ_ROWS)

    meta_chunk = HGRN_BASE
    pad = meta_chunk - N_META
    front = lambda a: jnp.pad(a, ((pad, 0), (0, 0)))
    s_zero = jnp.zeros((HEADS, HEAD_DIM, HEAD_DIM), F32)
    _, s_meta = _hgrn(front(proj_meta), front(lf_meta), front(kk_meta), g_norm_w, s_zero,
                      _hgrn_constants(meta_chunk), 1, meta_chunk, meta_chunk, meta_chunk)
    og, _ = _hgrn(proj, lf, kk, g_norm_w, s_meta[0], consts, bsz, seq, HGRN_ROWS, CHUNK)
    h1, xn_ffn, idx, gate = _mixer_out(og, proj, proj_meta, x2d, w_hgrn_out.astype(BF16), w_conv_out.astype(BF16),
                                   w_o.astype(BF16), conv_w, norm_ffn_w, w_router, b_router, seq, MIXER_ROWS)

    pos, counts, pad_start, used_sub, n_rows, sb_start, sb_nsub, sb_zero, sb_expert = _routing_tables(idx, m)
    pos_t = _pos_tiles(pos, m)
    x_rows = _dispatch(xn_ffn, pos_t, counts, pad_start, used_sub, n_rows)
    y_rows = _experts(x_rows, w_up, b_up, w_down, b_down, sb_start, sb_nsub, sb_zero, sb_expert)
    return _combine(y_rows, pos_t, gate, h1, final_norm_w)


def kernel(x, meta_tokens, norm_mix_w, w_in, lb_logits, g_norm_w, w_hgrn_out, conv_w, w_conv_out, w_o, norm_ffn_w,
           w_router, b_router, w_up, b_up, w_down, b_down, final_norm_w):
    bsz, seq, d = x.shape
    assert norm_mix_w.shape[0] == 1, "single-layer block"
    out = _layer(x.reshape(bsz * seq, d), meta_tokens.astype(x.dtype), bsz, seq, norm_mix_w[0], w_in[0], lb_logits,
                 g_norm_w[0], w_hgrn_out[0], conv_w[0], w_conv_out[0], w_o[0], norm_ffn_w[0], w_router[0],
                 b_router[0], w_up[0], b_up[0], w_down[0], b_down[0], final_norm_w)
    return out.reshape(bsz, seq, d)
```

```python
import functools

import numpy as np
import jax
import jax.numpy as jnp
from jax import lax
from jax.experimental import pallas as pl
from jax.experimental.pallas import tpu as pltpu

F32 = jnp.float32
BF16 = jnp.bfloat16

LANES = 128
N_META = 16
HEADS = 8
HEAD_DIM = 128
N_EXPERTS = 32
TOP_K = 4
SWIGLU_LIMIT = 7.0
SWIGLU_ALPHA = 1.702
EPS = 1e-6

CHUNK = 256
HGRN_BASE = 64
HGRN_ROWS = 512
NORM_ROWS = 512
INPROJ_ROWS = 1024
MIXER_ROWS = 256
COL_TILE = 1024
EXPERT_SUB = 128
EXPERT_SUBS = 18
EXPERT_CHUNK_SUBS = 8
EXPERT_FF_TILE = 512
EXPERT_OUT_TILE = 512
COMBINE_ROWS = 256
RANK_ROWS = 1024
RANK_LANES = 256

V7X_VMEM_BYTES = 64 * 1024 * 1024
V7X_VMEM_LIMIT = V7X_VMEM_BYTES - 8 * 1024 * 1024


def _dot(a, b):
    return jnp.dot(a, b, preferred_element_type=F32)


def _dot_bt(a, b):
    return lax.dot_general(a, b, (((1,), (1,)), ((), ())), preferred_element_type=F32)


def _dot_at(a, b):
    return lax.dot_general(a, b, (((0,), (0,)), ((), ())), preferred_element_type=F32)


def _split3(x):
    hi = x.astype(BF16)
    r1 = x - hi.astype(F32)
    mid = r1.astype(BF16)
    lo = (r1 - mid.astype(F32)).astype(BF16)
    return hi, mid, lo


def _params(*sem):
    return pltpu.CompilerParams(dimension_semantics=sem, vmem_limit_bytes=V7X_VMEM_LIMIT)


def _activate(z, code):
    s = jax.nn.sigmoid(z)
    return jnp.where(code == 0, z, jnp.where(code == 1, z * s, s))


def _inproj_kernel(col_ref, code_ref, x_ref, xm_ref, w_ref, o_ref, om_ref, wb_ref):
    code = code_ref[pl.program_id(0)]

    @pl.when(pl.program_id(1) == 0)
    def _():
        wb_ref[...] = w_ref[...].astype(BF16)
        om_ref[...] = _activate(_dot(xm_ref[...], wb_ref[...]), code).astype(om_ref.dtype)

    o_ref[...] = _activate(_dot(x_ref[...], wb_ref[...]), code).astype(o_ref.dtype)


def _inproj(xn, xn_meta, w_in, cols, codes, tm):
    m, d = xn.shape
    mm = xn_meta.shape[0]
    nt = len(cols)
    grid_spec = pltpu.PrefetchScalarGridSpec(
        num_scalar_prefetch=2,
        grid=(nt, m // tm),
        in_specs=[
            pl.BlockSpec((tm, d), lambda n, i, col, code: (i, 0)),
            pl.BlockSpec((mm, d), lambda n, i, col, code: (0, 0)),
            pl.BlockSpec((d, COL_TILE), lambda n, i, col, code: (0, col[n])),
        ],
        out_specs=[
            pl.BlockSpec((tm, COL_TILE), lambda n, i, col, code: (i, n)),
            pl.BlockSpec((mm, COL_TILE), lambda n, i, col, code: (0, n)),
        ],
        scratch_shapes=[pltpu.VMEM((d, COL_TILE), BF16)],
    )
    return pl.pallas_call(
        _inproj_kernel,
        grid_spec=grid_spec,
        out_shape=[jax.ShapeDtypeStruct((m, nt * COL_TILE), BF16), jax.ShapeDtypeStruct((mm, nt * COL_TILE), BF16)],
        compiler_params=_params("arbitrary", "arbitrary"),
        name="inproj",
    )(jnp.asarray(cols, jnp.int32), jnp.asarray(codes, jnp.int32), xn, xn_meta, w_in)


def _fgate_kernel(x_ref, xm_ref, nw_ref, w_ref, lbl_ref, xn_ref, lf_ref, kk_ref, xnm_ref, lfm_ref, kkm_ref, wb_ref):
    lbl = lbl_ref[...]
    e = jnp.exp(lbl - jnp.max(lbl, axis=0, keepdims=True))
    lb = e[0:1] / jnp.sum(e, axis=0, keepdims=True)

    def gate(x, xn_out, lf_out, kk_out):
        ms = jnp.mean(x * x, axis=-1, keepdims=True)
        xn = (x * lax.rsqrt(ms + EPS) * nw_ref[...]).astype(BF16)
        xn_out[...] = xn
        z = _dot(xn, wb_ref[...])
        lf_out[...] = jnp.log(lb + (1.0 - lb) * jax.nn.sigmoid(z))
        kk_out[...] = (1.0 - lb) * jax.nn.sigmoid(-z)

    @pl.when(pl.program_id(0) == 0)
    def _():
        wb_ref[...] = w_ref[...].astype(BF16)
        gate(xm_ref[...], xnm_ref, lfm_ref, kkm_ref)

    gate(x_ref[...], xn_ref, lf_ref, kk_ref)


def _fgate(x, x_meta, norm_w, w_in, lb_logits, col, tm):
    m, d = x.shape
    mm = x_meta.shape[0]
    r = lb_logits.shape[0]
    out = jax.ShapeDtypeStruct((m, COL_TILE), F32)
    out_meta = jax.ShapeDtypeStruct((mm, COL_TILE), F32)
    return pl.pallas_call(
        _fgate_kernel,
        grid=(m // tm,),
        in_specs=[
            pl.BlockSpec((tm, d), lambda i: (i, 0)),
            pl.BlockSpec((mm, d), lambda i: (0, 0)),
            pl.BlockSpec((1, d), lambda i: (0, 0)),
            pl.BlockSpec((d, COL_TILE), lambda i: (0, col)),
            pl.BlockSpec((r, COL_TILE), lambda i: (0, 0)),
        ],
        out_specs=[pl.BlockSpec((tm, d), lambda i: (i, 0))] + [pl.BlockSpec((tm, COL_TILE), lambda i: (i, 0))] * 2
        + [pl.BlockSpec((mm, d), lambda i: (0, 0))] + [pl.BlockSpec((mm, COL_TILE), lambda i: (0, 0))] * 2,
        out_shape=[jax.ShapeDtypeStruct((m, d), BF16), out, out,
                   jax.ShapeDtypeStruct((mm, d), BF16), out_meta, out_meta],
        scratch_shapes=[pltpu.VMEM((d, COL_TILE), BF16)],
        compiler_params=_params("arbitrary"),
        name="norm_fgate",
    )(x, x_meta, norm_w.reshape(1, d), w_in, lb_logits)


def _hgrn_constants(c):
    base = HGRN_BASE
    nlow = int(np.log2(base))
    nl = int(np.log2(c))
    assert (1 << nl) == c and (1 << nlow) == base and c >= base
    rr = np.arange(base)[:, None]
    uu = np.arange(base)[None, :]
    mats = [uu <= rr, uu > rr]
    sels = []
    for lvl in range(nlow):
        b = 1 << lvl
        start = (rr // (2 * b)) * (2 * b)
        mid = start + b - 1
        second = (rr - start) >= b
        mats.append(np.where(second, (uu > mid) & (uu <= rr), (uu > rr) & (uu <= mid)))
        sels.append(np.broadcast_to(second, (base, HEADS * HEAD_DIM)))
    tt = np.arange(c)[:, None]
    ss = np.arange(c)[None, :]
    masks = []
    for lvl in range(nl):
        b = 1 << lvl
        masks.append(((tt // (2 * b)) == (ss // (2 * b))) & ((tt % (2 * b)) >= b) & ((ss % (2 * b)) < b))
    masks.append(np.eye(c, dtype=bool))
    m1 = np.concatenate(mats, 0).astype(np.float32)
    mall = jnp.asarray(np.concatenate([m1, m1, m1], axis=1), BF16)
    return mall, jnp.asarray(np.stack(masks).astype(np.float32)), jnp.asarray(np.stack(sels).astype(np.float32))


def _hgrn_kernel(q_ref, v_ref, g_ref, lf_ref, kk_ref, gw_ref, s0_ref, mall_ref, mask_ref, sel_ref,
                 o_ref, sfin_ref, st_ref, *, chunk, n_chunks):
    base = HGRN_BASE
    nb = chunk // base
    nlow = sel_ref.shape[0]
    nl = mask_ref.shape[0] - 1
    step = pl.program_id(1)

    @pl.when(step == 0)
    def _():
        st_ref[...] = s0_ref[...]

    def chunk_body(ci, carry):
        r0 = pl.multiple_of(ci * chunk, chunk)
        rows = pl.ds(r0, chunk)
        qb = q_ref[rows, :]
        q = qb.astype(F32)
        k = kk_ref[rows, :]
        blk = lambda a, i: a[i * base:(i + 1) * base]

        pre, suf, e_low = [], [], []
        for i in range(nb):
            hi, mid, lo = _split3(lf_ref[pl.ds(r0 + i * base, base), :])
            args = _dot(mall_ref[...], jnp.concatenate([hi, mid, lo], axis=0))
            pre.append(args[0:base])
            suf.append(args[base:2 * base])
            e_low.append(jnp.exp(args[2 * base:]))
        tot = [p[base - 1:base] for p in pre]

        def span(lo_blk, hi_blk):
            acc = None
            for j in range(lo_blk, hi_blk):
                acc = tot[j] if acc is None else acc + tot[j]
            return acc

        def shifted(a, off):
            return a if off is None else a + off

        xs = []
        for lvl in range(nlow):
            parts = [jnp.where(sel_ref[lvl] > 0.5, blk(q, i), blk(k, i)) * blk(e_low[i], lvl) for i in range(nb)]
            xs.append(jnp.concatenate(parts, axis=0).astype(BF16))
        for lvl in range(nlow, nl):
            half = (1 << lvl) // base
            parts = []
            for i in range(nb):
                g = i % (2 * half)
                if g >= half:
                    parts.append(blk(q, i) * jnp.exp(shifted(pre[i], span(i - (g - half), i))))
                else:
                    parts.append(blk(k, i) * jnp.exp(shifted(suf[i], span(i + 1, i - g + half))))
            xs.append(jnp.concatenate(parts, axis=0).astype(BF16))
        q_in = jnp.concatenate([blk(q, i) * jnp.exp(shifted(pre[i], span(0, i))) for i in range(nb)],
                               axis=0).astype(BF16)
        k_out = jnp.concatenate([blk(k, i) * jnp.exp(shifted(suf[i], span(i + 1, nb))) for i in range(nb)],
                                axis=0).astype(BF16)
        dec = jnp.exp(span(0, nb))
        kb = k.astype(BF16)

        for h in range(HEADS):
            cs = slice(h * HEAD_DIM, (h + 1) * HEAD_DIM)
            scores = mask_ref[nl] * _dot_bt(qb[:, cs], kb[:, cs])
            for lvl in range(nl):
                x = xs[lvl][:, cs]
                scores = scores + mask_ref[lvl] * _dot_bt(x, x)
            v = v_ref[rows, cs]
            st = st_ref[h]
            o = _dot(scores.astype(BF16), v) + _dot_bt(q_in[:, cs], st.astype(BF16))
            st_ref[h] = st * dec[:, cs] + _dot_at(v, k_out[:, cs])
            ms = jnp.mean(o * o, axis=-1, keepdims=True)
            on = o * lax.rsqrt(ms + EPS) * gw_ref[...]
            o_ref[rows, cs] = (on * g_ref[rows, cs].astype(F32)).astype(o_ref.dtype)
        return carry

    lax.fori_loop(0, n_chunks, chunk_body, 0)

    @pl.when(step == pl.num_programs(1) - 1)
    def _():
        sfin_ref[0] = st_ref[...]


def _hgrn(proj, lf, kk, g_norm_w, s0, consts, bsz, seq, rows, chunk):
    mall, masks, sels = consts
    steps = seq // rows
    w = HEADS * HEAD_DIM
    assert w == COL_TILE
    row_map = lambda col: (lambda b, s: (b * steps + s, col))
    const2 = lambda b, s: (0, 0)
    const3 = lambda b, s: (0, 0, 0)
    kern = functools.partial(_hgrn_kernel, chunk=chunk, n_chunks=rows // chunk)
    return pl.pallas_call(
        kern,
        grid=(bsz, steps),
        in_specs=[
            pl.BlockSpec((rows, w), row_map(0)),
            pl.BlockSpec((rows, w), row_map(1)),
            pl.BlockSpec((rows, w), row_map(2)),
            pl.BlockSpec((rows, w), row_map(0)),
            pl.BlockSpec((rows, w), row_map(0)),
            pl.BlockSpec((1, HEAD_DIM), const2),
            pl.BlockSpec((HEADS, HEAD_DIM, HEAD_DIM), const3),
            pl.BlockSpec(mall.shape, const2),
            pl.BlockSpec(masks.shape, const3),
            pl.BlockSpec(sels.shape, const3),
        ],
        out_specs=[
            pl.BlockSpec((rows, w), row_map(0)),
            pl.BlockSpec((1, HEADS, HEAD_DIM, HEAD_DIM), lambda b, s: (b, 0, 0, 0)),
        ],
        out_shape=[
            jax.ShapeDtypeStruct((bsz * seq, w), BF16),
            jax.ShapeDtypeStruct((bsz, HEADS, HEAD_DIM, HEAD_DIM), F32),
        ],
        scratch_shapes=[pltpu.VMEM((HEADS, HEAD_DIM, HEAD_DIM), F32)],
        compiler_params=_params("arbitrary", "arbitrary"),
        name="hgrn2",
    )(proj, proj, proj, lf, kk, g_norm_w.reshape(1, HEAD_DIM), s0, mall, masks, sels)


def _mixer_out_kernel(og_ref, scv_ref, scb_ref, scc_ref, ga_ref, gb_ref, x_ref,
                      pv_ref, pc_ref, mv_ref, mc_ref,
                      wa_ref, wb_ref, wo_ref, cw_ref, nw_ref, wr_ref, br_ref,
                      h1_ref, xn_ref, idx_ref, gate_ref, hbuf, *, tiles_per_seq, n_tiles):
    i = pl.program_id(0)
    tm = x_ref.shape[0]
    tile = jnp.minimum(i, n_tiles - 1)
    first = (tile % tiles_per_seq) == 0

    @pl.when(i == 0)
    def _():
        hbuf[1] = jnp.zeros(hbuf.shape[1:], hbuf.dtype)

    hp = hbuf[(i + 1) % 2]
    ms = jnp.mean(hp * hp, axis=-1, keepdims=True)
    xn = hp * lax.rsqrt(ms + EPS) * nw_ref[...]
    xn_ref[...] = xn.astype(BF16).reshape(xn_ref.shape)

    xh = xn.astype(BF16)
    xl = (xn - xh.astype(F32)).astype(BF16)
    wr = wr_ref[...]
    wh = wr.astype(BF16)
    wl = (wr - wh.astype(F32)).astype(BF16)
    logits = _dot_bt(wh, xh) + _dot_bt(wh, xl) + _dot_bt(wl, xh) + br_ref[...]
    ne = logits.shape[0]
    ie = lax.broadcasted_iota(jnp.int32, logits.shape, 0)
    tops, idxs = [], []
    for _ in range(TOP_K):
        mx = jnp.max(logits, axis=0, keepdims=True)
        ix = jnp.min(jnp.where(logits == mx, ie, ne), axis=0, keepdims=True)
        tops.append(mx)
        idxs.append(ix)
        logits = jnp.where(ie == ix, -jnp.inf, logits)
    es = [jnp.exp(t - tops[0]) for t in tops]
    den = es[0]
    for e in es[1:]:
        den = den + e
    gate_ref[...] = jnp.concatenate([e / den for e in es], axis=0)
    idx_ref[...] = jnp.concatenate(idxs, axis=0)

    u = scc_ref[...].astype(F32) * scv_ref[...].astype(F32)
    halo_prev = pc_ref[...].astype(F32) * pv_ref[...].astype(F32)
    halo_meta = mc_ref[...].astype(F32) * mv_ref[...].astype(F32)
    halo = jnp.where(first, halo_meta, halo_prev)
    hr = halo.shape[0]
    r = lax.broadcasted_iota(jnp.int32, (tm, 1), 0)
    u1 = jnp.where(r == 0, halo[hr - 1:hr], pltpu.roll(u, 1, 0))
    u2 = jnp.where(r == 0, halo[hr - 2:hr - 1], jnp.where(r == 1, halo[hr - 1:hr], pltpu.roll(u, 2, 0)))
    conv = cw_ref[2:3] * u + cw_ref[1:2] * u1 + cw_ref[0:1] * u2
    yb_in = (scb_ref[...].astype(F32) * conv).astype(BF16)

    y_a = _dot(og_ref[...], wa_ref[...])
    y_b = _dot(yb_in, wb_ref[...])
    merged = (ga_ref[...].astype(F32) * y_a + gb_ref[...].astype(F32) * y_b).astype(BF16)
    h1 = x_ref[...] + _dot(merged, wo_ref[...])
    h1_ref[...] = h1
    hbuf[i % 2] = h1


def _mixer_out(og, proj, proj_meta, x2d, wa, wb, wo, conv_w, norm_w, w_router, b_router, seq, tm):
    m, d = x2d.shape
    w = COL_TILE
    halo = proj_meta.shape[0]
    assert tm % halo == 0 and seq % tm == 0 and d == 2 * w
    ne = w_router.shape[1]
    per_halo = tm // halo
    n_tiles = m // tm
    cur = lambda i: jnp.minimum(i, n_tiles - 1)
    done = lambda i: jnp.maximum(i - 1, 0)
    row = lambda col: (lambda i: (cur(i), col))
    prev = lambda col: (lambda i: (jnp.maximum(cur(i) * per_halo - 1, 0), col))
    const = lambda i: (0, 0)
    whole = lambda a: pl.BlockSpec(a.shape, const)
    wr_t = w_router.T
    kern = functools.partial(_mixer_out_kernel, tiles_per_seq=seq // tm, n_tiles=n_tiles)
    return pl.pallas_call(
        kern,
        grid=(n_tiles + 1,),
        in_specs=[
            pl.BlockSpec((tm, w), row(0)),
            pl.BlockSpec((tm, w), row(3)),
            pl.BlockSpec((tm, w), row(4)),
            pl.BlockSpec((tm, w), row(5)),
            pl.BlockSpec((tm, d), row(3)),
            pl.BlockSpec((tm, d), row(4)),
            pl.BlockSpec((tm, d), row(0)),
            pl.BlockSpec((halo, w), prev(3)),
            pl.BlockSpec((halo, w), prev(5)),
            pl.BlockSpec((halo, w), lambda i: (0, 3)),
            pl.BlockSpec((halo, w), lambda i: (0, 5)),
            whole(wa), whole(wb), whole(wo),
            pl.BlockSpec(conv_w.shape, const),
            pl.BlockSpec((1, d), const),
            pl.BlockSpec((ne, d), const),
            pl.BlockSpec((ne, 1), const),
        ],
        out_specs=[
            pl.BlockSpec((tm, d), row(0)),
            pl.BlockSpec((tm, d // LANES, LANES), lambda i: (done(i), 0, 0)),
            pl.BlockSpec((TOP_K, tm), lambda i: (0, done(i))),
            pl.BlockSpec((TOP_K, tm), lambda i: (0, done(i))),
        ],
        out_shape=[
            jax.ShapeDtypeStruct((m, d), F32),
            jax.ShapeDtypeStruct((m, d // LANES, LANES), BF16),
            jax.ShapeDtypeStruct((TOP_K, m), jnp.int32),
            jax.ShapeDtypeStruct((TOP_K, m), F32),
        ],
        scratch_shapes=[pltpu.VMEM((2, tm, d), F32)],
        compiler_params=_params("arbitrary"),
        name="mixer_out",
    )(og, proj, proj, proj, proj, proj, x2d, proj, proj, proj_meta, proj_meta,
      wa, wb, wo, conv_w, norm_w.reshape(1, d), wr_t, b_router.reshape(ne, 1))


def _dispatch_kernel(cnt_ref, pst_ref, used_ref, pos_ref, x_ref, o_hbm, xbuf, zbuf, sem, zsem):
    i = pl.program_id(0)
    n_steps = pl.num_programs(0)
    tt = x_ref.shape[0]
    n_sub = o_hbm.shape[0] // EXPERT_SUB
    bits = [1 << b for b in reversed(range(EXPERT_SUB.bit_length() - 1))]
    slot = i % 2

    xbuf[slot] = x_ref[...]

    def issue(r, carry):
        for k in range(TOP_K):
            p = pos_ref[0, 0, k * tt + r]
            pltpu.make_async_copy(xbuf.at[slot, pl.ds(r, 1)], o_hbm.at[pl.ds(p, 1)],
                                  sem.at[slot]).start(priority=k % 2)
        return carry

    lax.fori_loop(0, tt, issue, 0, unroll=4)

    def tokens_done(s):
        return [pltpu.make_async_copy(xbuf.at[s], o_hbm.at[pl.ds(0, tt)], sem.at[s]) for _ in range(TOP_K)]

    def zero_copies(fn):
        for e in range(cnt_ref.shape[0]):
            npad = (-cnt_ref[e]) & (EXPERT_SUB - 1)
            base = pst_ref[e] + cnt_ref[e]
            for bit in bits:
                @pl.when((npad & bit) != 0)
                def _():
                    row = base + (npad & ~(2 * bit - 1))
                    fn(pltpu.make_async_copy(zbuf.at[pl.ds(0, bit)], o_hbm.at[pl.ds(row, bit)], zsem))
        for j in range(cnt_ref.shape[0]):
            blk = used_ref[0] + j

            @pl.when(blk < n_sub)
            def _():
                row = pl.multiple_of(blk * EXPERT_SUB, EXPERT_SUB)
                fn(pltpu.make_async_copy(zbuf, o_hbm.at[pl.ds(row, EXPERT_SUB)], zsem))

    @pl.when(i == 0)
    def _():
        zbuf[...] = jnp.zeros_like(zbuf)
        zero_copies(lambda c: c.start())

    @pl.when(i > 0)
    def _():
        for c in tokens_done(1 - slot):
            c.wait()

    @pl.when(i == n_steps - 1)
    def _():
        for c in tokens_done(slot):
            c.wait()
        zero_copies(lambda c: c.wait())


def _dispatch(xn3, pos_t, counts, pad_start, used_sub, n_rows):
    m, sub, lanes = xn3.shape
    nt = pos_t.shape[0]
    tt = m // nt
    grid_spec = pltpu.PrefetchScalarGridSpec(
        num_scalar_prefetch=3,
        grid=(nt,),
        in_specs=[
            pl.BlockSpec((1, 1, pos_t.shape[2]), lambda i, c, p, u: (i, 0, 0), memory_space=pltpu.SMEM),
            pl.BlockSpec((tt, sub, lanes), lambda i, c, p, u: (i, 0, 0)),
        ],
        out_specs=pl.BlockSpec(memory_space=pl.ANY),
        scratch_shapes=[pltpu.VMEM((2, tt, sub, lanes), xn3.dtype), pltpu.VMEM((EXPERT_SUB, sub, lanes), xn3.dtype),
                        pltpu.SemaphoreType.DMA((2,)), pltpu.SemaphoreType.DMA(())],
    )
    return pl.pallas_call(
        _dispatch_kernel,
        grid_spec=grid_spec,
        out_shape=jax.ShapeDtypeStruct((n_rows, sub, lanes), xn3.dtype),
        compiler_params=_params("arbitrary"),
        name="dispatch",
    )(counts, pad_start, used_sub.reshape(1), pos_t, xn3)


def _row_chunks(nsub, chunk_fn):
    per = EXPERT_CHUNK_SUBS
    big = per * EXPERT_SUB
    n_big = lax.div(nsub, per)

    def body(c, carry):
        chunk_fn(c * per, pl.multiple_of(c * big, big), big)
        return carry

    lax.fori_loop(0, n_big, body, 0)
    rem = nsub - n_big * per
    bit = per // 2
    while bit >= 1:
        done = rem & ~(2 * bit - 1)

        @pl.when((rem & bit) != 0)
        def _():
            sub0 = n_big * per + done
            chunk_fn(sub0, pl.multiple_of(sub0 * EXPERT_SUB, EXPERT_SUB), bit * EXPERT_SUB)

        bit //= 2


def _sub_pieces(n, fn):
    bit = 1 << (EXPERT_SUBS.bit_length() - 1)
    while bit >= 1:
        @pl.when((n & bit) != 0)
        def _():
            fn(n & ~(2 * bit - 1), bit)

        bit //= 2


def _ffn_up_kernel(st_ref, ns_ref, zf_ref, se_ref, jm_ref, x_hbm, wg_ref, wu_ref, bg_ref, bu_ref, h_hbm,
                   xstage, xbuf, hbuf, wgb_ref, wub_ref, sem_x, sem_h):
    s = pl.program_id(0)
    j = pl.program_id(1)
    n_s = pl.num_programs(0)
    nj = pl.num_programs(1)
    step = s * nj + j
    nsub = ns_ref[s]
    real = jnp.logical_and(nsub > 0, zf_ref[s] == 0)
    hs = step % 2

    def x_copy(sb, first, count):
        return pltpu.make_async_copy(x_hbm.at[pl.ds(st_ref[sb] + first, count)],
                                     xstage.at[pl.ds(first, count)], sem_x)

    def h_copy(sb, jj, first, count, slot):
        row = pl.multiple_of((st_ref[sb] + first) * EXPERT_SUB, EXPERT_SUB)
        return pltpu.make_async_copy(
            hbuf.at[slot, pl.ds(pl.multiple_of(first * EXPERT_SUB, EXPERT_SUB), count * EXPERT_SUB)],
            h_hbm.at[jj, pl.ds(row, count * EXPERT_SUB)], sem_h.at[slot])

    def for_x_subs(sb, fn):
        _sub_pieces(ns_ref[sb] * (1 - zf_ref[sb]), fn)

    def for_h_subs(sb, fn):
        _sub_pieces(ns_ref[sb], fn)

    @pl.when(step == 0)
    def _():
        for_x_subs(0, lambda first, count: x_copy(0, first, count).start())

    @pl.when(j == 0)
    def _():
        for_x_subs(s, lambda first, count: x_copy(s, first, count).wait())

    @pl.when(step >= 2)
    def _():
        sp = lax.div(step - 2, nj)
        jp = step - 2 - sp * nj
        for_h_subs(sp, lambda first, count: h_copy(sp, jp, first, count, hs).wait())

    @pl.when(real)
    def _():
        wgb_ref[...] = wg_ref[0].astype(BF16)
        wub_ref[...] = wu_ref[0].astype(BF16)

        def make_chunk(first_tile):
            def chunk(sub0, row0, nrows):
                if first_tile:
                    xs = jnp.concatenate([xstage[sub0 + t].reshape(EXPERT_SUB, xbuf.shape[1])
                                          for t in range(nrows // EXPERT_SUB)], axis=0)
                    xbuf[pl.ds(row0, nrows), :] = xs
                else:
                    xs = xbuf[pl.ds(row0, nrows), :]
                g = _dot(xs, wgb_ref[...]) + bg_ref[0]
                u = _dot(xs, wub_ref[...]) + bu_ref[0]
                g = jnp.minimum(g, SWIGLU_LIMIT)
                u = jnp.clip(u, -SWIGLU_LIMIT, SWIGLU_LIMIT)
                hbuf[hs, pl.ds(row0, nrows), :] = ((u + 1.0) * (g * jax.nn.sigmoid(SWIGLU_ALPHA * g))).astype(BF16)

            return chunk

        @pl.when(j == 0)
        def _():
            _row_chunks(nsub, make_chunk(True))

        @pl.when(j > 0)
        def _():
            _row_chunks(nsub, make_chunk(False))

    @pl.when(jnp.logical_and(j == 0, s + 1 < n_s))
    def _():
        nxt = jnp.minimum(s + 1, n_s - 1)
        for_x_subs(nxt, lambda first, count: x_copy(nxt, first, count).start())

    @pl.when(zf_ref[s] == 1)
    def _():
        hbuf[hs] = jnp.zeros(hbuf.shape[1:], hbuf.dtype)

    for_h_subs(s, lambda first, count: h_copy(s, j, first, count, hs).start())

    @pl.when(step == n_s * nj - 1)
    def _():
        sp = lax.div(step - 1, nj)
        jp = step - 1 - sp * nj
        for_h_subs(sp, lambda first, count: h_copy(sp, jp, first, count, 1 - hs).wait())
        for_h_subs(s, lambda first, count: h_copy(s, j, first, count, hs).wait())


def _ffn_down_kernel(st_ref, ns_ref, zf_ref, se_ref, cm_ref, h_hbm, wd_ref, bd_ref, y_hbm,
                     hb, yrow, ytile, wdb_ref, sem_h, sem_y):
    s = pl.program_id(0)
    c = pl.program_id(1)
    n_s = pl.num_programs(0)
    nc = pl.num_programs(1)
    step = s * nc + c
    njh = hb.shape[1]
    tn = wdb_ref.shape[1]
    nsub = ns_ref[s]
    real = jnp.logical_and(nsub > 0, zf_ref[s] == 0)

    def h_copy(sb, jj, first, count, slot):
        row = pl.multiple_of((st_ref[sb] + first) * EXPERT_SUB, EXPERT_SUB)
        return pltpu.make_async_copy(
            h_hbm.at[jj, pl.ds(row, count * EXPERT_SUB)],
            hb.at[slot, jj, pl.ds(pl.multiple_of(first * EXPERT_SUB, EXPERT_SUB), count * EXPERT_SUB)],
            sem_h.at[slot])

    def y_copy(sb, first, count):
        return pltpu.make_async_copy(ytile.at[pl.ds(first, count)],
                                     y_hbm.at[pl.ds(st_ref[sb] + first, count)], sem_y)

    def for_h_subs(sb, fn):
        def all_tiles(first, count):
            for jj in range(njh):
                fn(jj, first, count)

        _sub_pieces(ns_ref[sb] * (1 - zf_ref[sb]), all_tiles)

    def for_y_subs(sb, fn):
        _sub_pieces(ns_ref[sb], fn)

    @pl.when(step == 0)
    def _():
        for_h_subs(0, lambda jj, first, count: h_copy(0, jj, first, count, 0).start())

    @pl.when(jnp.logical_and(c == 0, s + 1 < n_s))
    def _():
        nxt = jnp.minimum(s + 1, n_s - 1)
        for_h_subs(nxt, lambda jj, first, count: h_copy(nxt, jj, first, count, (s + 1) % 2).start())

    @pl.when(c == 0)
    def _():
        for_h_subs(s, lambda jj, first, count: h_copy(s, jj, first, count, s % 2).wait())

    @pl.when(jnp.logical_and(c == nc - 1, s > 0))
    def _():
        sp = jnp.maximum(s - 1, 0)
        for_y_subs(sp, lambda first, count: y_copy(sp, first, count).wait())

    @pl.when(real)
    def _():
        wdb_ref[...] = wd_ref[0].astype(BF16)
        h_slot = s % 2
        n_tiles = yrow.shape[1] // tn
        for cc in range(n_tiles):
            @pl.when(c == cc)
            def _():
                def chunk(sub0, row0, nrows):
                    hid = jnp.concatenate([hb[h_slot, jj, pl.ds(row0, nrows), :] for jj in range(njh)], axis=1)
                    y = (_dot(hid, wdb_ref[...]) + bd_ref[0]).astype(yrow.dtype)
                    if cc < n_tiles - 1:
                        yrow[pl.ds(row0, nrows), cc * tn:(cc + 1) * tn] = y
                    else:
                        full = jnp.concatenate([yrow[pl.ds(row0, nrows), 0:cc * tn], y], axis=1)
                        for t in range(nrows // EXPERT_SUB):
                            ytile[sub0 + t] = full[t * EXPERT_SUB:(t + 1) * EXPERT_SUB].reshape(ytile.shape[1:])

                _row_chunks(nsub, chunk)

    @pl.when(c == nc - 1)
    def _():
        @pl.when(zf_ref[s] == 1)
        def _():
            ytile[...] = jnp.zeros(ytile.shape, ytile.dtype)

        for_y_subs(s, lambda first, count: y_copy(s, first, count).start())

        @pl.when(s == n_s - 1)
        def _():
            for_y_subs(s, lambda first, count: y_copy(s, first, count).wait())


def _experts(x_rows, w_up, b_up, w_down, b_down, sb_start, sb_nsub, sb_zero, sb_expert):
    n_rows, x_sub, x_lanes = x_rows.shape
    d = x_sub * x_lanes
    ne, _, ff2 = w_up.shape
    ff = ff2 // 2
    tf = EXPERT_FF_TILE
    tn = EXPERT_OUT_TILE
    nj = ff // tf
    nc = d // tn
    n_sb = jnp.sum((sb_nsub > 0).astype(jnp.int32))
    rows = EXPERT_SUBS * EXPERT_SUB
    n_sub = n_rows // EXPERT_SUB
    sub_shape = (n_sub, EXPERT_SUB, x_sub, x_lanes)
    x_rows = x_rows.reshape(sub_shape)
    sb_start = sb_start // EXPERT_SUB
    is_real = jnp.logical_and(sb_nsub > 0, sb_zero == 0)[:, None]
    jm = jnp.where(is_real, jnp.arange(nj, dtype=jnp.int32)[None, :], nj - 1).astype(jnp.int32)
    cm = jnp.where(is_real, jnp.arange(nc, dtype=jnp.int32)[None, :], nc - 1).astype(jnp.int32)
    any_spec = pl.BlockSpec(memory_space=pl.ANY)

    up_spec = pltpu.PrefetchScalarGridSpec(
        num_scalar_prefetch=5,
        grid=(n_sb, nj),
        in_specs=[
            any_spec,
            pl.BlockSpec((1, d, tf), lambda s, j, st, ns, zf, se, jm: (se[s], 0, jm[s, j])),
            pl.BlockSpec((1, d, tf), lambda s, j, st, ns, zf, se, jm: (se[s], 0, nj + jm[s, j])),
            pl.BlockSpec((1, 1, tf), lambda s, j, st, ns, zf, se, jm: (se[s], 0, jm[s, j])),
            pl.BlockSpec((1, 1, tf), lambda s, j, st, ns, zf, se, jm: (se[s], 0, nj + jm[s, j])),
        ],
        out_specs=any_spec,
        scratch_shapes=[
            pltpu.VMEM((EXPERT_SUBS, EXPERT_SUB, x_sub, x_lanes), BF16),
            pltpu.VMEM((rows, d), BF16),
            pltpu.VMEM((2, rows, tf), BF16),
            pltpu.VMEM((d, tf), BF16),
            pltpu.VMEM((d, tf), BF16),
            pltpu.SemaphoreType.DMA(()),
            pltpu.SemaphoreType.DMA((2,)),
        ],
    )
    hidden = pl.pallas_call(
        _ffn_up_kernel,
        grid_spec=up_spec,
        out_shape=jax.ShapeDtypeStruct((nj, n_rows, tf), BF16),
        compiler_params=_params("arbitrary", "arbitrary"),
        name="ffn_up",
    )(sb_start, sb_nsub, sb_zero, sb_expert, jm, x_rows, w_up, w_up,
      b_up.reshape(ne, 1, ff2), b_up.reshape(ne, 1, ff2))

    down_spec = pltpu.PrefetchScalarGridSpec(
        num_scalar_prefetch=5,
        grid=(n_sb, nc),
        in_specs=[
            any_spec,
            pl.BlockSpec((1, ff, tn), lambda s, c, st, ns, zf, se, cm: (se[s], 0, cm[s, c])),
            pl.BlockSpec((1, 1, tn), lambda s, c, st, ns, zf, se, cm: (se[s], 0, cm[s, c])),
        ],
        out_specs=any_spec,
        scratch_shapes=[
            pltpu.VMEM((2, nj, rows, tf), BF16),
            pltpu.VMEM((rows, d), BF16),
            pltpu.VMEM((EXPERT_SUBS, EXPERT_SUB, x_sub, x_lanes), BF16),
            pltpu.VMEM((ff, tn), BF16),
            pltpu.SemaphoreType.DMA((2,)),
            pltpu.SemaphoreType.DMA(()),
        ],
    )
    y_rows = pl.pallas_call(
        _ffn_down_kernel,
        grid_spec=down_spec,
        out_shape=jax.ShapeDtypeStruct(sub_shape, BF16),
        compiler_params=_params("arbitrary", "arbitrary"),
        name="ffn_down",
    )(sb_start, sb_nsub, sb_zero, sb_expert, cm, hidden, w_down, b_down.reshape(ne, 1, d))
    return y_rows.reshape(n_rows, x_sub, x_lanes)


def _combine_kernel(pos_ref, pos_next_ref, gate_ref, h1_ref, fw_ref, y_hbm, o_ref, buf_ref, sem):
    i = pl.program_id(0)
    tt = h1_ref.shape[0]

    def start_rows(pos, slot):
        def issue(r, carry):
            for k in range(TOP_K):
                p = pos[0, 0, k * tt + r]
                pltpu.make_async_copy(y_hbm.at[pl.ds(p, 1)], buf_ref.at[slot, k, pl.ds(r, 1)],
                                      sem.at[slot]).start(priority=k % 2)
            return carry

        lax.fori_loop(0, tt, issue, 0, unroll=4)

    @pl.when(i == 0)
    def _():
        start_rows(pos_ref, 0)

    @pl.when(i + 1 < pl.num_programs(0))
    def _():
        start_rows(pos_next_ref, (i + 1) % 2)

    slot = i % 2
    for k in range(TOP_K):
        pltpu.make_async_copy(y_hbm.at[pl.ds(0, tt)], buf_ref.at[slot, k], sem.at[slot]).wait()

    gate = gate_ref[...]
    gpad = jnp.concatenate([gate, jnp.zeros((tt - TOP_K, tt), F32)], axis=0)
    gcol = gpad.T
    acc = h1_ref[...]
    for k in range(TOP_K):
        acc = acc + gcol[:, k:k + 1] * buf_ref[slot, k].reshape(acc.shape).astype(F32)
    ms = jnp.mean(acc * acc, axis=-1, keepdims=True)
    o_ref[...] = acc * lax.rsqrt(ms + EPS) * fw_ref[...]


def _pos_tiles(pos, m):
    tt = COMBINE_ROWS
    nt = m // tt
    return pos.reshape(TOP_K, nt, tt).transpose(1, 0, 2).reshape(nt, 1, TOP_K * tt)


def _combine(y_rows, pos_t, gate, h1, final_w):
    m, d = h1.shape
    tt = COMBINE_ROWS
    nt = m // tt
    return pl.pallas_call(
        _combine_kernel,
        grid=(nt,),
        in_specs=[
            pl.BlockSpec((1, 1, TOP_K * tt), lambda i: (i, 0, 0), memory_space=pltpu.SMEM),
            pl.BlockSpec((1, 1, TOP_K * tt), lambda i: (jnp.minimum(i + 1, nt - 1), 0, 0), memory_space=pltpu.SMEM),
            pl.BlockSpec((TOP_K, tt), lambda i: (0, i)),
            pl.BlockSpec((tt, d), lambda i: (i, 0)),
            pl.BlockSpec((1, d), lambda i: (0, 0)),
            pl.BlockSpec(memory_space=pl.ANY),
        ],
        out_specs=pl.BlockSpec((tt, d), lambda i: (i, 0)),
        out_shape=jax.ShapeDtypeStruct((m, d), F32),
        scratch_shapes=[pltpu.VMEM((2, TOP_K, tt) + y_rows.shape[1:], y_rows.dtype), pltpu.SemaphoreType.DMA((2,))],
        compiler_params=_params("arbitrary"),
        name="combine",
    )(pos_t, pos_t, gate, h1, final_w.reshape(1, d), y_rows)


def _rank_kernel(idx_ref, tri_ref, rank_ref, cnt_ref, carry_ref):
    i = pl.program_id(0)
    tt = idx_ref.shape[1]
    w = tri_ref.shape[0]

    @pl.when(i == 0)
    def _():
        carry_ref[...] = jnp.zeros_like(carry_ref)

    carry = carry_ref[:, 0:1]
    expert = lax.broadcasted_iota(jnp.int32, (N_EXPERTS, w), 0)
    for k in range(TOP_K):
        for c in range(tt // w):
            cols = slice(c * w, (c + 1) * w)
            hit = expert == idx_ref[k:k + 1, cols]
            prefix = _dot(hit.astype(BF16), tri_ref[...])
            before = jnp.where(hit, carry + prefix - 1.0, 0.0)
            rank_ref[k:k + 1, cols] = jnp.sum(before, axis=0, keepdims=True).astype(jnp.int32)
            carry = carry + prefix[:, w - 1:w]
    carry_ref[...] = jnp.broadcast_to(carry, carry_ref.shape)
    cnt_ref[...] = jnp.broadcast_to(carry, cnt_ref.shape)


def _ranks(idx):
    m = idx.shape[1]
    tt = RANK_ROWS
    w = RANK_LANES
    tri = jnp.asarray(np.triu(np.ones((w, w), np.float32)), BF16)
    rank, cnt = pl.pallas_call(
        _rank_kernel,
        grid=(m // tt,),
        in_specs=[pl.BlockSpec((TOP_K, tt), lambda i: (0, i)), pl.BlockSpec((w, w), lambda i: (0, 0))],
        out_specs=[pl.BlockSpec((TOP_K, tt), lambda i: (0, i)), pl.BlockSpec((N_EXPERTS, LANES), lambda i: (0, 0))],
        out_shape=[jax.ShapeDtypeStruct((TOP_K, m), jnp.int32), jax.ShapeDtypeStruct((N_EXPERTS, LANES), F32)],
        scratch_shapes=[pltpu.VMEM((N_EXPERTS, LANES), F32)],
        compiler_params=_params("arbitrary"),
        name="ranks",
    )(idx, tri)
    return rank, cnt[:, 0].astype(jnp.int32)


def _position_kernel(idx_ref, rank_ref, start_ref, pos_ref):
    start = start_ref[:, 0:1]
    expert = lax.broadcasted_iota(jnp.int32, (N_EXPERTS, idx_ref.shape[1]), 0)
    for k in range(TOP_K):
        hit = expert == idx_ref[k:k + 1, :]
        base = jnp.sum(jnp.where(hit, start, 0.0), axis=0, keepdims=True)
        pos_ref[k:k + 1, :] = base.astype(jnp.int32) + rank_ref[k:k + 1, :]


def _positions(idx, rank, group_start):
    m = idx.shape[1]
    tt = RANK_ROWS
    table = jnp.broadcast_to(group_start.astype(F32)[:, None], (N_EXPERTS, LANES))
    tile = pl.BlockSpec((TOP_K, tt), lambda i: (0, i))
    return pl.pallas_call(
        _position_kernel,
        grid=(m // tt,),
        in_specs=[tile, tile, pl.BlockSpec((N_EXPERTS, LANES), lambda i: (0, 0))],
        out_specs=tile,
        out_shape=jax.ShapeDtypeStruct((TOP_K, m), jnp.int32),
        compiler_params=_params("arbitrary"),
        name="positions",
    )(idx, rank, table)


def _routing_tables(idx, m):
    i32 = jnp.int32
    n_assign = TOP_K * m
    rank, counts = _ranks(idx)
    padded = (counts + EXPERT_SUB - 1) // EXPERT_SUB * EXPERT_SUB
    pad_end = jnp.cumsum(padded)
    pad_start = pad_end - padded
    pos = _positions(idx, rank, pad_start).reshape(n_assign)
    n_sub = -(-(n_assign + N_EXPERTS * (EXPERT_SUB - 1)) // EXPERT_SUB)
    n_rows = n_sub * EXPERT_SUB
    used_sub = pad_end[-1] // EXPERT_SUB

    rows = EXPERT_SUBS * EXPERT_SUB
    n_sb = -(-n_sub // EXPERT_SUBS) + N_EXPERTS + 1
    nsb_e = (padded + rows - 1) // rows
    sb_cum = jnp.cumsum(nsb_e)
    total_real = sb_cum[-1]
    s = jnp.arange(n_sb, dtype=i32)
    e_s = jnp.minimum(jnp.sum((sb_cum[None, :] <= s[:, None]).astype(i32), axis=1), N_EXPERTS - 1)
    local = s - (sb_cum[e_s] - nsb_e[e_s])
    real = s < total_real
    start_real = pad_start[e_s] + local * rows
    nsub_real = jnp.clip((padded[e_s] - local * rows) // EXPERT_SUB, 0, EXPERT_SUBS)
    fill_idx = s - total_real
    start_fill = pad_end[-1] + fill_idx * rows
    nsub_fill = jnp.clip(n_sub - used_sub - fill_idx * EXPERT_SUBS, 0, EXPERT_SUBS)
    is_fill = jnp.logical_and(jnp.logical_not(real), nsub_fill > 0)
    sb_start = jnp.where(real, start_real, jnp.where(is_fill, start_fill, 0)).astype(i32)
    sb_nsub = jnp.where(real, nsub_real, jnp.where(is_fill, nsub_fill, 0)).astype(i32)
    sb_zero = is_fill.astype(i32)
    last_e = e_s[jnp.maximum(total_real - 1, 0)]
    sb_expert = jnp.where(real, e_s, last_e).astype(i32)
    return pos, counts.astype(i32), pad_start.astype(i32), used_sub.astype(i32), n_rows, sb_start, sb_nsub, sb_zero, sb_expert


def _main_tiles(d):
    w = HEADS * HEAD_DIM
    sc = d // 2
    sizes = (w, w, w, w, sc, sc, sc, d, d)
    acts = (1, None, 0, 1, 0, 0, 0, 2, 2)
    order = (0, 2, 3, 4, 5, 6, 7, 8)
    starts = np.concatenate([[0], np.cumsum(sizes)])
    cols, codes = [], []
    for seg in order:
        assert sizes[seg] % COL_TILE == 0 and starts[seg] % COL_TILE == 0
        for t in range(sizes[seg] // COL_TILE):
            cols.append(int(starts[seg]) // COL_TILE + t)
            codes.append(acts[seg])
    return cols, codes, int(starts[1]) // COL_TILE


def _cast_kernel(a_ref, b_ref, c_ref, ao_ref, bo_ref, co_ref):
    ao_ref[...] = a_ref[...].astype(ao_ref.dtype)
    bo_ref[...] = b_ref[...].astype(bo_ref.dtype)
    co_ref[...] = c_ref[...].astype(co_ref.dtype)


def _cast_mixer_weights(w_a, w_b, w_o):
    steps = 8
    specs = [pl.BlockSpec((w.shape[0] // steps, w.shape[1]), lambda i: (i, 0)) for w in (w_a, w_b, w_o)]
    return pl.pallas_call(
        _cast_kernel,
        grid=(steps,),
        in_specs=specs,
        out_specs=specs,
        out_shape=[jax.ShapeDtypeStruct(w.shape, BF16) for w in (w_a, w_b, w_o)],
        compiler_params=_params("arbitrary"),
        name="cast_mixer_weights",
    )(w_a, w_b, w_o)


def _layer(x2d, meta, bsz, seq, norm_mix_w, w_in, lb_logits, g_norm_w, w_hgrn_out, conv_w, w_conv_out, w_o,
           norm_ffn_w, w_router, b_router, w_up, b_up, w_down, b_down, final_norm_w):
    m, d = x2d.shape
    cols, codes, fcol = _main_tiles(d)
    consts = _hgrn_constants(CHUNK)

    xn, lf, kk, xn_meta, lf_meta, kk_meta = _fgate(x2d, meta, norm_mix_w, w_in, lb_logits, fcol, NORM_ROWS)
    proj, proj_meta = _inproj(xn, xn_meta, w_in, cols, codes, INPROJ_ROWS)

    meta_chunk = HGRN_BASE
    pad = meta_chunk - N_META
    front = lambda a: jnp.pad(a, ((pad, 0), (0, 0)))
    s_zero = jnp.zeros((HEADS, HEAD_DIM, HEAD_DIM), F32)
    _, s_meta = _hgrn(front(proj_meta), front(lf_meta), front(kk_meta), g_norm_w, s_zero,
                      _hgrn_constants(meta_chunk), 1, meta_chunk, meta_chunk, meta_chunk)
    og, _ = _hgrn(proj, lf, kk, g_norm_w, s_meta[0], consts, bsz, seq, HGRN_ROWS, CHUNK)
    wa, wb, wo = _cast_mixer_weights(w_hgrn_out, w_conv_out, w_o)
    h1, xn_ffn, idx, gate = _mixer_out(og, proj, proj_meta, x2d, wa, wb, wo, conv_w, norm_ffn_w, w_router, b_router,
                                   seq, MIXER_ROWS)

    pos, counts, pad_start, used_sub, n_rows, sb_start, sb_nsub, sb_zero, sb_expert = _routing_tables(idx, m)
    pos_t = _pos_tiles(pos, m)
    x_rows = _dispatch(xn_ffn, pos_t, counts, pad_start, used_sub, n_rows)
    y_rows = _experts(x_rows, w_up, b_up, w_down, b_down, sb_start, sb_nsub, sb_zero, sb_expert)
    return _combine(y_rows, pos_t, gate, h1, final_norm_w)


def kernel(x, meta_tokens, norm_mix_w, w_in, lb_logits, g_norm_w, w_hgrn_out, conv_w, w_conv_out, w_o, norm_ffn_w,
           w_router, b_router, w_up, b_up, w_down, b_down, final_norm_w):
    bsz, seq, d = x.shape
    assert norm_mix_w.shape[0] == 1, "single-layer block"
    out = _layer(x.reshape(bsz * seq, d), meta_tokens.astype(x.dtype), bsz, seq, norm_mix_w[0], w_in[0], lb_logits,
                 g_norm_w[0], w_hgrn_out[0], conv_w[0], w_conv_out[0], w_o[0], norm_ffn_w[0], w_router[0],
                 b_router[0], w_up[0], b_up[0], w_down[0], b_down[0], final_norm_w)
    return out.reshape(bsz, seq, d)
```
